```python
import jax, jax.numpy as jnp
from jax import lax
import numpy as np

D_MODEL = 2048
BATCH = 4
SEQ = 8192
DEPTH = 2

MEM_LEN = 256
M_HEADS = 4
M_HEAD_DIM = D_MODEL // 8
M_WIDTH = M_HEADS * M_HEAD_DIM
M_CONV = 4
M_CHUNK = 64
N_HEADS = 16
N_KV_HEADS = 4
N_HEAD_DIM = D_MODEL // 32
N_WIDTH = N_HEADS * N_HEAD_DIM
N_KV_WIDTH = N_KV_HEADS * N_HEAD_DIM
CMP_LEN = 32
CMP_STRIDE = 16
CMP_HIDDEN = 2 * N_HEAD_DIM
SEL_BLOCK = 64
SEL_TOP = 16
WINDOW = 512
N_QBLOCK = 64
FORCE_SCORE = 1.0e4
C_HEADS = 4
C_HEAD_DIM = D_MODEL // 8
C_WIDTH = C_HEADS * C_HEAD_DIM
D_FF = -((-8 * D_MODEL) // (3 * 256)) * 256
ROPE_THETA = 10000.0
EPS = 1e-6
IN_SIZES = (2 * M_WIDTH, M_WIDTH, M_WIDTH, M_HEADS, M_HEADS,
            N_WIDTH, N_KV_WIDTH, N_KV_WIDTH, N_KV_WIDTH, N_KV_WIDTH, N_KV_WIDTH, N_KV_WIDTH, 3 * N_HEADS,
            C_WIDTH, D_MODEL, D_MODEL, D_MODEL)
D_IN = 4 * M_WIDTH + 2 * M_HEADS + N_WIDTH + 6 * N_KV_WIDTH + 3 * N_HEADS + C_WIDTH + 3 * D_MODEL

kernel_name = 'hybrid_mlstm_nsa_memory_block'


def rms_norm(x, g):
    xf = x.astype(jnp.float32)
    y = xf * lax.rsqrt(jnp.mean(xf * xf, axis=-1, keepdims=True) + EPS)
    return (y * g.astype(jnp.float32)).astype(x.dtype)


def rope(x, pos):
    hd = x.shape[-1]
    half = hd // 2
    inv = jnp.power(ROPE_THETA, -jnp.arange(half, dtype=jnp.float32) * 2.0 / hd)
    ang = pos.astype(jnp.float32)[:, None] * inv[None, :]
    cos = jnp.cos(ang)[:, None, :]
    sin = jnp.sin(ang)[:, None, :]
    xf = x.astype(jnp.float32)
    x1, x2 = xf[..., :half], xf[..., half:]
    return jnp.concatenate([x1 * cos - x2 * sin, x2 * cos + x1 * sin], axis=-1).astype(x.dtype)


def masked_softmax(s, mask):
    s = jnp.where(mask, s.astype(jnp.float32), -jnp.inf)
    m = jnp.max(s, axis=-1, keepdims=True)
    m = jnp.where(jnp.isfinite(m), m, 0.0)
    e = jnp.exp(s - m)
    d = jnp.sum(e, axis=-1, keepdims=True)
    return e / jnp.where(d > 0, d, 1.0)


def split_cols(z, sizes):
    offs = np.cumsum(np.array(sizes))[:-1].tolist()
    return jnp.split(z, offs, axis=-1)


def causal_dwconv(u, w):
    k_taps = w.shape[0]
    s_len = u.shape[1]
    up = jnp.pad(u, ((0, 0), (k_taps - 1, 0), (0, 0)))
    return sum(up[:, j:j + s_len] * w[j] for j in range(k_taps))


def mlstm_chunkwise(q, k, v, ig, lf):
    B, S, H, dk = q.shape
    dv = v.shape[-1]
    L = M_CHUNK
    nch = S // L

    def to_chunks(a):
        a = a.astype(jnp.float32).reshape((B, nch, L, H) + a.shape[3:])
        return jnp.moveaxis(a, (1, 3), (0, 2))

    xs = (to_chunks(q), to_chunks(k), to_chunks(v), to_chunks(ig), to_chunks(lf))
    tri = jnp.tril(jnp.ones((L, L), dtype=bool))

    def step(carry, inp):
        C, n, m = carry
        qc, kc, vc, ic, fc = inp
        b = jnp.cumsum(fc, axis=-1)
        log_d = jnp.where(tri, b[..., :, None] - b[..., None, :] + ic[..., None, :], -jnp.inf)
        log_inter = b + m[..., None]
        m_t = jnp.maximum(log_inter, jnp.max(log_d, axis=-1))
        w_intra = jnp.exp(log_d - m_t[..., None])
        w_inter = jnp.exp(log_inter - m_t)
        s = jnp.einsum('bhtd,bhsd->bhts', qc, kc) * w_intra
        num = jnp.einsum('bhts,bhsv->bhtv', s, vc) + w_inter[..., None] * jnp.einsum('bhtd,bhdv->bhtv', qc, C)
        den = jnp.sum(s, axis=-1) + w_inter * jnp.einsum('bhtd,bhd->bht', qc, n)
        h = num / jnp.maximum(jnp.abs(den), jnp.exp(-m_t))[..., None]
        b_last = b[..., -1]
        log_g = b_last[..., None] - b + ic
        m_new = jnp.maximum(b_last + m, jnp.max(log_g, axis=-1))
        decay = jnp.exp(b_last + m - m_new)
        wk = kc * jnp.exp(log_g - m_new[..., None])[..., None]
        C = decay[..., None, None] * C + jnp.einsum('bhsd,bhsv->bhdv', wk, vc)
        n = decay[..., None] * n + jnp.sum(wk, axis=2)
        return (C, n, m_new), h

    init = (jnp.zeros((B, H, dk, dv), jnp.float32), jnp.zeros((B, H, dk), jnp.float32),
            jnp.zeros((B, H), jnp.float32))
    _, hs = lax.scan(step, init, xs)
    return jnp.moveaxis(hs, (0, 2), (1, 3)).reshape(B, S, H, dv)


def nsa_attention(q, kc_raw, vc_raw, ks, vs, kw, vw, gates, gq, gkc, gks, gkw,
                  pe_k, w1_k, w2_k, pe_v, w1_v, w2_v):
    B, S = q.shape[:2]
    G, R, hd = N_KV_HEADS, N_HEADS // N_KV_HEADS, N_HEAD_DIM
    scale = hd ** -0.5
    pos = jnp.arange(S, dtype=jnp.int32)
    q = rope(rms_norm(q, gq), pos).reshape(B, S, G, R, hd).transpose(0, 2, 3, 1, 4)

    ns = S // CMP_STRIDE
    n_sub = CMP_LEN // CMP_STRIDE
    nc = ns - n_sub + 1

    def compress(u, pe, w1, w2):
        sub = u.reshape(B, ns, CMP_STRIDE, G, hd)
        blocks = jnp.concatenate([sub[:, j:nc + j] for j in range(n_sub)], axis=2)
        blocks = blocks + pe[None, None, :, None, :]
        flat = blocks.transpose(0, 1, 3, 2, 4).reshape(B, nc, G, CMP_LEN * hd)
        return jax.nn.silu(flat @ w1) @ w2

    cmp_end = jnp.arange(nc, dtype=jnp.int32) * CMP_STRIDE + CMP_LEN - 1
    kc = rope(rms_norm(compress(kc_raw, pe_k, w1_k, w2_k), gkc), cmp_end).transpose(0, 2, 1, 3)
    vc = compress(vc_raw, pe_v, w1_v, w2_v).transpose(0, 2, 1, 3)

    nsel = S // SEL_BLOCK
    top = min(SEL_TOP, nsel)
    ks_blk = rope(rms_norm(ks, gks), pos).transpose(0, 2, 1, 3).reshape(B, G, nsel, SEL_BLOCK, hd)
    vs_blk = vs.transpose(0, 2, 1, 3).reshape(B, G, nsel, SEL_BLOCK, hd)

    pad = ((0, 0), (0, 0), (WINDOW, 0), (0, 0))
    kw_pad = jnp.pad(rope(rms_norm(kw, gkw), pos).transpose(0, 2, 1, 3), pad)
    vw_pad = jnp.pad(vw.transpose(0, 2, 1, 3), pad)

    Q = N_QBLOCK
    blk_ids = jnp.arange(nsel, dtype=jnp.int32)
    bi = jnp.arange(B)[:, None, None, None]
    gi = jnp.arange(G)[None, :, None, None]
    lead = ((0, 0),) * 4

    def block(qb):
        t0 = qb * Q
        tpos = t0 + jnp.arange(Q, dtype=jnp.int32)
        qq = lax.dynamic_slice_in_dim(q, t0, Q, axis=3)
        gg = lax.dynamic_slice_in_dim(gates, t0, Q, axis=3)
        s_c = jnp.einsum('bgrqd,bgnd->bgrqn', qq, kc) * scale
        p_c = masked_softmax(s_c, cmp_end[None, :] <= tpos[:, None])
        o_c = jnp.einsum('bgrqn,bgnd->bgrqd', p_c.astype(vc.dtype), vc)
        p_pad = jnp.pad(p_c, lead + ((0, ns - nc),))
        p_sub = sum(jnp.pad(p_pad[..., :ns - j], lead + ((j, 0),)) for j in range(n_sub))
        imp = p_sub.reshape(B, G, R, Q, nsel, SEL_BLOCK // CMP_STRIDE).sum(axis=(2, 5))
        cur = tpos // SEL_BLOCK
        causal_b = blk_ids[None, :] <= cur[:, None]
        forced = (blk_ids[None, :] == 0) | (blk_ids[None, :] == cur[:, None]) | (blk_ids[None, :] == cur[:, None] - 1)
        score = jnp.where(forced, FORCE_SCORE, jnp.where(causal_b, imp, -1.0))
        _, idx = lax.top_k(score, top)
        k_sel = ks_blk[bi, gi, idx]
        v_sel = vs_blk[bi, gi, idx]
        s_s = jnp.einsum('bgrqd,bgqnkd->bgrqnk', qq, k_sel) * scale
        kpos = idx[..., None] * SEL_BLOCK + jnp.arange(SEL_BLOCK, dtype=jnp.int32)
        m_s = (kpos <= tpos[:, None, None])[:, :, None]
        p_s = masked_softmax(s_s.reshape(B, G, R, Q, top * SEL_BLOCK), m_s.reshape(B, G, 1, Q, top * SEL_BLOCK))
        o_s = jnp.einsum('bgrqm,bgqmd->bgrqd', p_s.astype(v_sel.dtype), v_sel.reshape(B, G, Q, top * SEL_BLOCK, hd))
        k_w = lax.dynamic_slice_in_dim(kw_pad, t0, WINDOW + Q, axis=2)
        v_w = lax.dynamic_slice_in_dim(vw_pad, t0, WINDOW + Q, axis=2)
        wpos = t0 - WINDOW + jnp.arange(WINDOW + Q, dtype=jnp.int32)
        m_w = (wpos[None, :] <= tpos[:, None]) & (wpos[None, :] > tpos[:, None] - WINDOW) & (wpos[None, :] >= 0)
        s_w = jnp.einsum('bgrqd,bgkd->bgrqk', qq, k_w) * scale
        p_w = masked_softmax(s_w, m_w)
        o_w = jnp.einsum('bgrqk,bgkd->bgrqd', p_w.astype(v_w.dtype), v_w)
        return gg[..., 0:1] * o_c + gg[..., 1:2] * o_s + gg[..., 2:3] * o_w

    out = lax.map(block, jnp.arange(S // Q))
    return out.transpose(1, 0, 4, 2, 3, 5).reshape(B, S, N_HEADS * hd)


def memory_cross_attention(c_q, mem_n, w_mem_k, w_mem_v, gq, gk):
    B, S = c_q.shape[:2]
    M = mem_n.shape[1]
    q = rms_norm(c_q.reshape(B, S, C_HEADS, C_HEAD_DIM), gq)
    k = rms_norm((mem_n @ w_mem_k).reshape(B, M, C_HEADS, C_HEAD_DIM), gk)
    v = (mem_n @ w_mem_v).reshape(B, M, C_HEADS, C_HEAD_DIM)
    s = jnp.einsum('bshd,bmhd->bhsm', q, k).astype(jnp.float32) * (C_HEAD_DIM ** -0.5)
    p = jax.nn.softmax(s, axis=-1).astype(v.dtype)
    return jnp.einsum('bhsm,bmhd->bshd', p, v).reshape(B, S, C_WIDTH)


def hybrid_layer(x, mem, norm_mix_g, norm_mem_g, norm_ffn_g, w_in, m_conv_w, m_i_bias, m_f_bias, m_norm_g,
                 n_q_norm_g, n_kc_norm_g, n_ks_norm_g, n_kw_norm_g, n_cmp_pe_k, n_cmp_w1_k, n_cmp_w2_k,
                 n_cmp_pe_v, n_cmp_w1_v, n_cmp_w2_v, c_q_norm_g, c_k_norm_g, w_mem_k, w_mem_v,
                 w_up_a, w_up_b, w_up_c, w_out, w_ffn_gate, w_ffn_up, w_ffn_down):
    B, S, _ = x.shape
    h = rms_norm(x, norm_mix_g)
    z = h @ w_in
    (m_qk, m_v, m_o, m_i, m_f, n_q, n_kc, n_vc, n_ks, n_vs, n_kw, n_vw, n_g,
     c_q, g_a, g_b, g_c) = split_cols(z, IN_SIZES)

    qk = jax.nn.silu(causal_dwconv(m_qk, m_conv_w))
    q_m = qk[..., :M_WIDTH].reshape(B, S, M_HEADS, M_HEAD_DIM)
    k_m = qk[..., M_WIDTH:].reshape(B, S, M_HEADS, M_HEAD_DIM) * (M_HEAD_DIM ** -0.5)
    v_m = m_v.reshape(B, S, M_HEADS, M_HEAD_DIM)
    ig = m_i.astype(jnp.float32) + m_i_bias.astype(jnp.float32)
    lf = jax.nn.log_sigmoid(m_f.astype(jnp.float32) + m_f_bias.astype(jnp.float32))
    h_m = mlstm_chunkwise(q_m, k_m, v_m, ig, lf).astype(x.dtype)
    y_a = jax.nn.sigmoid(m_o) * rms_norm(h_m, m_norm_g).reshape(B, S, M_WIDTH)

    G, R = N_KV_HEADS, N_HEADS // N_KV_HEADS
    kv = lambda u: u.reshape(B, S, G, N_HEAD_DIM)
    gates = jax.nn.sigmoid(n_g).reshape(B, S, G, R, 3).transpose(0, 2, 3, 1, 4)
    y_b = nsa_attention(n_q.reshape(B, S, N_HEADS, N_HEAD_DIM), kv(n_kc), kv(n_vc), kv(n_ks), kv(n_vs),
                        kv(n_kw), kv(n_vw), gates, n_q_norm_g, n_kc_norm_g, n_ks_norm_g, n_kw_norm_g,
                        n_cmp_pe_k, n_cmp_w1_k, n_cmp_w2_k, n_cmp_pe_v, n_cmp_w1_v, n_cmp_w2_v)

    y_c = memory_cross_attention(c_q, rms_norm(mem, norm_mem_g), w_mem_k, w_mem_v, c_q_norm_g, c_k_norm_g)

    mix = (jax.nn.sigmoid(g_a) * (y_a @ w_up_a) + jax.nn.sigmoid(g_b) * (y_b @ w_up_b)
           + jax.nn.sigmoid(g_c) * (y_c @ w_up_c))
    x = x + mix @ w_out

    h2 = rms_norm(x, norm_ffn_g)
    return x + (jax.nn.silu(h2 @ w_ffn_gate) * (h2 @ w_ffn_up)) @ w_ffn_down


def setup_inputs(seed: int = 0) -> dict:
    key = jax.random.key(seed)
    k = jax.random.split(key, 32)
    L, D = DEPTH, D_MODEL
    f32 = jnp.float32

    def dense(kk, shape, fan_in):
        return jax.random.normal(kk, shape, f32) * (fan_in ** -0.5)

    def gain(kk, shape):
        return 1.0 + 0.02 * jax.random.normal(kk, shape, f32)

    def small(kk, shape, s):
        return s * jax.random.normal(kk, shape, f32)

    cin = CMP_LEN * N_HEAD_DIM
    return {
        'x': jax.random.normal(k[0], (BATCH, SEQ, D), f32),
        'mem': jax.random.normal(k[1], (BATCH, MEM_LEN, D), f32),
        'norm_mix_g': gain(k[2], (L, D)),
        'norm_mem_g': gain(k[3], (L, D)),
        'norm_ffn_g': gain(k[4], (L, D)),
        'w_in': dense(k[5], (L, D, D_IN), D),
        'm_conv_w': dense(k[6], (L, M_CONV, 2 * M_WIDTH), M_CONV),
        'm_i_bias': small(k[7], (L, M_HEADS), 0.1),
        'm_f_bias': jnp.linspace(3.0, 6.0, M_HEADS, dtype=f32)[None, :] + small(k[8], (L, M_HEADS), 0.1),
        'm_norm_g': gain(k[9], (L, M_HEADS, M_HEAD_DIM)),
        'n_q_norm_g': gain(k[10], (L, N_HEAD_DIM)),
        'n_kc_norm_g': gain(k[11], (L, N_HEAD_DIM)),
        'n_ks_norm_g': gain(k[12], (L, N_HEAD_DIM)),
        'n_kw_norm_g': gain(k[13], (L, N_HEAD_DIM)),
        'n_cmp_pe_k': small(k[14], (L, CMP_LEN, N_HEAD_DIM), 0.1),
        'n_cmp_w1_k': dense(k[15], (L, cin, CMP_HIDDEN), cin),
        'n_cmp_w2_k': dense(k[16], (L, CMP_HIDDEN, N_HEAD_DIM), CMP_HIDDEN),
        'n_cmp_pe_v': small(k[17], (L, CMP_LEN, N_HEAD_DIM), 0.1),
        'n_cmp_w1_v': dense(k[18], (L, cin, CMP_HIDDEN), cin),
        'n_cmp_w2_v': dense(k[19], (L, CMP_HIDDEN, N_HEAD_DIM), CMP_HIDDEN),
        'c_q_norm_g': gain(k[20], (L, C_HEAD_DIM)),
        'c_k_norm_g': gain(k[21], (L, C_HEAD_DIM)),
        'w_mem_k': dense(k[22], (L, D, C_WIDTH), D),
        'w_mem_v': dense(k[23], (L, D, C_WIDTH), D),
        'w_up_a': dense(k[24], (L, M_WIDTH, D), M_WIDTH),
        'w_up_b': dense(k[25], (L, N_WIDTH, D), N_WIDTH),
        'w_up_c': dense(k[26], (L, C_WIDTH, D), C_WIDTH),
        'w_out': dense(k[27], (L, D, D), D),
        'w_ffn_gate': dense(k[28], (L, D, D_FF), D),
        'w_ffn_up': dense(k[29], (L, D, D_FF), D),
        'w_ffn_down': dense(k[30], (L, D_FF, D), D_FF),
    }


def reference(x, mem, norm_mix_g, norm_mem_g, norm_ffn_g, w_in, m_conv_w, m_i_bias, m_f_bias, m_norm_g,
              n_q_norm_g, n_kc_norm_g, n_ks_norm_g, n_kw_norm_g, n_cmp_pe_k, n_cmp_w1_k, n_cmp_w2_k,
              n_cmp_pe_v, n_cmp_w1_v, n_cmp_w2_v, c_q_norm_g, c_k_norm_g, w_mem_k, w_mem_v,
              w_up_a, w_up_b, w_up_c, w_out, w_ffn_gate, w_ffn_up, w_ffn_down):
    for l in range(DEPTH):
        x = hybrid_layer(x, mem, norm_mix_g[l], norm_mem_g[l], norm_ffn_g[l], w_in[l], m_conv_w[l],
                         m_i_bias[l], m_f_bias[l], m_norm_g[l], n_q_norm_g[l], n_kc_norm_g[l],
                         n_ks_norm_g[l], n_kw_norm_g[l], n_cmp_pe_k[l], n_cmp_w1_k[l], n_cmp_w2_k[l],
                         n_cmp_pe_v[l], n_cmp_w1_v[l], n_cmp_w2_v[l], c_q_norm_g[l], c_k_norm_g[l],
                         w_mem_k[l], w_mem_v[l], w_up_a[l], w_up_b[l], w_up_c[l], w_out[l],
                         w_ffn_gate[l], w_ffn_up[l], w_ffn_down[l])
    return x
```

```python
import functools

import numpy as np
import jax
import jax.numpy as jnp
from jax import lax
from jax.experimental import pallas as pl
from jax.experimental.pallas import tpu as pltpu

F32 = jnp.float32
BF16 = jnp.bfloat16

EPS = 1e-6
ROPE_THETA = 10000.0
M_HEADS = 4
M_HEAD_DIM = 256
M_CONV = 4
N_HEADS = 16
N_GROUPS = 4
N_REP = N_HEADS // N_GROUPS
N_HEAD_DIM = 64
CMP_LEN = 32
CMP_STRIDE = 16
CMP_HIDDEN = 128
SEL_BLOCK = 64
SEL_TOP = 16
WINDOW = 512
FORCE_SCORE = 1.0e4
C_HEADS = 4
C_HEAD_DIM = 256

NEG_BIG = -1.0e30
BLOCK_BIAS = -1.0e9
VMEM_LIMIT = 56 * 1024 * 1024

MLSTM_CHUNK = 256
NSA_TQ = 128
NSA_TK = 512


def _params(*sem):
    return pltpu.CompilerParams(dimension_semantics=sem, vmem_limit_bytes=VMEM_LIMIT)


def _sigmoid(x):
    return 1.0 / (1.0 + jnp.exp(-x))


def _silu(x):
    return x * _sigmoid(x)


def _log_sigmoid(x):
    return jnp.minimum(x, 0.0) - jnp.log1p(jnp.exp(-jnp.abs(x)))


def _rmsnorm_kernel(x_ref, g_ref, o_ref):
    x = x_ref[...]
    y = x * lax.rsqrt(jnp.mean(x * x, axis=-1, keepdims=True) + EPS)
    o_ref[...] = (y * g_ref[...]).astype(o_ref.dtype)


def _rmsnorm(x2d, g, tm=512):
    m, d = x2d.shape
    tm = min(tm, m)
    return pl.pallas_call(
        _rmsnorm_kernel,
        grid=(m // tm,),
        in_specs=[pl.BlockSpec((tm, d), lambda i: (i, 0)), pl.BlockSpec((1, d), lambda i: (0, 0))],
        out_specs=pl.BlockSpec((tm, d), lambda i: (i, 0)),
        out_shape=jax.ShapeDtypeStruct((m, d), BF16),
        compiler_params=_params("parallel"),
        name="rmsnorm",
    )(x2d, g.reshape(1, d))


def _mm_kernel(a_ref, w_ref, o_ref):
    o_ref[...] = jnp.dot(a_ref[...], w_ref[...], preferred_element_type=F32).astype(o_ref.dtype)


def _mm(a, w, out_dtype, tm=1024, tn=512):
    m, k = a.shape
    n = w.shape[1]
    tm, tn = min(tm, m), min(tn, n)
    return pl.pallas_call(
        _mm_kernel,
        grid=(m // tm, n // tn),
        in_specs=[pl.BlockSpec((tm, k), lambda i, j: (i, 0)), pl.BlockSpec((k, tn), lambda i, j: (0, j))],
        out_specs=pl.BlockSpec((tm, tn), lambda i, j: (i, j)),
        out_shape=jax.ShapeDtypeStruct((m, n), out_dtype),
        compiler_params=_params("parallel", "arbitrary"),
        name="matmul",
    )(a, w)


def _mm_res_kernel(a_ref, w_ref, x_ref, o_ref):
    o_ref[...] = x_ref[...] + jnp.dot(a_ref[...], w_ref[...], preferred_element_type=F32)


def _mm_residual(a, w, x, tm=512, tn=512):
    m, k = a.shape
    n = w.shape[1]
    tm, tn = min(tm, m), min(tn, n)
    return pl.pallas_call(
        _mm_res_kernel,
        grid=(m // tm, n // tn),
        in_specs=[pl.BlockSpec((tm, k), lambda i, j: (i, 0)), pl.BlockSpec((k, tn), lambda i, j: (0, j)),
                  pl.BlockSpec((tm, tn), lambda i, j: (i, j))],
        out_specs=pl.BlockSpec((tm, tn), lambda i, j: (i, j)),
        out_shape=jax.ShapeDtypeStruct((m, n), F32),
        compiler_params=_params("parallel", "arbitrary"),
        name="matmul_residual",
    )(a, w, x)


def _swiglu_kernel(h_ref, wg_ref, wu_ref, o_ref):
    h = h_ref[...]
    g = jnp.dot(h, wg_ref[...], preferred_element_type=F32)
    u = jnp.dot(h, wu_ref[...], preferred_element_type=F32)
    o_ref[...] = (_silu(g) * u).astype(o_ref.dtype)


def _swiglu(h, wg, wu, tm=1024, tn=512):
    m, k = h.shape
    n = wg.shape[1]
    tm, tn = min(tm, m), min(tn, n)
    return pl.pallas_call(
        _swiglu_kernel,
        grid=(m // tm, n // tn),
        in_specs=[pl.BlockSpec((tm, k), lambda i, j: (i, 0)), pl.BlockSpec((k, tn), lambda i, j: (0, j)),
                  pl.BlockSpec((k, tn), lambda i, j: (0, j))],
        out_specs=pl.BlockSpec((tm, tn), lambda i, j: (i, j)),
        out_shape=jax.ShapeDtypeStruct((m, n), BF16),
        compiler_params=_params("parallel", "arbitrary"),
        name="swiglu",
    )(h, wg, wu)


def _mix_kernel(ya_ref, yb_ref, yc_ref, wa_ref, wb_ref, wc_ref, ga_ref, gb_ref, gc_ref, o_ref):
    def branch(y_ref, w_ref, g_ref):
        return _sigmoid(g_ref[...].astype(F32)) * jnp.dot(y_ref[...], w_ref[...], preferred_element_type=F32)

    o_ref[...] = (branch(ya_ref, wa_ref, ga_ref) + branch(yb_ref, wb_ref, gb_ref)
                  + branch(yc_ref, wc_ref, gc_ref)).astype(o_ref.dtype)


def _mix(ya, yb, yc, wa, wb, wc, z, gate_col0, tm=1024, tn=512):
    m, k = ya.shape
    n = wa.shape[1]
    tm, tn = min(tm, m), min(tn, n)
    gofs = [(gate_col0 + b * n) // tn for b in range(3)]
    y_spec = pl.BlockSpec((tm, k), lambda i, j: (i, 0))
    w_spec = pl.BlockSpec((k, tn), lambda i, j: (0, j))
    g_specs = [pl.BlockSpec((tm, tn), functools.partial(lambda i, j, o: (i, o + j), o=o)) for o in gofs]
    return pl.pallas_call(
        _mix_kernel,
        grid=(m // tm, n // tn),
        in_specs=[y_spec, y_spec, y_spec, w_spec, w_spec, w_spec] + g_specs,
        out_specs=pl.BlockSpec((tm, tn), lambda i, j: (i, j)),
        out_shape=jax.ShapeDtypeStruct((m, n), BF16),
        compiler_params=_params("parallel", "arbitrary"),
        name="gated_mix",
    )(ya, yb, yc, wa, wb, wc, z, z, z)


def _mlstm_kernel(uq_ref, uk_ref, v_ref, o_ref, gcol_ref, grow_ref, bias_ref, cwq_ref, cwk_ref, ng_ref,
                  y_ref, c_scr, n_scr, m_scr, qbuf, kbuf):
    c = pl.program_id(2)
    L = uq_ref.shape[1]

    @pl.when(c == 0)
    def _():
        c_scr[...] = jnp.zeros_like(c_scr)
        n_scr[...] = jnp.zeros_like(n_scr)
        m_scr[...] = jnp.zeros_like(m_scr)
        qbuf[L:L + 8, :] = jnp.zeros((8, qbuf.shape[1]), F32)
        kbuf[L:L + 8, :] = jnp.zeros((8, kbuf.shape[1]), F32)

    def conv_silu(u_ref, buf, cw_ref):
        buf[0:8, :] = buf[L:L + 8, :]
        buf[8:L + 8, :] = u_ref[0].astype(F32)
        cw = cw_ref[...]
        y = cw[0:1, :] * buf[5:5 + L, :]
        for j in range(1, M_CONV):
            y = y + cw[j:j + 1, :] * buf[5 + j:5 + j + L, :]
        return _silu(y)

    q = conv_silu(uq_ref, qbuf, cwq_ref)
    k = conv_silu(uk_ref, kbuf, cwk_ref) * (M_HEAD_DIM ** -0.5)
    qb = q.astype(BF16)
    kb = k.astype(BF16)
    vb = v_ref[0]

    gcol = gcol_ref[0, 0]
    grow = grow_ref[0, 0]
    bias = bias_ref[0]
    ic = gcol[:, 0:1] + bias[:, 0:1]
    fc = _log_sigmoid(gcol[:, 1:2] + bias[:, 1:2])
    ir = grow[0:1, :] + bias[:, 0:1]
    fr = _log_sigmoid(grow[1:2, :] + bias[:, 1:2])

    row = lax.broadcasted_iota(jnp.int32, (L, L), 0)
    col = lax.broadcasted_iota(jnp.int32, (L, L), 1)
    tri = col <= row
    b_col = jnp.sum(jnp.where(tri, fr, 0.0), axis=1, keepdims=True)
    b_row = jnp.sum(jnp.where(row <= col, fc, 0.0), axis=0, keepdims=True)
    m_prev = m_scr[0:1, 0:1]

    log_d = jnp.where(tri, b_col - b_row + ir, NEG_BIG)
    log_inter = b_col + m_prev
    m_t = jnp.maximum(log_inter, jnp.max(log_d, axis=1, keepdims=True))
    w_intra = jnp.exp(log_d - m_t)
    w_inter = jnp.exp(log_inter - m_t)

    s = lax.dot_general(qb, kb, (((1,), (1,)), ((), ())), preferred_element_type=F32) * w_intra
    c_state = c_scr[...]
    n_state = n_scr[0:1, :]
    num = (jnp.dot(s.astype(BF16), vb, preferred_element_type=F32)
           + w_inter * jnp.dot(qb, c_state.astype(BF16), preferred_element_type=F32))
    den = jnp.sum(s, axis=1, keepdims=True) + w_inter * jnp.sum(q * n_state, axis=1, keepdims=True)
    h = num / jnp.maximum(jnp.abs(den), jnp.exp(-m_t))

    b_last = jnp.sum(fc, axis=0, keepdims=True)
    log_g = b_last - b_col + ic
    m_new = jnp.maximum(b_last + m_prev, jnp.max(log_g, axis=0, keepdims=True))
    decay = jnp.exp(b_last + m_prev - m_new)
    wk = k * jnp.exp(log_g - m_new)
    c_scr[...] = decay * c_state + lax.dot_general(wk.astype(BF16), vb, (((0,), (0,)), ((), ())),
                                                   preferred_element_type=F32)
    n_scr[0:1, :] = decay * n_state + jnp.sum(wk, axis=0, keepdims=True)
    m_scr[...] = jnp.broadcast_to(m_new, m_scr.shape)

    hn = h * lax.rsqrt(jnp.mean(h * h, axis=-1, keepdims=True) + EPS) * ng_ref[0]
    y_ref[0] = (_sigmoid(o_ref[0].astype(F32)) * hn).astype(y_ref.dtype)


def _mlstm(z, gcol, grow, gbias, conv_w, norm_g, batch, seq):
    L = min(MLSTM_CHUNK, seq)
    hd = M_HEAD_DIM
    nh = M_HEADS

    def zspec(blk0):
        return pl.BlockSpec((1, L, hd), functools.partial(lambda b, h, c, o: (b, c, o + h), o=blk0))

    return pl.pallas_call(
        _mlstm_kernel,
        grid=(batch, nh, seq // L),
        in_specs=[zspec(0), zspec(nh), zspec(2 * nh), zspec(3 * nh),
                  pl.BlockSpec((1, 1, L, 2), lambda b, h, c: (b, h, c, 0)),
                  pl.BlockSpec((1, 1, 2, L), lambda b, h, c: (b, h, 0, c)),
                  pl.BlockSpec((1, 1, 2), lambda b, h, c: (h, 0, 0)),
                  pl.BlockSpec((M_CONV, hd), lambda b, h, c: (0, h)),
                  pl.BlockSpec((M_CONV, hd), lambda b, h, c: (0, nh + h)),
                  pl.BlockSpec((1, 1, hd), lambda b, h, c: (h, 0, 0))],
        out_specs=pl.BlockSpec((1, L, hd), lambda b, h, c: (b, c, h)),
        out_shape=jax.ShapeDtypeStruct((batch, seq, nh * hd), BF16),
        scratch_shapes=[pltpu.VMEM((hd, hd), F32), pltpu.VMEM((8, hd), F32), pltpu.VMEM((8, 128), F32),
                        pltpu.VMEM((L + 8, hd), F32), pltpu.VMEM((L + 8, hd), F32)],
        compiler_params=_params("parallel", "parallel", "arbitrary"),
        name="mlstm",
    )(z, z, z, z, gcol, grow, gbias, conv_w, conv_w, norm_g.reshape(nh, 1, hd))


def _xattn_kernel(q_ref, k_ref, v_ref, gq_ref, gk_ref, o_ref):
    hd = C_HEAD_DIM
    for h in range(C_HEADS):
        sl = slice(h * hd, (h + 1) * hd)
        q = q_ref[0, :, sl].astype(F32)
        q = q * lax.rsqrt(jnp.mean(q * q, axis=-1, keepdims=True) + EPS) * gq_ref[...] * (hd ** -0.5)
        k = k_ref[0, :, sl].astype(F32)
        k = k * lax.rsqrt(jnp.mean(k * k, axis=-1, keepdims=True) + EPS) * gk_ref[...]
        s = lax.dot_general(q.astype(BF16), k.astype(BF16), (((1,), (1,)), ((), ())), preferred_element_type=F32)
        e = jnp.exp(s - jnp.max(s, axis=-1, keepdims=True))
        p = e / jnp.sum(e, axis=-1, keepdims=True)
        o = jnp.dot(p.astype(BF16), v_ref[0, :, sl], preferred_element_type=F32)
        o_ref[0, :, sl] = o.astype(o_ref.dtype)


def _xattn(z, q_blk, k, v, gq, gk, batch, seq, tq=512):
    tq = min(tq, seq)
    mlen, w = k.shape[1], k.shape[2]
    return pl.pallas_call(
        _xattn_kernel,
        grid=(batch, seq // tq),
        in_specs=[pl.BlockSpec((1, tq, w), lambda b, i: (b, i, q_blk)),
                  pl.BlockSpec((1, mlen, w), lambda b, i: (b, 0, 0)),
                  pl.BlockSpec((1, mlen, w), lambda b, i: (b, 0, 0)),
                  pl.BlockSpec((1, C_HEAD_DIM), lambda b, i: (0, 0)),
                  pl.BlockSpec((1, C_HEAD_DIM), lambda b, i: (0, 0))],
        out_specs=pl.BlockSpec((1, tq, w), lambda b, i: (b, i, 0)),
        out_shape=jax.ShapeDtypeStruct((batch, seq, w), BF16),
        compiler_params=_params("parallel", "arbitrary"),
        name="memory_cross_attention",
    )(z, k, v, gq.reshape(1, -1), gk.reshape(1, -1))


def _rope_rows(xn, cos, sin_signed):
    half = N_HEAD_DIM // 2
    rot = jnp.concatenate([xn[:, half:], xn[:, :half]], axis=-1)
    return xn * cos + rot * sin_signed


def _knorm_rope_kernel(x_ref, g_ref, cos_ref, sin_ref, o_ref, *, n_blocks):
    x = x_ref[0].astype(F32)
    xn = x * lax.rsqrt(jnp.mean(x * x, axis=-1, keepdims=True) + EPS) * g_ref[...]
    k = _rope_rows(xn, cos_ref[...], sin_ref[...]).astype(o_ref.dtype)
    if n_blocks == 0:
        o_ref[0] = k
    else:
        ts = x.shape[0]
        pos = pl.program_id(1) * ts + lax.broadcasted_iota(jnp.int32, (ts, n_blocks), 0)
        blk = lax.broadcasted_iota(jnp.int32, (ts, n_blocks), 1)
        o_ref[0, :, 0:n_blocks] = jnp.where(pos // SEL_BLOCK == blk, 1.0, 0.0).astype(o_ref.dtype)
        o_ref[0, :, n_blocks:n_blocks + N_HEAD_DIM] = k


def _knorm_rope(x, g, cos, sin_signed, n_blocks, ts=1024):
    n, s, hd = x.shape
    ts = min(ts, s)
    return pl.pallas_call(
        functools.partial(_knorm_rope_kernel, n_blocks=n_blocks),
        grid=(n, s // ts),
        in_specs=[pl.BlockSpec((1, ts, hd), lambda i, j: (i, j, 0)), pl.BlockSpec((1, hd), lambda i, j: (0, 0)),
                  pl.BlockSpec((ts, hd), lambda i, j: (j, 0)), pl.BlockSpec((ts, hd), lambda i, j: (j, 0))],
        out_specs=pl.BlockSpec((1, ts, n_blocks + hd), lambda i, j: (i, j, 0)),
        out_shape=jax.ShapeDtypeStruct((n, s, n_blocks + hd), BF16),
        compiler_params=_params("parallel", "arbitrary"),
        name="key_norm_rope",
    )(x, g.reshape(1, hd), cos, sin_signed)


def _compress_kernel(sub_ref, pe_ref, w1_ref, w2_ref, g_ref, cos_ref, sin_ref, o_ref, *, is_key):
    ns = sub_ref.shape[1]
    sub = sub_ref[0].astype(F32)
    lo = jnp.dot((sub + pe_ref[0:1, :]).astype(BF16), w1_ref[0], preferred_element_type=F32)
    hi = jnp.dot((sub + pe_ref[1:2, :]).astype(BF16), w1_ref[1], preferred_element_type=F32)
    hid = _silu(lo + pltpu.roll(hi, shift=ns - 1, axis=0))
    out = jnp.dot(hid.astype(BF16), w2_ref[...], preferred_element_type=F32)
    if is_key:
        out = out * lax.rsqrt(jnp.mean(out * out, axis=-1, keepdims=True) + EPS) * g_ref[...]
        out = _rope_rows(out, cos_ref[...], sin_ref[...])
    o_ref[0] = out.astype(o_ref.dtype)


def _compress(u, pe, w1, w2, g, cos, sin_signed, is_key):
    n, s, hd = u.shape
    ns = s // CMP_STRIDE
    width = CMP_STRIDE * hd
    sub = u.reshape(n, ns, width)
    pe2 = pe.reshape(CMP_LEN // CMP_STRIDE, width)
    w1s = w1.reshape(CMP_LEN // CMP_STRIDE, width, CMP_HIDDEN).astype(BF16)
    return pl.pallas_call(
        functools.partial(_compress_kernel, is_key=is_key),
        grid=(n,),
        in_specs=[pl.BlockSpec((1, ns, width), lambda i: (i, 0, 0)),
                  pl.BlockSpec(pe2.shape, lambda i: (0, 0)),
                  pl.BlockSpec(w1s.shape, lambda i: (0, 0, 0)),
                  pl.BlockSpec(w2.shape, lambda i: (0, 0)),
                  pl.BlockSpec((1, hd), lambda i: (0, 0)),
                  pl.BlockSpec((ns, hd), lambda i: (0, 0)),
                  pl.BlockSpec((ns, hd), lambda i: (0, 0))],
        out_specs=pl.BlockSpec((1, ns, hd), lambda i: (i, 0, 0)),
        out_shape=jax.ShapeDtypeStruct((n, ns, hd), BF16),
        compiler_params=_params("parallel"),
        name="compress_key" if is_key else "compress_value",
    )(sub, pe2, w1s, w2.astype(BF16), g.reshape(1, hd), cos, sin_signed)


def _softmax_cols(s, valid):
    m = jnp.max(s, axis=0, keepdims=True)
    e = jnp.where(valid, jnp.exp(s - m), 0.0)
    d = jnp.sum(e, axis=0, keepdims=True)
    return e / jnp.where(d > 0.0, d, 1.0)


def _nsa_kernel(q_ref, gate_ref, kc_ref, vct_ref, ks_ref, vst_ref, kw_ref, vwt_ref, gq_ref, cos_ref, sin_ref,
                wimp_ref, y_ref, *, seq):
    tq = q_ref.shape[4]
    hd = N_HEAD_DIM
    nb = seq // SEL_BLOCK
    top = min(SEL_TOP, nb)
    ncol = N_REP * tq
    t0 = pl.program_id(2) * tq

    def lanes(parts):
        return jnp.concatenate(parts, axis=1)

    q = lanes([q_ref[0, 0, r].astype(F32) for r in range(N_REP)])
    qn = q * lax.rsqrt(jnp.mean(q * q, axis=0, keepdims=True) + EPS) * gq_ref[...]
    cos = lanes([cos_ref[...]] * N_REP)
    sin = lanes([sin_ref[...]] * N_REP)
    rot = jnp.concatenate([qn[hd // 2:], qn[:hd // 2]], axis=0)
    qr = ((qn * cos + rot * sin) * (hd ** -0.5)).astype(BF16)
    tcol = t0 + lax.broadcasted_iota(jnp.int32, (1, ncol), 1) % tq

    nc = kc_ref.shape[2]
    s = jnp.dot(kc_ref[0, 0], qr, preferred_element_type=F32)
    cmp_end = lax.broadcasted_iota(jnp.int32, (nc, 1), 0) * CMP_STRIDE + (CMP_LEN - 1)
    valid = cmp_end <= tcol
    p = _softmax_cols(jnp.where(valid, s, NEG_BIG), valid)
    o_c = jnp.dot(vct_ref[0, 0], p.astype(BF16), preferred_element_type=F32)

    psum = p[:, 0:tq]
    for r in range(1, N_REP):
        psum = psum + p[:, r * tq:(r + 1) * tq]
    p1 = psum.astype(BF16)
    rem = psum - p1.astype(F32)
    p2 = rem.astype(BF16)
    p3 = (rem - p2.astype(F32)).astype(BF16)
    wimp = wimp_ref[...]
    imp = (jnp.dot(wimp, p1, preferred_element_type=F32) + jnp.dot(wimp, p2, preferred_element_type=F32)
           + jnp.dot(wimp, p3, preferred_element_type=F32))

    blk = lax.broadcasted_iota(jnp.int32, (nb, tq), 0)
    cur = (t0 + lax.broadcasted_iota(jnp.int32, (nb, tq), 1)) // SEL_BLOCK
    causal_b = blk <= cur
    forced = (blk == 0) | (blk == cur) | (blk == cur - 1)
    score = jnp.where(forced, FORCE_SCORE, jnp.where(causal_b, imp, -1.0))
    bias = jnp.full((nb, tq), BLOCK_BIAS, F32)
    for _ in range(top):
        mx = jnp.max(score, axis=0, keepdims=True)
        idx = jnp.min(jnp.where(score == mx, blk, nb), axis=0, keepdims=True)
        pick = blk == idx
        bias = jnp.where(pick, 0.0, bias)
        score = jnp.where(pick, -3.0e38, score)
    bias = jnp.where(causal_b, bias, BLOCK_BIAS)
    q_aug = jnp.concatenate([lanes([bias.astype(BF16)] * N_REP), qr], axis=0)

    tk = min(NSA_TK, seq)
    n_tiles = (t0 + tq + tk - 1) // tk

    def sel_body(kt, carry):
        m, l, acc = carry
        ks0 = pl.multiple_of(kt * tk, tk)
        s = jnp.dot(ks_ref[0, 0, pl.ds(ks0, tk), :], q_aug, preferred_element_type=F32)
        kpos = ks0 + lax.broadcasted_iota(jnp.int32, (tk, 1), 0)
        s = jnp.where(kpos <= tcol, s, NEG_BIG)
        m_new = jnp.maximum(m, jnp.max(s, axis=0, keepdims=True))
        alpha = jnp.exp(m - m_new)
        e = jnp.exp(s - m_new)
        l = alpha * l + jnp.sum(e, axis=0, keepdims=True)
        acc = alpha * acc + jnp.dot(vst_ref[0, 0, :, pl.ds(ks0, tk)], e.astype(BF16), preferred_element_type=F32)
        return m_new, l, acc

    m, l, acc = lax.fori_loop(0, n_tiles, sel_body,
                              (jnp.full((1, ncol), NEG_BIG, F32), jnp.zeros((1, ncol), F32), jnp.zeros((hd, ncol), F32)))
    o_s = acc / l

    span = WINDOW + tq
    ws0 = pl.multiple_of(jnp.maximum(t0 - WINDOW, 0), tq)
    s = jnp.dot(kw_ref[0, 0, pl.ds(ws0, span), :], qr, preferred_element_type=F32)
    kpos = ws0 + lax.broadcasted_iota(jnp.int32, (span, 1), 0)
    valid = (kpos <= tcol) & (kpos > tcol - WINDOW)
    p = _softmax_cols(jnp.where(valid, s, NEG_BIG), valid)
    o_w = jnp.dot(vwt_ref[0, 0, :, pl.ds(ws0, span)], p.astype(BF16), preferred_element_type=F32)

    g = _sigmoid(gate_ref[0, 0])
    gc, gs, gw = (lanes([g[j, r:r + 1, :] for r in range(N_REP)]) for j in range(3))
    out = gc * o_c + gs * o_s + gw * o_w
    for r in range(N_REP):
        y_ref[0, 0, r] = out[:, r * tq:(r + 1) * tq].astype(y_ref.dtype)


def _nsa(qt, gates, kc, vct, ks_aug, vst, kw, vwt, gq, cos_t, sin_t, wimp_t, seq):
    b, g = qt.shape[:2]
    hd = N_HEAD_DIM
    tq = min(NSA_TQ, seq)
    nc = kc.shape[2]
    nb = seq // SEL_BLOCK
    full = lambda *shape: pl.BlockSpec((1, 1) + shape, lambda bi, gi, qi: (bi, gi) + (0,) * len(shape))
    return pl.pallas_call(
        functools.partial(_nsa_kernel, seq=seq),
        grid=(b, g, seq // tq),
        in_specs=[pl.BlockSpec((1, 1, N_REP, hd, tq), lambda bi, gi, qi: (bi, gi, 0, 0, qi)),
                  pl.BlockSpec((1, 1, 3, N_REP, tq), lambda bi, gi, qi: (bi, gi, 0, 0, qi)),
                  full(nc, hd), full(hd, nc), full(seq, nb + hd), full(hd, seq), full(seq, hd), full(hd, seq),
                  pl.BlockSpec((hd, 1), lambda bi, gi, qi: (0, 0)),
                  pl.BlockSpec((hd, tq), lambda bi, gi, qi: (0, qi)),
                  pl.BlockSpec((hd, tq), lambda bi, gi, qi: (0, qi)),
                  pl.BlockSpec((nb, nc), lambda bi, gi, qi: (0, 0))],
        out_specs=pl.BlockSpec((1, 1, N_REP, hd, tq), lambda bi, gi, qi: (bi, gi, 0, 0, qi)),
        out_shape=jax.ShapeDtypeStruct((b, g, N_REP, hd, seq), BF16),
        compiler_params=_params("parallel", "parallel", "arbitrary"),
        name="nsa_attention",
    )(qt, gates, kc, vct, ks_aug, vst, kw, vwt, gq.reshape(hd, 1), cos_t, sin_t, wimp_t)


def _rope_tables(pos):
    half = N_HEAD_DIM // 2
    inv = jnp.power(ROPE_THETA, -jnp.arange(half, dtype=F32) * 2.0 / N_HEAD_DIM)
    ang = pos.astype(F32)[:, None] * inv[None, :]
    cos, sin = jnp.cos(ang), jnp.sin(ang)
    return jnp.concatenate([cos, cos], axis=-1), jnp.concatenate([-sin, sin], axis=-1)


def _importance_matrix(nb, nc_pad):
    per = SEL_BLOCK // CMP_STRIDE
    j = np.arange(nb)[:, None]
    n = np.arange(nc_pad)[None, :]
    w = np.zeros((nb, nc_pad), np.float32)
    for d in range(CMP_LEN // CMP_STRIDE):
        w += ((n + d >= per * j) & (n + d <= per * j + per - 1)).astype(np.float32)
    return jnp.asarray(w, BF16)


_W_MQK, _W_MV, _W_MO, _W_NQ, _W_CQ, _W_GA = 0, 2048, 3072, 4096, 5120, 6144
_W_NKV = 12288
_W_BIG = 13824


def _split_w_in(w_in):
    d = w_in.shape[0]
    sizes = (2048, 1024, 1024, 4, 4, 1024, 256, 256, 256, 256, 256, 256, 48, 1024, 2048, 2048, 2048)
    offs = np.concatenate([[0], np.cumsum(sizes)])
    part = lambda i: w_in[:, offs[i]:offs[i + 1]]
    big = jnp.concatenate([part(0), part(1), part(2), part(5), part(13), part(14), part(15), part(16)]
                          + [part(i) for i in range(6, 12)], axis=1).astype(BF16)
    small = jnp.concatenate([part(3), part(4), part(12), jnp.zeros((d, 128 - 56), w_in.dtype)], axis=1).astype(BF16)
    return big, small


def _layer(x, mem, norm_mix_g, norm_mem_g, norm_ffn_g, w_in, m_conv_w, m_i_bias, m_f_bias, m_norm_g,
           n_q_norm_g, n_kc_norm_g, n_ks_norm_g, n_kw_norm_g, n_cmp_pe_k, n_cmp_w1_k, n_cmp_w2_k,
           n_cmp_pe_v, n_cmp_w1_v, n_cmp_w2_v, c_q_norm_g, c_k_norm_g, w_mem_k, w_mem_v,
           w_up_a, w_up_b, w_up_c, w_out, w_ffn_gate, w_ffn_up, w_ffn_down):
    B, S, D = x.shape
    M = B * S
    G, R, hd = N_GROUPS, N_REP, N_HEAD_DIM
    x2 = x.reshape(M, D)

    w_big, w_small = _split_w_in(w_in)
    h = _rmsnorm(x2, norm_mix_g)
    z = _mm(h, w_big, BF16)
    zs = _mm(h, w_small, F32)
    z3 = z.reshape(B, S, _W_BIG)
    zs3 = zs.reshape(B, S, 128)

    gif = jnp.stack([zs3[..., 0:4], zs3[..., 4:8]], axis=-1)
    gcol = gif.transpose(0, 2, 1, 3)
    grow = gif.transpose(0, 2, 3, 1)
    gbias = jnp.stack([m_i_bias, m_f_bias], axis=-1).reshape(M_HEADS, 1, 2).astype(F32)
    y_a = _mlstm(z3, gcol, grow, gbias, m_conv_w, m_norm_g, B, S)

    pos = jnp.arange(S, dtype=jnp.int32)
    cos, sin_s = _rope_tables(pos)
    nc_pad = S // CMP_STRIDE
    cmp_end = jnp.arange(nc_pad, dtype=jnp.int32) * CMP_STRIDE + CMP_LEN - 1
    cos_c, sin_c = _rope_tables(cmp_end)
    nb = S // SEL_BLOCK

    def kv_heads(i):
        u = z3[..., _W_NKV + i * 256:_W_NKV + (i + 1) * 256].reshape(B, S, G, hd)
        return u.transpose(0, 2, 1, 3).reshape(B * G, S, hd)

    def t_last(u):
        return u.reshape(B, G, u.shape[1], hd).transpose(0, 1, 3, 2)

    kc = _compress(kv_heads(0), n_cmp_pe_k, n_cmp_w1_k, n_cmp_w2_k, n_kc_norm_g, cos_c, sin_c, True)
    vc = _compress(kv_heads(1), n_cmp_pe_v, n_cmp_w1_v, n_cmp_w2_v, n_kc_norm_g, cos_c, sin_c, False)
    ks_aug = _knorm_rope(kv_heads(2), n_ks_norm_g, cos, sin_s, nb)
    kw = _knorm_rope(kv_heads(4), n_kw_norm_g, cos, sin_s, 0)
    qt = z3[..., _W_NQ:_W_NQ + 1024].reshape(B, S, G, R, hd).transpose(0, 2, 3, 4, 1)
    gates = zs3[..., 8:8 + 3 * N_HEADS].reshape(B, S, G, R, 3).transpose(0, 2, 4, 3, 1)
    y_bt = _nsa(qt, gates, kc.reshape(B, G, nc_pad, hd), t_last(vc), ks_aug.reshape(B, G, S, nb + hd),
                t_last(kv_heads(3)), kw.reshape(B, G, S, hd), t_last(kv_heads(5)),
                n_q_norm_g, cos.T, sin_s.T, _importance_matrix(nb, nc_pad), S)
    y_b = y_bt.transpose(0, 4, 1, 2, 3).reshape(B, S, N_HEADS * hd)

    mlen = mem.shape[1]
    mem_n = _rmsnorm(mem.reshape(B * mlen, D), norm_mem_g)
    k_mem = _mm(mem_n, w_mem_k.astype(BF16), BF16).reshape(B, mlen, -1)
    v_mem = _mm(mem_n, w_mem_v.astype(BF16), BF16).reshape(B, mlen, -1)
    y_c = _xattn(z3, _W_CQ // 1024, k_mem, v_mem, c_q_norm_g, c_k_norm_g, B, S)

    mix = _mix(y_a.reshape(M, -1), y_b.reshape(M, -1), y_c.reshape(M, -1),
               w_up_a.astype(BF16), w_up_b.astype(BF16), w_up_c.astype(BF16), z, _W_GA)
    x2 = _mm_residual(mix, w_out.astype(BF16), x2)

    h2 = _rmsnorm(x2, norm_ffn_g)
    act = _swiglu(h2, w_ffn_gate.astype(BF16), w_ffn_up.astype(BF16))
    x2 = _mm_residual(act, w_ffn_down.astype(BF16), x2)
    return x2.reshape(B, S, D)


def kernel(x, mem, norm_mix_g, norm_mem_g, norm_ffn_g, w_in, m_conv_w, m_i_bias, m_f_bias, m_norm_g, n_q_norm_g, n_kc_norm_g, n_ks_norm_g, n_kw_norm_g, n_cmp_pe_k, n_cmp_w1_k, n_cmp_w2_k, n_cmp_pe_v, n_cmp_w1_v, n_cmp_w2_v, c_q_norm_g, c_k_norm_g, w_mem_k, w_mem_v, w_up_a, w_up_b, w_up_c, w_out, w_ffn_gate, w_ffn_up, w_ffn_down):
    stacked = (norm_mix_g, norm_mem_g, norm_ffn_g, w_in, m_conv_w, m_i_bias, m_f_bias, m_norm_g, n_q_norm_g,
               n_kc_norm_g, n_ks_norm_g, n_kw_norm_g, n_cmp_pe_k, n_cmp_w1_k, n_cmp_w2_k, n_cmp_pe_v, n_cmp_w1_v,
               n_cmp_w2_v, c_q_norm_g, c_k_norm_g, w_mem_k, w_mem_v, w_up_a, w_up_b, w_up_c, w_out, w_ffn_gate,
               w_ffn_up, w_ffn_down)
    for layer in range(w_in.shape[0]):
        x = _layer(x, mem, *(p[layer] for p in stacked))
    return x
```

```python
import functools

import numpy as np
import jax
import jax.numpy as jnp
from jax import lax
from jax.experimental import pallas as pl
from jax.experimental.pallas import tpu as pltpu

F32 = jnp.float32
BF16 = jnp.bfloat16

EPS = 1e-6
ROPE_THETA = 10000.0
M_HEADS = 4
M_HEAD_DIM = 256
M_CONV = 4
N_HEADS = 16
N_GROUPS = 4
N_REP = N_HEADS // N_GROUPS
N_HEAD_DIM = 64
CMP_LEN = 32
CMP_STRIDE = 16
CMP_HIDDEN = 128
SEL_BLOCK = 64
SEL_TOP = 16
WINDOW = 512
FORCE_SCORE = 1.0e4
C_HEADS = 4
C_HEAD_DIM = 256

LOG2_E = 1.4426950408889634
V_AUG_ROWS = 16
NEG_BIG = -1.0e30
BLOCK_BIAS = -1.0e9
VMEM_LIMIT = 56 * 1024 * 1024

MLSTM_CHUNK = 256
NSA_TQ = 256
NSA_TK = 512
NSA_RC = 32


def _params(*sem):
    return pltpu.CompilerParams(dimension_semantics=sem, vmem_limit_bytes=VMEM_LIMIT)


def _sigmoid(x):
    return 1.0 / (1.0 + jnp.exp(-x))


def _silu(x):
    return x * _sigmoid(x)


def _log_sigmoid(x):
    return jnp.minimum(x, 0.0) - jnp.log1p(jnp.exp(-jnp.abs(x)))


def _rmsnorm_kernel(x_ref, g_ref, o_ref):
    x = x_ref[...]
    y = x * lax.rsqrt(jnp.mean(x * x, axis=-1, keepdims=True) + EPS)
    o_ref[...] = (y * g_ref[...]).astype(o_ref.dtype)


def _rmsnorm(x2d, g, tm=512):
    m, d = x2d.shape
    tm = min(tm, m)
    return pl.pallas_call(
        _rmsnorm_kernel,
        grid=(m // tm,),
        in_specs=[pl.BlockSpec((tm, d), lambda i: (i, 0)), pl.BlockSpec((1, d), lambda i: (0, 0))],
        out_specs=pl.BlockSpec((tm, d), lambda i: (i, 0)),
        out_shape=jax.ShapeDtypeStruct((m, d), BF16),
        compiler_params=_params("parallel"),
        name="rmsnorm",
    )(x2d, g.reshape(1, d))


def _mm_kernel(a_ref, w_ref, o_ref):
    o_ref[...] = jnp.dot(a_ref[...], w_ref[...], preferred_element_type=F32).astype(o_ref.dtype)


def _mm(a, w, out_dtype, tm=1024, tn=512):
    m, k = a.shape
    n = w.shape[1]
    tm, tn = min(tm, m), min(tn, n)
    return pl.pallas_call(
        _mm_kernel,
        grid=(m // tm, n // tn),
        in_specs=[pl.BlockSpec((tm, k), lambda i, j: (i, 0)), pl.BlockSpec((k, tn), lambda i, j: (0, j))],
        out_specs=pl.BlockSpec((tm, tn), lambda i, j: (i, j)),
        out_shape=jax.ShapeDtypeStruct((m, n), out_dtype),
        compiler_params=_params("parallel", "arbitrary"),
        name="matmul",
    )(a, w)


def _mm_res_kernel(a_ref, w_ref, x_ref, o_ref):
    o_ref[...] = x_ref[...] + jnp.dot(a_ref[...], w_ref[...], preferred_element_type=F32)


def _mm_residual(a, w, x, tm=512, tn=512):
    m, k = a.shape
    n = w.shape[1]
    tm, tn = min(tm, m), min(tn, n)
    return pl.pallas_call(
        _mm_res_kernel,
        grid=(m // tm, n // tn),
        in_specs=[pl.BlockSpec((tm, k), lambda i, j: (i, 0)), pl.BlockSpec((k, tn), lambda i, j: (0, j)),
                  pl.BlockSpec((tm, tn), lambda i, j: (i, j))],
        out_specs=pl.BlockSpec((tm, tn), lambda i, j: (i, j)),
        out_shape=jax.ShapeDtypeStruct((m, n), F32),
        compiler_params=_params("parallel", "arbitrary"),
        name="matmul_residual",
    )(a, w, x)


def _swiglu_kernel(h_ref, wg_ref, wu_ref, o_ref):
    h = h_ref[...]
    g = jnp.dot(h, wg_ref[...], preferred_element_type=F32)
    u = jnp.dot(h, wu_ref[...], preferred_element_type=F32)
    o_ref[...] = (_silu(g) * u).astype(o_ref.dtype)


def _swiglu(h, wg, wu, tm=1024, tn=512):
    m, k = h.shape
    n = wg.shape[1]
    tm, tn = min(tm, m), min(tn, n)
    return pl.pallas_call(
        _swiglu_kernel,
        grid=(m // tm, n // tn),
        in_specs=[pl.BlockSpec((tm, k), lambda i, j: (i, 0)), pl.BlockSpec((k, tn), lambda i, j: (0, j)),
                  pl.BlockSpec((k, tn), lambda i, j: (0, j))],
        out_specs=pl.BlockSpec((tm, tn), lambda i, j: (i, j)),
        out_shape=jax.ShapeDtypeStruct((m, n), BF16),
        compiler_params=_params("parallel", "arbitrary"),
        name="swiglu",
    )(h, wg, wu)


def _mix_kernel(ya_ref, yb_ref, yc_ref, wa_ref, wb_ref, wc_ref, ga_ref, gb_ref, gc_ref, o_ref):
    def branch(y_ref, w_ref, g_ref):
        return _sigmoid(g_ref[...].astype(F32)) * jnp.dot(y_ref[...], w_ref[...], preferred_element_type=F32)

    o_ref[...] = (branch(ya_ref, wa_ref, ga_ref) + branch(yb_ref, wb_ref, gb_ref)
                  + branch(yc_ref, wc_ref, gc_ref)).astype(o_ref.dtype)


def _mix(ya, yb, yc, wa, wb, wc, z, gate_col0, tm=1024, tn=512):
    m, k = ya.shape
    n = wa.shape[1]
    tm, tn = min(tm, m), min(tn, n)
    gofs = [(gate_col0 + b * n) // tn for b in range(3)]
    y_spec = pl.BlockSpec((tm, k), lambda i, j: (i, 0))
    w_spec = pl.BlockSpec((k, tn), lambda i, j: (0, j))
    g_specs = [pl.BlockSpec((tm, tn), functools.partial(lambda i, j, o: (i, o + j), o=o)) for o in gofs]
    return pl.pallas_call(
        _mix_kernel,
        grid=(m // tm, n // tn),
        in_specs=[y_spec, y_spec, y_spec, w_spec, w_spec, w_spec] + g_specs,
        out_specs=pl.BlockSpec((tm, tn), lambda i, j: (i, j)),
        out_shape=jax.ShapeDtypeStruct((m, n), BF16),
        compiler_params=_params("parallel", "arbitrary"),
        name="gated_mix",
    )(ya, yb, yc, wa, wb, wc, z, z, z)


def _mlstm_kernel(uq_ref, uk_ref, v_ref, o_ref, gcol_ref, grow_ref, bias_ref, cwq_ref, cwk_ref, ng_ref,
                  y_ref, c_scr, n_scr, m_scr, qbuf, kbuf):
    c = pl.program_id(2)
    L = uq_ref.shape[1]

    @pl.when(c == 0)
    def _():
        c_scr[...] = jnp.zeros_like(c_scr)
        n_scr[...] = jnp.zeros_like(n_scr)
        m_scr[...] = jnp.zeros_like(m_scr)
        qbuf[L:L + 8, :] = jnp.zeros((8, qbuf.shape[1]), F32)
        kbuf[L:L + 8, :] = jnp.zeros((8, kbuf.shape[1]), F32)

    def conv_silu(u_ref, buf, cw_ref):
        buf[0:8, :] = buf[L:L + 8, :]
        buf[8:L + 8, :] = u_ref[0].astype(F32)
        cw = cw_ref[...]
        y = cw[0:1, :] * buf[5:5 + L, :]
        for j in range(1, M_CONV):
            y = y + cw[j:j + 1, :] * buf[5 + j:5 + j + L, :]
        return _silu(y)

    q = conv_silu(uq_ref, qbuf, cwq_ref)
    k = conv_silu(uk_ref, kbuf, cwk_ref) * (M_HEAD_DIM ** -0.5)
    qb = q.astype(BF16)
    kb = k.astype(BF16)
    vb = v_ref[0]

    gcol = gcol_ref[0, 0]
    grow = grow_ref[0, 0]
    bias = bias_ref[0]
    ic = gcol[:, 0:1] + bias[:, 0:1]
    fc = _log_sigmoid(gcol[:, 1:2] + bias[:, 1:2])
    ir = grow[0:1, :] + bias[:, 0:1]
    fr = _log_sigmoid(grow[1:2, :] + bias[:, 1:2])

    row = lax.broadcasted_iota(jnp.int32, (L, L), 0)
    col = lax.broadcasted_iota(jnp.int32, (L, L), 1)
    tri = col <= row
    b_col = jnp.sum(jnp.where(tri, fr, 0.0), axis=1, keepdims=True)
    b_row = jnp.sum(jnp.where(row <= col, fc, 0.0), axis=0, keepdims=True)
    m_prev = m_scr[0:1, 0:1]

    log_d = jnp.where(tri, b_col - b_row + ir, NEG_BIG)
    log_inter = b_col + m_prev
    m_t = jnp.maximum(log_inter, jnp.max(log_d, axis=1, keepdims=True))
    w_intra = jnp.exp(log_d - m_t)
    w_inter = jnp.exp(log_inter - m_t)

    s = lax.dot_general(qb, kb, (((1,), (1,)), ((), ())), preferred_element_type=F32) * w_intra
    c_state = c_scr[...]
    n_state = n_scr[0:1, :]
    num = (jnp.dot(s.astype(BF16), vb, preferred_element_type=F32)
           + w_inter * jnp.dot(qb, c_state.astype(BF16), preferred_element_type=F32))
    den = jnp.sum(s, axis=1, keepdims=True) + w_inter * jnp.sum(q * n_state, axis=1, keepdims=True)
    h = num / jnp.maximum(jnp.abs(den), jnp.exp(-m_t))

    b_last = jnp.sum(fc, axis=0, keepdims=True)
    log_g = b_last - b_col + ic
    m_new = jnp.maximum(b_last + m_prev, jnp.max(log_g, axis=0, keepdims=True))
    decay = jnp.exp(b_last + m_prev - m_new)
    wk = k * jnp.exp(log_g - m_new)
    c_scr[...] = decay * c_state + lax.dot_general(wk.astype(BF16), vb, (((0,), (0,)), ((), ())),
                                                   preferred_element_type=F32)
    n_scr[0:1, :] = decay * n_state + jnp.sum(wk, axis=0, keepdims=True)
    m_scr[...] = jnp.broadcast_to(m_new, m_scr.shape)

    hn = h * lax.rsqrt(jnp.mean(h * h, axis=-1, keepdims=True) + EPS) * ng_ref[0]
    y_ref[0] = (_sigmoid(o_ref[0].astype(F32)) * hn).astype(y_ref.dtype)


def _mlstm(z, gcol, grow, gbias, conv_w, norm_g, batch, seq):
    L = min(MLSTM_CHUNK, seq)
    hd = M_HEAD_DIM
    nh = M_HEADS

    def zspec(blk0):
        return pl.BlockSpec((1, L, hd), functools.partial(lambda b, h, c, o: (b, c, o + h), o=blk0))

    return pl.pallas_call(
        _mlstm_kernel,
        grid=(batch, nh, seq // L),
        in_specs=[zspec(0), zspec(nh), zspec(2 * nh), zspec(3 * nh),
                  pl.BlockSpec((1, 1, L, 2), lambda b, h, c: (b, h, c, 0)),
                  pl.BlockSpec((1, 1, 2, L), lambda b, h, c: (b, h, 0, c)),
                  pl.BlockSpec((1, 1, 2), lambda b, h, c: (h, 0, 0)),
                  pl.BlockSpec((M_CONV, hd), lambda b, h, c: (0, h)),
                  pl.BlockSpec((M_CONV, hd), lambda b, h, c: (0, nh + h)),
                  pl.BlockSpec((1, 1, hd), lambda b, h, c: (h, 0, 0))],
        out_specs=pl.BlockSpec((1, L, hd), lambda b, h, c: (b, c, h)),
        out_shape=jax.ShapeDtypeStruct((batch, seq, nh * hd), BF16),
        scratch_shapes=[pltpu.VMEM((hd, hd), F32), pltpu.VMEM((8, hd), F32), pltpu.VMEM((8, 128), F32),
                        pltpu.VMEM((L + 8, hd), F32), pltpu.VMEM((L + 8, hd), F32)],
        compiler_params=_params("parallel", "parallel", "arbitrary"),
        name="mlstm",
    )(z, z, z, z, gcol, grow, gbias, conv_w, conv_w, norm_g.reshape(nh, 1, hd))


def _xattn_kernel(q_ref, k_ref, v_ref, gq_ref, gk_ref, o_ref):
    hd = C_HEAD_DIM
    for h in range(C_HEADS):
        sl = slice(h * hd, (h + 1) * hd)
        q = q_ref[0, :, sl].astype(F32)
        q = q * lax.rsqrt(jnp.mean(q * q, axis=-1, keepdims=True) + EPS) * gq_ref[...] * (hd ** -0.5)
        k = k_ref[0, :, sl].astype(F32)
        k = k * lax.rsqrt(jnp.mean(k * k, axis=-1, keepdims=True) + EPS) * gk_ref[...]
        s = lax.dot_general(q.astype(BF16), k.astype(BF16), (((1,), (1,)), ((), ())), preferred_element_type=F32)
        e = jnp.exp(s - jnp.max(s, axis=-1, keepdims=True))
        p = e / jnp.sum(e, axis=-1, keepdims=True)
        o = jnp.dot(p.astype(BF16), v_ref[0, :, sl], preferred_element_type=F32)
        o_ref[0, :, sl] = o.astype(o_ref.dtype)


def _xattn(z, q_blk, k, v, gq, gk, batch, seq, tq=512):
    tq = min(tq, seq)
    mlen, w = k.shape[1], k.shape[2]
    return pl.pallas_call(
        _xattn_kernel,
        grid=(batch, seq // tq),
        in_specs=[pl.BlockSpec((1, tq, w), lambda b, i: (b, i, q_blk)),
                  pl.BlockSpec((1, mlen, w), lambda b, i: (b, 0, 0)),
                  pl.BlockSpec((1, mlen, w), lambda b, i: (b, 0, 0)),
                  pl.BlockSpec((1, C_HEAD_DIM), lambda b, i: (0, 0)),
                  pl.BlockSpec((1, C_HEAD_DIM), lambda b, i: (0, 0))],
        out_specs=pl.BlockSpec((1, tq, w), lambda b, i: (b, i, 0)),
        out_shape=jax.ShapeDtypeStruct((batch, seq, w), BF16),
        compiler_params=_params("parallel", "arbitrary"),
        name="memory_cross_attention",
    )(z, k, v, gq.reshape(1, -1), gk.reshape(1, -1))


def _rope_rows(xn, cos, sin_signed):
    half = N_HEAD_DIM // 2
    rot = jnp.concatenate([xn[:, half:], xn[:, :half]], axis=-1)
    return xn * cos + rot * sin_signed


def _knorm_rope_kernel(x_ref, g_ref, cos_ref, sin_ref, o_ref, *, n_blocks):
    x = x_ref[0].astype(F32)
    xn = x * lax.rsqrt(jnp.mean(x * x, axis=-1, keepdims=True) + EPS) * g_ref[...]
    k = _rope_rows(xn, cos_ref[...], sin_ref[...]).astype(o_ref.dtype)
    if n_blocks == 0:
        o_ref[0] = k
    else:
        ts = x.shape[0]
        pos = pl.program_id(1) * ts + lax.broadcasted_iota(jnp.int32, (ts, n_blocks), 0)
        blk = lax.broadcasted_iota(jnp.int32, (ts, n_blocks), 1)
        o_ref[0, :, 0:n_blocks] = jnp.where(pos // SEL_BLOCK == blk, 1.0, 0.0).astype(o_ref.dtype)
        o_ref[0, :, n_blocks:n_blocks + N_HEAD_DIM] = k


def _knorm_rope(x, g, cos, sin_signed, n_blocks, ts=1024):
    n, s, hd = x.shape
    ts = min(ts, s)
    return pl.pallas_call(
        functools.partial(_knorm_rope_kernel, n_blocks=n_blocks),
        grid=(n, s // ts),
        in_specs=[pl.BlockSpec((1, ts, hd), lambda i, j: (i, j, 0)), pl.BlockSpec((1, hd), lambda i, j: (0, 0)),
                  pl.BlockSpec((ts, hd), lambda i, j: (j, 0)), pl.BlockSpec((ts, hd), lambda i, j: (j, 0))],
        out_specs=pl.BlockSpec((1, ts, n_blocks + hd), lambda i, j: (i, j, 0)),
        out_shape=jax.ShapeDtypeStruct((n, s, n_blocks + hd), BF16),
        compiler_params=_params("parallel", "arbitrary"),
        name="key_norm_rope",
    )(x, g.reshape(1, hd), cos, sin_signed)


def _compress_kernel(sub_ref, pe_ref, w1_ref, w2_ref, g_ref, cos_ref, sin_ref, o_ref, *, is_key):
    ns = sub_ref.shape[1]
    sub = sub_ref[0].astype(F32)
    lo = jnp.dot((sub + pe_ref[0:1, :]).astype(BF16), w1_ref[0], preferred_element_type=F32)
    hi = jnp.dot((sub + pe_ref[1:2, :]).astype(BF16), w1_ref[1], preferred_element_type=F32)
    hid = _silu(lo + pltpu.roll(hi, shift=ns - 1, axis=0))
    out = jnp.dot(hid.astype(BF16), w2_ref[...], preferred_element_type=F32)
    if is_key:
        out = out * lax.rsqrt(jnp.mean(out * out, axis=-1, keepdims=True) + EPS) * g_ref[...]
        out = _rope_rows(out, cos_ref[...], sin_ref[...])
    o_ref[0] = out.astype(o_ref.dtype)


def _compress(u, pe, w1, w2, g, cos, sin_signed, is_key):
    n, s, hd = u.shape
    ns = s // CMP_STRIDE
    width = CMP_STRIDE * hd
    sub = u.reshape(n, ns, width)
    pe2 = pe.reshape(CMP_LEN // CMP_STRIDE, width)
    w1s = w1.reshape(CMP_LEN // CMP_STRIDE, width, CMP_HIDDEN).astype(BF16)
    return pl.pallas_call(
        functools.partial(_compress_kernel, is_key=is_key),
        grid=(n,),
        in_specs=[pl.BlockSpec((1, ns, width), lambda i: (i, 0, 0)),
                  pl.BlockSpec(pe2.shape, lambda i: (0, 0)),
                  pl.BlockSpec(w1s.shape, lambda i: (0, 0, 0)),
                  pl.BlockSpec(w2.shape, lambda i: (0, 0)),
                  pl.BlockSpec((1, hd), lambda i: (0, 0)),
                  pl.BlockSpec((ns, hd), lambda i: (0, 0)),
                  pl.BlockSpec((ns, hd), lambda i: (0, 0))],
        out_specs=pl.BlockSpec((1, ns, hd), lambda i: (i, 0, 0)),
        out_shape=jax.ShapeDtypeStruct((n, ns, hd), BF16),
        compiler_params=_params("parallel"),
        name="compress_key" if is_key else "compress_value",
    )(sub, pe2, w1s, w2.astype(BF16), g.reshape(1, hd), cos, sin_signed)


def _nsa_kernel(q_ref, gate_ref, kc_ref, vct_ref, ks_ref, vst_ref, kw_ref, vwt_ref, gq_ref, cos_ref, sin_ref,
                wimp_ref, y_ref, s_c, e_c, s_w, e_w, s_a, s_b, e_a, e_b, *, seq):
    tq = q_ref.shape[4]
    hd = N_HEAD_DIM
    nb = seq // SEL_BLOCK
    top = min(SEL_TOP, nb)
    ncol = N_REP * tq
    t0 = pl.program_id(2) * tq

    def lanes(parts):
        return jnp.concatenate(parts, axis=1)

    def head_cols(a, r):
        return a[:, r * tq:(r + 1) * tq]

    def chunk_scores(src, c, valid_fn):
        sc = src[c * NSA_RC:(c + 1) * NSA_RC, :]
        if valid_fn is None:
            return sc
        valid = valid_fn(c * NSA_RC + lax.broadcasted_iota(jnp.int32, (NSA_RC, 1), 0))
        return lanes([jnp.where(valid, head_cols(sc, r), NEG_BIG) for r in range(N_REP)])

    def col_max(src, rows, valid_fn):
        mx = jnp.full((8, ncol), NEG_BIG, F32)
        for c in range(rows // NSA_RC):
            sc = chunk_scores(src, c, valid_fn)
            mx = jnp.maximum(mx, jnp.max(sc.reshape(NSA_RC // 8, 8, ncol), axis=0))
        return jnp.max(mx, axis=0, keepdims=True)

    def put_weights(src, dst, rows, valid_fn, m):
        for c in range(rows // NSA_RC):
            sc = chunk_scores(src, c, valid_fn)
            dst[c * NSA_RC:(c + 1) * NSA_RC, :] = jnp.exp2(sc - m).astype(BF16)

    q = lanes([q_ref[0, 0, r].astype(F32) for r in range(N_REP)])
    qn = q * lax.rsqrt(jnp.mean(q * q, axis=0, keepdims=True) + EPS) * gq_ref[...]
    cos = lanes([cos_ref[...]] * N_REP)
    sin = lanes([sin_ref[...]] * N_REP)
    rot = jnp.concatenate([qn[hd // 2:], qn[:hd // 2]], axis=0)
    qr = ((qn * cos + rot * sin) * (hd ** -0.5 * LOG2_E)).astype(BF16)
    t_q = t0 + lax.broadcasted_iota(jnp.int32, (1, tq), 1)
    tcol = lanes([t_q] * N_REP)

    nc = kc_ref.shape[2]
    s_c[...] = jnp.dot(kc_ref[0, 0], qr, preferred_element_type=F32)
    valid_c = lambda n: n * CMP_STRIDE + (CMP_LEN - 1) <= t_q
    put_weights(s_c, e_c, nc, valid_c, col_max(s_c, nc, valid_c))
    e = e_c[...]
    o_c = jnp.dot(vct_ref[0, 0], e, preferred_element_type=F32)
    inv_c = jnp.where(tcol >= CMP_LEN - 1, 1.0 / o_c[hd:hd + 1], 0.0)
    o_c = o_c[0:hd] * inv_c

    imp_h = jnp.dot(wimp_ref[...], e, preferred_element_type=F32) * inv_c
    imp = head_cols(imp_h, 0)
    for r in range(1, N_REP):
        imp = imp + head_cols(imp_h, r)

    blk = lax.broadcasted_iota(jnp.int32, (nb, tq), 0)
    cur = (t0 + lax.broadcasted_iota(jnp.int32, (nb, tq), 1)) // SEL_BLOCK
    causal_b = blk <= cur
    forced = (blk == 0) | (blk == cur) | (blk == cur - 1)
    score = jnp.where(forced, FORCE_SCORE, jnp.where(causal_b, imp, -1.0))
    taken = -3.0e38
    for _ in range(top):
        mx = jnp.max(score, axis=0, keepdims=True)
        idx = jnp.min(jnp.where(score == mx, blk, nb), axis=0, keepdims=True)
        score = jnp.where(blk == idx, taken, score)
    bias = jnp.where((score == taken) & causal_b, 0.0, BLOCK_BIAS)
    q_aug = jnp.concatenate([lanes([bias.astype(BF16)] * N_REP), qr], axis=0)

    span = WINDOW + tq
    ws0 = pl.multiple_of(jnp.maximum(t0 - WINDOW, 0), tq)
    s_w[...] = jnp.dot(kw_ref[0, 0, pl.ds(ws0, span), :], qr, preferred_element_type=F32)
    valid_w = lambda k: (ws0 + k <= t_q) & (ws0 + k > t_q - WINDOW)
    put_weights(s_w, e_w, span, valid_w, col_max(s_w, span, valid_w))
    o_w = jnp.dot(vwt_ref[0, 0, :, pl.ds(ws0, span)], e_w[...], preferred_element_type=F32)
    o_w = o_w[0:hd] / o_w[hd:hd + 1]

    tk = min(NSA_TK, seq)
    last = (t0 + tq - 1) // tk
    s_bufs, e_bufs = (s_a, s_b), (e_a, e_b)

    def scores(t, par):
        ks0 = pl.multiple_of(t * tk, tk)
        s_bufs[par][...] = jnp.dot(ks_ref[0, 0, pl.ds(ks0, tk), :], q_aug, preferred_element_type=F32)

    def weights(t, par, m, masked):
        valid_s = (lambda k: t * tk + k <= t_q) if masked else None
        m_new = jnp.maximum(m, col_max(s_bufs[par], tk, valid_s))
        put_weights(s_bufs[par], e_bufs[par], tk, valid_s, m_new)
        return m_new, jnp.exp2(m - m_new)

    def values(t, par, alpha, acc):
        ks0 = pl.multiple_of(jnp.maximum(t, 0) * tk, tk)
        return alpha * acc + jnp.dot(vst_ref[0, 0, :, pl.ds(ks0, tk)], e_bufs[par][...], preferred_element_type=F32)

    def step(t, par, carry, masked, more):
        m, alpha, acc = carry
        if more:
            scores(t + 1, 1 - par)
        acc = values(t - 1, 1 - par, alpha, acc)
        m, alpha = weights(t, par, m, masked)
        return m, alpha, acc

    def pair_body(j, carry):
        return step(2 * j + 1, 1, step(2 * j, 0, carry, False, True), False, True)

    def tail_even(carry):
        m, alpha, acc = step(last, 0, carry, True, False)
        return values(last, 0, alpha, acc)

    def tail_odd(carry):
        m, alpha, acc = step(last, 1, step(last - 1, 0, carry, False, True), True, False)
        return values(last, 1, alpha, acc)

    scores(0, 0)
    e_b[...] = jnp.zeros(e_b.shape, e_b.dtype)
    carry = (jnp.full((1, ncol), NEG_BIG, F32), jnp.ones((1, ncol), F32), jnp.zeros((vst_ref.shape[2], ncol), F32))
    carry = lax.fori_loop(0, last // 2, pair_body, carry)
    acc = lax.cond(last % 2 == 0, tail_even, tail_odd, carry)
    o_s = acc[0:hd] / acc[hd:hd + 1]

    g = _sigmoid(gate_ref[0, 0])
    gc, gs, gw = (lanes([g[j, r:r + 1, :] for r in range(N_REP)]) for j in range(3))
    out = gc * o_c + gs * o_s + gw * o_w
    for r in range(N_REP):
        y_ref[0, 0, r] = out[:, r * tq:(r + 1) * tq].astype(y_ref.dtype)


def _nsa(qt, gates, kc, vct, ks_aug, vst, kw, vwt, gq, cos_t, sin_t, wimp_t, seq):
    b, g = qt.shape[:2]
    hd = N_HEAD_DIM
    tq = min(NSA_TQ, seq)
    tk = min(NSA_TK, seq)
    ncol = N_REP * tq
    nc = kc.shape[2]
    nb = seq // SEL_BLOCK
    full = lambda *shape: pl.BlockSpec((1, 1) + shape, lambda bi, gi, qi: (bi, gi) + (0,) * len(shape))
    return pl.pallas_call(
        functools.partial(_nsa_kernel, seq=seq),
        grid=(b, g, seq // tq),
        in_specs=[pl.BlockSpec((1, 1, N_REP, hd, tq), lambda bi, gi, qi: (bi, gi, 0, 0, qi)),
                  pl.BlockSpec((1, 1, 3, N_REP, tq), lambda bi, gi, qi: (bi, gi, 0, 0, qi)),
                  full(nc, hd), full(hd + V_AUG_ROWS, nc), full(seq, nb + hd), full(hd + V_AUG_ROWS, seq), full(seq, hd),
                  full(hd + V_AUG_ROWS, seq),
                  pl.BlockSpec((hd, 1), lambda bi, gi, qi: (0, 0)),
                  pl.BlockSpec((hd, tq), lambda bi, gi, qi: (0, qi)),
                  pl.BlockSpec((hd, tq), lambda bi, gi, qi: (0, qi)),
                  pl.BlockSpec((nb, nc), lambda bi, gi, qi: (0, 0))],
        out_specs=pl.BlockSpec((1, 1, N_REP, hd, tq), lambda bi, gi, qi: (bi, gi, 0, 0, qi)),
        out_shape=jax.ShapeDtypeStruct((b, g, N_REP, hd, seq), BF16),
        scratch_shapes=[pltpu.VMEM((nc, ncol), F32), pltpu.VMEM((nc, ncol), BF16),
                        pltpu.VMEM((WINDOW + tq, ncol), F32), pltpu.VMEM((WINDOW + tq, ncol), BF16),
                        pltpu.VMEM((tk, ncol), F32), pltpu.VMEM((tk, ncol), F32),
                        pltpu.VMEM((tk, ncol), BF16), pltpu.VMEM((tk, ncol), BF16)],
        compiler_params=_params("parallel", "parallel", "arbitrary"),
        name="nsa_attention",
    )(qt, gates, kc, vct, ks_aug, vst, kw, vwt, gq.reshape(hd, 1), cos_t, sin_t, wimp_t)


def _rope_tables(pos):
    half = N_HEAD_DIM // 2
    inv = jnp.power(ROPE_THETA, -jnp.arange(half, dtype=F32) * 2.0 / N_HEAD_DIM)
    ang = pos.astype(F32)[:, None] * inv[None, :]
    cos, sin = jnp.cos(ang), jnp.sin(ang)
    return jnp.concatenate([cos, cos], axis=-1), jnp.concatenate([-sin, sin], axis=-1)


def _importance_matrix(nb, nc_pad):
    per = SEL_BLOCK // CMP_STRIDE
    j = np.arange(nb)[:, None]
    n = np.arange(nc_pad)[None, :]
    w = np.zeros((nb, nc_pad), np.float32)
    for d in range(CMP_LEN // CMP_STRIDE):
        w += ((n + d >= per * j) & (n + d <= per * j + per - 1)).astype(np.float32)
    return jnp.asarray(w, BF16)


_W_MQK, _W_MV, _W_MO, _W_NQ, _W_CQ, _W_GA = 0, 2048, 3072, 4096, 5120, 6144
_W_NKV = 12288
_W_BIG = 13824


def _split_w_in(w_in):
    d = w_in.shape[0]
    sizes = (2048, 1024, 1024, 4, 4, 1024, 256, 256, 256, 256, 256, 256, 48, 1024, 2048, 2048, 2048)
    offs = np.concatenate([[0], np.cumsum(sizes)])
    part = lambda i: w_in[:, offs[i]:offs[i + 1]]
    big = jnp.concatenate([part(0), part(1), part(2), part(5), part(13), part(14), part(15), part(16)]
                          + [part(i) for i in range(6, 12)], axis=1).astype(BF16)
    small = jnp.concatenate([part(3), part(4), part(12), jnp.zeros((d, 128 - 56), w_in.dtype)], axis=1).astype(BF16)
    return big, small


def _layer(x, mem, norm_mix_g, norm_mem_g, norm_ffn_g, w_in, m_conv_w, m_i_bias, m_f_bias, m_norm_g,
           n_q_norm_g, n_kc_norm_g, n_ks_norm_g, n_kw_norm_g, n_cmp_pe_k, n_cmp_w1_k, n_cmp_w2_k,
           n_cmp_pe_v, n_cmp_w1_v, n_cmp_w2_v, c_q_norm_g, c_k_norm_g, w_mem_k, w_mem_v,
           w_up_a, w_up_b, w_up_c, w_out, w_ffn_gate, w_ffn_up, w_ffn_down):
    B, S, D = x.shape
    M = B * S
    G, R, hd = N_GROUPS, N_REP, N_HEAD_DIM
    x2 = x.reshape(M, D)

    w_big, w_small = _split_w_in(w_in)
    h = _rmsnorm(x2, norm_mix_g)
    z = _mm(h, w_big, BF16)
    zs = _mm(h, w_small, F32)
    z3 = z.reshape(B, S, _W_BIG)
    zs3 = zs.reshape(B, S, 128)

    gif = jnp.stack([zs3[..., 0:4], zs3[..., 4:8]], axis=-1)
    gcol = gif.transpose(0, 2, 1, 3)
    grow = gif.transpose(0, 2, 3, 1)
    gbias = jnp.stack([m_i_bias, m_f_bias], axis=-1).reshape(M_HEADS, 1, 2).astype(F32)
    y_a = _mlstm(z3, gcol, grow, gbias, m_conv_w, m_norm_g, B, S)

    pos = jnp.arange(S, dtype=jnp.int32)
    cos, sin_s = _rope_tables(pos)
    nc_pad = S // CMP_STRIDE
    cmp_end = jnp.arange(nc_pad, dtype=jnp.int32) * CMP_STRIDE + CMP_LEN - 1
    cos_c, sin_c = _rope_tables(cmp_end)
    nb = S // SEL_BLOCK

    def kv_heads(i):
        u = z3[..., _W_NKV + i * 256:_W_NKV + (i + 1) * 256].reshape(B, S, G, hd)
        return u.transpose(0, 2, 1, 3).reshape(B * G, S, hd)

    def t_last(u):
        return u.reshape(B, G, u.shape[1], hd).transpose(0, 1, 3, 2)

    def t_last_ones(u):
        n = u.shape[1]
        tail = jnp.concatenate([jnp.ones((B, G, 1, n), u.dtype), jnp.zeros((B, G, V_AUG_ROWS - 1, n), u.dtype)], axis=2)
        return jnp.concatenate([t_last(u), tail], axis=2)

    kc = _compress(kv_heads(0), n_cmp_pe_k, n_cmp_w1_k, n_cmp_w2_k, n_kc_norm_g, cos_c, sin_c, True)
    vc = _compress(kv_heads(1), n_cmp_pe_v, n_cmp_w1_v, n_cmp_w2_v, n_kc_norm_g, cos_c, sin_c, False)
    ks_aug = _knorm_rope(kv_heads(2), n_ks_norm_g, cos, sin_s, nb)
    kw = _knorm_rope(kv_heads(4), n_kw_norm_g, cos, sin_s, 0)
    qt = z3[..., _W_NQ:_W_NQ + 1024].reshape(B, S, G, R, hd).transpose(0, 2, 3, 4, 1)
    gates = zs3[..., 8:8 + 3 * N_HEADS].reshape(B, S, G, R, 3).transpose(0, 2, 4, 3, 1)
    y_bt = _nsa(qt, gates, kc.reshape(B, G, nc_pad, hd), t_last_ones(vc), ks_aug.reshape(B, G, S, nb + hd),
                t_last_ones(kv_heads(3)), kw.reshape(B, G, S, hd), t_last_ones(kv_heads(5)),
                n_q_norm_g, cos.T, sin_s.T, _importance_matrix(nb, nc_pad), S)
    y_b = y_bt.transpose(0, 4, 1, 2, 3).reshape(B, S, N_HEADS * hd)

    mlen = mem.shape[1]
    mem_n = _rmsnorm(mem.reshape(B * mlen, D), norm_mem_g)
    k_mem = _mm(mem_n, w_mem_k.astype(BF16), BF16).reshape(B, mlen, -1)
    v_mem = _mm(mem_n, w_mem_v.astype(BF16), BF16).reshape(B, mlen, -1)
    y_c = _xattn(z3, _W_CQ // 1024, k_mem, v_mem, c_q_norm_g, c_k_norm_g, B, S)

    mix = _mix(y_a.reshape(M, -1), y_b.reshape(M, -1), y_c.reshape(M, -1),
               w_up_a.astype(BF16), w_up_b.astype(BF16), w_up_c.astype(BF16), z, _W_GA)
    x2 = _mm_residual(mix, w_out.astype(BF16), x2)

    h2 = _rmsnorm(x2, norm_ffn_g)
    act = _swiglu(h2, w_ffn_gate.astype(BF16), w_ffn_up.astype(BF16))
    x2 = _mm_residual(act, w_ffn_down.astype(BF16), x2)
    return x2.reshape(B, S, D)


def kernel(x, mem, norm_mix_g, norm_mem_g, norm_ffn_g, w_in, m_conv_w, m_i_bias, m_f_bias, m_norm_g, n_q_norm_g, n_kc_norm_g, n_ks_norm_g, n_kw_norm_g, n_cmp_pe_k, n_cmp_w1_k, n_cmp_w2_k, n_cmp_pe_v, n_cmp_w1_v, n_cmp_w2_v, c_q_norm_g, c_k_norm_g, w_mem_k, w_mem_v, w_up_a, w_up_b, w_up_c, w_out, w_ffn_gate, w_ffn_up, w_ffn_down):
    stacked = (norm_mix_g, norm_mem_g, norm_ffn_g, w_in, m_conv_w, m_i_bias, m_f_bias, m_norm_g, n_q_norm_g,
               n_kc_norm_g, n_ks_norm_g, n_kw_norm_g, n_cmp_pe_k, n_cmp_w1_k, n_cmp_w2_k, n_cmp_pe_v, n_cmp_w1_v,
               n_cmp_w2_v, c_q_norm_g, c_k_norm_g, w_mem_k, w_mem_v, w_up_a, w_up_b, w_up_c, w_out, w_ffn_gate,
               w_ffn_up, w_ffn_down)
    for layer in range(w_in.shape[0]):
        x = _layer(x, mem, *(p[layer] for p in stacked))
    return x
```

```python
import functools

import numpy as np
import jax
import jax.numpy as jnp
from jax import lax
from jax.experimental import pallas as pl
from jax.experimental.pallas import tpu as pltpu

F32 = jnp.float32
BF16 = jnp.bfloat16

EPS = 1e-6
ROPE_THETA = 10000.0
M_HEADS = 4
M_HEAD_DIM = 256
M_CONV = 4
N_HEADS = 16
N_GROUPS = 4
N_REP = N_HEADS // N_GROUPS
N_HEAD_DIM = 64
CMP_LEN = 32
CMP_STRIDE = 16
CMP_HIDDEN = 128
SEL_BLOCK = 64
SEL_TOP = 16
WINDOW = 512
FORCE_SCORE = 1.0e4
C_HEADS = 4
C_HEAD_DIM = 256

LOG2_E = 1.4426950408889634
V_AUG_ROWS = 16
NEG_BIG = -1.0e30
BLOCK_BIAS = -1.0e9
VMEM_LIMIT = 56 * 1024 * 1024

MLSTM_CHUNK = 256
NSA_TQ = 256
NSA_TK = 512
NSA_RC = 32

Z_MQK, Z_MV, Z_MO, Z_CQ, Z_GATES, Z_KC, Z_VC, Z_KS, Z_KW, Z_WIDTH = (
    0, 2048, 3072, 4096, 5120, 11264, 11520, 11776, 12032, 12288)
ZT_Q, ZT_VS, ZT_VW, ZT_ROWS = 0, 1024, 1280, 1536


def _params(*sem):
    return pltpu.CompilerParams(dimension_semantics=sem, vmem_limit_bytes=VMEM_LIMIT)


def _sigmoid(x):
    return 1.0 / (1.0 + jnp.exp(-x))


def _silu(x):
    return x * _sigmoid(x)


def _log_sigmoid(x):
    return jnp.minimum(x, 0.0) - jnp.log1p(jnp.exp(-jnp.abs(x)))


def _rms_rows(x, g):
    return x * lax.rsqrt(jnp.mean(x * x, axis=-1, keepdims=True) + EPS) * g


def _rmsnorm_kernel(x_ref, g_ref, o_ref):
    o_ref[...] = _rms_rows(x_ref[...], g_ref[...]).astype(o_ref.dtype)


def _rmsnorm(x2d, g, tm=512):
    m, d = x2d.shape
    tm = min(tm, m)
    return pl.pallas_call(
        _rmsnorm_kernel,
        grid=(m // tm,),
        in_specs=[pl.BlockSpec((tm, d), lambda i: (i, 0)), pl.BlockSpec((1, d), lambda i: (0, 0))],
        out_specs=pl.BlockSpec((tm, d), lambda i: (i, 0)),
        out_shape=jax.ShapeDtypeStruct((m, d), BF16),
        compiler_params=_params("parallel"),
        name="rmsnorm",
    )(x2d, g.reshape(1, d))


def _mm_kernel(a_ref, w_ref, o_ref):
    o_ref[...] = jnp.dot(a_ref[...], w_ref[...], preferred_element_type=F32).astype(o_ref.dtype)


def _mm(a, w, out_dtype, tm=1024, tn=512):
    m, k = a.shape
    n = w.shape[1]
    tm, tn = min(tm, m), min(tn, n)
    return pl.pallas_call(
        _mm_kernel,
        grid=(m // tm, n // tn),
        in_specs=[pl.BlockSpec((tm, k), lambda i, j: (i, 0)), pl.BlockSpec((k, tn), lambda i, j: (0, j))],
        out_specs=pl.BlockSpec((tm, tn), lambda i, j: (i, j)),
        out_shape=jax.ShapeDtypeStruct((m, n), out_dtype),
        compiler_params=_params("parallel", "arbitrary"),
        name="matmul",
    )(a, w)


def _in_proj_kernel(x_ref, g_ref, w_ref, ws_ref, z_ref, zs_ref, h_ref):
    @pl.when(pl.program_id(1) == 0)
    def _():
        h = _rms_rows(x_ref[...], g_ref[...]).astype(BF16)
        h_ref[...] = h
        zs_ref[...] = jnp.dot(h, ws_ref[...], preferred_element_type=F32)

    z_ref[...] = jnp.dot(h_ref[...], w_ref[...], preferred_element_type=F32).astype(z_ref.dtype)


def _in_proj(x2d, g, w, w_small, tm=1024, tn=512):
    m, d = x2d.shape
    n, ns = w.shape[1], w_small.shape[1]
    tm, tn = min(tm, m), min(tn, n)
    return pl.pallas_call(
        _in_proj_kernel,
        grid=(m // tm, n // tn),
        in_specs=[pl.BlockSpec((tm, d), lambda i, j: (i, 0)), pl.BlockSpec((1, d), lambda i, j: (0, 0)),
                  pl.BlockSpec((d, tn), lambda i, j: (0, j)), pl.BlockSpec((d, ns), lambda i, j: (0, 0))],
        out_specs=[pl.BlockSpec((tm, tn), lambda i, j: (i, j)), pl.BlockSpec((tm, ns), lambda i, j: (i, 0)),
                   pl.BlockSpec((tm, d), lambda i, j: (i, 0))],
        out_shape=[jax.ShapeDtypeStruct((m, n), BF16), jax.ShapeDtypeStruct((m, ns), F32),
                   jax.ShapeDtypeStruct((m, d), BF16)],
        compiler_params=_params("parallel", "arbitrary"),
        name="in_proj",
    )(x2d, g.reshape(1, d), w, w_small)


def _mm_nt_kernel(w_ref, h_ref, o_ref):
    o_ref[...] = lax.dot_general(w_ref[...], h_ref[...], (((1,), (1,)), ((), ())),
                                 preferred_element_type=F32).astype(o_ref.dtype)


def _mm_feature_major(w_t, h, tm=1024, tn=512):
    n, k = w_t.shape
    m = h.shape[0]
    tm, tn = min(tm, m), min(tn, n)
    return pl.pallas_call(
        _mm_nt_kernel,
        grid=(m // tm, n // tn),
        in_specs=[pl.BlockSpec((tn, k), lambda i, j: (j, 0)), pl.BlockSpec((tm, k), lambda i, j: (i, 0))],
        out_specs=pl.BlockSpec((tn, tm), lambda i, j: (j, i)),
        out_shape=jax.ShapeDtypeStruct((n, m), BF16),
        compiler_params=_params("parallel", "arbitrary"),
        name="in_proj_feature_major",
    )(w_t, h)


def _mm_res_kernel(a_ref, w_ref, x_ref, o_ref):
    o_ref[...] = x_ref[...] + jnp.dot(a_ref[...], w_ref[...], preferred_element_type=F32)


def _mm_residual(a, w, x, tm=512, tn=512):
    m, k = a.shape
    n = w.shape[1]
    tm, tn = min(tm, m), min(tn, n)
    return pl.pallas_call(
        _mm_res_kernel,
        grid=(m // tm, n // tn),
        in_specs=[pl.BlockSpec((tm, k), lambda i, j: (i, 0)), pl.BlockSpec((k, tn), lambda i, j: (0, j)),
                  pl.BlockSpec((tm, tn), lambda i, j: (i, j))],
        out_specs=pl.BlockSpec((tm, tn), lambda i, j: (i, j)),
        out_shape=jax.ShapeDtypeStruct((m, n), F32),
        compiler_params=_params("parallel", "arbitrary"),
        name="matmul_residual",
    )(a, w, x)


def _swiglu_kernel(x_ref, g_ref, wg_ref, wu_ref, o_ref, h_scr):
    @pl.when(pl.program_id(1) == 0)
    def _():
        h_scr[...] = _rms_rows(x_ref[...], g_ref[...]).astype(BF16)

    h = h_scr[...]
    gate = jnp.dot(h, wg_ref[...], preferred_element_type=F32)
    up = jnp.dot(h, wu_ref[...], preferred_element_type=F32)
    o_ref[...] = (_silu(gate) * up).astype(o_ref.dtype)


def _swiglu(x2d, g, wg, wu, tm=1024, tn=512):
    m, d = x2d.shape
    n = wg.shape[1]
    tm, tn = min(tm, m), min(tn, n)
    return pl.pallas_call(
        _swiglu_kernel,
        grid=(m // tm, n // tn),
        in_specs=[pl.BlockSpec((tm, d), lambda i, j: (i, 0)), pl.BlockSpec((1, d), lambda i, j: (0, 0)),
                  pl.BlockSpec((d, tn), lambda i, j: (0, j)), pl.BlockSpec((d, tn), lambda i, j: (0, j))],
        out_specs=pl.BlockSpec((tm, tn), lambda i, j: (i, j)),
        out_shape=jax.ShapeDtypeStruct((m, n), BF16),
        scratch_shapes=[pltpu.VMEM((tm, d), BF16)],
        compiler_params=_params("parallel", "arbitrary"),
        name="swiglu",
    )(x2d, g.reshape(1, d), wg, wu)


def _mix_kernel(ya_ref, ybt_ref, yc_ref, wa_ref, wb_ref, wc_ref, ga_ref, gb_ref, gc_ref, o_ref):
    def gated(g_ref, prod):
        return _sigmoid(g_ref[...].astype(F32)) * prod

    a = jnp.dot(ya_ref[...], wa_ref[...], preferred_element_type=F32)
    b = lax.dot_general(ybt_ref[...], wb_ref[...], (((0,), (0,)), ((), ())), preferred_element_type=F32)
    c = jnp.dot(yc_ref[...], wc_ref[...], preferred_element_type=F32)
    o_ref[...] = (gated(ga_ref, a) + gated(gb_ref, b) + gated(gc_ref, c)).astype(o_ref.dtype)


def _mix(ya, ybt, yc, wa, wb, wc, z, gate_col0, tm=1024, tn=512):
    m, k = ya.shape
    n = wa.shape[1]
    tm, tn = min(tm, m), min(tn, n)
    gofs = [(gate_col0 + b * n) // tn for b in range(3)]
    y_spec = pl.BlockSpec((tm, k), lambda i, j: (i, 0))
    w_spec = pl.BlockSpec((k, tn), lambda i, j: (0, j))
    g_specs = [pl.BlockSpec((tm, tn), functools.partial(lambda i, j, o: (i, o + j), o=o)) for o in gofs]
    return pl.pallas_call(
        _mix_kernel,
        grid=(m // tm, n // tn),
        in_specs=[y_spec, pl.BlockSpec((k, tm), lambda i, j: (0, i)), y_spec, w_spec, w_spec, w_spec] + g_specs,
        out_specs=pl.BlockSpec((tm, tn), lambda i, j: (i, j)),
        out_shape=jax.ShapeDtypeStruct((m, n), BF16),
        compiler_params=_params("parallel", "arbitrary"),
        name="gated_mix",
    )(ya, ybt, yc, wa, wb, wc, z, z, z)


def _mlstm_kernel(uq_ref, uk_ref, v_ref, o_ref, gcol_ref, grow_ref, bias_ref, cwq_ref, cwk_ref, ng_ref,
                  y_ref, c_scr, n_scr, m_scr, qbuf, kbuf):
    c = pl.program_id(2)
    L = uq_ref.shape[1]

    @pl.when(c == 0)
    def _():
        c_scr[...] = jnp.zeros_like(c_scr)
        n_scr[...] = jnp.zeros_like(n_scr)
        m_scr[...] = jnp.zeros_like(m_scr)
        qbuf[L:L + 8, :] = jnp.zeros((8, qbuf.shape[1]), F32)
        kbuf[L:L + 8, :] = jnp.zeros((8, kbuf.shape[1]), F32)

    def conv_silu(u_ref, buf, cw_ref):
        buf[0:8, :] = buf[L:L + 8, :]
        buf[8:L + 8, :] = u_ref[0].astype(F32)
        cw = cw_ref[...]
        y = cw[0:1, :] * buf[5:5 + L, :]
        for j in range(1, M_CONV):
            y = y + cw[j:j + 1, :] * buf[5 + j:5 + j + L, :]
        return _silu(y)

    q = conv_silu(uq_ref, qbuf, cwq_ref)
    k = conv_silu(uk_ref, kbuf, cwk_ref) * (M_HEAD_DIM ** -0.5)
    qb = q.astype(BF16)
    kb = k.astype(BF16)
    vb = v_ref[0]

    gcol = gcol_ref[0, 0]
    grow = grow_ref[0, 0]
    bias = bias_ref[0]
    ic = gcol[:, 0:1] + bias[:, 0:1]
    fc = _log_sigmoid(gcol[:, 1:2] + bias[:, 1:2])
    ir = grow[0:1, :] + bias[:, 0:1]
    fr = _log_sigmoid(grow[1:2, :] + bias[:, 1:2])

    row = lax.broadcasted_iota(jnp.int32, (L, L), 0)
    col = lax.broadcasted_iota(jnp.int32, (L, L), 1)
    tri = col <= row
    b_col = jnp.sum(jnp.where(tri, fr, 0.0), axis=1, keepdims=True)
    b_row = jnp.sum(jnp.where(row <= col, fc, 0.0), axis=0, keepdims=True)
    m_prev = m_scr[0:1, 0:1]

    log_d = jnp.where(tri, b_col - b_row + ir, NEG_BIG)
    log_inter = b_col + m_prev
    m_t = jnp.maximum(log_inter, jnp.max(log_d, axis=1, keepdims=True))
    w_intra = jnp.exp(log_d - m_t)
    w_inter = jnp.exp(log_inter - m_t)

    s = lax.dot_general(qb, kb, (((1,), (1,)), ((), ())), preferred_element_type=F32) * w_intra
    c_state = c_scr[...]
    n_state = n_scr[0:1, :]
    num = (jnp.dot(s.astype(BF16), vb, preferred_element_type=F32)
           + w_inter * jnp.dot(qb, c_state.astype(BF16), preferred_element_type=F32))
    den = jnp.sum(s, axis=1, keepdims=True) + w_inter * jnp.sum(q * n_state, axis=1, keepdims=True)
    h = num / jnp.maximum(jnp.abs(den), jnp.exp(-m_t))

    b_last = jnp.sum(fc, axis=0, keepdims=True)
    log_g = b_last - b_col + ic
    m_new = jnp.maximum(b_last + m_prev, jnp.max(log_g, axis=0, keepdims=True))
    decay = jnp.exp(b_last + m_prev - m_new)
    wk = k * jnp.exp(log_g - m_new)
    c_scr[...] = decay * c_state + lax.dot_general(wk.astype(BF16), vb, (((0,), (0,)), ((), ())),
                                                   preferred_element_type=F32)
    n_scr[0:1, :] = decay * n_state + jnp.sum(wk, axis=0, keepdims=True)
    m_scr[...] = jnp.broadcast_to(m_new, m_scr.shape)

    hn = _rms_rows(h, ng_ref[0])
    y_ref[0] = (_sigmoid(o_ref[0].astype(F32)) * hn).astype(y_ref.dtype)


def _mlstm(z3, gcol, grow, gbias, conv_w, norm_g, batch, seq):
    L = min(MLSTM_CHUNK, seq)
    hd = M_HEAD_DIM
    nh = M_HEADS

    def zspec(col0):
        return pl.BlockSpec((1, L, hd), functools.partial(lambda b, h, c, o: (b, c, o + h), o=col0 // hd))

    return pl.pallas_call(
        _mlstm_kernel,
        grid=(batch, nh, seq // L),
        in_specs=[zspec(Z_MQK), zspec(Z_MQK + nh * hd), zspec(Z_MV), zspec(Z_MO),
                  pl.BlockSpec((1, 1, L, 2), lambda b, h, c: (b, h, c, 0)),
                  pl.BlockSpec((1, 1, 2, L), lambda b, h, c: (b, h, 0, c)),
                  pl.BlockSpec((1, 1, 2), lambda b, h, c: (h, 0, 0)),
                  pl.BlockSpec((M_CONV, hd), lambda b, h, c: (0, h)),
                  pl.BlockSpec((M_CONV, hd), lambda b, h, c: (0, nh + h)),
                  pl.BlockSpec((1, 1, hd), lambda b, h, c: (h, 0, 0))],
        out_specs=pl.BlockSpec((1, L, hd), lambda b, h, c: (b, c, h)),
        out_shape=jax.ShapeDtypeStruct((batch, seq, nh * hd), BF16),
        scratch_shapes=[pltpu.VMEM((hd, hd), F32), pltpu.VMEM((8, hd), F32), pltpu.VMEM((8, 128), F32),
                        pltpu.VMEM((L + 8, hd), F32), pltpu.VMEM((L + 8, hd), F32)],
        compiler_params=_params("parallel", "parallel", "arbitrary"),
        name="mlstm",
    )(z3, z3, z3, z3, gcol, grow, gbias, conv_w, conv_w, norm_g.reshape(nh, 1, hd))


def _xattn_kernel(q_ref, k_ref, v_ref, gq_ref, gk_ref, o_ref):
    hd = C_HEAD_DIM
    for h in range(C_HEADS):
        sl = slice(h * hd, (h + 1) * hd)
        q = _rms_rows(q_ref[0, :, sl].astype(F32), gq_ref[...]) * (hd ** -0.5)
        k = _rms_rows(k_ref[0, :, sl].astype(F32), gk_ref[...])
        s = lax.dot_general(q.astype(BF16), k.astype(BF16), (((1,), (1,)), ((), ())), preferred_element_type=F32)
        e = jnp.exp(s - jnp.max(s, axis=-1, keepdims=True))
        p = e / jnp.sum(e, axis=-1, keepdims=True)
        o = jnp.dot(p.astype(BF16), v_ref[0, :, sl], preferred_element_type=F32)
        o_ref[0, :, sl] = o.astype(o_ref.dtype)


def _xattn(z3, k, v, gq, gk, batch, seq, tq=512):
    tq = min(tq, seq)
    mlen, w = k.shape[1], k.shape[2]
    return pl.pallas_call(
        _xattn_kernel,
        grid=(batch, seq // tq),
        in_specs=[pl.BlockSpec((1, tq, w), lambda b, i: (b, i, Z_CQ // w)),
                  pl.BlockSpec((1, mlen, w), lambda b, i: (b, 0, 0)),
                  pl.BlockSpec((1, mlen, w), lambda b, i: (b, 0, 0)),
                  pl.BlockSpec((1, C_HEAD_DIM), lambda b, i: (0, 0)),
                  pl.BlockSpec((1, C_HEAD_DIM), lambda b, i: (0, 0))],
        out_specs=pl.BlockSpec((1, tq, w), lambda b, i: (b, i, 0)),
        out_shape=jax.ShapeDtypeStruct((batch, seq, w), BF16),
        compiler_params=_params("parallel", "arbitrary"),
        name="memory_cross_attention",
    )(z3, k, v, gq.reshape(1, -1), gk.reshape(1, -1))


def _rope_rows(xn, cos, sin_signed):
    half = N_HEAD_DIM // 2
    rot = jnp.concatenate([xn[:, half:], xn[:, :half]], axis=-1)
    return xn * cos + rot * sin_signed


def _knorm_rope_kernel(x_ref, g_ref, cos_ref, sin_ref, o_ref, *, n_blocks):
    hd = N_HEAD_DIM
    ts = x_ref.shape[1]
    if n_blocks:
        pos = pl.program_id(1) * ts + lax.broadcasted_iota(jnp.int32, (ts, n_blocks), 0)
        blk = lax.broadcasted_iota(jnp.int32, (ts, n_blocks), 1)
        onehot = jnp.where(pos // SEL_BLOCK == blk, 1.0, 0.0).astype(o_ref.dtype)
    for g in range(N_GROUPS):
        x = x_ref[0, :, g * hd:(g + 1) * hd].astype(F32)
        k = _rope_rows(_rms_rows(x, g_ref[...]), cos_ref[...], sin_ref[...]).astype(o_ref.dtype)
        if n_blocks:
            o_ref[0, g, :, 0:n_blocks] = onehot
        o_ref[0, g, :, n_blocks:n_blocks + hd] = k


def _knorm_rope(z3, col0, g, cos, sin_signed, n_blocks, ts=1024):
    b, s, _ = z3.shape
    hd, w = N_HEAD_DIM, N_GROUPS * N_HEAD_DIM
    ts = min(ts, s)
    return pl.pallas_call(
        functools.partial(_knorm_rope_kernel, n_blocks=n_blocks),
        grid=(b, s // ts),
        in_specs=[pl.BlockSpec((1, ts, w), lambda i, j: (i, j, col0 // w)), pl.BlockSpec((1, hd), lambda i, j: (0, 0)),
                  pl.BlockSpec((ts, hd), lambda i, j: (j, 0)), pl.BlockSpec((ts, hd), lambda i, j: (j, 0))],
        out_specs=pl.BlockSpec((1, N_GROUPS, ts, n_blocks + hd), lambda i, j: (i, 0, j, 0)),
        out_shape=jax.ShapeDtypeStruct((b, N_GROUPS, s, n_blocks + hd), BF16),
        compiler_params=_params("parallel", "arbitrary"),
        name="key_norm_rope",
    )(z3, g.reshape(1, hd), cos, sin_signed)


def _compress_kernel(sub_ref, pe_ref, w1_ref, w2_ref, g_ref, cos_ref, sin_ref, o_ref, *, is_key):
    ns = sub_ref.shape[1]
    sub = sub_ref[0].astype(F32)
    lo = jnp.dot((sub + pe_ref[0:1, :]).astype(BF16), w1_ref[0], preferred_element_type=F32)
    hi = jnp.dot((sub + pe_ref[1:2, :]).astype(BF16), w1_ref[1], preferred_element_type=F32)
    hid = _silu(lo + pltpu.roll(hi, shift=ns - 1, axis=0))
    out = jnp.dot(hid.astype(BF16), w2_ref[...], preferred_element_type=F32)
    if is_key:
        out = _rope_rows(_rms_rows(out, g_ref[...]), cos_ref[...], sin_ref[...])
    o_ref[0] = out.astype(o_ref.dtype)


def _compress(u, pe, w1, w2, g, cos, sin_signed, is_key):
    n, s, hd = u.shape
    ns = s // CMP_STRIDE
    width = CMP_STRIDE * hd
    sub = u.reshape(n, ns, width)
    pe2 = pe.reshape(CMP_LEN // CMP_STRIDE, width)
    w1s = w1.reshape(CMP_LEN // CMP_STRIDE, width, CMP_HIDDEN).astype(BF16)
    return pl.pallas_call(
        functools.partial(_compress_kernel, is_key=is_key),
        grid=(n,),
        in_specs=[pl.BlockSpec((1, ns, width), lambda i: (i, 0, 0)),
                  pl.BlockSpec(pe2.shape, lambda i: (0, 0)),
                  pl.BlockSpec(w1s.shape, lambda i: (0, 0, 0)),
                  pl.BlockSpec(w2.shape, lambda i: (0, 0)),
                  pl.BlockSpec((1, hd), lambda i: (0, 0)),
                  pl.BlockSpec((ns, hd), lambda i: (0, 0)),
                  pl.BlockSpec((ns, hd), lambda i: (0, 0))],
        out_specs=pl.BlockSpec((1, ns, hd), lambda i: (i, 0, 0)),
        out_shape=jax.ShapeDtypeStruct((n, ns, hd), BF16),
        compiler_params=_params("parallel"),
        name="compress_key" if is_key else "compress_value",
    )(sub, pe2, w1s, w2.astype(BF16), g.reshape(1, hd), cos, sin_signed)


def _nsa_kernel(q_ref, gate_ref, kc_ref, vct_ref, ks_ref, vst_ref, kw_ref, vwt_ref, gq_ref, cos_ref, sin_ref,
                wimp_ref, y_ref, s_c, e_c, s_w, e_w, s_a, s_b, e_a, e_b, *, seq):
    tq = q_ref.shape[1]
    hd = N_HEAD_DIM
    nb = seq // SEL_BLOCK
    top = min(SEL_TOP, nb)
    ncol = N_REP * tq
    t0 = pl.program_id(2) * tq

    def lanes(parts):
        return jnp.concatenate(parts, axis=1)

    def head_cols(a, r):
        return a[:, r * tq:(r + 1) * tq]

    def with_ones(v):
        n = v.shape[1]
        tail = jnp.where(lax.broadcasted_iota(jnp.int32, (V_AUG_ROWS, n), 0) == 0, 1.0, 0.0).astype(v.dtype)
        return jnp.concatenate([v, tail], axis=0)

    def chunk_scores(src, c, valid_fn):
        sc = src[c * NSA_RC:(c + 1) * NSA_RC, :]
        if valid_fn is None:
            return sc
        valid = valid_fn(c * NSA_RC + lax.broadcasted_iota(jnp.int32, (NSA_RC, 1), 0))
        return lanes([jnp.where(valid, head_cols(sc, r), NEG_BIG) for r in range(N_REP)])

    def col_max(src, rows, valid_fn):
        mx = jnp.full((8, ncol), NEG_BIG, F32)
        for c in range(rows // NSA_RC):
            sc = chunk_scores(src, c, valid_fn)
            mx = jnp.maximum(mx, jnp.max(sc.reshape(NSA_RC // 8, 8, ncol), axis=0))
        return jnp.max(mx, axis=0, keepdims=True)

    def put_weights(src, dst, rows, valid_fn, m):
        for c in range(rows // NSA_RC):
            sc = chunk_scores(src, c, valid_fn)
            dst[c * NSA_RC:(c + 1) * NSA_RC, :] = jnp.exp2((sc - m).astype(BF16))

    q = lanes([q_ref[r * hd:(r + 1) * hd, :].astype(F32) for r in range(N_REP)])
    qn = q * lax.rsqrt(jnp.mean(q * q, axis=0, keepdims=True) + EPS) * gq_ref[...]
    cos = lanes([cos_ref[...]] * N_REP)
    sin = lanes([sin_ref[...]] * N_REP)
    rot = jnp.concatenate([qn[hd // 2:], qn[:hd // 2]], axis=0)
    qr = ((qn * cos + rot * sin) * (hd ** -0.5 * LOG2_E)).astype(BF16)
    t_q = t0 + lax.broadcasted_iota(jnp.int32, (1, tq), 1)
    tcol = lanes([t_q] * N_REP)

    nc = kc_ref.shape[2]
    s_c[...] = jnp.dot(kc_ref[0, 0], qr, preferred_element_type=F32)
    valid_c = lambda n: n * CMP_STRIDE + (CMP_LEN - 1) <= t_q
    put_weights(s_c, e_c, nc, valid_c, col_max(s_c, nc, valid_c))
    e = e_c[...]
    o_c = jnp.dot(vct_ref[0, 0], e, preferred_element_type=F32)
    inv_c = jnp.where(tcol >= CMP_LEN - 1, 1.0 / o_c[hd:hd + 1], 0.0)
    o_c = o_c[0:hd] * inv_c

    imp_h = jnp.dot(wimp_ref[...], e, preferred_element_type=F32) * inv_c
    imp = head_cols(imp_h, 0)
    for r in range(1, N_REP):
        imp = imp + head_cols(imp_h, r)

    span = WINDOW + tq
    ws0 = pl.multiple_of(jnp.maximum(t0 - WINDOW, 0), tq)
    s_w[...] = jnp.dot(kw_ref[0, 0, pl.ds(ws0, span), :], qr, preferred_element_type=F32)
    valid_w = lambda k: (ws0 + k <= t_q) & (ws0 + k > t_q - WINDOW)
    put_weights(s_w, e_w, span, valid_w, col_max(s_w, span, valid_w))
    o_w = jnp.dot(with_ones(vwt_ref[:, pl.ds(ws0, span)]), e_w[...], preferred_element_type=F32)
    o_w = o_w[0:hd] / o_w[hd:hd + 1]

    blk = lax.broadcasted_iota(jnp.int32, (nb, tq), 0)
    cur = (t0 + lax.broadcasted_iota(jnp.int32, (nb, tq), 1)) // SEL_BLOCK
    causal_b = blk <= cur
    forced = (blk == 0) | (blk == cur) | (blk == cur - 1)
    score = jnp.where(forced, FORCE_SCORE, jnp.where(causal_b, imp, -1.0))
    taken = -3.0e38

    def pick_by_value(sc):
        for _ in range(top - 2):
            sc = jnp.where(sc == jnp.max(sc, axis=0, keepdims=True), taken, sc)
        return sc

    def pick_by_value_then_index(sc):
        for _ in range(top):
            mx = jnp.max(sc, axis=0, keepdims=True)
            idx = jnp.min(jnp.where(sc == mx, blk, nb), axis=0, keepdims=True)
            sc = jnp.where(blk == idx, taken, sc)
        return sc

    fast = pick_by_value(score)
    n_taken = jnp.sum(jnp.where((fast == taken) & causal_b, 1.0, 0.0), axis=0, keepdims=True)
    n_want = jnp.minimum(cur[0:1, :] + 1, top).astype(F32)
    tied = jnp.max(jnp.abs(n_taken - n_want)) > 0.0
    picked = lax.cond(tied, lambda: pick_by_value_then_index(score), lambda: fast)
    bias = jnp.where((picked == taken) & causal_b, 0.0, BLOCK_BIAS)
    q_aug = jnp.concatenate([lanes([bias.astype(BF16)] * N_REP), qr], axis=0)

    tk = min(NSA_TK, seq)
    last = (t0 + tq - 1) // tk
    s_bufs, e_bufs = (s_a, s_b), (e_a, e_b)

    def scores(t, par):
        ks0 = pl.multiple_of(t * tk, tk)
        s_bufs[par][...] = jnp.dot(ks_ref[0, 0, pl.ds(ks0, tk), :], q_aug, preferred_element_type=F32)

    def weights(t, par, m, masked):
        valid_s = (lambda k: t * tk + k <= t_q) if masked else None
        m_new = jnp.maximum(m, col_max(s_bufs[par], tk, valid_s))
        put_weights(s_bufs[par], e_bufs[par], tk, valid_s, m_new)
        return m_new, jnp.exp2(m - m_new)

    def values(t, par, alpha, acc):
        ks0 = pl.multiple_of(jnp.maximum(t, 0) * tk, tk)
        return alpha * acc + jnp.dot(with_ones(vst_ref[:, pl.ds(ks0, tk)]), e_bufs[par][...],
                                     preferred_element_type=F32)

    def step(t, par, carry, masked, more):
        m, alpha, acc = carry
        if more:
            scores(t + 1, 1 - par)
        acc = values(t - 1, 1 - par, alpha, acc)
        m, alpha = weights(t, par, m, masked)
        return m, alpha, acc

    def pair_body(j, carry):
        return step(2 * j + 1, 1, step(2 * j, 0, carry, False, True), False, True)

    def tail_even(carry):
        m, alpha, acc = step(last, 0, carry, True, False)
        return values(last, 0, alpha, acc)

    def tail_odd(carry):
        m, alpha, acc = step(last, 1, step(last - 1, 0, carry, False, True), True, False)
        return values(last, 1, alpha, acc)

    scores(0, 0)
    e_b[...] = jnp.zeros(e_b.shape, e_b.dtype)
    carry = (jnp.full((1, ncol), NEG_BIG, F32), jnp.ones((1, ncol), F32), jnp.zeros((hd + V_AUG_ROWS, ncol), F32))
    carry = lax.fori_loop(0, last // 2, pair_body, carry)
    acc = lax.cond(last % 2 == 0, tail_even, tail_odd, carry)
    o_s = acc[0:hd] / acc[hd:hd + 1]

    g = _sigmoid(gate_ref[0, 0])
    gc, gs, gw = (lanes([g[j, r:r + 1, :] for r in range(N_REP)]) for j in range(3))
    out = gc * o_c + gs * o_s + gw * o_w
    for r in range(N_REP):
        y_ref[r * hd:(r + 1) * hd, :] = head_cols(out, r).astype(y_ref.dtype)


def _nsa(zt, gates, kc, vct, ks_aug, kw, gq, cos_t, sin_t, wimp_t, batch, seq):
    g, hd = N_GROUPS, N_HEAD_DIM
    tq = min(NSA_TQ, seq)
    tk = min(NSA_TK, seq)
    nq = seq // tq
    ncol = N_REP * tq
    nc = kc.shape[2]
    nb = seq // SEL_BLOCK
    full = lambda *shape: pl.BlockSpec((1, 1) + shape, lambda bi, gi, qi: (bi, gi) + (0,) * len(shape))
    vrow = lambda row0: pl.BlockSpec((hd, seq), functools.partial(lambda bi, gi, qi, o: (o + gi, bi), o=row0 // hd))
    qspec = pl.BlockSpec((N_REP * hd, tq), lambda bi, gi, qi: (ZT_Q // (N_REP * hd) + gi, bi * nq + qi))
    return pl.pallas_call(
        functools.partial(_nsa_kernel, seq=seq),
        grid=(batch, g, nq),
        in_specs=[qspec,
                  pl.BlockSpec((1, 1, 3, N_REP, tq), lambda bi, gi, qi: (bi, gi, 0, 0, qi)),
                  full(nc, hd), full(hd + V_AUG_ROWS, nc), full(seq, nb + hd), vrow(ZT_VS), full(seq, hd),
                  vrow(ZT_VW),
                  pl.BlockSpec((hd, 1), lambda bi, gi, qi: (0, 0)),
                  pl.BlockSpec((hd, tq), lambda bi, gi, qi: (0, qi)),
                  pl.BlockSpec((hd, tq), lambda bi, gi, qi: (0, qi)),
                  pl.BlockSpec((nb, nc), lambda bi, gi, qi: (0, 0))],
        out_specs=pl.BlockSpec((N_REP * hd, tq), lambda bi, gi, qi: (gi, bi * nq + qi)),
        out_shape=jax.ShapeDtypeStruct((N_HEADS * hd, batch * seq), BF16),
        scratch_shapes=[pltpu.VMEM((nc, ncol), F32), pltpu.VMEM((nc, ncol), BF16),
                        pltpu.VMEM((WINDOW + tq, ncol), F32), pltpu.VMEM((WINDOW + tq, ncol), BF16),
                        pltpu.VMEM((tk, ncol), F32), pltpu.VMEM((tk, ncol), F32),
                        pltpu.VMEM((tk, ncol), BF16), pltpu.VMEM((tk, ncol), BF16)],
        compiler_params=_params("parallel", "parallel", "arbitrary"),
        name="nsa_attention",
    )(zt, gates, kc, vct, ks_aug, zt, kw, zt, gq.reshape(hd, 1), cos_t, sin_t, wimp_t)


def _rope_tables(pos):
    half = N_HEAD_DIM // 2
    inv = jnp.power(ROPE_THETA, -jnp.arange(half, dtype=F32) * 2.0 / N_HEAD_DIM)
    ang = pos.astype(F32)[:, None] * inv[None, :]
    cos, sin = jnp.cos(ang), jnp.sin(ang)
    return jnp.concatenate([cos, cos], axis=-1), jnp.concatenate([-sin, sin], axis=-1)


def _importance_matrix(nb, nc_pad):
    per = SEL_BLOCK // CMP_STRIDE
    j = np.arange(nb)[:, None]
    n = np.arange(nc_pad)[None, :]
    w = np.zeros((nb, nc_pad), np.float32)
    for d in range(CMP_LEN // CMP_STRIDE):
        w += ((n + d >= per * j) & (n + d <= per * j + per - 1)).astype(np.float32)
    return jnp.asarray(w, BF16)


def _split_w_in(w_in):
    d = w_in.shape[0]
    sizes = (2048, 1024, 1024, 4, 4, 1024, 256, 256, 256, 256, 256, 256, 48, 1024, 2048, 2048, 2048)
    offs = np.concatenate([[0], np.cumsum(sizes)])
    part = lambda i: w_in[:, offs[i]:offs[i + 1]]
    tok = jnp.concatenate([part(i) for i in (0, 1, 2, 13, 14, 15, 16, 6, 7, 8, 10)], axis=1).astype(BF16)
    feat = jnp.concatenate([part(i) for i in (5, 9, 11)], axis=1).T.astype(BF16)
    small = jnp.concatenate([part(3), part(4), part(12), jnp.zeros((d, 128 - 56), w_in.dtype)], axis=1).astype(BF16)
    return tok, feat, small


def _layer(x, mem, norm_mix_g, norm_mem_g, norm_ffn_g, w_in, m_conv_w, m_i_bias, m_f_bias, m_norm_g,
           n_q_norm_g, n_kc_norm_g, n_ks_norm_g, n_kw_norm_g, n_cmp_pe_k, n_cmp_w1_k, n_cmp_w2_k,
           n_cmp_pe_v, n_cmp_w1_v, n_cmp_w2_v, c_q_norm_g, c_k_norm_g, w_mem_k, w_mem_v,
           w_up_a, w_up_b, w_up_c, w_out, w_ffn_gate, w_ffn_up, w_ffn_down):
    B, S, D = x.shape
    M = B * S
    G, R, hd = N_GROUPS, N_REP, N_HEAD_DIM
    x2 = x.reshape(M, D)

    w_tok, w_feat, w_small = _split_w_in(w_in)
    z, zs, h = _in_proj(x2, norm_mix_g, w_tok, w_small)
    zt = _mm_feature_major(w_feat, h)
    z3 = z.reshape(B, S, Z_WIDTH)
    zs3 = zs.reshape(B, S, 128)

    gif = jnp.stack([zs3[..., 0:4], zs3[..., 4:8]], axis=-1)
    gcol = gif.transpose(0, 2, 1, 3)
    grow = gif.transpose(0, 2, 3, 1)
    gbias = jnp.stack([m_i_bias, m_f_bias], axis=-1).reshape(M_HEADS, 1, 2).astype(F32)
    y_a = _mlstm(z3, gcol, grow, gbias, m_conv_w, m_norm_g, B, S)

    pos = jnp.arange(S, dtype=jnp.int32)
    cos, sin_s = _rope_tables(pos)
    nc_pad = S // CMP_STRIDE
    cmp_end = jnp.arange(nc_pad, dtype=jnp.int32) * CMP_STRIDE + CMP_LEN - 1
    cos_c, sin_c = _rope_tables(cmp_end)
    nb = S // SEL_BLOCK

    def head_major(col0):
        u = z3[..., col0:col0 + G * hd].reshape(B, S, G, hd)
        return u.transpose(0, 2, 1, 3).reshape(B * G, S, hd)

    kc = _compress(head_major(Z_KC), n_cmp_pe_k, n_cmp_w1_k, n_cmp_w2_k, n_kc_norm_g, cos_c, sin_c, True)
    vc = _compress(head_major(Z_VC), n_cmp_pe_v, n_cmp_w1_v, n_cmp_w2_v, n_kc_norm_g, cos_c, sin_c, False)
    vct = vc.reshape(B, G, nc_pad, hd).transpose(0, 1, 3, 2)
    ones_rows = jnp.concatenate([jnp.ones((B, G, 1, nc_pad), BF16), jnp.zeros((B, G, V_AUG_ROWS - 1, nc_pad), BF16)],
                                axis=2)
    vct = jnp.concatenate([vct, ones_rows], axis=2)
    ks_aug = _knorm_rope(z3, Z_KS, n_ks_norm_g, cos, sin_s, nb)
    kw = _knorm_rope(z3, Z_KW, n_kw_norm_g, cos, sin_s, 0)
    gates = zs3[..., 8:8 + 3 * N_HEADS].reshape(B, S, G, R, 3).transpose(0, 2, 4, 3, 1)
    y_bt = _nsa(zt, gates, kc.reshape(B, G, nc_pad, hd), vct, ks_aug, kw, n_q_norm_g, cos.T, sin_s.T,
                _importance_matrix(nb, nc_pad), B, S)

    mlen = mem.shape[1]
    mem_n = _rmsnorm(mem.reshape(B * mlen, D), norm_mem_g)
    k_mem = _mm(mem_n, w_mem_k.astype(BF16), BF16).reshape(B, mlen, -1)
    v_mem = _mm(mem_n, w_mem_v.astype(BF16), BF16).reshape(B, mlen, -1)
    y_c = _xattn(z3, k_mem, v_mem, c_q_norm_g, c_k_norm_g, B, S)

    mix = _mix(y_a.reshape(M, -1), y_bt, y_c.reshape(M, -1),
               w_up_a.astype(BF16), w_up_b.astype(BF16), w_up_c.astype(BF16), z, Z_GATES)
    x2 = _mm_residual(mix, w_out.astype(BF16), x2)

    act = _swiglu(x2, norm_ffn_g, w_ffn_gate.astype(BF16), w_ffn_up.astype(BF16))
    x2 = _mm_residual(act, w_ffn_down.astype(BF16), x2)
    return x2.reshape(B, S, D)


def kernel(x, mem, norm_mix_g, norm_mem_g, norm_ffn_g, w_in, m_conv_w, m_i_bias, m_f_bias, m_norm_g, n_q_norm_g, n_kc_norm_g, n_ks_norm_g, n_kw_norm_g, n_cmp_pe_k, n_cmp_w1_k, n_cmp_w2_k, n_cmp_pe_v, n_cmp_w1_v, n_cmp_w2_v, c_q_norm_g, c_k_norm_g, w_mem_k, w_mem_v, w_up_a, w_up_b, w_up_c, w_out, w_ffn_gate, w_ffn_up, w_ffn_down):
    stacked = (norm_mix_g, norm_mem_g, norm_ffn_g, w_in, m_conv_w, m_i_bias, m_f_bias, m_norm_g, n_q_norm_g,
               n_kc_norm_g, n_ks_norm_g, n_kw_norm_g, n_cmp_pe_k, n_cmp_w1_k, n_cmp_w2_k, n_cmp_pe_v, n_cmp_w1_v,
               n_cmp_w2_v, c_q_norm_g, c_k_norm_g, w_mem_k, w_mem_v, w_up_a, w_up_b, w_up_c, w_out, w_ffn_gate,
               w_ffn_up, w_ffn_down)
    for layer in range(w_in.shape[0]):
        x = _layer(x, mem, *(p[layer] for p in stacked))
    return x
```

```python
import functools

import numpy as np
import jax
import jax.numpy as jnp
from jax import lax
from jax.experimental import pallas as pl
from jax.experimental.pallas import tpu as pltpu

F32 = jnp.float32
BF16 = jnp.bfloat16

EPS = 1e-6
ROPE_THETA = 10000.0
M_HEADS = 4
M_HEAD_DIM = 256
M_CONV = 4
N_HEADS = 16
N_GROUPS = 4
N_REP = N_HEADS // N_GROUPS
N_HEAD_DIM = 64
CMP_LEN = 32
CMP_STRIDE = 16
CMP_HIDDEN = 128
SEL_BLOCK = 64
SEL_TOP = 16
WINDOW = 512
FORCE_SCORE = 1.0e4
C_HEADS = 4
C_HEAD_DIM = 256

LOG2_E = 1.4426950408889634
V_AUG_ROWS = 16
NEG_BIG = -1.0e30
BLOCK_BIAS = -1.0e9
VMEM_LIMIT = 56 * 1024 * 1024

MLSTM_CHUNK = 256
NSA_TQ = 256
NSA_TK = 512
NSA_RC = 32

Z_MQK, Z_MV, Z_MO, Z_CQ, Z_GATES, Z_KC, Z_VC, Z_KS, Z_KW, Z_WIDTH = (
    0, 2048, 3072, 4096, 5120, 11264, 11520, 11776, 12032, 12288)
ZT_Q, ZT_VS, ZT_VW, ZT_ROWS = 0, 1024, 1280, 1536


def _params(*sem):
    return pltpu.CompilerParams(dimension_semantics=sem, vmem_limit_bytes=VMEM_LIMIT)


def _sigmoid(x):
    return 1.0 / (1.0 + jnp.exp(-x))


def _silu(x):
    return x * _sigmoid(x)


def _log_sigmoid(x):
    return jnp.minimum(x, 0.0) - jnp.log1p(jnp.exp(-jnp.abs(x)))


def _rms_rows(x, g):
    return x * lax.rsqrt(jnp.mean(x * x, axis=-1, keepdims=True) + EPS) * g


def _rmsnorm_kernel(x_ref, g_ref, o_ref):
    o_ref[...] = _rms_rows(x_ref[...], g_ref[...]).astype(o_ref.dtype)


def _rmsnorm(x2d, g, tm=512):
    m, d = x2d.shape
    tm = min(tm, m)
    return pl.pallas_call(
        _rmsnorm_kernel,
        grid=(m // tm,),
        in_specs=[pl.BlockSpec((tm, d), lambda i: (i, 0)), pl.BlockSpec((1, d), lambda i: (0, 0))],
        out_specs=pl.BlockSpec((tm, d), lambda i: (i, 0)),
        out_shape=jax.ShapeDtypeStruct((m, d), BF16),
        compiler_params=_params("parallel"),
        name="rmsnorm",
    )(x2d, g.reshape(1, d))


def _mm_kernel(a_ref, w_ref, o_ref):
    o_ref[...] = jnp.dot(a_ref[...], w_ref[...], preferred_element_type=F32).astype(o_ref.dtype)


def _mm(a, w, out_dtype, tm=1024, tn=512):
    m, k = a.shape
    n = w.shape[1]
    tm, tn = min(tm, m), min(tn, n)
    return pl.pallas_call(
        _mm_kernel,
        grid=(m // tm, n // tn),
        in_specs=[pl.BlockSpec((tm, k), lambda i, j: (i, 0)), pl.BlockSpec((k, tn), lambda i, j: (0, j))],
        out_specs=pl.BlockSpec((tm, tn), lambda i, j: (i, j)),
        out_shape=jax.ShapeDtypeStruct((m, n), out_dtype),
        compiler_params=_params("parallel", "arbitrary"),
        name="matmul",
    )(a, w)


def _in_proj_kernel(x_ref, g_ref, w_ref, ws_ref, z_ref, zs_ref, h_ref):
    @pl.when(pl.program_id(1) == 0)
    def _():
        h = _rms_rows(x_ref[...], g_ref[...]).astype(BF16)
        h_ref[...] = h
        zs_ref[...] = jnp.dot(h, ws_ref[...], preferred_element_type=F32)

    z_ref[...] = jnp.dot(h_ref[...], w_ref[...], preferred_element_type=F32).astype(z_ref.dtype)


def _in_proj(x2d, g, w, w_small, tm=1024, tn=1024):
    m, d = x2d.shape
    n, ns = w.shape[1], w_small.shape[1]
    tm, tn = min(tm, m), min(tn, n)
    return pl.pallas_call(
        _in_proj_kernel,
        grid=(m // tm, n // tn),
        in_specs=[pl.BlockSpec((tm, d), lambda i, j: (i, 0)), pl.BlockSpec((1, d), lambda i, j: (0, 0)),
                  pl.BlockSpec((d, tn), lambda i, j: (0, j)), pl.BlockSpec((d, ns), lambda i, j: (0, 0))],
        out_specs=[pl.BlockSpec((tm, tn), lambda i, j: (i, j)), pl.BlockSpec((tm, ns), lambda i, j: (i, 0)),
                   pl.BlockSpec((tm, d), lambda i, j: (i, 0))],
        out_shape=[jax.ShapeDtypeStruct((m, n), BF16), jax.ShapeDtypeStruct((m, ns), F32),
                   jax.ShapeDtypeStruct((m, d), BF16)],
        compiler_params=_params("parallel", "arbitrary"),
        name="in_proj",
    )(x2d, g.reshape(1, d), w, w_small)


def _mm_nt_kernel(w_ref, h_ref, o_ref):
    o_ref[...] = lax.dot_general(w_ref[...], h_ref[...], (((1,), (1,)), ((), ())),
                                 preferred_element_type=F32).astype(o_ref.dtype)


def _mm_feature_major(w_t, h, tm=1024, tn=1536):
    n, k = w_t.shape
    m = h.shape[0]
    tm, tn = min(tm, m), min(tn, n)
    return pl.pallas_call(
        _mm_nt_kernel,
        grid=(m // tm, n // tn),
        in_specs=[pl.BlockSpec((tn, k), lambda i, j: (j, 0)), pl.BlockSpec((tm, k), lambda i, j: (i, 0))],
        out_specs=pl.BlockSpec((tn, tm), lambda i, j: (j, i)),
        out_shape=jax.ShapeDtypeStruct((n, m), BF16),
        compiler_params=_params("parallel", "arbitrary"),
        name="in_proj_feature_major",
    )(w_t, h)


def _mm_res_kernel(a_ref, w_ref, x_ref, o_ref):
    o_ref[...] = x_ref[...] + jnp.dot(a_ref[...], w_ref[...], preferred_element_type=F32)


def _mm_residual(a, w, x, tm=512, tn=512):
    m, k = a.shape
    n = w.shape[1]
    tm, tn = min(tm, m), min(tn, n)
    return pl.pallas_call(
        _mm_res_kernel,
        grid=(m // tm, n // tn),
        in_specs=[pl.BlockSpec((tm, k), lambda i, j: (i, 0)), pl.BlockSpec((k, tn), lambda i, j: (0, j)),
                  pl.BlockSpec((tm, tn), lambda i, j: (i, j))],
        out_specs=pl.BlockSpec((tm, tn), lambda i, j: (i, j)),
        out_shape=jax.ShapeDtypeStruct((m, n), F32),
        compiler_params=_params("parallel", "arbitrary"),
        name="matmul_residual",
    )(a, w, x)


def _swiglu_kernel(x_ref, g_ref, wg_ref, wu_ref, o_ref, h_scr):
    @pl.when(pl.program_id(1) == 0)
    def _():
        h_scr[...] = _rms_rows(x_ref[...], g_ref[...]).astype(BF16)

    h = h_scr[...]
    gate = jnp.dot(h, wg_ref[...], preferred_element_type=F32)
    up = jnp.dot(h, wu_ref[...], preferred_element_type=F32)
    o_ref[...] = (_silu(gate) * up).astype(o_ref.dtype)


def _swiglu(x2d, g, wg, wu, tm=1024, tn=512):
    m, d = x2d.shape
    n = wg.shape[1]
    tm, tn = min(tm, m), min(tn, n)
    return pl.pallas_call(
        _swiglu_kernel,
        grid=(m // tm, n // tn),
        in_specs=[pl.BlockSpec((tm, d), lambda i, j: (i, 0)), pl.BlockSpec((1, d), lambda i, j: (0, 0)),
                  pl.BlockSpec((d, tn), lambda i, j: (0, j)), pl.BlockSpec((d, tn), lambda i, j: (0, j))],
        out_specs=pl.BlockSpec((tm, tn), lambda i, j: (i, j)),
        out_shape=jax.ShapeDtypeStruct((m, n), BF16),
        scratch_shapes=[pltpu.VMEM((tm, d), BF16)],
        compiler_params=_params("parallel", "arbitrary"),
        name="swiglu",
    )(x2d, g.reshape(1, d), wg, wu)


def _mix_kernel(ya_ref, ybt_ref, yc_ref, wa_ref, wb_ref, wc_ref, ga_ref, gb_ref, gc_ref, o_ref):
    def gated(g_ref, prod):
        return _sigmoid(g_ref[...].astype(F32)) * prod

    a = jnp.dot(ya_ref[...], wa_ref[...], preferred_element_type=F32)
    b = lax.dot_general(ybt_ref[...], wb_ref[...], (((0,), (0,)), ((), ())), preferred_element_type=F32)
    c = jnp.dot(yc_ref[...], wc_ref[...], preferred_element_type=F32)
    o_ref[...] = (gated(ga_ref, a) + gated(gb_ref, b) + gated(gc_ref, c)).astype(o_ref.dtype)


def _mix(ya, ybt, yc, wa, wb, wc, z, gate_col0, tm=1024, tn=512):
    m, k = ya.shape
    n = wa.shape[1]
    tm, tn = min(tm, m), min(tn, n)
    gofs = [(gate_col0 + b * n) // tn for b in range(3)]
    y_spec = pl.BlockSpec((tm, k), lambda i, j: (i, 0))
    w_spec = pl.BlockSpec((k, tn), lambda i, j: (0, j))
    g_specs = [pl.BlockSpec((tm, tn), functools.partial(lambda i, j, o: (i, o + j), o=o)) for o in gofs]
    return pl.pallas_call(
        _mix_kernel,
        grid=(m // tm, n // tn),
        in_specs=[y_spec, pl.BlockSpec((k, tm), lambda i, j: (0, i)), y_spec, w_spec, w_spec, w_spec] + g_specs,
        out_specs=pl.BlockSpec((tm, tn), lambda i, j: (i, j)),
        out_shape=jax.ShapeDtypeStruct((m, n), BF16),
        compiler_params=_params("parallel", "arbitrary"),
        name="gated_mix",
    )(ya, ybt, yc, wa, wb, wc, z, z, z)


def _mlstm_kernel(uq_ref, uk_ref, v_ref, o_ref, gcol_ref, grow_ref, bias_ref, cwq_ref, cwk_ref, ng_ref,
                  y_ref, c_scr, n_scr, m_scr, qbuf, kbuf):
    c = pl.program_id(2)
    L = uq_ref.shape[1]

    @pl.when(c == 0)
    def _():
        c_scr[...] = jnp.zeros_like(c_scr)
        n_scr[...] = jnp.zeros_like(n_scr)
        m_scr[...] = jnp.zeros_like(m_scr)
        qbuf[L:L + 8, :] = jnp.zeros((8, qbuf.shape[1]), F32)
        kbuf[L:L + 8, :] = jnp.zeros((8, kbuf.shape[1]), F32)

    def conv_silu(u_ref, buf, cw_ref):
        buf[0:8, :] = buf[L:L + 8, :]
        buf[8:L + 8, :] = u_ref[0].astype(F32)
        cw = cw_ref[...]
        y = cw[0:1, :] * buf[5:5 + L, :]
        for j in range(1, M_CONV):
            y = y + cw[j:j + 1, :] * buf[5 + j:5 + j + L, :]
        return _silu(y)

    q = conv_silu(uq_ref, qbuf, cwq_ref)
    k = conv_silu(uk_ref, kbuf, cwk_ref) * (M_HEAD_DIM ** -0.5)
    qb = q.astype(BF16)
    kb = k.astype(BF16)
    vb = v_ref[0]

    gcol = gcol_ref[0, 0]
    grow = grow_ref[0, 0]
    bias = bias_ref[0]
    ic = gcol[:, 0:1] + bias[:, 0:1]
    fc = _log_sigmoid(gcol[:, 1:2] + bias[:, 1:2])
    ir = grow[0:1, :] + bias[:, 0:1]
    fr = _log_sigmoid(grow[1:2, :] + bias[:, 1:2])

    row = lax.broadcasted_iota(jnp.int32, (L, L), 0)
    col = lax.broadcasted_iota(jnp.int32, (L, L), 1)
    tri = col <= row
    b_col = jnp.sum(jnp.where(tri, fr, 0.0), axis=1, keepdims=True)
    b_row = jnp.sum(jnp.where(row <= col, fc, 0.0), axis=0, keepdims=True)
    m_prev = m_scr[0:1, 0:1]

    log_d = jnp.where(tri, b_col - b_row + ir, NEG_BIG)
    log_inter = b_col + m_prev
    m_t = jnp.maximum(log_inter, jnp.max(log_d, axis=1, keepdims=True))
    w_intra = jnp.exp(log_d - m_t)
    w_inter = jnp.exp(log_inter - m_t)

    s = lax.dot_general(qb, kb, (((1,), (1,)), ((), ())), preferred_element_type=F32) * w_intra
    c_state = c_scr[...]
    n_state = n_scr[0:1, :]
    num = (jnp.dot(s.astype(BF16), vb, preferred_element_type=F32)
           + w_inter * jnp.dot(qb, c_state.astype(BF16), preferred_element_type=F32))
    den = jnp.sum(s, axis=1, keepdims=True) + w_inter * jnp.sum(q * n_state, axis=1, keepdims=True)
    h = num / jnp.maximum(jnp.abs(den), jnp.exp(-m_t))

    b_last = jnp.sum(fc, axis=0, keepdims=True)
    log_g = b_last - b_col + ic
    m_new = jnp.maximum(b_last + m_prev, jnp.max(log_g, axis=0, keepdims=True))
    decay = jnp.exp(b_last + m_prev - m_new)
    wk = k * jnp.exp(log_g - m_new)
    c_scr[...] = decay * c_state + lax.dot_general(wk.astype(BF16), vb, (((0,), (0,)), ((), ())),
                                                   preferred_element_type=F32)
    n_scr[0:1, :] = decay * n_state + jnp.sum(wk, axis=0, keepdims=True)
    m_scr[...] = jnp.broadcast_to(m_new, m_scr.shape)

    hn = _rms_rows(h, ng_ref[0])
    y_ref[0] = (_sigmoid(o_ref[0].astype(F32)) * hn).astype(y_ref.dtype)


def _mlstm(z3, gcol, grow, gbias, conv_w, norm_g, batch, seq):
    L = min(MLSTM_CHUNK, seq)
    hd = M_HEAD_DIM
    nh = M_HEADS

    def zspec(col0):
        return pl.BlockSpec((1, L, hd), functools.partial(lambda b, h, c, o: (b, c, o + h), o=col0 // hd))

    return pl.pallas_call(
        _mlstm_kernel,
        grid=(batch, nh, seq // L),
        in_specs=[zspec(Z_MQK), zspec(Z_MQK + nh * hd), zspec(Z_MV), zspec(Z_MO),
                  pl.BlockSpec((1, 1, L, 2), lambda b, h, c: (b, h, c, 0)),
                  pl.BlockSpec((1, 1, 2, L), lambda b, h, c: (b, h, 0, c)),
                  pl.BlockSpec((1, 1, 2), lambda b, h, c: (h, 0, 0)),
                  pl.BlockSpec((M_CONV, hd), lambda b, h, c: (0, h)),
                  pl.BlockSpec((M_CONV, hd), lambda b, h, c: (0, nh + h)),
                  pl.BlockSpec((1, 1, hd), lambda b, h, c: (h, 0, 0))],
        out_specs=pl.BlockSpec((1, L, hd), lambda b, h, c: (b, c, h)),
        out_shape=jax.ShapeDtypeStruct((batch, seq, nh * hd), BF16),
        scratch_shapes=[pltpu.VMEM((hd, hd), F32), pltpu.VMEM((8, hd), F32), pltpu.VMEM((8, 128), F32),
                        pltpu.VMEM((L + 8, hd), F32), pltpu.VMEM((L + 8, hd), F32)],
        compiler_params=_params("parallel", "parallel", "arbitrary"),
        name="mlstm",
    )(z3, z3, z3, z3, gcol, grow, gbias, conv_w, conv_w, norm_g.reshape(nh, 1, hd))


def _xattn_kernel(q_ref, k_ref, v_ref, gq_ref, gk_ref, o_ref):
    hd = C_HEAD_DIM
    for h in range(C_HEADS):
        sl = slice(h * hd, (h + 1) * hd)
        q = _rms_rows(q_ref[0, :, sl].astype(F32), gq_ref[...]) * (hd ** -0.5)
        k = _rms_rows(k_ref[0, :, sl].astype(F32), gk_ref[...])
        s = lax.dot_general(q.astype(BF16), k.astype(BF16), (((1,), (1,)), ((), ())), preferred_element_type=F32)
        e = jnp.exp(s - jnp.max(s, axis=-1, keepdims=True))
        p = e / jnp.sum(e, axis=-1, keepdims=True)
        o = jnp.dot(p.astype(BF16), v_ref[0, :, sl], preferred_element_type=F32)
        o_ref[0, :, sl] = o.astype(o_ref.dtype)


def _xattn(z3, k, v, gq, gk, batch, seq, tq=512):
    tq = min(tq, seq)
    mlen, w = k.shape[1], k.shape[2]
    return pl.pallas_call(
        _xattn_kernel,
        grid=(batch, seq // tq),
        in_specs=[pl.BlockSpec((1, tq, w), lambda b, i: (b, i, Z_CQ // w)),
                  pl.BlockSpec((1, mlen, w), lambda b, i: (b, 0, 0)),
                  pl.BlockSpec((1, mlen, w), lambda b, i: (b, 0, 0)),
                  pl.BlockSpec((1, C_HEAD_DIM), lambda b, i: (0, 0)),
                  pl.BlockSpec((1, C_HEAD_DIM), lambda b, i: (0, 0))],
        out_specs=pl.BlockSpec((1, tq, w), lambda b, i: (b, i, 0)),
        out_shape=jax.ShapeDtypeStruct((batch, seq, w), BF16),
        compiler_params=_params("parallel", "arbitrary"),
        name="memory_cross_attention",
    )(z3, k, v, gq.reshape(1, -1), gk.reshape(1, -1))


def _rope_rows(xn, cos, sin_signed):
    half = N_HEAD_DIM // 2
    rot = jnp.concatenate([xn[:, half:], xn[:, :half]], axis=-1)
    return xn * cos + rot * sin_signed


def _knorm_rope_kernel(x_ref, g_ref, cos_ref, sin_ref, o_ref, *, n_blocks):
    hd = N_HEAD_DIM
    ts = x_ref.shape[1]
    if n_blocks:
        pos = pl.program_id(1) * ts + lax.broadcasted_iota(jnp.int32, (ts, n_blocks), 0)
        blk = lax.broadcasted_iota(jnp.int32, (ts, n_blocks), 1)
        onehot = jnp.where(pos // SEL_BLOCK == blk, 1.0, 0.0).astype(o_ref.dtype)
    for g in range(N_GROUPS):
        x = x_ref[0, :, g * hd:(g + 1) * hd].astype(F32)
        k = _rope_rows(_rms_rows(x, g_ref[...]), cos_ref[...], sin_ref[...]).astype(o_ref.dtype)
        if n_blocks:
            o_ref[0, g, :, 0:n_blocks] = onehot
        o_ref[0, g, :, n_blocks:n_blocks + hd] = k


def _knorm_rope(z3, col0, g, cos, sin_signed, n_blocks, ts=1024):
    b, s, _ = z3.shape
    hd, w = N_HEAD_DIM, N_GROUPS * N_HEAD_DIM
    ts = min(ts, s)
    return pl.pallas_call(
        functools.partial(_knorm_rope_kernel, n_blocks=n_blocks),
        grid=(b, s // ts),
        in_specs=[pl.BlockSpec((1, ts, w), lambda i, j: (i, j, col0 // w)), pl.BlockSpec((1, hd), lambda i, j: (0, 0)),
                  pl.BlockSpec((ts, hd), lambda i, j: (j, 0)), pl.BlockSpec((ts, hd), lambda i, j: (j, 0))],
        out_specs=pl.BlockSpec((1, N_GROUPS, ts, n_blocks + hd), lambda i, j: (i, 0, j, 0)),
        out_shape=jax.ShapeDtypeStruct((b, N_GROUPS, s, n_blocks + hd), BF16),
        compiler_params=_params("parallel", "arbitrary"),
        name="key_norm_rope",
    )(z3, g.reshape(1, hd), cos, sin_signed)


def _compress_kernel(sub_ref, pe_ref, w1_ref, w2_ref, g_ref, cos_ref, sin_ref, o_ref, *, is_key):
    ns = sub_ref.shape[1]
    sub = sub_ref[0].astype(F32)
    lo = jnp.dot((sub + pe_ref[0:1, :]).astype(BF16), w1_ref[0], preferred_element_type=F32)
    hi = jnp.dot((sub + pe_ref[1:2, :]).astype(BF16), w1_ref[1], preferred_element_type=F32)
    hid = _silu(lo + pltpu.roll(hi, shift=ns - 1, axis=0))
    out = jnp.dot(hid.astype(BF16), w2_ref[...], preferred_element_type=F32)
    if is_key:
        out = _rope_rows(_rms_rows(out, g_ref[...]), cos_ref[...], sin_ref[...])
    o_ref[0] = out.astype(o_ref.dtype)


def _compress(u, pe, w1, w2, g, cos, sin_signed, is_key):
    n, s, hd = u.shape
    ns = s // CMP_STRIDE
    width = CMP_STRIDE * hd
    sub = u.reshape(n, ns, width)
    pe2 = pe.reshape(CMP_LEN // CMP_STRIDE, width)
    w1s = w1.reshape(CMP_LEN // CMP_STRIDE, width, CMP_HIDDEN).astype(BF16)
    return pl.pallas_call(
        functools.partial(_compress_kernel, is_key=is_key),
        grid=(n,),
        in_specs=[pl.BlockSpec((1, ns, width), lambda i: (i, 0, 0)),
                  pl.BlockSpec(pe2.shape, lambda i: (0, 0)),
                  pl.BlockSpec(w1s.shape, lambda i: (0, 0, 0)),
                  pl.BlockSpec(w2.shape, lambda i: (0, 0)),
                  pl.BlockSpec((1, hd), lambda i: (0, 0)),
                  pl.BlockSpec((ns, hd), lambda i: (0, 0)),
                  pl.BlockSpec((ns, hd), lambda i: (0, 0))],
        out_specs=pl.BlockSpec((1, ns, hd), lambda i: (i, 0, 0)),
        out_shape=jax.ShapeDtypeStruct((n, ns, hd), BF16),
        compiler_params=_params("parallel"),
        name="compress_key" if is_key else "compress_value",
    )(sub, pe2, w1s, w2.astype(BF16), g.reshape(1, hd), cos, sin_signed)


def _nsa_kernel(q_ref, gate_ref, kc_ref, vct_ref, ks_ref, vst_ref, kw_ref, vwt_ref, gq_ref, cos_ref, sin_ref,
                wimp_ref, y_ref, s_c, e_c, s_w, e_w, s_a, s_b, e_a, e_b, *, seq):
    tq = q_ref.shape[1]
    hd = N_HEAD_DIM
    nb = seq // SEL_BLOCK
    top = min(SEL_TOP, nb)
    ncol = N_REP * tq
    t0 = pl.program_id(2) * tq

    def lanes(parts):
        return jnp.concatenate(parts, axis=1)

    def head_cols(a, r):
        return a[:, r * tq:(r + 1) * tq]

    def with_ones(v):
        n = v.shape[1]
        tail = jnp.where(lax.broadcasted_iota(jnp.int32, (V_AUG_ROWS, n), 0) == 0, 1.0, 0.0).astype(v.dtype)
        return jnp.concatenate([v, tail], axis=0)

    def chunk_scores(src, c, valid_fn):
        sc = src[c * NSA_RC:(c + 1) * NSA_RC, :]
        if valid_fn is None:
            return sc
        valid = valid_fn(c * NSA_RC + lax.broadcasted_iota(jnp.int32, (NSA_RC, 1), 0))
        return lanes([jnp.where(valid, head_cols(sc, r), NEG_BIG) for r in range(N_REP)])

    def col_max(src, rows, valid_fn):
        mx = jnp.full((8, ncol), NEG_BIG, F32)
        for c in range(rows // NSA_RC):
            sc = chunk_scores(src, c, valid_fn)
            mx = jnp.maximum(mx, jnp.max(sc.reshape(NSA_RC // 8, 8, ncol), axis=0))
        return jnp.max(mx, axis=0, keepdims=True)

    def put_weights(src, dst, rows, valid_fn, m):
        for c in range(rows // NSA_RC):
            sc = chunk_scores(src, c, valid_fn)
            dst[c * NSA_RC:(c + 1) * NSA_RC, :] = jnp.exp2((sc - m).astype(BF16))

    q = lanes([q_ref[r * hd:(r + 1) * hd, :].astype(F32) for r in range(N_REP)])
    qn = q * lax.rsqrt(jnp.mean(q * q, axis=0, keepdims=True) + EPS) * gq_ref[...]
    cos = lanes([cos_ref[...]] * N_REP)
    sin = lanes([sin_ref[...]] * N_REP)
    rot = jnp.concatenate([qn[hd // 2:], qn[:hd // 2]], axis=0)
    qr = ((qn * cos + rot * sin) * (hd ** -0.5 * LOG2_E)).astype(BF16)
    t_q = t0 + lax.broadcasted_iota(jnp.int32, (1, tq), 1)
    tcol = lanes([t_q] * N_REP)

    nc = kc_ref.shape[2]
    s_c[...] = jnp.dot(kc_ref[0, 0], qr, preferred_element_type=F32)
    valid_c = lambda n: n * CMP_STRIDE + (CMP_LEN - 1) <= t_q
    put_weights(s_c, e_c, nc, valid_c, col_max(s_c, nc, valid_c))
    both = jnp.dot(jnp.concatenate([vct_ref[0, 0], wimp_ref[...]], axis=0), e_c[...], preferred_element_type=F32)
    inv_c = jnp.where(tcol >= CMP_LEN - 1, 1.0 / both[hd:hd + 1], 0.0)
    o_c = both[0:hd] * inv_c

    imp_h = both[hd + V_AUG_ROWS:] * inv_c
    imp = head_cols(imp_h, 0)
    for r in range(1, N_REP):
        imp = imp + head_cols(imp_h, r)

    span = WINDOW + tq
    ws0 = pl.multiple_of(jnp.maximum(t0 - WINDOW, 0), tq)
    s_w[...] = jnp.dot(kw_ref[0, 0, pl.ds(ws0, span), :], qr, preferred_element_type=F32)
    valid_w = lambda k: (ws0 + k <= t_q) & (ws0 + k > t_q - WINDOW)
    put_weights(s_w, e_w, span, valid_w, col_max(s_w, span, valid_w))
    o_w = jnp.dot(with_ones(vwt_ref[:, pl.ds(ws0, span)]), e_w[...], preferred_element_type=F32)
    o_w = o_w[0:hd] / o_w[hd:hd + 1]

    blk = lax.broadcasted_iota(jnp.int32, (nb, tq), 0)
    cur = (t0 + lax.broadcasted_iota(jnp.int32, (nb, tq), 1)) // SEL_BLOCK
    causal_b = blk <= cur
    forced = (blk == 0) | (blk == cur) | (blk == cur - 1)
    score = jnp.where(forced, FORCE_SCORE, jnp.where(causal_b, imp, -1.0))
    taken = -3.0e38

    def pick_by_value(sc):
        for _ in range(top - 2):
            sc = jnp.where(sc == jnp.max(sc, axis=0, keepdims=True), taken, sc)
        return sc

    def pick_by_value_then_index(sc):
        for _ in range(top):
            mx = jnp.max(sc, axis=0, keepdims=True)
            idx = jnp.min(jnp.where(sc == mx, blk, nb), axis=0, keepdims=True)
            sc = jnp.where(blk == idx, taken, sc)
        return sc

    fast = pick_by_value(score)
    n_taken = jnp.sum(jnp.where((fast == taken) & causal_b, 1.0, 0.0), axis=0, keepdims=True)
    n_want = jnp.minimum(cur[0:1, :] + 1, top).astype(F32)
    tied = jnp.max(jnp.abs(n_taken - n_want)) > 0.0
    picked = lax.cond(tied, lambda: pick_by_value_then_index(score), lambda: fast)
    bias = jnp.where((picked == taken) & causal_b, 0.0, BLOCK_BIAS)
    q_aug = jnp.concatenate([lanes([bias.astype(BF16)] * N_REP), qr], axis=0)

    tk = min(NSA_TK, seq)
    last = (t0 + tq - 1) // tk
    s_bufs, e_bufs = (s_a, s_b), (e_a, e_b)

    def scores(t, par):
        ks0 = pl.multiple_of(t * tk, tk)
        s_bufs[par][...] = jnp.dot(ks_ref[0, 0, pl.ds(ks0, tk), :], q_aug, preferred_element_type=F32)

    def weights(t, par, m, masked):
        valid_s = (lambda k: t * tk + k <= t_q) if masked else None
        m_new = jnp.maximum(m, col_max(s_bufs[par], tk, valid_s))
        put_weights(s_bufs[par], e_bufs[par], tk, valid_s, m_new)
        return m_new, jnp.exp2(m - m_new)

    def values(t, par, alpha, acc):
        ks0 = pl.multiple_of(jnp.maximum(t, 0) * tk, tk)
        return alpha * acc + jnp.dot(with_ones(vst_ref[:, pl.ds(ks0, tk)]), e_bufs[par][...],
                                     preferred_element_type=F32)

    def step(t, par, carry, masked, more):
        m, alpha, acc = carry
        if more:
            scores(t + 1, 1 - par)
        acc = values(t - 1, 1 - par, alpha, acc)
        m, alpha = weights(t, par, m, masked)
        return m, alpha, acc

    def pair_body(j, carry):
        return step(2 * j + 1, 1, step(2 * j, 0, carry, False, True), False, True)

    def tail_even(carry):
        m, alpha, acc = step(last, 0, carry, True, False)
        return values(last, 0, alpha, acc)

    def tail_odd(carry):
        m, alpha, acc = step(last, 1, step(last - 1, 0, carry, False, True), True, False)
        return values(last, 1, alpha, acc)

    scores(0, 0)
    e_b[...] = jnp.zeros(e_b.shape, e_b.dtype)
    carry = (jnp.full((1, ncol), NEG_BIG, F32), jnp.ones((1, ncol), F32), jnp.zeros((hd + V_AUG_ROWS, ncol), F32))
    carry = lax.fori_loop(0, last // 2, pair_body, carry)
    acc = lax.cond(last % 2 == 0, tail_even, tail_odd, carry)
    o_s = acc[0:hd] / acc[hd:hd + 1]

    g = _sigmoid(gate_ref[0, 0])
    gc, gs, gw = (lanes([g[j, r:r + 1, :] for r in range(N_REP)]) for j in range(3))
    out = gc * o_c + gs * o_s + gw * o_w
    for r in range(N_REP):
        y_ref[r * hd:(r + 1) * hd, :] = head_cols(out, r).astype(y_ref.dtype)


def _nsa(zt, gates, kc, vct, ks_aug, kw, gq, cos_t, sin_t, wimp_t, batch, seq):
    g, hd = N_GROUPS, N_HEAD_DIM
    tq = min(NSA_TQ, seq)
    tk = min(NSA_TK, seq)
    nq = seq // tq
    ncol = N_REP * tq
    nc = kc.shape[2]
    nb = seq // SEL_BLOCK
    full = lambda *shape: pl.BlockSpec((1, 1) + shape, lambda bi, gi, qi: (bi, gi) + (0,) * len(shape))
    vrow = lambda row0: pl.BlockSpec((hd, seq), functools.partial(lambda bi, gi, qi, o: (o + gi, bi), o=row0 // hd))
    qspec = pl.BlockSpec((N_REP * hd, tq), lambda bi, gi, qi: (ZT_Q // (N_REP * hd) + gi, bi * nq + qi))
    return pl.pallas_call(
        functools.partial(_nsa_kernel, seq=seq),
        grid=(batch, g, nq),
        in_specs=[qspec,
                  pl.BlockSpec((1, 1, 3, N_REP, tq), lambda bi, gi, qi: (bi, gi, 0, 0, qi)),
                  full(nc, hd), full(hd + V_AUG_ROWS, nc), full(seq, nb + hd), vrow(ZT_VS), full(seq, hd),
                  vrow(ZT_VW),
                  pl.BlockSpec((hd, 1), lambda bi, gi, qi: (0, 0)),
                  pl.BlockSpec((hd, tq), lambda bi, gi, qi: (0, qi)),
                  pl.BlockSpec((hd, tq), lambda bi, gi, qi: (0, qi)),
                  pl.BlockSpec((nb, nc), lambda bi, gi, qi: (0, 0))],
        out_specs=pl.BlockSpec((N_REP * hd, tq), lambda bi, gi, qi: (gi, bi * nq + qi)),
        out_shape=jax.ShapeDtypeStruct((N_HEADS * hd, batch * seq), BF16),
        scratch_shapes=[pltpu.VMEM((nc, ncol), F32), pltpu.VMEM((nc, ncol), BF16),
                        pltpu.VMEM((WINDOW + tq, ncol), F32), pltpu.VMEM((WINDOW + tq, ncol), BF16),
                        pltpu.VMEM((tk, ncol), F32), pltpu.VMEM((tk, ncol), F32),
                        pltpu.VMEM((tk, ncol), BF16), pltpu.VMEM((tk, ncol), BF16)],
        compiler_params=_params("parallel", "parallel", "arbitrary"),
        name="nsa_attention",
    )(zt, gates, kc, vct, ks_aug, zt, kw, zt, gq.reshape(hd, 1), cos_t, sin_t, wimp_t)


def _rope_tables(pos):
    half = N_HEAD_DIM // 2
    inv = jnp.power(ROPE_THETA, -jnp.arange(half, dtype=F32) * 2.0 / N_HEAD_DIM)
    ang = pos.astype(F32)[:, None] * inv[None, :]
    cos, sin = jnp.cos(ang), jnp.sin(ang)
    return jnp.concatenate([cos, cos], axis=-1), jnp.concatenate([-sin, sin], axis=-1)


def _importance_matrix(nb, nc_pad):
    per = SEL_BLOCK // CMP_STRIDE
    j = np.arange(nb)[:, None]
    n = np.arange(nc_pad)[None, :]
    w = np.zeros((nb, nc_pad), np.float32)
    for d in range(CMP_LEN // CMP_STRIDE):
        w += ((n + d >= per * j) & (n + d <= per * j + per - 1)).astype(np.float32)
    return jnp.asarray(w, BF16)


def _split_w_in(w_in):
    d = w_in.shape[0]
    sizes = (2048, 1024, 1024, 4, 4, 1024, 256, 256, 256, 256, 256, 256, 48, 1024, 2048, 2048, 2048)
    offs = np.concatenate([[0], np.cumsum(sizes)])
    part = lambda i: w_in[:, offs[i]:offs[i + 1]]
    tok = jnp.concatenate([part(i) for i in (0, 1, 2, 13, 14, 15, 16, 6, 7, 8, 10)], axis=1).astype(BF16)
    feat = jnp.concatenate([part(i) for i in (5, 9, 11)], axis=1).T.astype(BF16)
    small = jnp.concatenate([part(3), part(4), part(12), jnp.zeros((d, 128 - 56), w_in.dtype)], axis=1).astype(BF16)
    return tok, feat, small


def _layer(x, mem, norm_mix_g, norm_mem_g, norm_ffn_g, w_in, m_conv_w, m_i_bias, m_f_bias, m_norm_g,
           n_q_norm_g, n_kc_norm_g, n_ks_norm_g, n_kw_norm_g, n_cmp_pe_k, n_cmp_w1_k, n_cmp_w2_k,
           n_cmp_pe_v, n_cmp_w1_v, n_cmp_w2_v, c_q_norm_g, c_k_norm_g, w_mem_k, w_mem_v,
           w_up_a, w_up_b, w_up_c, w_out, w_ffn_gate, w_ffn_up, w_ffn_down):
    B, S, D = x.shape
    M = B * S
    G, R, hd = N_GROUPS, N_REP, N_HEAD_DIM
    x2 = x.reshape(M, D)

    w_tok, w_feat, w_small = _split_w_in(w_in)
    z, zs, h = _in_proj(x2, norm_mix_g, w_tok, w_small)
    zt = _mm_feature_major(w_feat, h)
    z3 = z.reshape(B, S, Z_WIDTH)
    zs3 = zs.reshape(B, S, 128)

    gif = jnp.stack([zs3[..., 0:4], zs3[..., 4:8]], axis=-1)
    gcol = gif.transpose(0, 2, 1, 3)
    grow = gif.transpose(0, 2, 3, 1)
    gbias = jnp.stack([m_i_bias, m_f_bias], axis=-1).reshape(M_HEADS, 1, 2).astype(F32)
    y_a = _mlstm(z3, gcol, grow, gbias, m_conv_w, m_norm_g, B, S)

    pos = jnp.arange(S, dtype=jnp.int32)
    cos, sin_s = _rope_tables(pos)
    nc_pad = S // CMP_STRIDE
    cmp_end = jnp.arange(nc_pad, dtype=jnp.int32) * CMP_STRIDE + CMP_LEN - 1
    cos_c, sin_c = _rope_tables(cmp_end)
    nb = S // SEL_BLOCK

    def head_major(col0):
        u = z3[..., col0:col0 + G * hd].reshape(B, S, G, hd)
        return u.transpose(0, 2, 1, 3).reshape(B * G, S, hd)

    kc = _compress(head_major(Z_KC), n_cmp_pe_k, n_cmp_w1_k, n_cmp_w2_k, n_kc_norm_g, cos_c, sin_c, True)
    vc = _compress(head_major(Z_VC), n_cmp_pe_v, n_cmp_w1_v, n_cmp_w2_v, n_kc_norm_g, cos_c, sin_c, False)
    vct = vc.reshape(B, G, nc_pad, hd).transpose(0, 1, 3, 2)
    ones_rows = jnp.concatenate([jnp.ones((B, G, 1, nc_pad), BF16), jnp.zeros((B, G, V_AUG_ROWS - 1, nc_pad), BF16)],
                                axis=2)
    vct = jnp.concatenate([vct, ones_rows], axis=2)
    ks_aug = _knorm_rope(z3, Z_KS, n_ks_norm_g, cos, sin_s, nb)
    kw = _knorm_rope(z3, Z_KW, n_kw_norm_g, cos, sin_s, 0)
    gates = zs3[..., 8:8 + 3 * N_HEADS].reshape(B, S, G, R, 3).transpose(0, 2, 4, 3, 1)
    y_bt = _nsa(zt, gates, kc.reshape(B, G, nc_pad, hd), vct, ks_aug, kw, n_q_norm_g, cos.T, sin_s.T,
                _importance_matrix(nb, nc_pad), B, S)

    mlen = mem.shape[1]
    mem_n = _rmsnorm(mem.reshape(B * mlen, D), norm_mem_g)
    k_mem = _mm(mem_n, w_mem_k.astype(BF16), BF16).reshape(B, mlen, -1)
    v_mem = _mm(mem_n, w_mem_v.astype(BF16), BF16).reshape(B, mlen, -1)
    y_c = _xattn(z3, k_mem, v_mem, c_q_norm_g, c_k_norm_g, B, S)

    mix = _mix(y_a.reshape(M, -1), y_bt, y_c.reshape(M, -1),
               w_up_a.astype(BF16), w_up_b.astype(BF16), w_up_c.astype(BF16), z, Z_GATES)
    x2 = _mm_residual(mix, w_out.astype(BF16), x2, tm=512, tn=D)

    act = _swiglu(x2, norm_ffn_g, w_ffn_gate.astype(BF16), w_ffn_up.astype(BF16))
    x2 = _mm_residual(act, w_ffn_down.astype(BF16), x2, tm=1024, tn=512)
    return x2.reshape(B, S, D)


def kernel(x, mem, norm_mix_g, norm_mem_g, norm_ffn_g, w_in, m_conv_w, m_i_bias, m_f_bias, m_norm_g, n_q_norm_g, n_kc_norm_g, n_ks_norm_g, n_kw_norm_g, n_cmp_pe_k, n_cmp_w1_k, n_cmp_w2_k, n_cmp_pe_v, n_cmp_w1_v, n_cmp_w2_v, c_q_norm_g, c_k_norm_g, w_mem_k, w_mem_v, w_up_a, w_up_b, w_up_c, w_out, w_ffn_gate, w_ffn_up, w_ffn_down):
    stacked = (norm_mix_g, norm_mem_g, norm_ffn_g, w_in, m_conv_w, m_i_bias, m_f_bias, m_norm_g, n_q_norm_g,
               n_kc_norm_g, n_ks_norm_g, n_kw_norm_g, n_cmp_pe_k, n_cmp_w1_k, n_cmp_w2_k, n_cmp_pe_v, n_cmp_w1_v,
               n_cmp_w2_v, c_q_norm_g, c_k_norm_g, w_mem_k, w_mem_v, w_up_a, w_up_b, w_up_c, w_out, w_ffn_gate,
               w_ffn_up, w_ffn_down)
    for layer in range(w_in.shape[0]):
        x = _layer(x, mem, *(p[layer] for p in stacked))
    return x
```

```python
import functools

import numpy as np
import jax
import jax.numpy as jnp
from jax import lax
from jax.experimental import pallas as pl
from jax.experimental.pallas import tpu as pltpu

F32 = jnp.float32
BF16 = jnp.bfloat16

EPS = 1e-6
ROPE_THETA = 10000.0
M_HEADS = 4
M_HEAD_DIM = 256
M_CONV = 4
N_HEADS = 16
N_GROUPS = 4
N_REP = N_HEADS // N_GROUPS
N_HEAD_DIM = 64
CMP_LEN = 32
CMP_STRIDE = 16
CMP_HIDDEN = 128
SEL_BLOCK = 64
SEL_TOP = 16
WINDOW = 512
FORCE_SCORE = 1.0e4
C_HEADS = 4
C_HEAD_DIM = 256

LOG2_E = 1.4426950408889634
V_AUG_ROWS = 16
NEG_BIG = -1.0e30
F8 = jnp.float8_e4m3fn
BLOCK_BIAS = -448.0
SEL_WEIGHT_SHIFT = 8.0
VMEM_LIMIT = 56 * 1024 * 1024

MLSTM_CHUNK = 256
NSA_TQ = 256
NSA_TK = 512
NSA_RC = 32

Z_MQK, Z_MV, Z_MO, Z_CQ, Z_GATES, Z_KC, Z_VC, Z_KS, Z_KW, Z_WIDTH = (
    0, 2048, 3072, 4096, 5120, 11264, 11520, 11776, 12032, 12288)
ZT_Q, ZT_VS, ZT_VW, ZT_ROWS = 0, 1024, 1280, 1536


def _params(*sem):
    return pltpu.CompilerParams(dimension_semantics=sem, vmem_limit_bytes=VMEM_LIMIT)


def _sigmoid(x):
    return 1.0 / (1.0 + jnp.exp(-x))


def _silu(x):
    return x * _sigmoid(x)


def _log_sigmoid(x):
    return jnp.minimum(x, 0.0) - jnp.log1p(jnp.exp(-jnp.abs(x)))


def _rms_rows(x, g):
    return x * lax.rsqrt(jnp.mean(x * x, axis=-1, keepdims=True) + EPS) * g


def _rmsnorm_kernel(x_ref, g_ref, o_ref):
    o_ref[...] = _rms_rows(x_ref[...], g_ref[...]).astype(o_ref.dtype)


def _rmsnorm(x2d, g, tm=512):
    m, d = x2d.shape
    tm = min(tm, m)
    return pl.pallas_call(
        _rmsnorm_kernel,
        grid=(m // tm,),
        in_specs=[pl.BlockSpec((tm, d), lambda i: (i, 0)), pl.BlockSpec((1, d), lambda i: (0, 0))],
        out_specs=pl.BlockSpec((tm, d), lambda i: (i, 0)),
        out_shape=jax.ShapeDtypeStruct((m, d), BF16),
        compiler_params=_params("parallel"),
        name="rmsnorm",
    )(x2d, g.reshape(1, d))


def _mm_kernel(a_ref, w_ref, o_ref):
    o_ref[...] = jnp.dot(a_ref[...], w_ref[...], preferred_element_type=F32).astype(o_ref.dtype)


def _mm(a, w, out_dtype, tm=1024, tn=512):
    m, k = a.shape
    n = w.shape[1]
    tm, tn = min(tm, m), min(tn, n)
    return pl.pallas_call(
        _mm_kernel,
        grid=(m // tm, n // tn),
        in_specs=[pl.BlockSpec((tm, k), lambda i, j: (i, 0)), pl.BlockSpec((k, tn), lambda i, j: (0, j))],
        out_specs=pl.BlockSpec((tm, tn), lambda i, j: (i, j)),
        out_shape=jax.ShapeDtypeStruct((m, n), out_dtype),
        compiler_params=_params("parallel", "arbitrary"),
        name="matmul",
    )(a, w)


def _in_proj_kernel(x_ref, g_ref, w_ref, ws_ref, z_ref, zs_ref, h_ref):
    @pl.when(pl.program_id(1) == 0)
    def _():
        h = _rms_rows(x_ref[...], g_ref[...]).astype(BF16)
        h_ref[...] = h
        zs_ref[...] = jnp.dot(h, ws_ref[...], preferred_element_type=F32)

    z_ref[...] = jnp.dot(h_ref[...], w_ref[...], preferred_element_type=F32).astype(z_ref.dtype)


def _in_proj(x2d, g, w, w_small, tm=1024, tn=1024):
    m, d = x2d.shape
    n, ns = w.shape[1], w_small.shape[1]
    tm, tn = min(tm, m), min(tn, n)
    return pl.pallas_call(
        _in_proj_kernel,
        grid=(m // tm, n // tn),
        in_specs=[pl.BlockSpec((tm, d), lambda i, j: (i, 0)), pl.BlockSpec((1, d), lambda i, j: (0, 0)),
                  pl.BlockSpec((d, tn), lambda i, j: (0, j)), pl.BlockSpec((d, ns), lambda i, j: (0, 0))],
        out_specs=[pl.BlockSpec((tm, tn), lambda i, j: (i, j)), pl.BlockSpec((tm, ns), lambda i, j: (i, 0)),
                   pl.BlockSpec((tm, d), lambda i, j: (i, 0))],
        out_shape=[jax.ShapeDtypeStruct((m, n), BF16), jax.ShapeDtypeStruct((m, ns), F32),
                   jax.ShapeDtypeStruct((m, d), BF16)],
        compiler_params=_params("parallel", "arbitrary"),
        name="in_proj",
    )(x2d, g.reshape(1, d), w, w_small)


def _mm_nt_kernel(w_ref, h_ref, o_ref):
    o_ref[...] = lax.dot_general(w_ref[...], h_ref[...], (((1,), (1,)), ((), ())),
                                 preferred_element_type=F32).astype(o_ref.dtype)


def _mm_feature_major(w_t, h, tm=1024, tn=1536):
    n, k = w_t.shape
    m = h.shape[0]
    tm, tn = min(tm, m), min(tn, n)
    return pl.pallas_call(
        _mm_nt_kernel,
        grid=(m // tm, n // tn),
        in_specs=[pl.BlockSpec((tn, k), lambda i, j: (j, 0)), pl.BlockSpec((tm, k), lambda i, j: (i, 0))],
        out_specs=pl.BlockSpec((tn, tm), lambda i, j: (j, i)),
        out_shape=jax.ShapeDtypeStruct((n, m), BF16),
        compiler_params=_params("parallel", "arbitrary"),
        name="in_proj_feature_major",
    )(w_t, h)


def _mm_res_kernel(a_ref, w_ref, x_ref, o_ref):
    o_ref[...] = x_ref[...] + jnp.dot(a_ref[...], w_ref[...], preferred_element_type=F32)


def _mm_residual(a, w, x, tm=512, tn=512):
    m, k = a.shape
    n = w.shape[1]
    tm, tn = min(tm, m), min(tn, n)
    return pl.pallas_call(
        _mm_res_kernel,
        grid=(m // tm, n // tn),
        in_specs=[pl.BlockSpec((tm, k), lambda i, j: (i, 0)), pl.BlockSpec((k, tn), lambda i, j: (0, j)),
                  pl.BlockSpec((tm, tn), lambda i, j: (i, j))],
        out_specs=pl.BlockSpec((tm, tn), lambda i, j: (i, j)),
        out_shape=jax.ShapeDtypeStruct((m, n), F32),
        compiler_params=_params("parallel", "arbitrary"),
        name="matmul_residual",
    )(a, w, x)


def _swiglu_kernel(x_ref, g_ref, wg_ref, wu_ref, o_ref, h_scr):
    @pl.when(pl.program_id(1) == 0)
    def _():
        h_scr[...] = _rms_rows(x_ref[...], g_ref[...]).astype(BF16)

    h = h_scr[...]
    gate = jnp.dot(h, wg_ref[...], preferred_element_type=F32)
    up = jnp.dot(h, wu_ref[...], preferred_element_type=F32)
    o_ref[...] = (_silu(gate) * up).astype(o_ref.dtype)


def _swiglu(x2d, g, wg, wu, tm=1024, tn=512):
    m, d = x2d.shape
    n = wg.shape[1]
    tm, tn = min(tm, m), min(tn, n)
    return pl.pallas_call(
        _swiglu_kernel,
        grid=(m // tm, n // tn),
        in_specs=[pl.BlockSpec((tm, d), lambda i, j: (i, 0)), pl.BlockSpec((1, d), lambda i, j: (0, 0)),
                  pl.BlockSpec((d, tn), lambda i, j: (0, j)), pl.BlockSpec((d, tn), lambda i, j: (0, j))],
        out_specs=pl.BlockSpec((tm, tn), lambda i, j: (i, j)),
        out_shape=jax.ShapeDtypeStruct((m, n), BF16),
        scratch_shapes=[pltpu.VMEM((tm, d), BF16)],
        compiler_params=_params("parallel", "arbitrary"),
        name="swiglu",
    )(x2d, g.reshape(1, d), wg, wu)


def _mix_kernel(ya_ref, ybt_ref, yc_ref, wa_ref, wb_ref, wc_ref, ga_ref, gb_ref, gc_ref, o_ref):
    def gated(g_ref, prod):
        return _sigmoid(g_ref[...].astype(F32)) * prod

    a = jnp.dot(ya_ref[...], wa_ref[...], preferred_element_type=F32)
    b = lax.dot_general(ybt_ref[...], wb_ref[...], (((0,), (0,)), ((), ())), preferred_element_type=F32)
    c = jnp.dot(yc_ref[...], wc_ref[...], preferred_element_type=F32)
    o_ref[...] = (gated(ga_ref, a) + gated(gb_ref, b) + gated(gc_ref, c)).astype(o_ref.dtype)


def _mix(ya, ybt, yc, wa, wb, wc, z, gate_col0, tm=1024, tn=512):
    m, k = ya.shape
    n = wa.shape[1]
    tm, tn = min(tm, m), min(tn, n)
    gofs = [(gate_col0 + b * n) // tn for b in range(3)]
    y_spec = pl.BlockSpec((tm, k), lambda i, j: (i, 0))
    w_spec = pl.BlockSpec((k, tn), lambda i, j: (0, j))
    g_specs = [pl.BlockSpec((tm, tn), functools.partial(lambda i, j, o: (i, o + j), o=o)) for o in gofs]
    return pl.pallas_call(
        _mix_kernel,
        grid=(m // tm, n // tn),
        in_specs=[y_spec, pl.BlockSpec((k, tm), lambda i, j: (0, i)), y_spec, w_spec, w_spec, w_spec] + g_specs,
        out_specs=pl.BlockSpec((tm, tn), lambda i, j: (i, j)),
        out_shape=jax.ShapeDtypeStruct((m, n), BF16),
        compiler_params=_params("parallel", "arbitrary"),
        name="gated_mix",
    )(ya, ybt, yc, wa, wb, wc, z, z, z)


def _mlstm_kernel(uq_ref, uk_ref, v_ref, o_ref, gcol_ref, grow_ref, bias_ref, cwq_ref, cwk_ref, ng_ref,
                  y_ref, c_scr, n_scr, m_scr, qbuf, kbuf):
    c = pl.program_id(1)
    L = uq_ref.shape[1]
    hd = M_HEAD_DIM

    @pl.when(c == 0)
    def _():
        c_scr[...] = jnp.zeros_like(c_scr)
        n_scr[...] = jnp.zeros_like(n_scr)
        m_scr[...] = jnp.zeros_like(m_scr)
        qbuf[L:L + 8, :] = jnp.zeros((8, qbuf.shape[1]), F32)
        kbuf[L:L + 8, :] = jnp.zeros((8, kbuf.shape[1]), F32)

    def conv_silu(u_ref, buf, cw_ref):
        buf[0:8, :] = buf[L:L + 8, :]
        buf[8:L + 8, :] = u_ref[0].astype(F32)
        cw = cw_ref[...]
        y = cw[0:1, :] * buf[5:5 + L, :]
        for j in range(1, M_CONV):
            y = y + cw[j:j + 1, :] * buf[5 + j:5 + j + L, :]
        return _silu(y)

    q_all = conv_silu(uq_ref, qbuf, cwq_ref)
    k_all = conv_silu(uk_ref, kbuf, cwk_ref) * (hd ** -0.5)
    row = lax.broadcasted_iota(jnp.int32, (L, L), 0)
    col = lax.broadcasted_iota(jnp.int32, (L, L), 1)
    tri = col <= row

    for hh in range(M_HEADS):
        sl = slice(hh * hd, (hh + 1) * hd)
        q, k = q_all[:, sl], k_all[:, sl]
        qb = q.astype(BF16)
        kb = k.astype(BF16)
        vb = v_ref[0, :, sl]

        gcol = gcol_ref[0, hh]
        grow = grow_ref[0, hh]
        bias = bias_ref[hh]
        ic = gcol[:, 0:1] + bias[:, 0:1]
        fc = _log_sigmoid(gcol[:, 1:2] + bias[:, 1:2])
        ir = grow[0:1, :] + bias[:, 0:1]
        fr = _log_sigmoid(grow[1:2, :] + bias[:, 1:2])

        b_col = jnp.sum(jnp.where(tri, fr, 0.0), axis=1, keepdims=True)
        b_row = jnp.sum(jnp.where(row <= col, fc, 0.0), axis=0, keepdims=True)
        m_prev = m_scr[hh, 0:1, 0:1]

        log_d = jnp.where(tri, b_col - b_row + ir, NEG_BIG)
        log_inter = b_col + m_prev
        m_t = jnp.maximum(log_inter, jnp.max(log_d, axis=1, keepdims=True))
        w_intra = jnp.exp(log_d - m_t)
        w_inter = jnp.exp(log_inter - m_t)

        s = lax.dot_general(qb, kb, (((1,), (1,)), ((), ())), preferred_element_type=F32) * w_intra
        c_state = c_scr[hh]
        n_state = n_scr[hh, 0:1, :]
        num = (jnp.dot(s.astype(BF16), vb, preferred_element_type=F32)
               + w_inter * jnp.dot(qb, c_state.astype(BF16), preferred_element_type=F32))
        den = jnp.sum(s, axis=1, keepdims=True) + w_inter * jnp.sum(q * n_state, axis=1, keepdims=True)
        h = num / jnp.maximum(jnp.abs(den), jnp.exp(-m_t))

        b_last = jnp.sum(fc, axis=0, keepdims=True)
        log_g = b_last - b_col + ic
        m_new = jnp.maximum(b_last + m_prev, jnp.max(log_g, axis=0, keepdims=True))
        decay = jnp.exp(b_last + m_prev - m_new)
        wk = k * jnp.exp(log_g - m_new)
        c_scr[hh] = decay * c_state + lax.dot_general(wk.astype(BF16), vb, (((0,), (0,)), ((), ())),
                                                      preferred_element_type=F32)
        n_scr[hh, 0:1, :] = decay * n_state + jnp.sum(wk, axis=0, keepdims=True)
        m_scr[hh] = jnp.broadcast_to(m_new, m_scr.shape[1:])

        hn = _rms_rows(h, ng_ref[hh])
        y_ref[0, :, sl] = (_sigmoid(o_ref[0, :, sl].astype(F32)) * hn).astype(y_ref.dtype)


def _mlstm(z3, gcol, grow, gbias, conv_w, norm_g, batch, seq):
    L = min(MLSTM_CHUNK, seq)
    hd = M_HEAD_DIM
    nh = M_HEADS
    w = nh * hd

    def zspec(col0):
        return pl.BlockSpec((1, L, w), functools.partial(lambda b, c, o: (b, c, o), o=col0 // w))

    return pl.pallas_call(
        _mlstm_kernel,
        grid=(batch, seq // L),
        in_specs=[zspec(Z_MQK), zspec(Z_MQK + w), zspec(Z_MV), zspec(Z_MO),
                  pl.BlockSpec((1, nh, L, 2), lambda b, c: (b, 0, c, 0)),
                  pl.BlockSpec((1, nh, 2, L), lambda b, c: (b, 0, 0, c)),
                  pl.BlockSpec((nh, 1, 2), lambda b, c: (0, 0, 0)),
                  pl.BlockSpec((M_CONV, w), lambda b, c: (0, 0)),
                  pl.BlockSpec((M_CONV, w), lambda b, c: (0, 1)),
                  pl.BlockSpec((nh, 1, hd), lambda b, c: (0, 0, 0))],
        out_specs=pl.BlockSpec((1, L, w), lambda b, c: (b, c, 0)),
        out_shape=jax.ShapeDtypeStruct((batch, seq, w), BF16),
        scratch_shapes=[pltpu.VMEM((nh, hd, hd), F32), pltpu.VMEM((nh, 8, hd), F32), pltpu.VMEM((nh, 8, 128), F32),
                        pltpu.VMEM((L + 8, w), F32), pltpu.VMEM((L + 8, w), F32)],
        compiler_params=_params("parallel", "arbitrary"),
        name="mlstm",
    )(z3, z3, z3, z3, gcol, grow, gbias, conv_w, conv_w, norm_g.reshape(nh, 1, hd))


def _xattn_kernel(q_ref, k_ref, v_ref, gq_ref, gk_ref, o_ref):
    hd = C_HEAD_DIM
    for h in range(C_HEADS):
        sl = slice(h * hd, (h + 1) * hd)
        q = _rms_rows(q_ref[0, :, sl].astype(F32), gq_ref[...]) * (hd ** -0.5)
        k = _rms_rows(k_ref[0, :, sl].astype(F32), gk_ref[...])
        s = lax.dot_general(q.astype(BF16), k.astype(BF16), (((1,), (1,)), ((), ())), preferred_element_type=F32)
        e = jnp.exp(s - jnp.max(s, axis=-1, keepdims=True))
        p = e / jnp.sum(e, axis=-1, keepdims=True)
        o = jnp.dot(p.astype(BF16), v_ref[0, :, sl], preferred_element_type=F32)
        o_ref[0, :, sl] = o.astype(o_ref.dtype)


def _xattn(z3, k, v, gq, gk, batch, seq, tq=512):
    tq = min(tq, seq)
    mlen, w = k.shape[1], k.shape[2]
    return pl.pallas_call(
        _xattn_kernel,
        grid=(batch, seq // tq),
        in_specs=[pl.BlockSpec((1, tq, w), lambda b, i: (b, i, Z_CQ // w)),
                  pl.BlockSpec((1, mlen, w), lambda b, i: (b, 0, 0)),
                  pl.BlockSpec((1, mlen, w), lambda b, i: (b, 0, 0)),
                  pl.BlockSpec((1, C_HEAD_DIM), lambda b, i: (0, 0)),
                  pl.BlockSpec((1, C_HEAD_DIM), lambda b, i: (0, 0))],
        out_specs=pl.BlockSpec((1, tq, w), lambda b, i: (b, i, 0)),
        out_shape=jax.ShapeDtypeStruct((batch, seq, w), BF16),
        compiler_params=_params("parallel", "arbitrary"),
        name="memory_cross_attention",
    )(z3, k, v, gq.reshape(1, -1), gk.reshape(1, -1))


def _rope_rows(xn, cos, sin_signed):
    half = N_HEAD_DIM // 2
    rot = jnp.concatenate([xn[:, half:], xn[:, :half]], axis=-1)
    return xn * cos + rot * sin_signed


def _knorm_rope_kernel(x_ref, g_ref, cos_ref, sin_ref, o_ref, *, n_blocks):
    hd = N_HEAD_DIM
    ts = x_ref.shape[1]
    if n_blocks:
        pos = pl.program_id(1) * ts + lax.broadcasted_iota(jnp.int32, (ts, n_blocks), 0)
        blk = lax.broadcasted_iota(jnp.int32, (ts, n_blocks), 1)
        onehot = jnp.where(pos // SEL_BLOCK == blk, 1.0, 0.0).astype(o_ref.dtype)
    for g in range(N_GROUPS):
        x = x_ref[0, :, g * hd:(g + 1) * hd].astype(F32)
        k = _rope_rows(_rms_rows(x, g_ref[...]), cos_ref[...], sin_ref[...]).astype(o_ref.dtype)
        if n_blocks:
            o_ref[0, g, :, 0:n_blocks] = onehot
        o_ref[0, g, :, n_blocks:n_blocks + hd] = k


def _knorm_rope(z3, col0, g, cos, sin_signed, n_blocks, ts=1024):
    b, s, _ = z3.shape
    hd, w = N_HEAD_DIM, N_GROUPS * N_HEAD_DIM
    ts = min(ts, s)
    return pl.pallas_call(
        functools.partial(_knorm_rope_kernel, n_blocks=n_blocks),
        grid=(b, s // ts),
        in_specs=[pl.BlockSpec((1, ts, w), lambda i, j: (i, j, col0 // w)), pl.BlockSpec((1, hd), lambda i, j: (0, 0)),
                  pl.BlockSpec((ts, hd), lambda i, j: (j, 0)), pl.BlockSpec((ts, hd), lambda i, j: (j, 0))],
        out_specs=pl.BlockSpec((1, N_GROUPS, ts, n_blocks + hd), lambda i, j: (i, 0, j, 0)),
        out_shape=jax.ShapeDtypeStruct((b, N_GROUPS, s, n_blocks + hd), F8 if n_blocks else BF16),
        compiler_params=_params("parallel", "arbitrary"),
        name="key_norm_rope",
    )(z3, g.reshape(1, hd), cos, sin_signed)


def _compress_kernel(sub_ref, pe_ref, w1_ref, w2_ref, g_ref, cos_ref, sin_ref, o_ref, *, is_key):
    ns = sub_ref.shape[1]
    sub = sub_ref[0].astype(F32)
    lo = jnp.dot((sub + pe_ref[0:1, :]).astype(BF16), w1_ref[0], preferred_element_type=F32)
    hi = jnp.dot((sub + pe_ref[1:2, :]).astype(BF16), w1_ref[1], preferred_element_type=F32)
    hid = _silu(lo + pltpu.roll(hi, shift=ns - 1, axis=0))
    out = jnp.dot(hid.astype(BF16), w2_ref[...], preferred_element_type=F32)
    if is_key:
        out = _rope_rows(_rms_rows(out, g_ref[...]), cos_ref[...], sin_ref[...])
    o_ref[0] = out.astype(o_ref.dtype)


def _compress(u, pe, w1, w2, g, cos, sin_signed, is_key):
    n, s, hd = u.shape
    ns = s // CMP_STRIDE
    width = CMP_STRIDE * hd
    sub = u.reshape(n, ns, width)
    pe2 = pe.reshape(CMP_LEN // CMP_STRIDE, width)
    w1s = w1.reshape(CMP_LEN // CMP_STRIDE, width, CMP_HIDDEN).astype(BF16)
    return pl.pallas_call(
        functools.partial(_compress_kernel, is_key=is_key),
        grid=(n,),
        in_specs=[pl.BlockSpec((1, ns, width), lambda i: (i, 0, 0)),
                  pl.BlockSpec(pe2.shape, lambda i: (0, 0)),
                  pl.BlockSpec(w1s.shape, lambda i: (0, 0, 0)),
                  pl.BlockSpec(w2.shape, lambda i: (0, 0)),
                  pl.BlockSpec((1, hd), lambda i: (0, 0)),
                  pl.BlockSpec((ns, hd), lambda i: (0, 0)),
                  pl.BlockSpec((ns, hd), lambda i: (0, 0))],
        out_specs=pl.BlockSpec((1, ns, hd), lambda i: (i, 0, 0)),
        out_shape=jax.ShapeDtypeStruct((n, ns, hd), BF16),
        compiler_params=_params("parallel"),
        name="compress_key" if is_key else "compress_value",
    )(sub, pe2, w1s, w2.astype(BF16), g.reshape(1, hd), cos, sin_signed)


def _nsa_kernel(q_ref, gate_ref, kc_ref, vct_ref, ks_ref, vst_ref, kw_ref, vwt_ref, gq_ref, cos_ref, sin_ref,
                wimp_ref, y_ref, s_c, e_c, s_w, e_w, s_a, s_b, e_a, e_b, *, seq):
    tq = q_ref.shape[1]
    hd = N_HEAD_DIM
    nb = seq // SEL_BLOCK
    top = min(SEL_TOP, nb)
    ncol = N_REP * tq
    t0 = pl.program_id(2) * tq

    def lanes(parts):
        return jnp.concatenate(parts, axis=1)

    def head_cols(a, r):
        return a[:, r * tq:(r + 1) * tq]

    def with_ones(v):
        n = v.shape[1]
        tail = jnp.where(lax.broadcasted_iota(jnp.int32, (V_AUG_ROWS, n), 0) == 0, 1.0, 0.0).astype(v.dtype)
        return jnp.concatenate([v, tail], axis=0)

    def chunk_scores(src, c, valid_fn):
        sc = src[c * NSA_RC:(c + 1) * NSA_RC, :]
        if valid_fn is None:
            return sc
        valid = valid_fn(c * NSA_RC + lax.broadcasted_iota(jnp.int32, (NSA_RC, 1), 0))
        return lanes([jnp.where(valid, head_cols(sc, r), NEG_BIG) for r in range(N_REP)])

    def col_max(src, rows, valid_fn):
        mx = jnp.full((8, ncol), NEG_BIG, F32)
        for c in range(rows // NSA_RC):
            sc = chunk_scores(src, c, valid_fn)
            mx = jnp.maximum(mx, jnp.max(sc.reshape(NSA_RC // 8, 8, ncol), axis=0))
        return jnp.max(mx, axis=0, keepdims=True)

    def put_weights(src, dst, rows, valid_fn, m):
        for c in range(rows // NSA_RC):
            sc = chunk_scores(src, c, valid_fn)
            dst[c * NSA_RC:(c + 1) * NSA_RC, :] = jnp.exp2((sc - m).astype(BF16)).astype(dst.dtype)

    q = lanes([q_ref[r * hd:(r + 1) * hd, :].astype(F32) for r in range(N_REP)])
    qn = q * lax.rsqrt(jnp.mean(q * q, axis=0, keepdims=True) + EPS) * gq_ref[...]
    cos = lanes([cos_ref[...]] * N_REP)
    sin = lanes([sin_ref[...]] * N_REP)
    rot = jnp.concatenate([qn[hd // 2:], qn[:hd // 2]], axis=0)
    qr = ((qn * cos + rot * sin) * (hd ** -0.5 * LOG2_E)).astype(BF16)
    t_q = t0 + lax.broadcasted_iota(jnp.int32, (1, tq), 1)
    tcol = lanes([t_q] * N_REP)

    nc = kc_ref.shape[2]
    s_c[...] = jnp.dot(kc_ref[0, 0], qr, preferred_element_type=F32)
    valid_c = lambda n: n * CMP_STRIDE + (CMP_LEN - 1) <= t_q
    put_weights(s_c, e_c, nc, valid_c, col_max(s_c, nc, valid_c))
    both = jnp.dot(jnp.concatenate([vct_ref[0, 0], wimp_ref[...]], axis=0), e_c[...], preferred_element_type=F32)
    inv_c = jnp.where(tcol >= CMP_LEN - 1, 1.0 / both[hd:hd + 1], 0.0)
    o_c = both[0:hd] * inv_c

    imp_h = both[hd + V_AUG_ROWS:] * inv_c
    imp = head_cols(imp_h, 0)
    for r in range(1, N_REP):
        imp = imp + head_cols(imp_h, r)

    span = WINDOW + tq
    ws0 = pl.multiple_of(jnp.maximum(t0 - WINDOW, 0), tq)
    s_w[...] = jnp.dot(kw_ref[0, 0, pl.ds(ws0, span), :], qr, preferred_element_type=F32)
    valid_w = lambda k: (ws0 + k <= t_q) & (ws0 + k > t_q - WINDOW)
    put_weights(s_w, e_w, span, valid_w, col_max(s_w, span, valid_w))
    o_w = jnp.dot(with_ones(vwt_ref[:, pl.ds(ws0, span)]), e_w[...], preferred_element_type=F32)
    o_w = o_w[0:hd] / o_w[hd:hd + 1]

    blk = lax.broadcasted_iota(jnp.int32, (nb, tq), 0)
    cur = (t0 + lax.broadcasted_iota(jnp.int32, (nb, tq), 1)) // SEL_BLOCK
    causal_b = blk <= cur
    forced = (blk == 0) | (blk == cur) | (blk == cur - 1)
    score = jnp.where(forced, FORCE_SCORE, jnp.where(causal_b, imp, -1.0))
    taken = -3.0e38

    def pick_by_value(sc):
        for _ in range(top - 2):
            sc = jnp.where(sc == jnp.max(sc, axis=0, keepdims=True), taken, sc)
        return sc

    def pick_by_value_then_index(sc):
        for _ in range(top):
            mx = jnp.max(sc, axis=0, keepdims=True)
            idx = jnp.min(jnp.where(sc == mx, blk, nb), axis=0, keepdims=True)
            sc = jnp.where(blk == idx, taken, sc)
        return sc

    fast = pick_by_value(score)
    n_taken = jnp.sum(jnp.where((fast == taken) & causal_b, 1.0, 0.0), axis=0, keepdims=True)
    n_want = jnp.minimum(cur[0:1, :] + 1, top).astype(F32)
    tied = jnp.max(jnp.abs(n_taken - n_want)) > 0.0
    picked = lax.cond(tied, lambda: pick_by_value_then_index(score), lambda: fast)
    bias = jnp.where((picked == taken) & causal_b, 0.0, BLOCK_BIAS)
    q_aug = jnp.concatenate([lanes([bias.astype(BF16)] * N_REP), qr], axis=0).astype(F8)

    tk = min(NSA_TK, seq)
    last = (t0 + tq - 1) // tk
    s_bufs, e_bufs = (s_a, s_b), (e_a, e_b)

    def scores(t, par):
        ks0 = pl.multiple_of(t * tk, tk)
        s_bufs[par][...] = jnp.dot(ks_ref[0, 0, pl.ds(ks0, tk), :], q_aug, preferred_element_type=F32)

    def weights(t, par, m, masked):
        valid_s = (lambda k: t * tk + k <= t_q) if masked else None
        m_new = jnp.maximum(m, col_max(s_bufs[par], tk, valid_s))
        put_weights(s_bufs[par], e_bufs[par], tk, valid_s, m_new - SEL_WEIGHT_SHIFT)
        return m_new, jnp.exp2(m - m_new)

    def values(t, par, alpha, acc):
        ks0 = pl.multiple_of(jnp.maximum(t, 0) * tk, tk)
        return alpha * acc + jnp.dot(with_ones(vst_ref[:, pl.ds(ks0, tk)]).astype(F8), e_bufs[par][...],
                                     preferred_element_type=F32)

    def step(t, par, carry, masked, more):
        m, alpha, acc = carry
        if more:
            scores(t + 1, 1 - par)
        acc = values(t - 1, 1 - par, alpha, acc)
        m, alpha = weights(t, par, m, masked)
        return m, alpha, acc

    def pair_body(j, carry):
        return step(2 * j + 1, 1, step(2 * j, 0, carry, False, True), False, True)

    def tail_even(carry):
        m, alpha, acc = step(last, 0, carry, True, False)
        return values(last, 0, alpha, acc)

    def tail_odd(carry):
        m, alpha, acc = step(last, 1, step(last - 1, 0, carry, False, True), True, False)
        return values(last, 1, alpha, acc)

    scores(0, 0)
    e_b[...] = jnp.zeros(e_b.shape, e_b.dtype)
    carry = (jnp.full((1, ncol), NEG_BIG, F32), jnp.ones((1, ncol), F32), jnp.zeros((hd + V_AUG_ROWS, ncol), F32))
    carry = lax.fori_loop(0, last // 2, pair_body, carry)
    acc = lax.cond(last % 2 == 0, tail_even, tail_odd, carry)
    o_s = acc[0:hd] / acc[hd:hd + 1]

    g = _sigmoid(gate_ref[0, 0])
    gc, gs, gw = (lanes([g[j, r:r + 1, :] for r in range(N_REP)]) for j in range(3))
    out = gc * o_c + gs * o_s + gw * o_w
    for r in range(N_REP):
        y_ref[r * hd:(r + 1) * hd, :] = head_cols(out, r).astype(y_ref.dtype)


def _nsa(zt, gates, kc, vct, ks_aug, kw, gq, cos_t, sin_t, wimp_t, batch, seq):
    g, hd = N_GROUPS, N_HEAD_DIM
    tq = min(NSA_TQ, seq)
    tk = min(NSA_TK, seq)
    nq = seq // tq
    ncol = N_REP * tq
    nc = kc.shape[2]
    nb = seq // SEL_BLOCK
    full = lambda *shape: pl.BlockSpec((1, 1) + shape, lambda bi, gi, qi: (bi, gi) + (0,) * len(shape))
    vrow = lambda row0: pl.BlockSpec((hd, seq), functools.partial(lambda bi, gi, qi, o: (o + gi, bi), o=row0 // hd))
    qspec = pl.BlockSpec((N_REP * hd, tq), lambda bi, gi, qi: (ZT_Q // (N_REP * hd) + gi, bi * nq + qi))
    return pl.pallas_call(
        functools.partial(_nsa_kernel, seq=seq),
        grid=(batch, g, nq),
        in_specs=[qspec,
                  pl.BlockSpec((1, 1, 3, N_REP, tq), lambda bi, gi, qi: (bi, gi, 0, 0, qi)),
                  full(nc, hd), full(hd + V_AUG_ROWS, nc), full(seq, nb + hd), vrow(ZT_VS), full(seq, hd),
                  vrow(ZT_VW),
                  pl.BlockSpec((hd, 1), lambda bi, gi, qi: (0, 0)),
                  pl.BlockSpec((hd, tq), lambda bi, gi, qi: (0, qi)),
                  pl.BlockSpec((hd, tq), lambda bi, gi, qi: (0, qi)),
                  pl.BlockSpec((nb, nc), lambda bi, gi, qi: (0, 0))],
        out_specs=pl.BlockSpec((N_REP * hd, tq), lambda bi, gi, qi: (gi, bi * nq + qi)),
        out_shape=jax.ShapeDtypeStruct((N_HEADS * hd, batch * seq), BF16),
        scratch_shapes=[pltpu.VMEM((nc, ncol), F32), pltpu.VMEM((nc, ncol), BF16),
                        pltpu.VMEM((WINDOW + tq, ncol), F32), pltpu.VMEM((WINDOW + tq, ncol), BF16),
                        pltpu.VMEM((tk, ncol), F32), pltpu.VMEM((tk, ncol), F32),
                        pltpu.VMEM((tk, ncol), F8), pltpu.VMEM((tk, ncol), F8)],
        compiler_params=_params("parallel", "parallel", "arbitrary"),
        name="nsa_attention",
    )(zt, gates, kc, vct, ks_aug, zt, kw, zt, gq.reshape(hd, 1), cos_t, sin_t, wimp_t)


def _rope_tables(pos):
    half = N_HEAD_DIM // 2
    inv = jnp.power(ROPE_THETA, -jnp.arange(half, dtype=F32) * 2.0 / N_HEAD_DIM)
    ang = pos.astype(F32)[:, None] * inv[None, :]
    cos, sin = jnp.cos(ang), jnp.sin(ang)
    return jnp.concatenate([cos, cos], axis=-1), jnp.concatenate([-sin, sin], axis=-1)


def _importance_matrix(nb, nc_pad):
    per = SEL_BLOCK // CMP_STRIDE
    j = np.arange(nb)[:, None]
    n = np.arange(nc_pad)[None, :]
    w = np.zeros((nb, nc_pad), np.float32)
    for d in range(CMP_LEN // CMP_STRIDE):
        w += ((n + d >= per * j) & (n + d <= per * j + per - 1)).astype(np.float32)
    return jnp.asarray(w, BF16)


def _split_w_in(w_in):
    d = w_in.shape[0]
    sizes = (2048, 1024, 1024, 4, 4, 1024, 256, 256, 256, 256, 256, 256, 48, 1024, 2048, 2048, 2048)
    offs = np.concatenate([[0], np.cumsum(sizes)])
    part = lambda i: w_in[:, offs[i]:offs[i + 1]]
    tok = jnp.concatenate([part(i) for i in (0, 1, 2, 13, 14, 15, 16, 6, 7, 8, 10)], axis=1).astype(BF16)
    feat = jnp.concatenate([part(i) for i in (5, 9, 11)], axis=1).T.astype(BF16)
    small = jnp.concatenate([part(3), part(4), part(12), jnp.zeros((d, 128 - 56), w_in.dtype)], axis=1).astype(BF16)
    return tok, feat, small


def _layer(x, mem, norm_mix_g, norm_mem_g, norm_ffn_g, w_in, m_conv_w, m_i_bias, m_f_bias, m_norm_g,
           n_q_norm_g, n_kc_norm_g, n_ks_norm_g, n_kw_norm_g, n_cmp_pe_k, n_cmp_w1_k, n_cmp_w2_k,
           n_cmp_pe_v, n_cmp_w1_v, n_cmp_w2_v, c_q_norm_g, c_k_norm_g, w_mem_k, w_mem_v,
           w_up_a, w_up_b, w_up_c, w_out, w_ffn_gate, w_ffn_up, w_ffn_down):
    B, S, D = x.shape
    M = B * S
    G, R, hd = N_GROUPS, N_REP, N_HEAD_DIM
    x2 = x.reshape(M, D)

    w_tok, w_feat, w_small = _split_w_in(w_in)
    z, zs, h = _in_proj(x2, norm_mix_g, w_tok, w_small)
    zt = _mm_feature_major(w_feat, h)
    z3 = z.reshape(B, S, Z_WIDTH)
    zs3 = zs.reshape(B, S, 128)

    gif = jnp.stack([zs3[..., 0:4], zs3[..., 4:8]], axis=-1)
    gcol = gif.transpose(0, 2, 1, 3)
    grow = gif.transpose(0, 2, 3, 1)
    gbias = jnp.stack([m_i_bias, m_f_bias], axis=-1).reshape(M_HEADS, 1, 2).astype(F32)
    y_a = _mlstm(z3, gcol, grow, gbias, m_conv_w, m_norm_g, B, S)

    pos = jnp.arange(S, dtype=jnp.int32)
    cos, sin_s = _rope_tables(pos)
    nc_pad = S // CMP_STRIDE
    cmp_end = jnp.arange(nc_pad, dtype=jnp.int32) * CMP_STRIDE + CMP_LEN - 1
    cos_c, sin_c = _rope_tables(cmp_end)
    nb = S // SEL_BLOCK

    def head_major(col0):
        u = z3[..., col0:col0 + G * hd].reshape(B, S, G, hd)
        return u.transpose(0, 2, 1, 3).reshape(B * G, S, hd)

    kc = _compress(head_major(Z_KC), n_cmp_pe_k, n_cmp_w1_k, n_cmp_w2_k, n_kc_norm_g, cos_c, sin_c, True)
    vc = _compress(head_major(Z_VC), n_cmp_pe_v, n_cmp_w1_v, n_cmp_w2_v, n_kc_norm_g, cos_c, sin_c, False)
    vct = vc.reshape(B, G, nc_pad, hd).transpose(0, 1, 3, 2)
    ones_rows = jnp.concatenate([jnp.ones((B, G, 1, nc_pad), BF16), jnp.zeros((B, G, V_AUG_ROWS - 1, nc_pad), BF16)],
                                axis=2)
    vct = jnp.concatenate([vct, ones_rows], axis=2)
    ks_aug = _knorm_rope(z3, Z_KS, n_ks_norm_g, cos, sin_s, nb)
    kw = _knorm_rope(z3, Z_KW, n_kw_norm_g, cos, sin_s, 0)
    gates = zs3[..., 8:8 + 3 * N_HEADS].reshape(B, S, G, R, 3).transpose(0, 2, 4, 3, 1)
    y_bt = _nsa(zt, gates, kc.reshape(B, G, nc_pad, hd), vct, ks_aug, kw, n_q_norm_g, cos.T, sin_s.T,
                _importance_matrix(nb, nc_pad), B, S)

    mlen = mem.shape[1]
    mem_n = _rmsnorm(mem.reshape(B * mlen, D), norm_mem_g)
    k_mem = _mm(mem_n, w_mem_k.astype(BF16), BF16).reshape(B, mlen, -1)
    v_mem = _mm(mem_n, w_mem_v.astype(BF16), BF16).reshape(B, mlen, -1)
    y_c = _xattn(z3, k_mem, v_mem, c_q_norm_g, c_k_norm_g, B, S)

    mix = _mix(y_a.reshape(M, -1), y_bt, y_c.reshape(M, -1),
               w_up_a.astype(BF16), w_up_b.astype(BF16), w_up_c.astype(BF16), z, Z_GATES)
    x2 = _mm_residual(mix, w_out.astype(BF16), x2, tm=512, tn=D)

    act = _swiglu(x2, norm_ffn_g, w_ffn_gate.astype(BF16), w_ffn_up.astype(BF16))
    x2 = _mm_residual(act, w_ffn_down.astype(BF16), x2, tm=1024, tn=512)
    return x2.reshape(B, S, D)


def kernel(x, mem, norm_mix_g, norm_mem_g, norm_ffn_g, w_in, m_conv_w, m_i_bias, m_f_bias, m_norm_g, n_q_norm_g, n_kc_norm_g, n_ks_norm_g, n_kw_norm_g, n_cmp_pe_k, n_cmp_w1_k, n_cmp_w2_k, n_cmp_pe_v, n_cmp_w1_v, n_cmp_w2_v, c_q_norm_g, c_k_norm_g, w_mem_k, w_mem_v, w_up_a, w_up_b, w_up_c, w_out, w_ffn_gate, w_ffn_up, w_ffn_down):
    stacked = (norm_mix_g, norm_mem_g, norm_ffn_g, w_in, m_conv_w, m_i_bias, m_f_bias, m_norm_g, n_q_norm_g,
               n_kc_norm_g, n_ks_norm_g, n_kw_norm_g, n_cmp_pe_k, n_cmp_w1_k, n_cmp_w2_k, n_cmp_pe_v, n_cmp_w1_v,
               n_cmp_w2_v, c_q_norm_g, c_k_norm_g, w_mem_k, w_mem_v, w_up_a, w_up_b, w_up_c, w_out, w_ffn_gate,
               w_ffn_up, w_ffn_down)
    for layer in range(w_in.shape[0]):
        x = _layer(x, mem, *(p[layer] for p in stacked))
    return x
```

```python
import functools

import numpy as np
import jax
import jax.numpy as jnp
from jax import lax
from jax.experimental import pallas as pl
from jax.experimental.pallas import tpu as pltpu

F32 = jnp.float32
BF16 = jnp.bfloat16

EPS = 1e-6
ROPE_THETA = 10000.0
M_HEADS = 4
M_HEAD_DIM = 256
M_CONV = 4
N_HEADS = 16
N_GROUPS = 4
N_REP = N_HEADS // N_GROUPS
N_HEAD_DIM = 64
CMP_LEN = 32
CMP_STRIDE = 16
CMP_HIDDEN = 128
SEL_BLOCK = 64
SEL_TOP = 16
WINDOW = 512
FORCE_SCORE = 1.0e4
C_HEADS = 4
C_HEAD_DIM = 256

LOG2_E = 1.4426950408889634
V_AUG_ROWS = 16
NEG_BIG = -1.0e30
F8 = jnp.float8_e4m3fn
BLOCK_BIAS = -448.0
SEL_WEIGHT_SHIFT = 8.0
VMEM_LIMIT = 56 * 1024 * 1024

MLSTM_CHUNK = 256
NSA_TQ = 256
NSA_TK = 512
NSA_RC = 32

Z_MQK, Z_MV, Z_MO, Z_CQ, Z_GATES, Z_KC, Z_VC, Z_KS, Z_KW, Z_WIDTH = (
    0, 2048, 3072, 4096, 5120, 11264, 11520, 11776, 12032, 12288)
ZT_Q, ZT_VS, ZT_VW, ZT_ROWS = 0, 1024, 1280, 1536


def _params(*sem):
    return pltpu.CompilerParams(dimension_semantics=sem, vmem_limit_bytes=VMEM_LIMIT)


def _sigmoid(x):
    return 1.0 / (1.0 + jnp.exp(-x))


def _silu(x):
    return x * _sigmoid(x)


def _log_sigmoid(x):
    return jnp.minimum(x, 0.0) - jnp.log1p(jnp.exp(-jnp.abs(x)))


def _rms_rows(x, g):
    return x * lax.rsqrt(jnp.mean(x * x, axis=-1, keepdims=True) + EPS) * g


def _rmsnorm_kernel(x_ref, g_ref, o_ref):
    o_ref[...] = _rms_rows(x_ref[...], g_ref[...]).astype(o_ref.dtype)


def _rmsnorm(x2d, g, tm=512):
    m, d = x2d.shape
    tm = min(tm, m)
    return pl.pallas_call(
        _rmsnorm_kernel,
        grid=(m // tm,),
        in_specs=[pl.BlockSpec((tm, d), lambda i: (i, 0)), pl.BlockSpec((1, d), lambda i: (0, 0))],
        out_specs=pl.BlockSpec((tm, d), lambda i: (i, 0)),
        out_shape=jax.ShapeDtypeStruct((m, d), BF16),
        compiler_params=_params("parallel"),
        name="rmsnorm",
    )(x2d, g.reshape(1, d))


def _mm_kernel(a_ref, w_ref, o_ref):
    o_ref[...] = jnp.dot(a_ref[...], w_ref[...], preferred_element_type=F32).astype(o_ref.dtype)


def _mm(a, w, out_dtype, tm=1024, tn=512):
    m, k = a.shape
    n = w.shape[1]
    tm, tn = min(tm, m), min(tn, n)
    return pl.pallas_call(
        _mm_kernel,
        grid=(m // tm, n // tn),
        in_specs=[pl.BlockSpec((tm, k), lambda i, j: (i, 0)), pl.BlockSpec((k, tn), lambda i, j: (0, j))],
        out_specs=pl.BlockSpec((tm, tn), lambda i, j: (i, j)),
        out_shape=jax.ShapeDtypeStruct((m, n), out_dtype),
        compiler_params=_params("parallel", "arbitrary"),
        name="matmul",
    )(a, w)


def _in_proj_kernel(x_ref, g_ref, w_ref, ws_ref, z_ref, zs_ref, h_ref):
    @pl.when(pl.program_id(1) == 0)
    def _():
        h = _rms_rows(x_ref[...], g_ref[...]).astype(BF16)
        h_ref[...] = h
        zs_ref[...] = jnp.dot(h, ws_ref[...], preferred_element_type=F32)

    z_ref[...] = jnp.dot(h_ref[...], w_ref[...], preferred_element_type=F32).astype(z_ref.dtype)


def _in_proj(x2d, g, w, w_small, tm=1024, tn=1024):
    m, d = x2d.shape
    n, ns = w.shape[1], w_small.shape[1]
    tm, tn = min(tm, m), min(tn, n)
    return pl.pallas_call(
        _in_proj_kernel,
        grid=(m // tm, n // tn),
        in_specs=[pl.BlockSpec((tm, d), lambda i, j: (i, 0)), pl.BlockSpec((1, d), lambda i, j: (0, 0)),
                  pl.BlockSpec((d, tn), lambda i, j: (0, j)), pl.BlockSpec((d, ns), lambda i, j: (0, 0))],
        out_specs=[pl.BlockSpec((tm, tn), lambda i, j: (i, j)), pl.BlockSpec((tm, ns), lambda i, j: (i, 0)),
                   pl.BlockSpec((tm, d), lambda i, j: (i, 0))],
        out_shape=[jax.ShapeDtypeStruct((m, n), BF16), jax.ShapeDtypeStruct((m, ns), F32),
                   jax.ShapeDtypeStruct((m, d), BF16)],
        compiler_params=_params("parallel", "arbitrary"),
        name="in_proj",
    )(x2d, g.reshape(1, d), w, w_small)


def _mm_nt_kernel(w_ref, h_ref, o_ref):
    o_ref[...] = lax.dot_general(w_ref[...], h_ref[...], (((1,), (1,)), ((), ())),
                                 preferred_element_type=F32).astype(o_ref.dtype)


def _mm_feature_major(w_t, h, tm=1024, tn=1536):
    n, k = w_t.shape
    m = h.shape[0]
    tm, tn = min(tm, m), min(tn, n)
    return pl.pallas_call(
        _mm_nt_kernel,
        grid=(m // tm, n // tn),
        in_specs=[pl.BlockSpec((tn, k), lambda i, j: (j, 0)), pl.BlockSpec((tm, k), lambda i, j: (i, 0))],
        out_specs=pl.BlockSpec((tn, tm), lambda i, j: (j, i)),
        out_shape=jax.ShapeDtypeStruct((n, m), BF16),
        compiler_params=_params("parallel", "arbitrary"),
        name="in_proj_feature_major",
    )(w_t, h)


def _mm_res_kernel(a_ref, w_ref, x_ref, o_ref):
    o_ref[...] = x_ref[...] + jnp.dot(a_ref[...], w_ref[...], preferred_element_type=F32)


def _mm_residual(a, w, x, tm=512, tn=512):
    m, k = a.shape
    n = w.shape[1]
    tm, tn = min(tm, m), min(tn, n)
    return pl.pallas_call(
        _mm_res_kernel,
        grid=(m // tm, n // tn),
        in_specs=[pl.BlockSpec((tm, k), lambda i, j: (i, 0)), pl.BlockSpec((k, tn), lambda i, j: (0, j)),
                  pl.BlockSpec((tm, tn), lambda i, j: (i, j))],
        out_specs=pl.BlockSpec((tm, tn), lambda i, j: (i, j)),
        out_shape=jax.ShapeDtypeStruct((m, n), F32),
        compiler_params=_params("parallel", "arbitrary"),
        name="matmul_residual",
    )(a, w, x)


def _swiglu_kernel(x_ref, g_ref, wg_ref, wu_ref, o_ref, h_scr):
    @pl.when(pl.program_id(1) == 0)
    def _():
        h_scr[...] = _rms_rows(x_ref[...], g_ref[...]).astype(BF16)

    h = h_scr[...]
    gate = jnp.dot(h, wg_ref[...], preferred_element_type=F32)
    up = jnp.dot(h, wu_ref[...], preferred_element_type=F32)
    o_ref[...] = (_silu(gate) * up).astype(o_ref.dtype)


def _swiglu(x2d, g, wg, wu, tm=1024, tn=512):
    m, d = x2d.shape
    n = wg.shape[1]
    tm, tn = min(tm, m), min(tn, n)
    return pl.pallas_call(
        _swiglu_kernel,
        grid=(m // tm, n // tn),
        in_specs=[pl.BlockSpec((tm, d), lambda i, j: (i, 0)), pl.BlockSpec((1, d), lambda i, j: (0, 0)),
                  pl.BlockSpec((d, tn), lambda i, j: (0, j)), pl.BlockSpec((d, tn), lambda i, j: (0, j))],
        out_specs=pl.BlockSpec((tm, tn), lambda i, j: (i, j)),
        out_shape=jax.ShapeDtypeStruct((m, n), BF16),
        scratch_shapes=[pltpu.VMEM((tm, d), BF16)],
        compiler_params=_params("parallel", "arbitrary"),
        name="swiglu",
    )(x2d, g.reshape(1, d), wg, wu)


def _mix_kernel(ya_ref, ybt_ref, yc_ref, wa_ref, wb_ref, wc_ref, ga_ref, gb_ref, gc_ref, o_ref):
    def gated(g_ref, prod):
        return _sigmoid(g_ref[...].astype(F32)) * prod

    a = jnp.dot(ya_ref[...], wa_ref[...], preferred_element_type=F32)
    b = lax.dot_general(ybt_ref[...], wb_ref[...], (((0,), (0,)), ((), ())), preferred_element_type=F32)
    c = jnp.dot(yc_ref[...], wc_ref[...], preferred_element_type=F32)
    o_ref[...] = (gated(ga_ref, a) + gated(gb_ref, b) + gated(gc_ref, c)).astype(o_ref.dtype)


def _mix(ya, ybt, yc, wa, wb, wc, z, gate_col0, tm=1024, tn=512):
    m, k = ya.shape
    n = wa.shape[1]
    tm, tn = min(tm, m), min(tn, n)
    gofs = [(gate_col0 + b * n) // tn for b in range(3)]
    y_spec = pl.BlockSpec((tm, k), lambda i, j: (i, 0))
    w_spec = pl.BlockSpec((k, tn), lambda i, j: (0, j))
    g_specs = [pl.BlockSpec((tm, tn), functools.partial(lambda i, j, o: (i, o + j), o=o)) for o in gofs]
    return pl.pallas_call(
        _mix_kernel,
        grid=(m // tm, n // tn),
        in_specs=[y_spec, pl.BlockSpec((k, tm), lambda i, j: (0, i)), y_spec, w_spec, w_spec, w_spec] + g_specs,
        out_specs=pl.BlockSpec((tm, tn), lambda i, j: (i, j)),
        out_shape=jax.ShapeDtypeStruct((m, n), BF16),
        compiler_params=_params("parallel", "arbitrary"),
        name="gated_mix",
    )(ya, ybt, yc, wa, wb, wc, z, z, z)


def _mlstm_kernel(uq_ref, uk_ref, v_ref, o_ref, gcol_ref, grow_ref, bias_ref, cwq_ref, cwk_ref, ng_ref,
                  y_ref, c_scr, n_scr, m_scr, qbuf, kbuf):
    c = pl.program_id(1)
    L = uq_ref.shape[1]
    hd = M_HEAD_DIM

    @pl.when(c == 0)
    def _():
        c_scr[...] = jnp.zeros_like(c_scr)
        n_scr[...] = jnp.zeros_like(n_scr)
        m_scr[...] = jnp.zeros_like(m_scr)
        qbuf[L:L + 8, :] = jnp.zeros((8, qbuf.shape[1]), F32)
        kbuf[L:L + 8, :] = jnp.zeros((8, kbuf.shape[1]), F32)

    def conv_silu(u_ref, buf, cw_ref):
        buf[0:8, :] = buf[L:L + 8, :]
        buf[8:L + 8, :] = u_ref[0].astype(F32)
        cw = cw_ref[...]
        y = cw[0:1, :] * buf[5:5 + L, :]
        for j in range(1, M_CONV):
            y = y + cw[j:j + 1, :] * buf[5 + j:5 + j + L, :]
        return _silu(y)

    q_all = conv_silu(uq_ref, qbuf, cwq_ref)
    k_all = conv_silu(uk_ref, kbuf, cwk_ref) * (hd ** -0.5)
    row = lax.broadcasted_iota(jnp.int32, (L, L), 0)
    col = lax.broadcasted_iota(jnp.int32, (L, L), 1)
    tri = col <= row

    for hh in range(M_HEADS):
        sl = slice(hh * hd, (hh + 1) * hd)
        q, k = q_all[:, sl], k_all[:, sl]
        qb = q.astype(BF16)
        kb = k.astype(BF16)
        vb = v_ref[0, :, sl]

        gcol = gcol_ref[0, hh]
        grow = grow_ref[0, hh]
        bias = bias_ref[hh]
        ic = gcol[:, 0:1] + bias[:, 0:1]
        fc = _log_sigmoid(gcol[:, 1:2] + bias[:, 1:2])
        ir = grow[0:1, :] + bias[:, 0:1]
        fr = _log_sigmoid(grow[1:2, :] + bias[:, 1:2])

        b_col = jnp.sum(jnp.where(tri, fr, 0.0), axis=1, keepdims=True)
        b_row = jnp.sum(jnp.where(row <= col, fc, 0.0), axis=0, keepdims=True)
        m_prev = m_scr[hh, 0:1, 0:1]

        log_d = jnp.where(tri, b_col - b_row + ir, NEG_BIG)
        log_inter = b_col + m_prev
        m_t = jnp.maximum(log_inter, jnp.max(log_d, axis=1, keepdims=True))
        w_intra = jnp.exp(log_d - m_t)
        w_inter = jnp.exp(log_inter - m_t)

        s = lax.dot_general(qb, kb, (((1,), (1,)), ((), ())), preferred_element_type=F32) * w_intra
        c_state = c_scr[hh]
        n_state = n_scr[hh, 0:1, :]
        num = (jnp.dot(s.astype(BF16), vb, preferred_element_type=F32)
               + w_inter * jnp.dot(qb, c_state.astype(BF16), preferred_element_type=F32))
        den = jnp.sum(s, axis=1, keepdims=True) + w_inter * jnp.sum(q * n_state, axis=1, keepdims=True)
        h = num / jnp.maximum(jnp.abs(den), jnp.exp(-m_t))

        b_last = jnp.sum(fc, axis=0, keepdims=True)
        log_g = b_last - b_col + ic
        m_new = jnp.maximum(b_last + m_prev, jnp.max(log_g, axis=0, keepdims=True))
        decay = jnp.exp(b_last + m_prev - m_new)
        wk = k * jnp.exp(log_g - m_new)
        c_scr[hh] = decay * c_state + lax.dot_general(wk.astype(BF16), vb, (((0,), (0,)), ((), ())),
                                                      preferred_element_type=F32)
        n_scr[hh, 0:1, :] = decay * n_state + jnp.sum(wk, axis=0, keepdims=True)
        m_scr[hh] = jnp.broadcast_to(m_new, m_scr.shape[1:])

        hn = _rms_rows(h, ng_ref[hh])
        y_ref[0, :, sl] = (_sigmoid(o_ref[0, :, sl].astype(F32)) * hn).astype(y_ref.dtype)


def _mlstm(z3, gcol, grow, gbias, conv_w, norm_g, batch, seq):
    L = min(MLSTM_CHUNK, seq)
    hd = M_HEAD_DIM
    nh = M_HEADS
    w = nh * hd

    def zspec(col0):
        return pl.BlockSpec((1, L, w), functools.partial(lambda b, c, o: (b, c, o), o=col0 // w))

    return pl.pallas_call(
        _mlstm_kernel,
        grid=(batch, seq // L),
        in_specs=[zspec(Z_MQK), zspec(Z_MQK + w), zspec(Z_MV), zspec(Z_MO),
                  pl.BlockSpec((1, nh, L, 2), lambda b, c: (b, 0, c, 0)),
                  pl.BlockSpec((1, nh, 2, L), lambda b, c: (b, 0, 0, c)),
                  pl.BlockSpec((nh, 1, 2), lambda b, c: (0, 0, 0)),
                  pl.BlockSpec((M_CONV, w), lambda b, c: (0, 0)),
                  pl.BlockSpec((M_CONV, w), lambda b, c: (0, 1)),
                  pl.BlockSpec((nh, 1, hd), lambda b, c: (0, 0, 0))],
        out_specs=pl.BlockSpec((1, L, w), lambda b, c: (b, c, 0)),
        out_shape=jax.ShapeDtypeStruct((batch, seq, w), BF16),
        scratch_shapes=[pltpu.VMEM((nh, hd, hd), F32), pltpu.VMEM((nh, 8, hd), F32), pltpu.VMEM((nh, 8, 128), F32),
                        pltpu.VMEM((L + 8, w), F32), pltpu.VMEM((L + 8, w), F32)],
        compiler_params=_params("parallel", "arbitrary"),
        name="mlstm",
    )(z3, z3, z3, z3, gcol, grow, gbias, conv_w, conv_w, norm_g.reshape(nh, 1, hd))


def _xattn_kernel(q_ref, k_ref, v_ref, gq_ref, gk_ref, o_ref):
    hd = C_HEAD_DIM
    for h in range(C_HEADS):
        sl = slice(h * hd, (h + 1) * hd)
        q = _rms_rows(q_ref[0, :, sl].astype(F32), gq_ref[...]) * (hd ** -0.5)
        k = _rms_rows(k_ref[0, :, sl].astype(F32), gk_ref[...])
        s = lax.dot_general(q.astype(BF16), k.astype(BF16), (((1,), (1,)), ((), ())), preferred_element_type=F32)
        e = jnp.exp(s - jnp.max(s, axis=-1, keepdims=True))
        p = e / jnp.sum(e, axis=-1, keepdims=True)
        o = jnp.dot(p.astype(BF16), v_ref[0, :, sl], preferred_element_type=F32)
        o_ref[0, :, sl] = o.astype(o_ref.dtype)


def _xattn(z3, k, v, gq, gk, batch, seq, tq=512):
    tq = min(tq, seq)
    mlen, w = k.shape[1], k.shape[2]
    return pl.pallas_call(
        _xattn_kernel,
        grid=(batch, seq // tq),
        in_specs=[pl.BlockSpec((1, tq, w), lambda b, i: (b, i, Z_CQ // w)),
                  pl.BlockSpec((1, mlen, w), lambda b, i: (b, 0, 0)),
                  pl.BlockSpec((1, mlen, w), lambda b, i: (b, 0, 0)),
                  pl.BlockSpec((1, C_HEAD_DIM), lambda b, i: (0, 0)),
                  pl.BlockSpec((1, C_HEAD_DIM), lambda b, i: (0, 0))],
        out_specs=pl.BlockSpec((1, tq, w), lambda b, i: (b, i, 0)),
        out_shape=jax.ShapeDtypeStruct((batch, seq, w), BF16),
        compiler_params=_params("parallel", "arbitrary"),
        name="memory_cross_attention",
    )(z3, k, v, gq.reshape(1, -1), gk.reshape(1, -1))


def _rope_rows(xn, cos, sin_signed):
    half = N_HEAD_DIM // 2
    rot = jnp.concatenate([xn[:, half:], xn[:, :half]], axis=-1)
    return xn * cos + rot * sin_signed


def _knorm_rope_kernel(x_ref, g_ref, cos_ref, sin_ref, o_ref, *, n_blocks):
    hd = N_HEAD_DIM
    ts = x_ref.shape[1]
    if n_blocks:
        pos = pl.program_id(1) * ts + lax.broadcasted_iota(jnp.int32, (ts, n_blocks), 0)
        blk = lax.broadcasted_iota(jnp.int32, (ts, n_blocks), 1)
        onehot = jnp.where(pos // SEL_BLOCK == blk, 1.0, 0.0).astype(o_ref.dtype)
    for g in range(N_GROUPS):
        x = x_ref[0, :, g * hd:(g + 1) * hd].astype(F32)
        k = _rope_rows(_rms_rows(x, g_ref[...]), cos_ref[...], sin_ref[...]).astype(o_ref.dtype)
        if n_blocks:
            o_ref[0, g, :, 0:n_blocks] = onehot
        o_ref[0, g, :, n_blocks:n_blocks + hd] = k


def _knorm_rope(z3, col0, g, cos, sin_signed, n_blocks, ts=1024):
    b, s, _ = z3.shape
    hd, w = N_HEAD_DIM, N_GROUPS * N_HEAD_DIM
    ts = min(ts, s)
    return pl.pallas_call(
        functools.partial(_knorm_rope_kernel, n_blocks=n_blocks),
        grid=(b, s // ts),
        in_specs=[pl.BlockSpec((1, ts, w), lambda i, j: (i, j, col0 // w)), pl.BlockSpec((1, hd), lambda i, j: (0, 0)),
                  pl.BlockSpec((ts, hd), lambda i, j: (j, 0)), pl.BlockSpec((ts, hd), lambda i, j: (j, 0))],
        out_specs=pl.BlockSpec((1, N_GROUPS, ts, n_blocks + hd), lambda i, j: (i, 0, j, 0)),
        out_shape=jax.ShapeDtypeStruct((b, N_GROUPS, s, n_blocks + hd), F8 if n_blocks else BF16),
        compiler_params=_params("parallel", "arbitrary"),
        name="key_norm_rope",
    )(z3, g.reshape(1, hd), cos, sin_signed)


def _compress_kernel(sub_ref, pe_ref, w1_ref, w2_ref, g_ref, cos_ref, sin_ref, o_ref, *, is_key):
    ns = sub_ref.shape[1]
    sub = sub_ref[0].astype(F32)
    lo = jnp.dot((sub + pe_ref[0:1, :]).astype(BF16), w1_ref[0], preferred_element_type=F32)
    hi = jnp.dot((sub + pe_ref[1:2, :]).astype(BF16), w1_ref[1], preferred_element_type=F32)
    hid = _silu(lo + pltpu.roll(hi, shift=ns - 1, axis=0))
    out = jnp.dot(hid.astype(BF16), w2_ref[...], preferred_element_type=F32)
    if is_key:
        out = _rope_rows(_rms_rows(out, g_ref[...]), cos_ref[...], sin_ref[...])
    o_ref[0] = out.astype(o_ref.dtype)


def _compress(u, pe, w1, w2, g, cos, sin_signed, is_key):
    n, s, hd = u.shape
    ns = s // CMP_STRIDE
    width = CMP_STRIDE * hd
    sub = u.reshape(n, ns, width)
    pe2 = pe.reshape(CMP_LEN // CMP_STRIDE, width)
    w1s = w1.reshape(CMP_LEN // CMP_STRIDE, width, CMP_HIDDEN).astype(BF16)
    return pl.pallas_call(
        functools.partial(_compress_kernel, is_key=is_key),
        grid=(n,),
        in_specs=[pl.BlockSpec((1, ns, width), lambda i: (i, 0, 0)),
                  pl.BlockSpec(pe2.shape, lambda i: (0, 0)),
                  pl.BlockSpec(w1s.shape, lambda i: (0, 0, 0)),
                  pl.BlockSpec(w2.shape, lambda i: (0, 0)),
                  pl.BlockSpec((1, hd), lambda i: (0, 0)),
                  pl.BlockSpec((ns, hd), lambda i: (0, 0)),
                  pl.BlockSpec((ns, hd), lambda i: (0, 0))],
        out_specs=pl.BlockSpec((1, ns, hd), lambda i: (i, 0, 0)),
        out_shape=jax.ShapeDtypeStruct((n, ns, hd), BF16),
        compiler_params=_params("parallel"),
        name="compress_key" if is_key else "compress_value",
    )(sub, pe2, w1s, w2.astype(BF16), g.reshape(1, hd), cos, sin_signed)


def _nsa_kernel(q_ref, gate_ref, kc_ref, vct_ref, ks_ref, vst_ref, kw_ref, vwt_ref, gq_ref, cos_ref, sin_ref,
                wimp_ref, y_ref, s_c, e_c, s_w, e_w, s_a, s_b, e_a, e_b, *, seq):
    tq = q_ref.shape[1]
    hd = N_HEAD_DIM
    nb = seq // SEL_BLOCK
    top = min(SEL_TOP, nb)
    ncol = N_REP * tq
    t0 = pl.program_id(2) * tq

    def lanes(parts):
        return jnp.concatenate(parts, axis=1)

    def head_cols(a, r):
        return a[:, r * tq:(r + 1) * tq]

    def with_ones(v):
        n = v.shape[1]
        tail = jnp.where(lax.broadcasted_iota(jnp.int32, (V_AUG_ROWS, n), 0) == 0, 1.0, 0.0).astype(v.dtype)
        return jnp.concatenate([v, tail], axis=0)

    def chunk_scores(src, c, valid_fn):
        sc = src[c * NSA_RC:(c + 1) * NSA_RC, :]
        if valid_fn is None:
            return sc
        valid = valid_fn(c * NSA_RC + lax.broadcasted_iota(jnp.int32, (NSA_RC, 1), 0))
        return lanes([jnp.where(valid, head_cols(sc, r), NEG_BIG) for r in range(N_REP)])

    def put_scores(dst, s):
        dst[...] = s
        return jnp.max(s.reshape(s.shape[0] // 8, 8, ncol), axis=0)

    def col_max(src, rows, valid_fn):
        mx = jnp.full((8, ncol), NEG_BIG, F32)
        for c in range(rows // NSA_RC):
            sc = chunk_scores(src, c, valid_fn)
            mx = jnp.maximum(mx, jnp.max(sc.reshape(NSA_RC // 8, 8, ncol), axis=0))
        return jnp.max(mx, axis=0, keepdims=True)

    def put_weights(src, dst, rows, valid_fn, m):
        for c in range(rows // NSA_RC):
            sc = chunk_scores(src, c, valid_fn)
            dst[c * NSA_RC:(c + 1) * NSA_RC, :] = jnp.exp2((sc - m).astype(BF16)).astype(dst.dtype)

    q = lanes([q_ref[r * hd:(r + 1) * hd, :].astype(F32) for r in range(N_REP)])
    qn = q * lax.rsqrt(jnp.mean(q * q, axis=0, keepdims=True) + EPS) * gq_ref[...]
    cos = lanes([cos_ref[...]] * N_REP)
    sin = lanes([sin_ref[...]] * N_REP)
    rot = jnp.concatenate([qn[hd // 2:], qn[:hd // 2]], axis=0)
    qr = ((qn * cos + rot * sin) * (hd ** -0.5 * LOG2_E)).astype(BF16)
    t_q = t0 + lax.broadcasted_iota(jnp.int32, (1, tq), 1)
    tcol = lanes([t_q] * N_REP)

    nc = kc_ref.shape[2]
    s_c[...] = jnp.dot(kc_ref[0, 0], qr, preferred_element_type=F32)
    valid_c = lambda n: n * CMP_STRIDE + (CMP_LEN - 1) <= t_q
    put_weights(s_c, e_c, nc, valid_c, col_max(s_c, nc, valid_c))
    both = jnp.dot(jnp.concatenate([vct_ref[0, 0], wimp_ref[...]], axis=0), e_c[...], preferred_element_type=F32)
    inv_c = jnp.where(tcol >= CMP_LEN - 1, 1.0 / both[hd:hd + 1], 0.0)
    o_c = both[0:hd] * inv_c

    imp_h = both[hd + V_AUG_ROWS:] * inv_c
    imp = head_cols(imp_h, 0)
    for r in range(1, N_REP):
        imp = imp + head_cols(imp_h, r)

    span = WINDOW + tq
    ws0 = pl.multiple_of(jnp.maximum(t0 - WINDOW, 0), tq)
    s_w[...] = jnp.dot(kw_ref[0, 0, pl.ds(ws0, span), :], qr, preferred_element_type=F32)
    valid_w = lambda k: (ws0 + k <= t_q) & (ws0 + k > t_q - WINDOW)
    put_weights(s_w, e_w, span, valid_w, col_max(s_w, span, valid_w))
    o_w = jnp.dot(with_ones(vwt_ref[:, pl.ds(ws0, span)]), e_w[...], preferred_element_type=F32)
    o_w = o_w[0:hd] / o_w[hd:hd + 1]

    blk = lax.broadcasted_iota(jnp.int32, (nb, tq), 0)
    cur = (t0 + lax.broadcasted_iota(jnp.int32, (nb, tq), 1)) // SEL_BLOCK
    causal_b = blk <= cur
    forced = (blk == 0) | (blk == cur) | (blk == cur - 1)
    score = jnp.where(forced, FORCE_SCORE, jnp.where(causal_b, imp, -1.0))
    taken = -3.0e38

    def pick_by_value(sc):
        for _ in range(top - 2):
            sc = jnp.where(sc == jnp.max(sc, axis=0, keepdims=True), taken, sc)
        return sc

    def pick_by_value_then_index(sc):
        for _ in range(top):
            mx = jnp.max(sc, axis=0, keepdims=True)
            idx = jnp.min(jnp.where(sc == mx, blk, nb), axis=0, keepdims=True)
            sc = jnp.where(blk == idx, taken, sc)
        return sc

    fast = pick_by_value(score)
    n_taken = jnp.sum(jnp.where((fast == taken) & causal_b, 1.0, 0.0), axis=0, keepdims=True)
    n_want = jnp.minimum(cur[0:1, :] + 1, top).astype(F32)
    tied = jnp.max(jnp.abs(n_taken - n_want)) > 0.0
    picked = lax.cond(tied, lambda: pick_by_value_then_index(score), lambda: fast)
    bias = jnp.where((picked == taken) & causal_b, 0.0, BLOCK_BIAS)
    q_aug = jnp.concatenate([lanes([bias.astype(BF16)] * N_REP), qr], axis=0).astype(F8)

    tk = min(NSA_TK, seq)
    last = (t0 + tq - 1) // tk
    s_bufs, e_bufs = (s_a, s_b), (e_a, e_b)

    def scores(t, par):
        ks0 = pl.multiple_of(t * tk, tk)
        return put_scores(s_bufs[par], jnp.dot(ks_ref[0, 0, pl.ds(ks0, tk), :], q_aug, preferred_element_type=F32))

    def weights(t, par, m, mx, masked):
        valid_s = (lambda k: t * tk + k <= t_q) if masked else None
        m_new = jnp.maximum(m, jnp.max(mx, axis=0, keepdims=True))
        put_weights(s_bufs[par], e_bufs[par], tk, valid_s, m_new - SEL_WEIGHT_SHIFT)
        return m_new, jnp.exp2(m - m_new)

    def values(t, par, alpha, acc):
        ks0 = pl.multiple_of(jnp.maximum(t, 0) * tk, tk)
        return alpha * acc + jnp.dot(with_ones(vst_ref[:, pl.ds(ks0, tk)]).astype(F8), e_bufs[par][...],
                                     preferred_element_type=F32)

    def step(t, par, carry, masked, more):
        m, alpha, acc, mx = carry
        mx_next = scores(t + 1, 1 - par) if more else mx
        acc = values(t - 1, 1 - par, alpha, acc)
        m, alpha = weights(t, par, m, mx, masked)
        return m, alpha, acc, mx_next

    def pair_body(j, carry):
        return step(2 * j + 1, 1, step(2 * j, 0, carry, False, True), False, True)

    def tail_even(carry):
        m, alpha, acc, _ = step(last, 0, carry, True, False)
        return values(last, 0, alpha, acc)

    def tail_odd(carry):
        m, alpha, acc, _ = step(last, 1, step(last - 1, 0, carry, False, True), True, False)
        return values(last, 1, alpha, acc)

    mx0 = scores(0, 0)
    e_b[...] = jnp.zeros(e_b.shape, e_b.dtype)
    carry = (jnp.full((1, ncol), NEG_BIG, F32), jnp.ones((1, ncol), F32), jnp.zeros((hd + V_AUG_ROWS, ncol), F32),
             mx0)
    carry = lax.fori_loop(0, last // 2, pair_body, carry)
    acc = lax.cond(last % 2 == 0, tail_even, tail_odd, carry)
    o_s = acc[0:hd] / acc[hd:hd + 1]

    g = _sigmoid(gate_ref[0, 0])
    gc, gs, gw = (lanes([g[j, r:r + 1, :] for r in range(N_REP)]) for j in range(3))
    out = gc * o_c + gs * o_s + gw * o_w
    for r in range(N_REP):
        y_ref[r * hd:(r + 1) * hd, :] = head_cols(out, r).astype(y_ref.dtype)


def _nsa(zt, gates, kc, vct, ks_aug, kw, gq, cos_t, sin_t, wimp_t, batch, seq):
    g, hd = N_GROUPS, N_HEAD_DIM
    tq = min(NSA_TQ, seq)
    tk = min(NSA_TK, seq)
    nq = seq // tq
    ncol = N_REP * tq
    nc = kc.shape[2]
    nb = seq // SEL_BLOCK
    full = lambda *shape: pl.BlockSpec((1, 1) + shape, lambda bi, gi, qi: (bi, gi) + (0,) * len(shape))
    vrow = lambda row0: pl.BlockSpec((hd, seq), functools.partial(lambda bi, gi, qi, o: (o + gi, bi), o=row0 // hd))
    qspec = pl.BlockSpec((N_REP * hd, tq), lambda bi, gi, qi: (ZT_Q // (N_REP * hd) + gi, bi * nq + qi))
    return pl.pallas_call(
        functools.partial(_nsa_kernel, seq=seq),
        grid=(batch, g, nq),
        in_specs=[qspec,
                  pl.BlockSpec((1, 1, 3, N_REP, tq), lambda bi, gi, qi: (bi, gi, 0, 0, qi)),
                  full(nc, hd), full(hd + V_AUG_ROWS, nc), full(seq, nb + hd), vrow(ZT_VS), full(seq, hd),
                  vrow(ZT_VW),
                  pl.BlockSpec((hd, 1), lambda bi, gi, qi: (0, 0)),
                  pl.BlockSpec((hd, tq), lambda bi, gi, qi: (0, qi)),
                  pl.BlockSpec((hd, tq), lambda bi, gi, qi: (0, qi)),
                  pl.BlockSpec((nb, nc), lambda bi, gi, qi: (0, 0))],
        out_specs=pl.BlockSpec((N_REP * hd, tq), lambda bi, gi, qi: (gi, bi * nq + qi)),
        out_shape=jax.ShapeDtypeStruct((N_HEADS * hd, batch * seq), BF16),
        scratch_shapes=[pltpu.VMEM((nc, ncol), F32), pltpu.VMEM((nc, ncol), BF16),
                        pltpu.VMEM((WINDOW + tq, ncol), F32), pltpu.VMEM((WINDOW + tq, ncol), BF16),
                        pltpu.VMEM((tk, ncol), F32), pltpu.VMEM((tk, ncol), F32),
                        pltpu.VMEM((tk, ncol), F8), pltpu.VMEM((tk, ncol), F8)],
        compiler_params=_params("parallel", "parallel", "arbitrary"),
        name="nsa_attention",
    )(zt, gates, kc, vct, ks_aug, zt, kw, zt, gq.reshape(hd, 1), cos_t, sin_t, wimp_t)


def _rope_tables(pos):
    half = N_HEAD_DIM // 2
    inv = jnp.power(ROPE_THETA, -jnp.arange(half, dtype=F32) * 2.0 / N_HEAD_DIM)
    ang = pos.astype(F32)[:, None] * inv[None, :]
    cos, sin = jnp.cos(ang), jnp.sin(ang)
    return jnp.concatenate([cos, cos], axis=-1), jnp.concatenate([-sin, sin], axis=-1)


def _importance_matrix(nb, nc_pad):
    per = SEL_BLOCK // CMP_STRIDE
    j = np.arange(nb)[:, None]
    n = np.arange(nc_pad)[None, :]
    w = np.zeros((nb, nc_pad), np.float32)
    for d in range(CMP_LEN // CMP_STRIDE):
        w += ((n + d >= per * j) & (n + d <= per * j + per - 1)).astype(np.float32)
    return jnp.asarray(w, BF16)


def _split_w_in(w_in):
    d = w_in.shape[0]
    sizes = (2048, 1024, 1024, 4, 4, 1024, 256, 256, 256, 256, 256, 256, 48, 1024, 2048, 2048, 2048)
    offs = np.concatenate([[0], np.cumsum(sizes)])
    part = lambda i: w_in[:, offs[i]:offs[i + 1]]
    tok = jnp.concatenate([part(i) for i in (0, 1, 2, 13, 14, 15, 16, 6, 7, 8, 10)], axis=1).astype(BF16)
    feat = jnp.concatenate([part(i) for i in (5, 9, 11)], axis=1).T.astype(BF16)
    small = jnp.concatenate([part(3), part(4), part(12), jnp.zeros((d, 128 - 56), w_in.dtype)], axis=1).astype(BF16)
    return tok, feat, small


def _layer(x, mem, norm_mix_g, norm_mem_g, norm_ffn_g, w_in, m_conv_w, m_i_bias, m_f_bias, m_norm_g,
           n_q_norm_g, n_kc_norm_g, n_ks_norm_g, n_kw_norm_g, n_cmp_pe_k, n_cmp_w1_k, n_cmp_w2_k,
           n_cmp_pe_v, n_cmp_w1_v, n_cmp_w2_v, c_q_norm_g, c_k_norm_g, w_mem_k, w_mem_v,
           w_up_a, w_up_b, w_up_c, w_out, w_ffn_gate, w_ffn_up, w_ffn_down):
    B, S, D = x.shape
    M = B * S
    G, R, hd = N_GROUPS, N_REP, N_HEAD_DIM
    x2 = x.reshape(M, D)

    w_tok, w_feat, w_small = _split_w_in(w_in)
    z, zs, h = _in_proj(x2, norm_mix_g, w_tok, w_small)
    zt = _mm_feature_major(w_feat, h)
    z3 = z.reshape(B, S, Z_WIDTH)
    zs3 = zs.reshape(B, S, 128)

    gif = jnp.stack([zs3[..., 0:4], zs3[..., 4:8]], axis=-1)
    gcol = gif.transpose(0, 2, 1, 3)
    grow = gif.transpose(0, 2, 3, 1)
    gbias = jnp.stack([m_i_bias, m_f_bias], axis=-1).reshape(M_HEADS, 1, 2).astype(F32)
    y_a = _mlstm(z3, gcol, grow, gbias, m_conv_w, m_norm_g, B, S)

    pos = jnp.arange(S, dtype=jnp.int32)
    cos, sin_s = _rope_tables(pos)
    nc_pad = S // CMP_STRIDE
    cmp_end = jnp.arange(nc_pad, dtype=jnp.int32) * CMP_STRIDE + CMP_LEN - 1
    cos_c, sin_c = _rope_tables(cmp_end)
    nb = S // SEL_BLOCK

    def head_major(col0):
        u = z3[..., col0:col0 + G * hd].reshape(B, S, G, hd)
        return u.transpose(0, 2, 1, 3).reshape(B * G, S, hd)

    kc = _compress(head_major(Z_KC), n_cmp_pe_k, n_cmp_w1_k, n_cmp_w2_k, n_kc_norm_g, cos_c, sin_c, True)
    vc = _compress(head_major(Z_VC), n_cmp_pe_v, n_cmp_w1_v, n_cmp_w2_v, n_kc_norm_g, cos_c, sin_c, False)
    vct = vc.reshape(B, G, nc_pad, hd).transpose(0, 1, 3, 2)
    ones_rows = jnp.concatenate([jnp.ones((B, G, 1, nc_pad), BF16), jnp.zeros((B, G, V_AUG_ROWS - 1, nc_pad), BF16)],
                                axis=2)
    vct = jnp.concatenate([vct, ones_rows], axis=2)
    ks_aug = _knorm_rope(z3, Z_KS, n_ks_norm_g, cos, sin_s, nb)
    kw = _knorm_rope(z3, Z_KW, n_kw_norm_g, cos, sin_s, 0)
    gates = zs3[..., 8:8 + 3 * N_HEADS].reshape(B, S, G, R, 3).transpose(0, 2, 4, 3, 1)
    y_bt = _nsa(zt, gates, kc.reshape(B, G, nc_pad, hd), vct, ks_aug, kw, n_q_norm_g, cos.T, sin_s.T,
                _importance_matrix(nb, nc_pad), B, S)

    mlen = mem.shape[1]
    mem_n = _rmsnorm(mem.reshape(B * mlen, D), norm_mem_g)
    k_mem = _mm(mem_n, w_mem_k.astype(BF16), BF16).reshape(B, mlen, -1)
    v_mem = _mm(mem_n, w_mem_v.astype(BF16), BF16).reshape(B, mlen, -1)
    y_c = _xattn(z3, k_mem, v_mem, c_q_norm_g, c_k_norm_g, B, S)

    mix = _mix(y_a.reshape(M, -1), y_bt, y_c.reshape(M, -1),
               w_up_a.astype(BF16), w_up_b.astype(BF16), w_up_c.astype(BF16), z, Z_GATES)
    x2 = _mm_residual(mix, w_out.astype(BF16), x2, tm=512, tn=D)

    act = _swiglu(x2, norm_ffn_g, w_ffn_gate.astype(BF16), w_ffn_up.astype(BF16))
    x2 = _mm_residual(act, w_ffn_down.astype(BF16), x2, tm=1024, tn=512)
    return x2.reshape(B, S, D)


def kernel(x, mem, norm_mix_g, norm_mem_g, norm_ffn_g, w_in, m_conv_w, m_i_bias, m_f_bias, m_norm_g, n_q_norm_g, n_kc_norm_g, n_ks_norm_g, n_kw_norm_g, n_cmp_pe_k, n_cmp_w1_k, n_cmp_w2_k, n_cmp_pe_v, n_cmp_w1_v, n_cmp_w2_v, c_q_norm_g, c_k_norm_g, w_mem_k, w_mem_v, w_up_a, w_up_b, w_up_c, w_out, w_ffn_gate, w_ffn_up, w_ffn_down):
    stacked = (norm_mix_g, norm_mem_g, norm_ffn_g, w_in, m_conv_w, m_i_bias, m_f_bias, m_norm_g, n_q_norm_g,
               n_kc_norm_g, n_ks_norm_g, n_kw_norm_g, n_cmp_pe_k, n_cmp_w1_k, n_cmp_w2_k, n_cmp_pe_v, n_cmp_w1_v,
               n_cmp_w2_v, c_q_norm_g, c_k_norm_g, w_mem_k, w_mem_v, w_up_a, w_up_b, w_up_c, w_out, w_ffn_gate,
               w_ffn_up, w_ffn_down)
    for layer in range(w_in.shape[0]):
        x = _layer(x, mem, *(p[layer] for p in stacked))
    return x
```

```python
import functools

import numpy as np
import jax
import jax.numpy as jnp
from jax import lax
from jax.experimental import pallas as pl
from jax.experimental.pallas import tpu as pltpu

F32 = jnp.float32
BF16 = jnp.bfloat16

EPS = 1e-6
ROPE_THETA = 10000.0
M_HEADS = 4
M_HEAD_DIM = 256
M_CONV = 4
N_HEADS = 16
N_GROUPS = 4
N_REP = N_HEADS // N_GROUPS
N_HEAD_DIM = 64
CMP_LEN = 32
CMP_STRIDE = 16
CMP_HIDDEN = 128
SEL_BLOCK = 64
SEL_TOP = 16
WINDOW = 512
FORCE_SCORE = 1.0e4
C_HEADS = 4
C_HEAD_DIM = 256

LOG2_E = 1.4426950408889634
V_AUG_ROWS = 16
NEG_BIG = -1.0e30
F8 = jnp.float8_e4m3fn
BLOCK_BIAS = -448.0
SEL_WEIGHT_SHIFT = 8.0
VMEM_LIMIT = 56 * 1024 * 1024

MLSTM_CHUNK = 256
NSA_TQ = 256
NSA_TK = 512
NSA_RC = 32
NSA_CB = 1024

Z_MQK, Z_MV, Z_MO, Z_CQ, Z_GATES, Z_KC, Z_VC, Z_KS, Z_KW, Z_WIDTH = (
    0, 2048, 3072, 4096, 5120, 11264, 11520, 11776, 12032, 12288)
ZT_Q, ZT_VS, ZT_VW, ZT_ROWS = 0, 1024, 1280, 1536


def _params(*sem):
    return pltpu.CompilerParams(dimension_semantics=sem, vmem_limit_bytes=VMEM_LIMIT)


def _sigmoid(x):
    return 1.0 / (1.0 + jnp.exp(-x))


def _silu(x):
    return x * _sigmoid(x)


def _log_sigmoid(x):
    return jnp.minimum(x, 0.0) - jnp.log1p(jnp.exp(-jnp.abs(x)))


def _rms_rows(x, g):
    return x * lax.rsqrt(jnp.mean(x * x, axis=-1, keepdims=True) + EPS) * g


def _rmsnorm_kernel(x_ref, g_ref, o_ref):
    o_ref[...] = _rms_rows(x_ref[...], g_ref[...]).astype(o_ref.dtype)


def _rmsnorm(x2d, g, tm=512):
    m, d = x2d.shape
    tm = min(tm, m)
    return pl.pallas_call(
        _rmsnorm_kernel,
        grid=(m // tm,),
        in_specs=[pl.BlockSpec((tm, d), lambda i: (i, 0)), pl.BlockSpec((1, d), lambda i: (0, 0))],
        out_specs=pl.BlockSpec((tm, d), lambda i: (i, 0)),
        out_shape=jax.ShapeDtypeStruct((m, d), BF16),
        compiler_params=_params("parallel"),
        name="rmsnorm",
    )(x2d, g.reshape(1, d))


def _mm_kernel(a_ref, w_ref, o_ref):
    o_ref[...] = jnp.dot(a_ref[...], w_ref[...], preferred_element_type=F32).astype(o_ref.dtype)


def _mm(a, w, out_dtype, tm=1024, tn=512):
    m, k = a.shape
    n = w.shape[1]
    tm, tn = min(tm, m), min(tn, n)
    return pl.pallas_call(
        _mm_kernel,
        grid=(m // tm, n // tn),
        in_specs=[pl.BlockSpec((tm, k), lambda i, j: (i, 0)), pl.BlockSpec((k, tn), lambda i, j: (0, j))],
        out_specs=pl.BlockSpec((tm, tn), lambda i, j: (i, j)),
        out_shape=jax.ShapeDtypeStruct((m, n), out_dtype),
        compiler_params=_params("parallel", "arbitrary"),
        name="matmul",
    )(a, w)


def _in_proj_kernel(x_ref, g_ref, w_ref, ws_ref, z_ref, zs_ref, h_ref):
    @pl.when(pl.program_id(1) == 0)
    def _():
        h = _rms_rows(x_ref[...], g_ref[...]).astype(BF16)
        h_ref[...] = h
        zs_ref[...] = jnp.dot(h, ws_ref[...], preferred_element_type=F32)

    z_ref[...] = jnp.dot(h_ref[...], w_ref[...], preferred_element_type=F32).astype(z_ref.dtype)


def _in_proj(x2d, g, w, w_small, tm=1024, tn=1024):
    m, d = x2d.shape
    n, ns = w.shape[1], w_small.shape[1]
    tm, tn = min(tm, m), min(tn, n)
    return pl.pallas_call(
        _in_proj_kernel,
        grid=(m // tm, n // tn),
        in_specs=[pl.BlockSpec((tm, d), lambda i, j: (i, 0)), pl.BlockSpec((1, d), lambda i, j: (0, 0)),
                  pl.BlockSpec((d, tn), lambda i, j: (0, j)), pl.BlockSpec((d, ns), lambda i, j: (0, 0))],
        out_specs=[pl.BlockSpec((tm, tn), lambda i, j: (i, j)), pl.BlockSpec((tm, ns), lambda i, j: (i, 0)),
                   pl.BlockSpec((tm, d), lambda i, j: (i, 0))],
        out_shape=[jax.ShapeDtypeStruct((m, n), BF16), jax.ShapeDtypeStruct((m, ns), F32),
                   jax.ShapeDtypeStruct((m, d), BF16)],
        compiler_params=_params("parallel", "arbitrary"),
        name="in_proj",
    )(x2d, g.reshape(1, d), w, w_small)


def _mm_nt_kernel(w_ref, h_ref, o_ref):
    o_ref[...] = lax.dot_general(w_ref[...], h_ref[...], (((1,), (1,)), ((), ())),
                                 preferred_element_type=F32).astype(o_ref.dtype)


def _mm_feature_major(w_t, h, tm=1024, tn=1536):
    n, k = w_t.shape
    m = h.shape[0]
    tm, tn = min(tm, m), min(tn, n)
    return pl.pallas_call(
        _mm_nt_kernel,
        grid=(m // tm, n // tn),
        in_specs=[pl.BlockSpec((tn, k), lambda i, j: (j, 0)), pl.BlockSpec((tm, k), lambda i, j: (i, 0))],
        out_specs=pl.BlockSpec((tn, tm), lambda i, j: (j, i)),
        out_shape=jax.ShapeDtypeStruct((n, m), BF16),
        compiler_params=_params("parallel", "arbitrary"),
        name="in_proj_feature_major",
    )(w_t, h)


def _mm_res_kernel(a_ref, w_ref, x_ref, o_ref):
    o_ref[...] = x_ref[...] + jnp.dot(a_ref[...], w_ref[...], preferred_element_type=F32)


def _mm_residual(a, w, x, tm=512, tn=512):
    m, k = a.shape
    n = w.shape[1]
    tm, tn = min(tm, m), min(tn, n)
    return pl.pallas_call(
        _mm_res_kernel,
        grid=(m // tm, n // tn),
        in_specs=[pl.BlockSpec((tm, k), lambda i, j: (i, 0)), pl.BlockSpec((k, tn), lambda i, j: (0, j)),
                  pl.BlockSpec((tm, tn), lambda i, j: (i, j))],
        out_specs=pl.BlockSpec((tm, tn), lambda i, j: (i, j)),
        out_shape=jax.ShapeDtypeStruct((m, n), F32),
        compiler_params=_params("parallel", "arbitrary"),
        name="matmul_residual",
    )(a, w, x)


def _swiglu_kernel(x_ref, g_ref, wg_ref, wu_ref, o_ref, h_scr):
    @pl.when(pl.program_id(1) == 0)
    def _():
        h_scr[...] = _rms_rows(x_ref[...], g_ref[...]).astype(BF16)

    h = h_scr[...]
    gate = jnp.dot(h, wg_ref[...], preferred_element_type=F32)
    up = jnp.dot(h, wu_ref[...], preferred_element_type=F32)
    o_ref[...] = (_silu(gate) * up).astype(o_ref.dtype)


def _swiglu(x2d, g, wg, wu, tm=1024, tn=512):
    m, d = x2d.shape
    n = wg.shape[1]
    tm, tn = min(tm, m), min(tn, n)
    return pl.pallas_call(
        _swiglu_kernel,
        grid=(m // tm, n // tn),
        in_specs=[pl.BlockSpec((tm, d), lambda i, j: (i, 0)), pl.BlockSpec((1, d), lambda i, j: (0, 0)),
                  pl.BlockSpec((d, tn), lambda i, j: (0, j)), pl.BlockSpec((d, tn), lambda i, j: (0, j))],
        out_specs=pl.BlockSpec((tm, tn), lambda i, j: (i, j)),
        out_shape=jax.ShapeDtypeStruct((m, n), BF16),
        scratch_shapes=[pltpu.VMEM((tm, d), BF16)],
        compiler_params=_params("parallel", "arbitrary"),
        name="swiglu",
    )(x2d, g.reshape(1, d), wg, wu)


def _mix_kernel(ya_ref, ybt_ref, yc_ref, wa_ref, wb_ref, wc_ref, ga_ref, gb_ref, gc_ref, o_ref):
    def gated(g_ref, prod):
        return _sigmoid(g_ref[...].astype(F32)) * prod

    a = jnp.dot(ya_ref[...], wa_ref[...], preferred_element_type=F32)
    b = lax.dot_general(ybt_ref[...], wb_ref[...], (((0,), (0,)), ((), ())), preferred_element_type=F32)
    c = jnp.dot(yc_ref[...], wc_ref[...], preferred_element_type=F32)
    o_ref[...] = (gated(ga_ref, a) + gated(gb_ref, b) + gated(gc_ref, c)).astype(o_ref.dtype)


def _mix(ya, ybt, yc, wa, wb, wc, z, gate_col0, tm=1024, tn=512):
    m, k = ya.shape
    n = wa.shape[1]
    tm, tn = min(tm, m), min(tn, n)
    gofs = [(gate_col0 + b * n) // tn for b in range(3)]
    y_spec = pl.BlockSpec((tm, k), lambda i, j: (i, 0))
    w_spec = pl.BlockSpec((k, tn), lambda i, j: (0, j))
    g_specs = [pl.BlockSpec((tm, tn), functools.partial(lambda i, j, o: (i, o + j), o=o)) for o in gofs]
    return pl.pallas_call(
        _mix_kernel,
        grid=(m // tm, n // tn),
        in_specs=[y_spec, pl.BlockSpec((k, tm), lambda i, j: (0, i)), y_spec, w_spec, w_spec, w_spec] + g_specs,
        out_specs=pl.BlockSpec((tm, tn), lambda i, j: (i, j)),
        out_shape=jax.ShapeDtypeStruct((m, n), BF16),
        compiler_params=_params("parallel", "arbitrary"),
        name="gated_mix",
    )(ya, ybt, yc, wa, wb, wc, z, z, z)


def _mlstm_kernel(uq_ref, uk_ref, v_ref, o_ref, gcol_ref, grow_ref, bias_ref, cwq_ref, cwk_ref, ng_ref, shift_ref,
                  y_ref, c_scr, n_scr, m_scr, qtail, ktail):
    c = pl.program_id(1)
    L = uq_ref.shape[1]
    hd = M_HEAD_DIM

    @pl.when(c == 0)
    def _():
        c_scr[...] = jnp.zeros_like(c_scr)
        n_scr[...] = jnp.zeros_like(n_scr)
        m_scr[...] = jnp.zeros_like(m_scr)
        qtail[...] = jnp.zeros_like(qtail)
        ktail[...] = jnp.zeros_like(ktail)

    def conv_silu(u_ref, tail, cw_ref):
        u = u_ref[0]
        cw = cw_ref[...]
        cwb = cw.astype(BF16)
        x = jnp.concatenate([u * cwb[j:j + 1, :] for j in range(M_CONV)], axis=0)
        y = jnp.dot(shift_ref[...], x, preferred_element_type=F32)
        prev = tail[...]
        row = lax.broadcasted_iota(jnp.int32, prev.shape, 0)
        edge = jnp.zeros(prev.shape, F32)
        for t in range(M_CONV - 1):
            acc = cw[0:1, :] * prev[8 - (M_CONV - 1) + t:8 - (M_CONV - 1) + t + 1, :]
            for j in range(1, M_CONV - 1 - t):
                acc = acc + cw[j:j + 1, :] * prev[8 - (M_CONV - 1) + t + j:8 - (M_CONV - 1) + t + j + 1, :]
            edge = jnp.where(row == t, acc, edge)
        tail[...] = u[L - 8:L, :].astype(F32)
        return _silu(jnp.concatenate([y[0:8] + edge, y[8:]], axis=0))

    q_all = conv_silu(uq_ref, qtail, cwq_ref)
    k_all = conv_silu(uk_ref, ktail, cwk_ref) * (hd ** -0.5)
    row = lax.broadcasted_iota(jnp.int32, (L, L), 0)
    col = lax.broadcasted_iota(jnp.int32, (L, L), 1)
    tri = col <= row

    for hh in range(M_HEADS):
        sl = slice(hh * hd, (hh + 1) * hd)
        q, k = q_all[:, sl], k_all[:, sl]
        qb = q.astype(BF16)
        kb = k.astype(BF16)
        vb = v_ref[0, :, sl]

        gcol = gcol_ref[0, hh]
        grow = grow_ref[0, hh]
        bias = bias_ref[hh]
        ic = gcol[:, 0:1] + bias[:, 0:1]
        fc = _log_sigmoid(gcol[:, 1:2] + bias[:, 1:2])
        ir = grow[0:1, :] + bias[:, 0:1]
        fr = _log_sigmoid(grow[1:2, :] + bias[:, 1:2])

        b_col = jnp.sum(jnp.where(tri, fr, 0.0), axis=1, keepdims=True)
        b_row = jnp.sum(jnp.where(row <= col, fc, 0.0), axis=0, keepdims=True)
        m_prev = m_scr[hh, 0:1, 0:1]

        log_d = jnp.where(tri, b_col - b_row + ir, NEG_BIG)
        log_inter = b_col + m_prev
        m_t = jnp.maximum(log_inter, jnp.max(log_d, axis=1, keepdims=True))
        w_intra = jnp.exp(log_d - m_t)
        w_inter = jnp.exp(log_inter - m_t)

        s = lax.dot_general(qb, kb, (((1,), (1,)), ((), ())), preferred_element_type=F32) * w_intra
        c_state = c_scr[hh]
        n_state = n_scr[hh, 0:1, :]
        num = (jnp.dot(s.astype(BF16), vb, preferred_element_type=F32)
               + w_inter * jnp.dot(qb, c_state.astype(BF16), preferred_element_type=F32))
        den = jnp.sum(s, axis=1, keepdims=True) + w_inter * jnp.sum(q * n_state, axis=1, keepdims=True)
        h = num / jnp.maximum(jnp.abs(den), jnp.exp(-m_t))

        b_last = jnp.sum(fc, axis=0, keepdims=True)
        log_g = b_last - b_col + ic
        m_new = jnp.maximum(b_last + m_prev, jnp.max(log_g, axis=0, keepdims=True))
        decay = jnp.exp(b_last + m_prev - m_new)
        wk = k * jnp.exp(log_g - m_new)
        c_scr[hh] = decay * c_state + lax.dot_general(wk.astype(BF16), vb, (((0,), (0,)), ((), ())),
                                                      preferred_element_type=F32)
        n_scr[hh, 0:1, :] = decay * n_state + jnp.sum(wk, axis=0, keepdims=True)
        m_scr[hh] = jnp.broadcast_to(m_new, m_scr.shape[1:])

        hn = _rms_rows(h, ng_ref[hh])
        y_ref[0, :, sl] = (_sigmoid(o_ref[0, :, sl].astype(F32)) * hn).astype(y_ref.dtype)


def _conv_shift_matrix(L):
    t = np.arange(L)[:, None]
    r = np.arange(L)[None, :]
    return jnp.asarray(np.concatenate([(r == t - (M_CONV - 1) + j) for j in range(M_CONV)], axis=1), BF16)


def _mlstm(z3, gcol, grow, gbias, conv_w, norm_g, batch, seq):
    L = min(MLSTM_CHUNK, seq)
    hd = M_HEAD_DIM
    nh = M_HEADS
    w = nh * hd

    def zspec(col0):
        return pl.BlockSpec((1, L, w), functools.partial(lambda b, c, o: (b, c, o), o=col0 // w))

    return pl.pallas_call(
        _mlstm_kernel,
        grid=(batch, seq // L),
        in_specs=[zspec(Z_MQK), zspec(Z_MQK + w), zspec(Z_MV), zspec(Z_MO),
                  pl.BlockSpec((1, nh, L, 2), lambda b, c: (b, 0, c, 0)),
                  pl.BlockSpec((1, nh, 2, L), lambda b, c: (b, 0, 0, c)),
                  pl.BlockSpec((nh, 1, 2), lambda b, c: (0, 0, 0)),
                  pl.BlockSpec((M_CONV, w), lambda b, c: (0, 0)),
                  pl.BlockSpec((M_CONV, w), lambda b, c: (0, 1)),
                  pl.BlockSpec((nh, 1, hd), lambda b, c: (0, 0, 0)),
                  pl.BlockSpec((L, M_CONV * L), lambda b, c: (0, 0))],
        out_specs=pl.BlockSpec((1, L, w), lambda b, c: (b, c, 0)),
        out_shape=jax.ShapeDtypeStruct((batch, seq, w), BF16),
        scratch_shapes=[pltpu.VMEM((nh, hd, hd), F32), pltpu.VMEM((nh, 8, hd), F32), pltpu.VMEM((nh, 8, 128), F32),
                        pltpu.VMEM((8, w), F32), pltpu.VMEM((8, w), F32)],
        compiler_params=_params("parallel", "arbitrary"),
        name="mlstm",
    )(z3, z3, z3, z3, gcol, grow, gbias, conv_w, conv_w, norm_g.reshape(nh, 1, hd), _conv_shift_matrix(L))


def _xattn_kernel(q_ref, k_ref, v_ref, gq_ref, gk_ref, o_ref):
    hd = C_HEAD_DIM
    for h in range(C_HEADS):
        sl = slice(h * hd, (h + 1) * hd)
        q = _rms_rows(q_ref[0, :, sl].astype(F32), gq_ref[...]) * (hd ** -0.5)
        k = _rms_rows(k_ref[0, :, sl].astype(F32), gk_ref[...])
        s = lax.dot_general(q.astype(BF16), k.astype(BF16), (((1,), (1,)), ((), ())), preferred_element_type=F32)
        e = jnp.exp(s - jnp.max(s, axis=-1, keepdims=True))
        p = e / jnp.sum(e, axis=-1, keepdims=True)
        o = jnp.dot(p.astype(BF16), v_ref[0, :, sl], preferred_element_type=F32)
        o_ref[0, :, sl] = o.astype(o_ref.dtype)


def _xattn(z3, k, v, gq, gk, batch, seq, tq=512):
    tq = min(tq, seq)
    mlen, w = k.shape[1], k.shape[2]
    return pl.pallas_call(
        _xattn_kernel,
        grid=(batch, seq // tq),
        in_specs=[pl.BlockSpec((1, tq, w), lambda b, i: (b, i, Z_CQ // w)),
                  pl.BlockSpec((1, mlen, w), lambda b, i: (b, 0, 0)),
                  pl.BlockSpec((1, mlen, w), lambda b, i: (b, 0, 0)),
                  pl.BlockSpec((1, C_HEAD_DIM), lambda b, i: (0, 0)),
                  pl.BlockSpec((1, C_HEAD_DIM), lambda b, i: (0, 0))],
        out_specs=pl.BlockSpec((1, tq, w), lambda b, i: (b, i, 0)),
        out_shape=jax.ShapeDtypeStruct((batch, seq, w), BF16),
        compiler_params=_params("parallel", "arbitrary"),
        name="memory_cross_attention",
    )(z3, k, v, gq.reshape(1, -1), gk.reshape(1, -1))


def _rope_rows(xn, cos, sin_signed):
    half = N_HEAD_DIM // 2
    rot = jnp.concatenate([xn[:, half:], xn[:, :half]], axis=-1)
    return xn * cos + rot * sin_signed


def _knorm_rope_kernel(x_ref, g_ref, cos_ref, sin_ref, o_ref, *, n_blocks):
    hd = N_HEAD_DIM
    ts = x_ref.shape[1]
    if n_blocks:
        pos = pl.program_id(1) * ts + lax.broadcasted_iota(jnp.int32, (ts, n_blocks), 0)
        blk = lax.broadcasted_iota(jnp.int32, (ts, n_blocks), 1)
        onehot = jnp.where(pos // SEL_BLOCK == blk, 1.0, 0.0).astype(o_ref.dtype)
    for g in range(N_GROUPS):
        x = x_ref[0, :, g * hd:(g + 1) * hd].astype(F32)
        k = _rope_rows(_rms_rows(x, g_ref[...]), cos_ref[...], sin_ref[...]).astype(o_ref.dtype)
        if n_blocks:
            o_ref[0, g, :, 0:n_blocks] = onehot
        o_ref[0, g, :, n_blocks:n_blocks + hd] = k


def _knorm_rope(z3, col0, g, cos, sin_signed, n_blocks, ts=1024):
    b, s, _ = z3.shape
    hd, w = N_HEAD_DIM, N_GROUPS * N_HEAD_DIM
    ts = min(ts, s)
    return pl.pallas_call(
        functools.partial(_knorm_rope_kernel, n_blocks=n_blocks),
        grid=(b, s // ts),
        in_specs=[pl.BlockSpec((1, ts, w), lambda i, j: (i, j, col0 // w)), pl.BlockSpec((1, hd), lambda i, j: (0, 0)),
                  pl.BlockSpec((ts, hd), lambda i, j: (j, 0)), pl.BlockSpec((ts, hd), lambda i, j: (j, 0))],
        out_specs=pl.BlockSpec((1, N_GROUPS, ts, n_blocks + hd), lambda i, j: (i, 0, j, 0)),
        out_shape=jax.ShapeDtypeStruct((b, N_GROUPS, s, n_blocks + hd), F8 if n_blocks else BF16),
        compiler_params=_params("parallel", "arbitrary"),
        name="key_norm_rope",
    )(z3, g.reshape(1, hd), cos, sin_signed)


def _compress_kernel(sub_ref, pe_ref, w1_ref, w2_ref, g_ref, cos_ref, sin_ref, o_ref, *, is_key):
    ns = sub_ref.shape[1]
    sub = sub_ref[0].astype(F32)
    lo = jnp.dot((sub + pe_ref[0:1, :]).astype(BF16), w1_ref[0], preferred_element_type=F32)
    hi = jnp.dot((sub + pe_ref[1:2, :]).astype(BF16), w1_ref[1], preferred_element_type=F32)
    hid = _silu(lo + pltpu.roll(hi, shift=ns - 1, axis=0))
    out = jnp.dot(hid.astype(BF16), w2_ref[...], preferred_element_type=F32)
    if is_key:
        out = _rope_rows(_rms_rows(out, g_ref[...]), cos_ref[...], sin_ref[...])
    o_ref[0] = out.astype(o_ref.dtype)


def _compress(u, pe, w1, w2, g, cos, sin_signed, is_key):
    n, s, hd = u.shape
    ns = s // CMP_STRIDE
    width = CMP_STRIDE * hd
    sub = u.reshape(n, ns, width)
    pe2 = pe.reshape(CMP_LEN // CMP_STRIDE, width)
    w1s = w1.reshape(CMP_LEN // CMP_STRIDE, width, CMP_HIDDEN).astype(BF16)
    return pl.pallas_call(
        functools.partial(_compress_kernel, is_key=is_key),
        grid=(n,),
        in_specs=[pl.BlockSpec((1, ns, width), lambda i: (i, 0, 0)),
                  pl.BlockSpec(pe2.shape, lambda i: (0, 0)),
                  pl.BlockSpec(w1s.shape, lambda i: (0, 0, 0)),
                  pl.BlockSpec(w2.shape, lambda i: (0, 0)),
                  pl.BlockSpec((1, hd), lambda i: (0, 0)),
                  pl.BlockSpec((ns, hd), lambda i: (0, 0)),
                  pl.BlockSpec((ns, hd), lambda i: (0, 0))],
        out_specs=pl.BlockSpec((1, ns, hd), lambda i: (i, 0, 0)),
        out_shape=jax.ShapeDtypeStruct((n, ns, hd), BF16),
        compiler_params=_params("parallel"),
        name="compress_key" if is_key else "compress_value",
    )(sub, pe2, w1s, w2.astype(BF16), g.reshape(1, hd), cos, sin_signed)


def _nsa_kernel(q_ref, gate_ref, kc_ref, vct_ref, ks_ref, vst_ref, kw_ref, vwt_ref, gq_ref, cos_ref, sin_ref,
                wimp_ref, y_ref, s_c, e_c, s_w, e_w, s_a, s_b, e_a, e_b, *, seq):
    tq = q_ref.shape[1]
    hd = N_HEAD_DIM
    nb = seq // SEL_BLOCK
    top = min(SEL_TOP, nb)
    ncol = N_REP * tq
    t0 = pl.program_id(2) * tq

    def lanes(parts):
        return jnp.concatenate(parts, axis=1)

    def head_cols(a, r):
        return a[:, r * tq:(r + 1) * tq]

    def with_ones(v):
        n = v.shape[1]
        tail = jnp.where(lax.broadcasted_iota(jnp.int32, (V_AUG_ROWS, n), 0) == 0, 1.0, 0.0).astype(v.dtype)
        return jnp.concatenate([v, tail], axis=0)

    cb = min(ncol, NSA_CB)

    def chunk_scores(src, c, j, valid_fn):
        sc = src[c * NSA_RC:(c + 1) * NSA_RC, j * cb:(j + 1) * cb]
        if valid_fn is None:
            return sc
        valid = valid_fn(c * NSA_RC + lax.broadcasted_iota(jnp.int32, (NSA_RC, 1), 0))
        return lanes([jnp.where(valid, head_cols(sc, r), NEG_BIG) for r in range(cb // tq)])

    def put_scores(dst, s):
        dst[...] = s
        return jnp.max(s.reshape(s.shape[0] // 8, 8, ncol), axis=0)

    def col_max(src, rows, valid_fn):
        out = []
        for j in range(ncol // cb):
            mx = jnp.full((8, cb), NEG_BIG, F32)
            for c in range(rows // NSA_RC):
                sc = chunk_scores(src, c, j, valid_fn)
                mx = jnp.maximum(mx, jnp.max(sc.reshape(NSA_RC // 8, 8, cb), axis=0))
            out.append(jnp.max(mx, axis=0, keepdims=True))
        return lanes(out)

    def put_weights(src, dst, rows, valid_fn, m):
        for j in range(ncol // cb):
            mj = m[:, j * cb:(j + 1) * cb]
            for c in range(rows // NSA_RC):
                sc = chunk_scores(src, c, j, valid_fn)
                dst[c * NSA_RC:(c + 1) * NSA_RC, j * cb:(j + 1) * cb] = (
                    jnp.exp2((sc - mj).astype(BF16)).astype(dst.dtype))

    q = lanes([q_ref[r * hd:(r + 1) * hd, :].astype(F32) for r in range(N_REP)])
    qn = q * lax.rsqrt(jnp.mean(q * q, axis=0, keepdims=True) + EPS) * gq_ref[...]
    cos = lanes([cos_ref[...]] * N_REP)
    sin = lanes([sin_ref[...]] * N_REP)
    rot = jnp.concatenate([qn[hd // 2:], qn[:hd // 2]], axis=0)
    qr = ((qn * cos + rot * sin) * (hd ** -0.5 * LOG2_E)).astype(BF16)
    t_q = t0 + lax.broadcasted_iota(jnp.int32, (1, tq), 1)
    tcol = lanes([t_q] * N_REP)

    nc = kc_ref.shape[2]
    s_c[...] = jnp.dot(kc_ref[0, 0], qr, preferred_element_type=F32)
    valid_c = lambda n: n * CMP_STRIDE + (CMP_LEN - 1) <= t_q
    put_weights(s_c, e_c, nc, valid_c, col_max(s_c, nc, valid_c))
    both = jnp.dot(jnp.concatenate([vct_ref[0, 0], wimp_ref[...]], axis=0), e_c[...], preferred_element_type=F32)
    inv_c = jnp.where(tcol >= CMP_LEN - 1, 1.0 / both[hd:hd + 1], 0.0)
    o_c = both[0:hd] * inv_c

    imp_h = both[hd + V_AUG_ROWS:] * inv_c
    imp = head_cols(imp_h, 0)
    for r in range(1, N_REP):
        imp = imp + head_cols(imp_h, r)

    span = WINDOW + tq
    ws0 = pl.multiple_of(jnp.maximum(t0 - WINDOW, 0), tq)
    s_w[...] = jnp.dot(kw_ref[0, 0, pl.ds(ws0, span), :], qr, preferred_element_type=F32)
    valid_w = lambda k: (ws0 + k <= t_q) & (ws0 + k > t_q - WINDOW)
    put_weights(s_w, e_w, span, valid_w, col_max(s_w, span, valid_w))
    o_w = jnp.dot(with_ones(vwt_ref[:, pl.ds(ws0, span)]), e_w[...], preferred_element_type=F32)
    o_w = o_w[0:hd] / o_w[hd:hd + 1]

    blk = lax.broadcasted_iota(jnp.int32, (nb, tq), 0)
    cur = (t0 + lax.broadcasted_iota(jnp.int32, (nb, tq), 1)) // SEL_BLOCK
    causal_b = blk <= cur
    forced = (blk == 0) | (blk == cur) | (blk == cur - 1)
    score = jnp.where(forced, FORCE_SCORE, jnp.where(causal_b, imp, -1.0))
    taken = -3.0e38

    def pick_by_value(sc):
        for _ in range(top - 2):
            sc = jnp.where(sc == jnp.max(sc, axis=0, keepdims=True), taken, sc)
        return sc

    def pick_by_value_then_index(sc):
        for _ in range(top):
            mx = jnp.max(sc, axis=0, keepdims=True)
            idx = jnp.min(jnp.where(sc == mx, blk, nb), axis=0, keepdims=True)
            sc = jnp.where(blk == idx, taken, sc)
        return sc

    fast = pick_by_value(score)
    n_taken = jnp.sum(jnp.where((fast == taken) & causal_b, 1.0, 0.0), axis=0, keepdims=True)
    n_want = jnp.minimum(cur[0:1, :] + 1, top).astype(F32)
    tied = jnp.max(jnp.abs(n_taken - n_want)) > 0.0
    picked = lax.cond(tied, lambda: pick_by_value_then_index(score), lambda: fast)
    bias = jnp.where((picked == taken) & causal_b, 0.0, BLOCK_BIAS)
    q_aug = jnp.concatenate([lanes([bias.astype(BF16)] * N_REP), qr], axis=0).astype(F8)

    tk = min(NSA_TK, seq)
    last = (t0 + tq - 1) // tk
    s_bufs, e_bufs = (s_a, s_b), (e_a, e_b)

    def scores(t, par):
        ks0 = pl.multiple_of(t * tk, tk)
        return put_scores(s_bufs[par], jnp.dot(ks_ref[0, 0, pl.ds(ks0, tk), :], q_aug, preferred_element_type=F32))

    def weights(t, par, m, mx, masked):
        valid_s = (lambda k: t * tk + k <= t_q) if masked else None
        m_new = jnp.maximum(m, jnp.max(mx, axis=0, keepdims=True))
        put_weights(s_bufs[par], e_bufs[par], tk, valid_s, m_new - SEL_WEIGHT_SHIFT)
        return m_new, jnp.exp2(m - m_new)

    def values(t, par, alpha, acc):
        ks0 = pl.multiple_of(jnp.maximum(t, 0) * tk, tk)
        return alpha * acc + jnp.dot(with_ones(vst_ref[:, pl.ds(ks0, tk)]).astype(F8), e_bufs[par][...],
                                     preferred_element_type=F32)

    def step(t, par, carry, masked, more):
        m, alpha, acc, mx = carry
        mx_next = scores(t + 1, 1 - par) if more else mx
        acc = values(t - 1, 1 - par, alpha, acc)
        m, alpha = weights(t, par, m, mx, masked)
        return m, alpha, acc, mx_next

    def pair_body(j, carry):
        return step(2 * j + 1, 1, step(2 * j, 0, carry, False, True), False, True)

    def tail_even(carry):
        m, alpha, acc, _ = step(last, 0, carry, True, False)
        return values(last, 0, alpha, acc)

    def tail_odd(carry):
        m, alpha, acc, _ = step(last, 1, step(last - 1, 0, carry, False, True), True, False)
        return values(last, 1, alpha, acc)

    mx0 = scores(0, 0)
    e_b[...] = jnp.zeros(e_b.shape, e_b.dtype)
    carry = (jnp.full((1, ncol), NEG_BIG, F32), jnp.ones((1, ncol), F32), jnp.zeros((hd + V_AUG_ROWS, ncol), F32),
             mx0)
    carry = lax.fori_loop(0, last // 2, pair_body, carry)
    acc = lax.cond(last % 2 == 0, tail_even, tail_odd, carry)
    o_s = acc[0:hd] / acc[hd:hd + 1]

    g = _sigmoid(gate_ref[0, 0])
    gc, gs, gw = (lanes([g[j, r:r + 1, :] for r in range(N_REP)]) for j in range(3))
    out = gc * o_c + gs * o_s + gw * o_w
    for r in range(N_REP):
        y_ref[r * hd:(r + 1) * hd, :] = head_cols(out, r).astype(y_ref.dtype)


def _nsa(zt, gates, kc, vct, ks_aug, kw, gq, cos_t, sin_t, wimp_t, batch, seq):
    g, hd = N_GROUPS, N_HEAD_DIM
    tq = min(NSA_TQ, seq)
    tk = min(NSA_TK, seq)
    nq = seq // tq
    ncol = N_REP * tq
    nc = kc.shape[2]
    nb = seq // SEL_BLOCK
    full = lambda *shape: pl.BlockSpec((1, 1) + shape, lambda bi, gi, qi: (bi, gi) + (0,) * len(shape))
    vrow = lambda row0: pl.BlockSpec((hd, seq), functools.partial(lambda bi, gi, qi, o: (o + gi, bi), o=row0 // hd))
    qspec = pl.BlockSpec((N_REP * hd, tq), lambda bi, gi, qi: (ZT_Q // (N_REP * hd) + gi, bi * nq + qi))
    return pl.pallas_call(
        functools.partial(_nsa_kernel, seq=seq),
        grid=(batch, g, nq),
        in_specs=[qspec,
                  pl.BlockSpec((1, 1, 3, N_REP, tq), lambda bi, gi, qi: (bi, gi, 0, 0, qi)),
                  full(nc, hd), full(hd + V_AUG_ROWS, nc), full(seq, nb + hd), vrow(ZT_VS), full(seq, hd),
                  vrow(ZT_VW),
                  pl.BlockSpec((hd, 1), lambda bi, gi, qi: (0, 0)),
                  pl.BlockSpec((hd, tq), lambda bi, gi, qi: (0, qi)),
                  pl.BlockSpec((hd, tq), lambda bi, gi, qi: (0, qi)),
                  pl.BlockSpec((nb, nc), lambda bi, gi, qi: (0, 0))],
        out_specs=pl.BlockSpec((N_REP * hd, tq), lambda bi, gi, qi: (gi, bi * nq + qi)),
        out_shape=jax.ShapeDtypeStruct((N_HEADS * hd, batch * seq), BF16),
        scratch_shapes=[pltpu.VMEM((nc, ncol), F32), pltpu.VMEM((nc, ncol), BF16),
                        pltpu.VMEM((WINDOW + tq, ncol), F32), pltpu.VMEM((WINDOW + tq, ncol), BF16),
                        pltpu.VMEM((tk, ncol), F32), pltpu.VMEM((tk, ncol), F32),
                        pltpu.VMEM((tk, ncol), F8), pltpu.VMEM((tk, ncol), F8)],
        compiler_params=_params("parallel", "parallel", "arbitrary"),
        name="nsa_attention",
    )(zt, gates, kc, vct, ks_aug, zt, kw, zt, gq.reshape(hd, 1), cos_t, sin_t, wimp_t)


def _rope_tables(pos):
    half = N_HEAD_DIM // 2
    inv = jnp.power(ROPE_THETA, -jnp.arange(half, dtype=F32) * 2.0 / N_HEAD_DIM)
    ang = pos.astype(F32)[:, None] * inv[None, :]
    cos, sin = jnp.cos(ang), jnp.sin(ang)
    return jnp.concatenate([cos, cos], axis=-1), jnp.concatenate([-sin, sin], axis=-1)


def _importance_matrix(nb, nc_pad):
    per = SEL_BLOCK // CMP_STRIDE
    j = np.arange(nb)[:, None]
    n = np.arange(nc_pad)[None, :]
    w = np.zeros((nb, nc_pad), np.float32)
    for d in range(CMP_LEN // CMP_STRIDE):
        w += ((n + d >= per * j) & (n + d <= per * j + per - 1)).astype(np.float32)
    return jnp.asarray(w, BF16)


def _split_w_in(w_in):
    d = w_in.shape[0]
    sizes = (2048, 1024, 1024, 4, 4, 1024, 256, 256, 256, 256, 256, 256, 48, 1024, 2048, 2048, 2048)
    offs = np.concatenate([[0], np.cumsum(sizes)])
    part = lambda i: w_in[:, offs[i]:offs[i + 1]]
    tok = jnp.concatenate([part(i) for i in (0, 1, 2, 13, 14, 15, 16, 6, 7, 8, 10)], axis=1).astype(BF16)
    feat = jnp.concatenate([part(i) for i in (5, 9, 11)], axis=1).T.astype(BF16)
    small = jnp.concatenate([part(3), part(4), part(12), jnp.zeros((d, 128 - 56), w_in.dtype)], axis=1).astype(BF16)
    return tok, feat, small


def _layer(x, mem, norm_mix_g, norm_mem_g, norm_ffn_g, w_in, m_conv_w, m_i_bias, m_f_bias, m_norm_g,
           n_q_norm_g, n_kc_norm_g, n_ks_norm_g, n_kw_norm_g, n_cmp_pe_k, n_cmp_w1_k, n_cmp_w2_k,
           n_cmp_pe_v, n_cmp_w1_v, n_cmp_w2_v, c_q_norm_g, c_k_norm_g, w_mem_k, w_mem_v,
           w_up_a, w_up_b, w_up_c, w_out, w_ffn_gate, w_ffn_up, w_ffn_down):
    B, S, D = x.shape
    M = B * S
    G, R, hd = N_GROUPS, N_REP, N_HEAD_DIM
    x2 = x.reshape(M, D)

    w_tok, w_feat, w_small = _split_w_in(w_in)
    z, zs, h = _in_proj(x2, norm_mix_g, w_tok, w_small)
    zt = _mm_feature_major(w_feat, h)
    z3 = z.reshape(B, S, Z_WIDTH)
    zs3 = zs.reshape(B, S, 128)

    gif = jnp.stack([zs3[..., 0:4], zs3[..., 4:8]], axis=-1)
    gcol = gif.transpose(0, 2, 1, 3)
    grow = gif.transpose(0, 2, 3, 1)
    gbias = jnp.stack([m_i_bias, m_f_bias], axis=-1).reshape(M_HEADS, 1, 2).astype(F32)
    y_a = _mlstm(z3, gcol, grow, gbias, m_conv_w, m_norm_g, B, S)

    pos = jnp.arange(S, dtype=jnp.int32)
    cos, sin_s = _rope_tables(pos)
    nc_pad = S // CMP_STRIDE
    cmp_end = jnp.arange(nc_pad, dtype=jnp.int32) * CMP_STRIDE + CMP_LEN - 1
    cos_c, sin_c = _rope_tables(cmp_end)
    nb = S // SEL_BLOCK

    def head_major(col0):
        u = z3[..., col0:col0 + G * hd].reshape(B, S, G, hd)
        return u.transpose(0, 2, 1, 3).reshape(B * G, S, hd)

    kc = _compress(head_major(Z_KC), n_cmp_pe_k, n_cmp_w1_k, n_cmp_w2_k, n_kc_norm_g, cos_c, sin_c, True)
    vc = _compress(head_major(Z_VC), n_cmp_pe_v, n_cmp_w1_v, n_cmp_w2_v, n_kc_norm_g, cos_c, sin_c, False)
    vct = vc.reshape(B, G, nc_pad, hd).transpose(0, 1, 3, 2)
    ones_rows = jnp.concatenate([jnp.ones((B, G, 1, nc_pad), BF16), jnp.zeros((B, G, V_AUG_ROWS - 1, nc_pad), BF16)],
                                axis=2)
    vct = jnp.concatenate([vct, ones_rows], axis=2)
    ks_aug = _knorm_rope(z3, Z_KS, n_ks_norm_g, cos, sin_s, nb)
    kw = _knorm_rope(z3, Z_KW, n_kw_norm_g, cos, sin_s, 0)
    gates = zs3[..., 8:8 + 3 * N_HEADS].reshape(B, S, G, R, 3).transpose(0, 2, 4, 3, 1)
    y_bt = _nsa(zt, gates, kc.reshape(B, G, nc_pad, hd), vct, ks_aug, kw, n_q_norm_g, cos.T, sin_s.T,
                _importance_matrix(nb, nc_pad), B, S)

    mlen = mem.shape[1]
    mem_n = _rmsnorm(mem.reshape(B * mlen, D), norm_mem_g)
    k_mem = _mm(mem_n, w_mem_k.astype(BF16), BF16).reshape(B, mlen, -1)
    v_mem = _mm(mem_n, w_mem_v.astype(BF16), BF16).reshape(B, mlen, -1)
    y_c = _xattn(z3, k_mem, v_mem, c_q_norm_g, c_k_norm_g, B, S)

    mix = _mix(y_a.reshape(M, -1), y_bt, y_c.reshape(M, -1),
               w_up_a.astype(BF16), w_up_b.astype(BF16), w_up_c.astype(BF16), z, Z_GATES)
    x2 = _mm_residual(mix, w_out.astype(BF16), x2, tm=512, tn=D)

    act = _swiglu(x2, norm_ffn_g, w_ffn_gate.astype(BF16), w_ffn_up.astype(BF16))
    x2 = _mm_residual(act, w_ffn_down.astype(BF16), x2, tm=1024, tn=512)
    return x2.reshape(B, S, D)


def kernel(x, mem, norm_mix_g, norm_mem_g, norm_ffn_g, w_in, m_conv_w, m_i_bias, m_f_bias, m_norm_g, n_q_norm_g, n_kc_norm_g, n_ks_norm_g, n_kw_norm_g, n_cmp_pe_k, n_cmp_w1_k, n_cmp_w2_k, n_cmp_pe_v, n_cmp_w1_v, n_cmp_w2_v, c_q_norm_g, c_k_norm_g, w_mem_k, w_mem_v, w_up_a, w_up_b, w_up_c, w_out, w_ffn_gate, w_ffn_up, w_ffn_down):
    stacked = (norm_mix_g, norm_mem_g, norm_ffn_g, w_in, m_conv_w, m_i_bias, m_f_bias, m_norm_g, n_q_norm_g,
               n_kc_norm_g, n_ks_norm_g, n_kw_norm_g, n_cmp_pe_k, n_cmp_w1_k, n_cmp_w2_k, n_cmp_pe_v, n_cmp_w1_v,
               n_cmp_w2_v, c_q_norm_g, c_k_norm_g, w_mem_k, w_mem_v, w_up_a, w_up_b, w_up_c, w_out, w_ffn_gate,
               w_ffn_up, w_ffn_down)
    for layer in range(w_in.shape[0]):
        x = _layer(x, mem, *(p[layer] for p in stacked))
    return x
```

```python
import functools

import numpy as np
import jax
import jax.numpy as jnp
from jax import lax
from jax.experimental import pallas as pl
from jax.experimental.pallas import tpu as pltpu

F32 = jnp.float32
BF16 = jnp.bfloat16

EPS = 1e-6
ROPE_THETA = 10000.0
M_HEADS = 4
M_HEAD_DIM = 256
M_CONV = 4
N_HEADS = 16
N_GROUPS = 4
N_REP = N_HEADS // N_GROUPS
N_HEAD_DIM = 64
CMP_LEN = 32
CMP_STRIDE = 16
CMP_HIDDEN = 128
SEL_BLOCK = 64
SEL_TOP = 16
WINDOW = 512
FORCE_SCORE = 1.0e4
C_HEADS = 4
C_HEAD_DIM = 256

LOG2_E = 1.4426950408889634
V_AUG_ROWS = 16
NEG_BIG = -1.0e30
F8 = jnp.float8_e4m3fn
BLOCK_BIAS = -448.0
SEL_WEIGHT_SHIFT = 8.0
V7X_VMEM_BYTES = 64 * 1024 * 1024
VMEM_LIMIT = V7X_VMEM_BYTES - 8 * 1024 * 1024

MLSTM_CHUNK = 256
NSA_TQ = 256
NSA_TK = 512
NSA_RC = 32
NSA_CB = 1024

Z_MQK, Z_MV, Z_MO, Z_CQ, Z_GATES, Z_KC, Z_VC, Z_KS, Z_KW, Z_WIDTH = (
    0, 2048, 3072, 4096, 5120, 11264, 11520, 11776, 12032, 12288)
ZT_Q, ZT_VS, ZT_VW, ZT_ROWS = 0, 1024, 1280, 1536


def _params(*sem):
    return pltpu.CompilerParams(dimension_semantics=sem, vmem_limit_bytes=VMEM_LIMIT)


def _sigmoid(x):
    return 1.0 / (1.0 + jnp.exp(-x))


def _silu(x):
    return x * _sigmoid(x)


def _log_sigmoid(x):
    return jnp.minimum(x, 0.0) - jnp.log1p(jnp.exp(-jnp.abs(x)))


def _rms_rows(x, g):
    return x * lax.rsqrt(jnp.mean(x * x, axis=-1, keepdims=True) + EPS) * g


def _rmsnorm_kernel(x_ref, g_ref, o_ref):
    o_ref[...] = _rms_rows(x_ref[...], g_ref[...]).astype(o_ref.dtype)


def _rmsnorm(x2d, g, tm=512):
    m, d = x2d.shape
    tm = min(tm, m)
    return pl.pallas_call(
        _rmsnorm_kernel,
        grid=(m // tm,),
        in_specs=[pl.BlockSpec((tm, d), lambda i: (i, 0)), pl.BlockSpec((1, d), lambda i: (0, 0))],
        out_specs=pl.BlockSpec((tm, d), lambda i: (i, 0)),
        out_shape=jax.ShapeDtypeStruct((m, d), BF16),
        compiler_params=_params("parallel"),
        name="rmsnorm",
    )(x2d, g.reshape(1, d))


def _mm_kernel(a_ref, w_ref, o_ref):
    o_ref[...] = jnp.dot(a_ref[...], w_ref[...], preferred_element_type=F32).astype(o_ref.dtype)


def _mm(a, w, out_dtype, tm=1024, tn=512):
    m, k = a.shape
    n = w.shape[1]
    tm, tn = min(tm, m), min(tn, n)
    return pl.pallas_call(
        _mm_kernel,
        grid=(m // tm, n // tn),
        in_specs=[pl.BlockSpec((tm, k), lambda i, j: (i, 0)), pl.BlockSpec((k, tn), lambda i, j: (0, j))],
        out_specs=pl.BlockSpec((tm, tn), lambda i, j: (i, j)),
        out_shape=jax.ShapeDtypeStruct((m, n), out_dtype),
        compiler_params=_params("parallel", "arbitrary"),
        name="matmul",
    )(a, w)


def _in_proj_kernel(x_ref, g_ref, w_ref, ws_ref, z_ref, zs_ref, h_ref):
    @pl.when(pl.program_id(1) == 0)
    def _():
        h = _rms_rows(x_ref[...], g_ref[...]).astype(BF16)
        h_ref[...] = h
        zs_ref[...] = jnp.dot(h, ws_ref[...], preferred_element_type=F32)

    z_ref[...] = jnp.dot(h_ref[...], w_ref[...], preferred_element_type=F32).astype(z_ref.dtype)


def _in_proj(x2d, g, w, w_small, tm=1024, tn=1024):
    m, d = x2d.shape
    n, ns = w.shape[1], w_small.shape[1]
    tm, tn = min(tm, m), min(tn, n)
    return pl.pallas_call(
        _in_proj_kernel,
        grid=(m // tm, n // tn),
        in_specs=[pl.BlockSpec((tm, d), lambda i, j: (i, 0)), pl.BlockSpec((1, d), lambda i, j: (0, 0)),
                  pl.BlockSpec((d, tn), lambda i, j: (0, j)), pl.BlockSpec((d, ns), lambda i, j: (0, 0))],
        out_specs=[pl.BlockSpec((tm, tn), lambda i, j: (i, j)), pl.BlockSpec((tm, ns), lambda i, j: (i, 0)),
                   pl.BlockSpec((tm, d), lambda i, j: (i, 0))],
        out_shape=[jax.ShapeDtypeStruct((m, n), BF16), jax.ShapeDtypeStruct((m, ns), F32),
                   jax.ShapeDtypeStruct((m, d), BF16)],
        compiler_params=_params("parallel", "arbitrary"),
        name="in_proj",
    )(x2d, g.reshape(1, d), w, w_small)


def _mm_nt_kernel(w_ref, h_ref, o_ref):
    o_ref[...] = lax.dot_general(w_ref[...], h_ref[...], (((1,), (1,)), ((), ())),
                                 preferred_element_type=F32).astype(o_ref.dtype)


def _mm_feature_major(w_t, h, tm=1024, tn=1536):
    n, k = w_t.shape
    m = h.shape[0]
    tm, tn = min(tm, m), min(tn, n)
    return pl.pallas_call(
        _mm_nt_kernel,
        grid=(m // tm, n // tn),
        in_specs=[pl.BlockSpec((tn, k), lambda i, j: (j, 0)), pl.BlockSpec((tm, k), lambda i, j: (i, 0))],
        out_specs=pl.BlockSpec((tn, tm), lambda i, j: (j, i)),
        out_shape=jax.ShapeDtypeStruct((n, m), BF16),
        compiler_params=_params("parallel", "arbitrary"),
        name="in_proj_feature_major",
    )(w_t, h)


def _mm_res_kernel(a_ref, w_ref, x_ref, o_ref):
    o_ref[...] = x_ref[...] + jnp.dot(a_ref[...], w_ref[...], preferred_element_type=F32)


def _mm_residual(a, w, x, tm=512, tn=512):
    m, k = a.shape
    n = w.shape[1]
    tm, tn = min(tm, m), min(tn, n)
    return pl.pallas_call(
        _mm_res_kernel,
        grid=(m // tm, n // tn),
        in_specs=[pl.BlockSpec((tm, k), lambda i, j: (i, 0)), pl.BlockSpec((k, tn), lambda i, j: (0, j)),
                  pl.BlockSpec((tm, tn), lambda i, j: (i, j))],
        out_specs=pl.BlockSpec((tm, tn), lambda i, j: (i, j)),
        out_shape=jax.ShapeDtypeStruct((m, n), F32),
        compiler_params=_params("parallel", "arbitrary"),
        name="matmul_residual",
    )(a, w, x)


def _swiglu_kernel(x_ref, g_ref, wg_ref, wu_ref, o_ref, h_scr):
    @pl.when(pl.program_id(1) == 0)
    def _():
        h_scr[...] = _rms_rows(x_ref[...], g_ref[...]).astype(BF16)

    h = h_scr[...]
    gate = jnp.dot(h, wg_ref[...], preferred_element_type=F32)
    up = jnp.dot(h, wu_ref[...], preferred_element_type=F32)
    o_ref[...] = (_silu(gate) * up).astype(o_ref.dtype)


def _swiglu(x2d, g, wg, wu, tm=1024, tn=512):
    m, d = x2d.shape
    n = wg.shape[1]
    tm, tn = min(tm, m), min(tn, n)
    return pl.pallas_call(
        _swiglu_kernel,
        grid=(m // tm, n // tn),
        in_specs=[pl.BlockSpec((tm, d), lambda i, j: (i, 0)), pl.BlockSpec((1, d), lambda i, j: (0, 0)),
                  pl.BlockSpec((d, tn), lambda i, j: (0, j)), pl.BlockSpec((d, tn), lambda i, j: (0, j))],
        out_specs=pl.BlockSpec((tm, tn), lambda i, j: (i, j)),
        out_shape=jax.ShapeDtypeStruct((m, n), BF16),
        scratch_shapes=[pltpu.VMEM((tm, d), BF16)],
        compiler_params=_params("parallel", "arbitrary"),
        name="swiglu",
    )(x2d, g.reshape(1, d), wg, wu)


def _mix_kernel(ya_ref, ybt_ref, yc_ref, wa_ref, wb_ref, wc_ref, ga_ref, gb_ref, gc_ref, o_ref):
    def gated(g_ref, prod):
        return _sigmoid(g_ref[...].astype(F32)) * prod

    a = jnp.dot(ya_ref[...], wa_ref[...], preferred_element_type=F32)
    b = lax.dot_general(ybt_ref[...], wb_ref[...], (((0,), (0,)), ((), ())), preferred_element_type=F32)
    c = jnp.dot(yc_ref[...], wc_ref[...], preferred_element_type=F32)
    o_ref[...] = (gated(ga_ref, a) + gated(gb_ref, b) + gated(gc_ref, c)).astype(o_ref.dtype)


def _mix(ya, ybt, yc, wa, wb, wc, z, gate_col0, tm=1024, tn=512):
    m, k = ya.shape
    n = wa.shape[1]
    tm, tn = min(tm, m), min(tn, n)
    gofs = [(gate_col0 + b * n) // tn for b in range(3)]
    y_spec = pl.BlockSpec((tm, k), lambda i, j: (i, 0))
    w_spec = pl.BlockSpec((k, tn), lambda i, j: (0, j))
    g_specs = [pl.BlockSpec((tm, tn), functools.partial(lambda i, j, o: (i, o + j), o=o)) for o in gofs]
    return pl.pallas_call(
        _mix_kernel,
        grid=(m // tm, n // tn),
        in_specs=[y_spec, pl.BlockSpec((k, tm), lambda i, j: (0, i)), y_spec, w_spec, w_spec, w_spec] + g_specs,
        out_specs=pl.BlockSpec((tm, tn), lambda i, j: (i, j)),
        out_shape=jax.ShapeDtypeStruct((m, n), BF16),
        compiler_params=_params("parallel", "arbitrary"),
        name="gated_mix",
    )(ya, ybt, yc, wa, wb, wc, z, z, z)


def _mlstm_kernel(uq_ref, uk_ref, v_ref, o_ref, gcol_ref, grow_ref, bias_ref, cwq_ref, cwk_ref, ng_ref, shift_ref,
                  y_ref, c_scr, n_scr, m_scr, qtail, ktail):
    c = pl.program_id(1)
    L = uq_ref.shape[1]
    hd = M_HEAD_DIM

    @pl.when(c == 0)
    def _():
        c_scr[...] = jnp.zeros_like(c_scr)
        n_scr[...] = jnp.zeros_like(n_scr)
        m_scr[...] = jnp.zeros_like(m_scr)
        qtail[...] = jnp.zeros_like(qtail)
        ktail[...] = jnp.zeros_like(ktail)

    def conv_silu(u_ref, tail, cw_ref):
        u = u_ref[0]
        cw = cw_ref[...]
        cwb = cw.astype(BF16)
        x = jnp.concatenate([u * cwb[j:j + 1, :] for j in range(M_CONV)], axis=0)
        y = jnp.dot(shift_ref[...], x, preferred_element_type=F32)
        prev = tail[...]
        row = lax.broadcasted_iota(jnp.int32, prev.shape, 0)
        edge = jnp.zeros(prev.shape, F32)
        for t in range(M_CONV - 1):
            acc = cw[0:1, :] * prev[8 - (M_CONV - 1) + t:8 - (M_CONV - 1) + t + 1, :]
            for j in range(1, M_CONV - 1 - t):
                acc = acc + cw[j:j + 1, :] * prev[8 - (M_CONV - 1) + t + j:8 - (M_CONV - 1) + t + j + 1, :]
            edge = jnp.where(row == t, acc, edge)
        tail[...] = u[L - 8:L, :].astype(F32)
        return _silu(jnp.concatenate([y[0:8] + edge, y[8:]], axis=0))

    q_all = conv_silu(uq_ref, qtail, cwq_ref)
    k_all = conv_silu(uk_ref, ktail, cwk_ref) * (hd ** -0.5)
    row = lax.broadcasted_iota(jnp.int32, (L, L), 0)
    col = lax.broadcasted_iota(jnp.int32, (L, L), 1)
    tri = col <= row

    for hh in range(M_HEADS):
        sl = slice(hh * hd, (hh + 1) * hd)
        q, k = q_all[:, sl], k_all[:, sl]
        qb = q.astype(BF16)
        kb = k.astype(BF16)
        vb = v_ref[0, :, sl]

        gcol = gcol_ref[0, hh]
        grow = grow_ref[0, hh]
        bias = bias_ref[hh]
        ic = gcol[:, 0:1] + bias[:, 0:1]
        fc = _log_sigmoid(gcol[:, 1:2] + bias[:, 1:2])
        ir = grow[0:1, :] + bias[:, 0:1]
        fr = _log_sigmoid(grow[1:2, :] + bias[:, 1:2])

        b_col = jnp.sum(jnp.where(tri, fr, 0.0), axis=1, keepdims=True)
        b_row = jnp.sum(jnp.where(row <= col, fc, 0.0), axis=0, keepdims=True)
        m_prev = m_scr[hh, 0:1, 0:1]

        log_d = jnp.where(tri, b_col - b_row + ir, NEG_BIG)
        log_inter = b_col + m_prev
        m_t = jnp.maximum(log_inter, jnp.max(log_d, axis=1, keepdims=True))
        w_intra = jnp.exp(log_d - m_t)
        w_inter = jnp.exp(log_inter - m_t)

        s = lax.dot_general(qb, kb, (((1,), (1,)), ((), ())), preferred_element_type=F32) * w_intra
        c_state = c_scr[hh]
        n_state = n_scr[hh, 0:1, :]
        num = (jnp.dot(s.astype(BF16), vb, preferred_element_type=F32)
               + w_inter * jnp.dot(qb, c_state.astype(BF16), preferred_element_type=F32))
        den = jnp.sum(s, axis=1, keepdims=True) + w_inter * jnp.sum(q * n_state, axis=1, keepdims=True)
        h = num / jnp.maximum(jnp.abs(den), jnp.exp(-m_t))

        b_last = jnp.sum(fc, axis=0, keepdims=True)
        log_g = b_last - b_col + ic
        m_new = jnp.maximum(b_last + m_prev, jnp.max(log_g, axis=0, keepdims=True))
        decay = jnp.exp(b_last + m_prev - m_new)
        wk = k * jnp.exp(log_g - m_new)
        c_scr[hh] = decay * c_state + lax.dot_general(wk.astype(BF16), vb, (((0,), (0,)), ((), ())),
                                                      preferred_element_type=F32)
        n_scr[hh, 0:1, :] = decay * n_state + jnp.sum(wk, axis=0, keepdims=True)
        m_scr[hh] = jnp.broadcast_to(m_new, m_scr.shape[1:])

        hn = _rms_rows(h, ng_ref[hh])
        y_ref[0, :, sl] = (_sigmoid(o_ref[0, :, sl].astype(F32)) * hn).astype(y_ref.dtype)


def _conv_shift_matrix(L):
    t = np.arange(L)[:, None]
    r = np.arange(L)[None, :]
    return jnp.asarray(np.concatenate([(r == t - (M_CONV - 1) + j) for j in range(M_CONV)], axis=1), BF16)


def _mlstm(z3, gcol, grow, gbias, conv_w, norm_g, batch, seq):
    L = min(MLSTM_CHUNK, seq)
    hd = M_HEAD_DIM
    nh = M_HEADS
    w = nh * hd

    def zspec(col0):
        return pl.BlockSpec((1, L, w), functools.partial(lambda b, c, o: (b, c, o), o=col0 // w))

    return pl.pallas_call(
        _mlstm_kernel,
        grid=(batch, seq // L),
        in_specs=[zspec(Z_MQK), zspec(Z_MQK + w), zspec(Z_MV), zspec(Z_MO),
                  pl.BlockSpec((1, nh, L, 2), lambda b, c: (b, 0, c, 0)),
                  pl.BlockSpec((1, nh, 2, L), lambda b, c: (b, 0, 0, c)),
                  pl.BlockSpec((nh, 1, 2), lambda b, c: (0, 0, 0)),
                  pl.BlockSpec((M_CONV, w), lambda b, c: (0, 0)),
                  pl.BlockSpec((M_CONV, w), lambda b, c: (0, 1)),
                  pl.BlockSpec((nh, 1, hd), lambda b, c: (0, 0, 0)),
                  pl.BlockSpec((L, M_CONV * L), lambda b, c: (0, 0))],
        out_specs=pl.BlockSpec((1, L, w), lambda b, c: (b, c, 0)),
        out_shape=jax.ShapeDtypeStruct((batch, seq, w), BF16),
        scratch_shapes=[pltpu.VMEM((nh, hd, hd), F32), pltpu.VMEM((nh, 8, hd), F32), pltpu.VMEM((nh, 8, 128), F32),
                        pltpu.VMEM((8, w), F32), pltpu.VMEM((8, w), F32)],
        compiler_params=_params("parallel", "arbitrary"),
        name="mlstm",
    )(z3, z3, z3, z3, gcol, grow, gbias, conv_w, conv_w, norm_g.reshape(nh, 1, hd), _conv_shift_matrix(L))


def _xattn_kernel(q_ref, k_ref, v_ref, gq_ref, gk_ref, o_ref):
    hd = C_HEAD_DIM
    for h in range(C_HEADS):
        sl = slice(h * hd, (h + 1) * hd)
        q = _rms_rows(q_ref[0, :, sl].astype(F32), gq_ref[...]) * (hd ** -0.5)
        k = _rms_rows(k_ref[0, :, sl].astype(F32), gk_ref[...])
        s = lax.dot_general(q.astype(BF16), k.astype(BF16), (((1,), (1,)), ((), ())), preferred_element_type=F32)
        e = jnp.exp(s - jnp.max(s, axis=-1, keepdims=True))
        p = e / jnp.sum(e, axis=-1, keepdims=True)
        o = jnp.dot(p.astype(BF16), v_ref[0, :, sl], preferred_element_type=F32)
        o_ref[0, :, sl] = o.astype(o_ref.dtype)


def _xattn(z3, k, v, gq, gk, batch, seq, tq=512):
    tq = min(tq, seq)
    mlen, w = k.shape[1], k.shape[2]
    return pl.pallas_call(
        _xattn_kernel,
        grid=(batch, seq // tq),
        in_specs=[pl.BlockSpec((1, tq, w), lambda b, i: (b, i, Z_CQ // w)),
                  pl.BlockSpec((1, mlen, w), lambda b, i: (b, 0, 0)),
                  pl.BlockSpec((1, mlen, w), lambda b, i: (b, 0, 0)),
                  pl.BlockSpec((1, C_HEAD_DIM), lambda b, i: (0, 0)),
                  pl.BlockSpec((1, C_HEAD_DIM), lambda b, i: (0, 0))],
        out_specs=pl.BlockSpec((1, tq, w), lambda b, i: (b, i, 0)),
        out_shape=jax.ShapeDtypeStruct((batch, seq, w), BF16),
        compiler_params=_params("parallel", "arbitrary"),
        name="memory_cross_attention",
    )(z3, k, v, gq.reshape(1, -1), gk.reshape(1, -1))


def _rope_rows(xn, cos, sin_signed):
    half = N_HEAD_DIM // 2
    rot = jnp.concatenate([xn[:, half:], xn[:, :half]], axis=-1)
    return xn * cos + rot * sin_signed


def _knorm_rope_kernel(x_ref, g_ref, cos_ref, sin_ref, o_ref, *, n_blocks):
    hd = N_HEAD_DIM
    ts = x_ref.shape[1]
    if n_blocks:
        pos = pl.program_id(1) * ts + lax.broadcasted_iota(jnp.int32, (ts, n_blocks), 0)
        blk = lax.broadcasted_iota(jnp.int32, (ts, n_blocks), 1)
        onehot = jnp.where(pos // SEL_BLOCK == blk, 1.0, 0.0).astype(o_ref.dtype)
    for g in range(N_GROUPS):
        x = x_ref[0, :, g * hd:(g + 1) * hd].astype(F32)
        k = _rope_rows(_rms_rows(x, g_ref[...]), cos_ref[...], sin_ref[...]).astype(o_ref.dtype)
        if n_blocks:
            o_ref[0, g, :, 0:n_blocks] = onehot
        o_ref[0, g, :, n_blocks:n_blocks + hd] = k


def _knorm_rope(z3, col0, g, cos, sin_signed, n_blocks, ts=1024):
    b, s, _ = z3.shape
    hd, w = N_HEAD_DIM, N_GROUPS * N_HEAD_DIM
    ts = min(ts, s)
    return pl.pallas_call(
        functools.partial(_knorm_rope_kernel, n_blocks=n_blocks),
        grid=(b, s // ts),
        in_specs=[pl.BlockSpec((1, ts, w), lambda i, j: (i, j, col0 // w)), pl.BlockSpec((1, hd), lambda i, j: (0, 0)),
                  pl.BlockSpec((ts, hd), lambda i, j: (j, 0)), pl.BlockSpec((ts, hd), lambda i, j: (j, 0))],
        out_specs=pl.BlockSpec((1, N_GROUPS, ts, n_blocks + hd), lambda i, j: (i, 0, j, 0)),
        out_shape=jax.ShapeDtypeStruct((b, N_GROUPS, s, n_blocks + hd), F8 if n_blocks else BF16),
        compiler_params=_params("parallel", "arbitrary"),
        name="key_norm_rope",
    )(z3, g.reshape(1, hd), cos, sin_signed)


def _compress_kernel(sub_ref, pe_ref, w1_ref, w2_ref, g_ref, cos_ref, sin_ref, o_ref, *, is_key):
    ns = sub_ref.shape[1]
    sub = sub_ref[0].astype(F32)
    lo = jnp.dot((sub + pe_ref[0:1, :]).astype(BF16), w1_ref[0], preferred_element_type=F32)
    hi = jnp.dot((sub + pe_ref[1:2, :]).astype(BF16), w1_ref[1], preferred_element_type=F32)
    hid = _silu(lo + pltpu.roll(hi, shift=ns - 1, axis=0))
    out = jnp.dot(hid.astype(BF16), w2_ref[...], preferred_element_type=F32)
    if is_key:
        out = _rope_rows(_rms_rows(out, g_ref[...]), cos_ref[...], sin_ref[...])
    o_ref[0] = out.astype(o_ref.dtype)


def _compress(u, pe, w1, w2, g, cos, sin_signed, is_key):
    n, s, hd = u.shape
    ns = s // CMP_STRIDE
    width = CMP_STRIDE * hd
    sub = u.reshape(n, ns, width)
    pe2 = pe.reshape(CMP_LEN // CMP_STRIDE, width)
    w1s = w1.reshape(CMP_LEN // CMP_STRIDE, width, CMP_HIDDEN).astype(BF16)
    return pl.pallas_call(
        functools.partial(_compress_kernel, is_key=is_key),
        grid=(n,),
        in_specs=[pl.BlockSpec((1, ns, width), lambda i: (i, 0, 0)),
                  pl.BlockSpec(pe2.shape, lambda i: (0, 0)),
                  pl.BlockSpec(w1s.shape, lambda i: (0, 0, 0)),
                  pl.BlockSpec(w2.shape, lambda i: (0, 0)),
                  pl.BlockSpec((1, hd), lambda i: (0, 0)),
                  pl.BlockSpec((ns, hd), lambda i: (0, 0)),
                  pl.BlockSpec((ns, hd), lambda i: (0, 0))],
        out_specs=pl.BlockSpec((1, ns, hd), lambda i: (i, 0, 0)),
        out_shape=jax.ShapeDtypeStruct((n, ns, hd), BF16),
        compiler_params=_params("parallel"),
        name="compress_key" if is_key else "compress_value",
    )(sub, pe2, w1s, w2.astype(BF16), g.reshape(1, hd), cos, sin_signed)


def _nsa_kernel(q_ref, gate_ref, kc_ref, vct_ref, ks_ref, vst_ref, kw_ref, vwt_ref, gq_ref, cos_ref, sin_ref,
                wimp_ref, y_ref, s_c, e_c, s_w, e_w, s_a, s_b, e_a, e_b, *, seq):
    tq = q_ref.shape[1]
    hd = N_HEAD_DIM
    nb = seq // SEL_BLOCK
    top = min(SEL_TOP, nb)
    ncol = N_REP * tq
    t0 = pl.program_id(2) * tq

    def lanes(parts):
        return jnp.concatenate(parts, axis=1)

    def head_cols(a, r):
        return a[:, r * tq:(r + 1) * tq]

    def with_ones(v):
        n = v.shape[1]
        tail = jnp.where(lax.broadcasted_iota(jnp.int32, (V_AUG_ROWS, n), 0) == 0, 1.0, 0.0).astype(v.dtype)
        return jnp.concatenate([v, tail], axis=0)

    cb = min(ncol, NSA_CB)

    def chunk_scores(src, c, j, valid_fn):
        sc = src[c * NSA_RC:(c + 1) * NSA_RC, j * cb:(j + 1) * cb]
        if valid_fn is None:
            return sc
        valid = valid_fn(c * NSA_RC + lax.broadcasted_iota(jnp.int32, (NSA_RC, 1), 0))
        return lanes([jnp.where(valid, head_cols(sc, r), NEG_BIG) for r in range(cb // tq)])

    def put_scores(dst, s):
        dst[...] = s
        return jnp.max(s.reshape(s.shape[0] // 8, 8, ncol), axis=0)

    def col_max(src, rows, valid_fn):
        out = []
        for j in range(ncol // cb):
            mx = jnp.full((8, cb), NEG_BIG, F32)
            for c in range(rows // NSA_RC):
                sc = chunk_scores(src, c, j, valid_fn)
                mx = jnp.maximum(mx, jnp.max(sc.reshape(NSA_RC // 8, 8, cb), axis=0))
            out.append(jnp.max(mx, axis=0, keepdims=True))
        return lanes(out)

    def put_weights(src, dst, rows, valid_fn, m):
        for j in range(ncol // cb):
            mj = m[:, j * cb:(j + 1) * cb]
            for c in range(rows // NSA_RC):
                sc = chunk_scores(src, c, j, valid_fn)
                dst[c * NSA_RC:(c + 1) * NSA_RC, j * cb:(j + 1) * cb] = (
                    jnp.exp2((sc - mj).astype(BF16)).astype(dst.dtype))

    q = lanes([q_ref[r * hd:(r + 1) * hd, :].astype(F32) for r in range(N_REP)])
    qn = q * lax.rsqrt(jnp.mean(q * q, axis=0, keepdims=True) + EPS) * gq_ref[...]
    cos = lanes([cos_ref[...]] * N_REP)
    sin = lanes([sin_ref[...]] * N_REP)
    rot = jnp.concatenate([qn[hd // 2:], qn[:hd // 2]], axis=0)
    qr = ((qn * cos + rot * sin) * (hd ** -0.5 * LOG2_E)).astype(BF16)
    t_q = t0 + lax.broadcasted_iota(jnp.int32, (1, tq), 1)
    tcol = lanes([t_q] * N_REP)

    nc = kc_ref.shape[2]
    s_c[...] = jnp.dot(kc_ref[0, 0], qr, preferred_element_type=F32)
    valid_c = lambda n: n * CMP_STRIDE + (CMP_LEN - 1) <= t_q
    put_weights(s_c, e_c, nc, valid_c, col_max(s_c, nc, valid_c))
    both = jnp.dot(jnp.concatenate([vct_ref[0, 0], wimp_ref[...]], axis=0), e_c[...], preferred_element_type=F32)
    inv_c = jnp.where(tcol >= CMP_LEN - 1, 1.0 / both[hd:hd + 1], 0.0)
    o_c = both[0:hd] * inv_c

    imp_h = both[hd + V_AUG_ROWS:] * inv_c
    imp = head_cols(imp_h, 0)
    for r in range(1, N_REP):
        imp = imp + head_cols(imp_h, r)

    span = WINDOW + tq
    ws0 = pl.multiple_of(jnp.maximum(t0 - WINDOW, 0), tq)
    s_w[...] = jnp.dot(kw_ref[0, 0, pl.ds(ws0, span), :], qr, preferred_element_type=F32)
    valid_w = lambda k: (ws0 + k <= t_q) & (ws0 + k > t_q - WINDOW)
    m_w = col_max(s_w, span, valid_w)

    blk = lax.broadcasted_iota(jnp.int32, (nb, tq), 0)
    cur = (t0 + lax.broadcasted_iota(jnp.int32, (nb, tq), 1)) // SEL_BLOCK
    causal_b = blk <= cur
    forced = (blk == 0) | (blk == cur) | (blk == cur - 1)
    score = jnp.where(forced, FORCE_SCORE, jnp.where(causal_b, imp, -1.0))
    taken = -3.0e38

    def pick_by_value(sc):
        for _ in range(top - 2):
            sc = jnp.where(sc == jnp.max(sc, axis=0, keepdims=True), taken, sc)
        return sc

    def pick_by_value_then_index(sc):
        for _ in range(top):
            mx = jnp.max(sc, axis=0, keepdims=True)
            idx = jnp.min(jnp.where(sc == mx, blk, nb), axis=0, keepdims=True)
            sc = jnp.where(blk == idx, taken, sc)
        return sc

    fast = pick_by_value(score)
    n_taken = jnp.sum(jnp.where((fast == taken) & causal_b, 1.0, 0.0), axis=0, keepdims=True)
    n_want = jnp.minimum(cur[0:1, :] + 1, top).astype(F32)
    tied = jnp.max(jnp.abs(n_taken - n_want)) > 0.0
    picked = lax.cond(tied, lambda: pick_by_value_then_index(score), lambda: fast)
    bias = jnp.where((picked == taken) & causal_b, 0.0, BLOCK_BIAS)
    q_aug = jnp.concatenate([lanes([bias.astype(BF16)] * N_REP), qr], axis=0).astype(F8)

    tk = min(NSA_TK, seq)
    last = (t0 + tq - 1) // tk
    s_bufs, e_bufs = (s_a, s_b), (e_a, e_b)

    def scores(t, par):
        ks0 = pl.multiple_of(t * tk, tk)
        return put_scores(s_bufs[par], jnp.dot(ks_ref[0, 0, pl.ds(ks0, tk), :], q_aug, preferred_element_type=F32))

    def weights(t, par, m, mx, masked):
        valid_s = (lambda k: t * tk + k <= t_q) if masked else None
        m_new = jnp.maximum(m, jnp.max(mx, axis=0, keepdims=True))
        put_weights(s_bufs[par], e_bufs[par], tk, valid_s, m_new - SEL_WEIGHT_SHIFT)
        return m_new, jnp.exp2(m - m_new)

    def values(t, par, alpha, acc):
        ks0 = pl.multiple_of(jnp.maximum(t, 0) * tk, tk)
        return alpha * acc + jnp.dot(with_ones(vst_ref[:, pl.ds(ks0, tk)]).astype(F8), e_bufs[par][...],
                                     preferred_element_type=F32)

    def step(t, par, carry, masked, more):
        m, alpha, acc, mx = carry
        mx_next = scores(t + 1, 1 - par) if more else mx
        acc = values(t - 1, 1 - par, alpha, acc)
        m, alpha = weights(t, par, m, mx, masked)
        return m, alpha, acc, mx_next

    def pair_body(j, carry):
        return step(2 * j + 1, 1, step(2 * j, 0, carry, False, True), False, True)

    def tail_even(carry):
        m, alpha, acc, _ = step(last, 0, carry, True, False)
        return values(last, 0, alpha, acc)

    def tail_odd(carry):
        m, alpha, acc, _ = step(last, 1, step(last - 1, 0, carry, False, True), True, False)
        return values(last, 1, alpha, acc)

    mx0 = scores(0, 0)
    e_b[...] = jnp.zeros(e_b.shape, e_b.dtype)
    put_weights(s_w, e_w, span, valid_w, m_w)
    o_w = jnp.dot(with_ones(vwt_ref[:, pl.ds(ws0, span)]), e_w[...], preferred_element_type=F32)
    o_w = o_w[0:hd] / o_w[hd:hd + 1]
    carry = (jnp.full((1, ncol), NEG_BIG, F32), jnp.ones((1, ncol), F32), jnp.zeros((hd + V_AUG_ROWS, ncol), F32),
             mx0)
    carry = lax.fori_loop(0, last // 2, pair_body, carry)
    acc = lax.cond(last % 2 == 0, tail_even, tail_odd, carry)
    o_s = acc[0:hd] / acc[hd:hd + 1]

    g = _sigmoid(gate_ref[0, 0])
    gc, gs, gw = (lanes([g[j, r:r + 1, :] for r in range(N_REP)]) for j in range(3))
    out = gc * o_c + gs * o_s + gw * o_w
    for r in range(N_REP):
        y_ref[r * hd:(r + 1) * hd, :] = head_cols(out, r).astype(y_ref.dtype)


def _nsa(zt, gates, kc, vct, ks_aug, kw, gq, cos_t, sin_t, wimp_t, batch, seq):
    g, hd = N_GROUPS, N_HEAD_DIM
    tq = min(NSA_TQ, seq)
    tk = min(NSA_TK, seq)
    nq = seq // tq
    ncol = N_REP * tq
    nc = kc.shape[2]
    nb = seq // SEL_BLOCK
    full = lambda *shape: pl.BlockSpec((1, 1) + shape, lambda bi, gi, qi: (bi, gi) + (0,) * len(shape))
    vrow = lambda row0: pl.BlockSpec((hd, seq), functools.partial(lambda bi, gi, qi, o: (o + gi, bi), o=row0 // hd))
    qspec = pl.BlockSpec((N_REP * hd, tq), lambda bi, gi, qi: (ZT_Q // (N_REP * hd) + gi, bi * nq + qi))
    return pl.pallas_call(
        functools.partial(_nsa_kernel, seq=seq),
        grid=(batch, g, nq),
        in_specs=[qspec,
                  pl.BlockSpec((1, 1, 3, N_REP, tq), lambda bi, gi, qi: (bi, gi, 0, 0, qi)),
                  full(nc, hd), full(hd + V_AUG_ROWS, nc), full(seq, nb + hd), vrow(ZT_VS), full(seq, hd),
                  vrow(ZT_VW),
                  pl.BlockSpec((hd, 1), lambda bi, gi, qi: (0, 0)),
                  pl.BlockSpec((hd, tq), lambda bi, gi, qi: (0, qi)),
                  pl.BlockSpec((hd, tq), lambda bi, gi, qi: (0, qi)),
                  pl.BlockSpec((nb, nc), lambda bi, gi, qi: (0, 0))],
        out_specs=pl.BlockSpec((N_REP * hd, tq), lambda bi, gi, qi: (gi, bi * nq + qi)),
        out_shape=jax.ShapeDtypeStruct((N_HEADS * hd, batch * seq), BF16),
        scratch_shapes=[pltpu.VMEM((nc, ncol), F32), pltpu.VMEM((nc, ncol), BF16),
                        pltpu.VMEM((WINDOW + tq, ncol), F32), pltpu.VMEM((WINDOW + tq, ncol), BF16),
                        pltpu.VMEM((tk, ncol), F32), pltpu.VMEM((tk, ncol), F32),
                        pltpu.VMEM((tk, ncol), F8), pltpu.VMEM((tk, ncol), F8)],
        compiler_params=_params("parallel", "parallel", "arbitrary"),
        name="nsa_attention",
    )(zt, gates, kc, vct, ks_aug, zt, kw, zt, gq.reshape(hd, 1), cos_t, sin_t, wimp_t)


def _rope_tables(pos):
    half = N_HEAD_DIM // 2
    inv = jnp.power(ROPE_THETA, -jnp.arange(half, dtype=F32) * 2.0 / N_HEAD_DIM)
    ang = pos.astype(F32)[:, None] * inv[None, :]
    cos, sin = jnp.cos(ang), jnp.sin(ang)
    return jnp.concatenate([cos, cos], axis=-1), jnp.concatenate([-sin, sin], axis=-1)


def _importance_matrix(nb, nc_pad):
    per = SEL_BLOCK // CMP_STRIDE
    j = np.arange(nb)[:, None]
    n = np.arange(nc_pad)[None, :]
    w = np.zeros((nb, nc_pad), np.float32)
    for d in range(CMP_LEN // CMP_STRIDE):
        w += ((n + d >= per * j) & (n + d <= per * j + per - 1)).astype(np.float32)
    return jnp.asarray(w, BF16)


def _split_w_in(w_in):
    d = w_in.shape[0]
    sizes = (2048, 1024, 1024, 4, 4, 1024, 256, 256, 256, 256, 256, 256, 48, 1024, 2048, 2048, 2048)
    offs = np.concatenate([[0], np.cumsum(sizes)])
    part = lambda i: w_in[:, offs[i]:offs[i + 1]]
    tok = jnp.concatenate([part(i) for i in (0, 1, 2, 13, 14, 15, 16, 6, 7, 8, 10)], axis=1).astype(BF16)
    feat = jnp.concatenate([part(i) for i in (5, 9, 11)], axis=1).T.astype(BF16)
    small = jnp.concatenate([part(3), part(4), part(12), jnp.zeros((d, 128 - 56), w_in.dtype)], axis=1).astype(BF16)
    return tok, feat, small


def _layer(x, mem, norm_mix_g, norm_mem_g, norm_ffn_g, w_in, m_conv_w, m_i_bias, m_f_bias, m_norm_g,
           n_q_norm_g, n_kc_norm_g, n_ks_norm_g, n_kw_norm_g, n_cmp_pe_k, n_cmp_w1_k, n_cmp_w2_k,
           n_cmp_pe_v, n_cmp_w1_v, n_cmp_w2_v, c_q_norm_g, c_k_norm_g, w_mem_k, w_mem_v,
           w_up_a, w_up_b, w_up_c, w_out, w_ffn_gate, w_ffn_up, w_ffn_down):
    B, S, D = x.shape
    M = B * S
    G, R, hd = N_GROUPS, N_REP, N_HEAD_DIM
    x2 = x.reshape(M, D)

    w_tok, w_feat, w_small = _split_w_in(w_in)
    z, zs, h = _in_proj(x2, norm_mix_g, w_tok, w_small)
    zt = _mm_feature_major(w_feat, h)
    z3 = z.reshape(B, S, Z_WIDTH)
    zs3 = zs.reshape(B, S, 128)

    gif = jnp.stack([zs3[..., 0:4], zs3[..., 4:8]], axis=-1)
    gcol = gif.transpose(0, 2, 1, 3)
    grow = gif.transpose(0, 2, 3, 1)
    gbias = jnp.stack([m_i_bias, m_f_bias], axis=-1).reshape(M_HEADS, 1, 2).astype(F32)
    y_a = _mlstm(z3, gcol, grow, gbias, m_conv_w, m_norm_g, B, S)

    pos = jnp.arange(S, dtype=jnp.int32)
    cos, sin_s = _rope_tables(pos)
    nc_pad = S // CMP_STRIDE
    cmp_end = jnp.arange(nc_pad, dtype=jnp.int32) * CMP_STRIDE + CMP_LEN - 1
    cos_c, sin_c = _rope_tables(cmp_end)
    nb = S // SEL_BLOCK

    def head_major(col0):
        u = z3[..., col0:col0 + G * hd].reshape(B, S, G, hd)
        return u.transpose(0, 2, 1, 3).reshape(B * G, S, hd)

    kc = _compress(head_major(Z_KC), n_cmp_pe_k, n_cmp_w1_k, n_cmp_w2_k, n_kc_norm_g, cos_c, sin_c, True)
    vc = _compress(head_major(Z_VC), n_cmp_pe_v, n_cmp_w1_v, n_cmp_w2_v, n_kc_norm_g, cos_c, sin_c, False)
    vct = vc.reshape(B, G, nc_pad, hd).transpose(0, 1, 3, 2)
    ones_rows = jnp.concatenate([jnp.ones((B, G, 1, nc_pad), BF16), jnp.zeros((B, G, V_AUG_ROWS - 1, nc_pad), BF16)],
                                axis=2)
    vct = jnp.concatenate([vct, ones_rows], axis=2)
    ks_aug = _knorm_rope(z3, Z_KS, n_ks_norm_g, cos, sin_s, nb)
    kw = _knorm_rope(z3, Z_KW, n_kw_norm_g, cos, sin_s, 0)
    gates = zs3[..., 8:8 + 3 * N_HEADS].reshape(B, S, G, R, 3).transpose(0, 2, 4, 3, 1)
    y_bt = _nsa(zt, gates, kc.reshape(B, G, nc_pad, hd), vct, ks_aug, kw, n_q_norm_g, cos.T, sin_s.T,
                _importance_matrix(nb, nc_pad), B, S)

    mlen = mem.shape[1]
    mem_n = _rmsnorm(mem.reshape(B * mlen, D), norm_mem_g)
    k_mem = _mm(mem_n, w_mem_k.astype(BF16), BF16).reshape(B, mlen, -1)
    v_mem = _mm(mem_n, w_mem_v.astype(BF16), BF16).reshape(B, mlen, -1)
    y_c = _xattn(z3, k_mem, v_mem, c_q_norm_g, c_k_norm_g, B, S)

    mix = _mix(y_a.reshape(M, -1), y_bt, y_c.reshape(M, -1),
               w_up_a.astype(BF16), w_up_b.astype(BF16), w_up_c.astype(BF16), z, Z_GATES)
    x2 = _mm_residual(mix, w_out.astype(BF16), x2, tm=512, tn=D)

    act = _swiglu(x2, norm_ffn_g, w_ffn_gate.astype(BF16), w_ffn_up.astype(BF16))
    x2 = _mm_residual(act, w_ffn_down.astype(BF16), x2, tm=1024, tn=512)
    return x2.reshape(B, S, D)


def kernel(x, mem, norm_mix_g, norm_mem_g, norm_ffn_g, w_in, m_conv_w, m_i_bias, m_f_bias, m_norm_g, n_q_norm_g, n_kc_norm_g, n_ks_norm_g, n_kw_norm_g, n_cmp_pe_k, n_cmp_w1_k, n_cmp_w2_k, n_cmp_pe_v, n_cmp_w1_v, n_cmp_w2_v, c_q_norm_g, c_k_norm_g, w_mem_k, w_mem_v, w_up_a, w_up_b, w_up_c, w_out, w_ffn_gate, w_ffn_up, w_ffn_down):
    stacked = (norm_mix_g, norm_mem_g, norm_ffn_g, w_in, m_conv_w, m_i_bias, m_f_bias, m_norm_g, n_q_norm_g,
               n_kc_norm_g, n_ks_norm_g, n_kw_norm_g, n_cmp_pe_k, n_cmp_w1_k, n_cmp_w2_k, n_cmp_pe_v, n_cmp_w1_v,
               n_cmp_w2_v, c_q_norm_g, c_k_norm_g, w_mem_k, w_mem_v, w_up_a, w_up_b, w_up_c, w_out, w_ffn_gate,
               w_ffn_up, w_ffn_down)
    for layer in range(w_in.shape[0]):
        x = _layer(x, mem, *(p[layer] for p in stacked))
    return x
```

```python
import functools

import numpy as np
import jax
import jax.numpy as jnp
from jax import lax
from jax.experimental import pallas as pl
from jax.experimental.pallas import tpu as pltpu

F32 = jnp.float32
BF16 = jnp.bfloat16

EPS = 1e-6
ROPE_THETA = 10000.0
M_HEADS = 4
M_HEAD_DIM = 256
M_CONV = 4
N_HEADS = 16
N_GROUPS = 4
N_REP = N_HEADS // N_GROUPS
N_HEAD_DIM = 64
CMP_LEN = 32
CMP_STRIDE = 16
CMP_HIDDEN = 128
SEL_BLOCK = 64
SEL_TOP = 16
WINDOW = 512
FORCE_SCORE = 1.0e4
C_HEADS = 4
C_HEAD_DIM = 256

LOG2_E = 1.4426950408889634
V_AUG_ROWS = 16
NEG_BIG = -1.0e30
F8 = jnp.float8_e4m3fn
BLOCK_BIAS = -448.0
SEL_WEIGHT_SHIFT = 8.0
V7X_VMEM_BYTES = 64 * 1024 * 1024
VMEM_LIMIT = V7X_VMEM_BYTES - 8 * 1024 * 1024

MLSTM_CHUNK = 256
NSA_TQ = 256
NSA_TK = 512
NSA_RC = 32
NSA_CB = 1024

Z_MQK, Z_MV, Z_MO, Z_CQ, Z_GATES, Z_KC, Z_VC, Z_KS, Z_KW, Z_WIDTH = (
    0, 2048, 3072, 4096, 5120, 11264, 11520, 11776, 12032, 12288)
ZT_Q, ZT_VS, ZT_VW, ZT_ROWS = 0, 1024, 1280, 1536


def _params(*sem):
    return pltpu.CompilerParams(dimension_semantics=sem, vmem_limit_bytes=VMEM_LIMIT)


def _sigmoid(x):
    return 1.0 / (1.0 + jnp.exp(-x))


def _silu(x):
    return x * _sigmoid(x)


def _log_sigmoid(x):
    return jnp.minimum(x, 0.0) - jnp.log1p(jnp.exp(-jnp.abs(x)))


def _rms_rows(x, g):
    return x * lax.rsqrt(jnp.mean(x * x, axis=-1, keepdims=True) + EPS) * g


def _rmsnorm_kernel(x_ref, g_ref, o_ref):
    o_ref[...] = _rms_rows(x_ref[...], g_ref[...]).astype(o_ref.dtype)


def _rmsnorm(x2d, g, tm=512):
    m, d = x2d.shape
    tm = min(tm, m)
    return pl.pallas_call(
        _rmsnorm_kernel,
        grid=(m // tm,),
        in_specs=[pl.BlockSpec((tm, d), lambda i: (i, 0)), pl.BlockSpec((1, d), lambda i: (0, 0))],
        out_specs=pl.BlockSpec((tm, d), lambda i: (i, 0)),
        out_shape=jax.ShapeDtypeStruct((m, d), BF16),
        compiler_params=_params("parallel"),
        name="rmsnorm",
    )(x2d, g.reshape(1, d))


def _mm_kernel(a_ref, w_ref, o_ref):
    o_ref[...] = jnp.dot(a_ref[...], w_ref[...], preferred_element_type=F32).astype(o_ref.dtype)


def _mm(a, w, out_dtype, tm=1024, tn=512):
    m, k = a.shape
    n = w.shape[1]
    tm, tn = min(tm, m), min(tn, n)
    return pl.pallas_call(
        _mm_kernel,
        grid=(m // tm, n // tn),
        in_specs=[pl.BlockSpec((tm, k), lambda i, j: (i, 0)), pl.BlockSpec((k, tn), lambda i, j: (0, j))],
        out_specs=pl.BlockSpec((tm, tn), lambda i, j: (i, j)),
        out_shape=jax.ShapeDtypeStruct((m, n), out_dtype),
        compiler_params=_params("parallel", "arbitrary"),
        name="matmul",
    )(a, w)


def _in_proj_kernel(x_ref, g_ref, w_ref, ws_ref, z_ref, zs_ref, h_ref):
    @pl.when(pl.program_id(1) == 0)
    def _():
        h = _rms_rows(x_ref[...], g_ref[...]).astype(BF16)
        h_ref[...] = h
        zs_ref[...] = jnp.dot(h, ws_ref[...], preferred_element_type=F32)

    z_ref[...] = jnp.dot(h_ref[...], w_ref[...], preferred_element_type=F32).astype(z_ref.dtype)


def _in_proj(x2d, g, w, w_small, tm=1024, tn=1024):
    m, d = x2d.shape
    n, ns = w.shape[1], w_small.shape[1]
    tm, tn = min(tm, m), min(tn, n)
    return pl.pallas_call(
        _in_proj_kernel,
        grid=(m // tm, n // tn),
        in_specs=[pl.BlockSpec((tm, d), lambda i, j: (i, 0)), pl.BlockSpec((1, d), lambda i, j: (0, 0)),
                  pl.BlockSpec((d, tn), lambda i, j: (0, j)), pl.BlockSpec((d, ns), lambda i, j: (0, 0))],
        out_specs=[pl.BlockSpec((tm, tn), lambda i, j: (i, j)), pl.BlockSpec((tm, ns), lambda i, j: (i, 0)),
                   pl.BlockSpec((tm, d), lambda i, j: (i, 0))],
        out_shape=[jax.ShapeDtypeStruct((m, n), BF16), jax.ShapeDtypeStruct((m, ns), F32),
                   jax.ShapeDtypeStruct((m, d), BF16)],
        compiler_params=_params("parallel", "arbitrary"),
        name="in_proj",
    )(x2d, g.reshape(1, d), w, w_small)


def _mm_nt_kernel(w_ref, h_ref, o_ref):
    o_ref[...] = lax.dot_general(w_ref[...], h_ref[...], (((1,), (1,)), ((), ())),
                                 preferred_element_type=F32).astype(o_ref.dtype)


def _mm_feature_major(w_t, h, tm=1024, tn=1536):
    n, k = w_t.shape
    m = h.shape[0]
    tm, tn = min(tm, m), min(tn, n)
    return pl.pallas_call(
        _mm_nt_kernel,
        grid=(m // tm, n // tn),
        in_specs=[pl.BlockSpec((tn, k), lambda i, j: (j, 0)), pl.BlockSpec((tm, k), lambda i, j: (i, 0))],
        out_specs=pl.BlockSpec((tn, tm), lambda i, j: (j, i)),
        out_shape=jax.ShapeDtypeStruct((n, m), BF16),
        compiler_params=_params("parallel", "arbitrary"),
        name="in_proj_feature_major",
    )(w_t, h)


def _mm_res_kernel(a_ref, w_ref, x_ref, o_ref):
    o_ref[...] = x_ref[...] + jnp.dot(a_ref[...], w_ref[...], preferred_element_type=F32)


def _layer_block(layer, block, index_map):
    return pl.BlockSpec((None,) + block, lambda *idx: (layer,) + index_map(*idx))


def _mm_residual(a, w, layer, x, tm=512, tn=512):
    m, k = a.shape
    n = w.shape[2]
    tm, tn = min(tm, m), min(tn, n)
    return pl.pallas_call(
        _mm_res_kernel,
        grid=(m // tm, n // tn),
        in_specs=[pl.BlockSpec((tm, k), lambda i, j: (i, 0)), _layer_block(layer, (k, tn), lambda i, j: (0, j)),
                  pl.BlockSpec((tm, tn), lambda i, j: (i, j))],
        out_specs=pl.BlockSpec((tm, tn), lambda i, j: (i, j)),
        out_shape=jax.ShapeDtypeStruct((m, n), F32),
        compiler_params=_params("parallel", "arbitrary"),
        name="matmul_residual",
    )(a, w, x)


def _swiglu_kernel(x_ref, g_ref, wg_ref, wu_ref, o_ref, h_scr):
    @pl.when(pl.program_id(1) == 0)
    def _():
        h_scr[...] = _rms_rows(x_ref[...], g_ref[...]).astype(BF16)

    h = h_scr[...]
    gate = jnp.dot(h, wg_ref[...], preferred_element_type=F32)
    up = jnp.dot(h, wu_ref[...], preferred_element_type=F32)
    o_ref[...] = (_silu(gate) * up).astype(o_ref.dtype)


def _swiglu(x2d, g, wg, wu, layer, tm=1024, tn=512):
    m, d = x2d.shape
    n = wg.shape[2]
    tm, tn = min(tm, m), min(tn, n)
    return pl.pallas_call(
        _swiglu_kernel,
        grid=(m // tm, n // tn),
        in_specs=[pl.BlockSpec((tm, d), lambda i, j: (i, 0)), pl.BlockSpec((1, d), lambda i, j: (0, 0)),
                  _layer_block(layer, (d, tn), lambda i, j: (0, j)), _layer_block(layer, (d, tn), lambda i, j: (0, j))],
        out_specs=pl.BlockSpec((tm, tn), lambda i, j: (i, j)),
        out_shape=jax.ShapeDtypeStruct((m, n), BF16),
        scratch_shapes=[pltpu.VMEM((tm, d), BF16)],
        compiler_params=_params("parallel", "arbitrary"),
        name="swiglu",
    )(x2d, g.reshape(1, d), wg, wu)


def _mix_kernel(ya_ref, ybt_ref, yc_ref, wa_ref, wb_ref, wc_ref, ga_ref, gb_ref, gc_ref, o_ref):
    def gated(g_ref, prod):
        return _sigmoid(g_ref[...].astype(F32)) * prod

    a = jnp.dot(ya_ref[...], wa_ref[...], preferred_element_type=F32)
    b = lax.dot_general(ybt_ref[...], wb_ref[...], (((0,), (0,)), ((), ())), preferred_element_type=F32)
    c = jnp.dot(yc_ref[...], wc_ref[...], preferred_element_type=F32)
    o_ref[...] = (gated(ga_ref, a) + gated(gb_ref, b) + gated(gc_ref, c)).astype(o_ref.dtype)


def _mix(ya, ybt, yc, wa, wb, wc, layer, z, gate_col0, tm=1024, tn=512):
    m, k = ya.shape
    n = wa.shape[2]
    tm, tn = min(tm, m), min(tn, n)
    gofs = [(gate_col0 + b * n) // tn for b in range(3)]
    y_spec = pl.BlockSpec((tm, k), lambda i, j: (i, 0))
    w_spec = _layer_block(layer, (k, tn), lambda i, j: (0, j))
    g_specs = [pl.BlockSpec((tm, tn), functools.partial(lambda i, j, o: (i, o + j), o=o)) for o in gofs]
    return pl.pallas_call(
        _mix_kernel,
        grid=(m // tm, n // tn),
        in_specs=[y_spec, pl.BlockSpec((k, tm), lambda i, j: (0, i)), y_spec, w_spec, w_spec, w_spec] + g_specs,
        out_specs=pl.BlockSpec((tm, tn), lambda i, j: (i, j)),
        out_shape=jax.ShapeDtypeStruct((m, n), BF16),
        compiler_params=_params("parallel", "arbitrary"),
        name="gated_mix",
    )(ya, ybt, yc, wa, wb, wc, z, z, z)


def _mlstm_kernel(uq_ref, uk_ref, v_ref, o_ref, gcol_ref, grow_ref, bias_ref, cwq_ref, cwk_ref, ng_ref, shift_ref,
                  y_ref, c_scr, n_scr, m_scr, qtail, ktail):
    c = pl.program_id(1)
    L = uq_ref.shape[1]
    hd = M_HEAD_DIM

    @pl.when(c == 0)
    def _():
        c_scr[...] = jnp.zeros_like(c_scr)
        n_scr[...] = jnp.zeros_like(n_scr)
        m_scr[...] = jnp.zeros_like(m_scr)
        qtail[...] = jnp.zeros_like(qtail)
        ktail[...] = jnp.zeros_like(ktail)

    def conv_silu(u_ref, tail, cw_ref):
        u = u_ref[0]
        cw = cw_ref[...]
        cwb = cw.astype(BF16)
        x = jnp.concatenate([u * cwb[j:j + 1, :] for j in range(M_CONV)], axis=0)
        y = jnp.dot(shift_ref[...], x, preferred_element_type=F32)
        prev = tail[...]
        row = lax.broadcasted_iota(jnp.int32, prev.shape, 0)
        edge = jnp.zeros(prev.shape, F32)
        for t in range(M_CONV - 1):
            acc = cw[0:1, :] * prev[8 - (M_CONV - 1) + t:8 - (M_CONV - 1) + t + 1, :]
            for j in range(1, M_CONV - 1 - t):
                acc = acc + cw[j:j + 1, :] * prev[8 - (M_CONV - 1) + t + j:8 - (M_CONV - 1) + t + j + 1, :]
            edge = jnp.where(row == t, acc, edge)
        tail[...] = u[L - 8:L, :].astype(F32)
        return _silu(jnp.concatenate([y[0:8] + edge, y[8:]], axis=0))

    q_all = conv_silu(uq_ref, qtail, cwq_ref)
    k_all = conv_silu(uk_ref, ktail, cwk_ref) * (hd ** -0.5)
    row = lax.broadcasted_iota(jnp.int32, (L, L), 0)
    col = lax.broadcasted_iota(jnp.int32, (L, L), 1)
    tri = col <= row

    for hh in range(M_HEADS):
        sl = slice(hh * hd, (hh + 1) * hd)
        q, k = q_all[:, sl], k_all[:, sl]
        qb = q.astype(BF16)
        kb = k.astype(BF16)
        vb = v_ref[0, :, sl]

        gcol = gcol_ref[0, hh]
        grow = grow_ref[0, hh]
        bias = bias_ref[hh]
        ic = gcol[:, 0:1] + bias[:, 0:1]
        fc = _log_sigmoid(gcol[:, 1:2] + bias[:, 1:2])
        ir = grow[0:1, :] + bias[:, 0:1]
        fr = _log_sigmoid(grow[1:2, :] + bias[:, 1:2])

        b_col = jnp.sum(jnp.where(tri, fr, 0.0), axis=1, keepdims=True)
        b_row = jnp.sum(jnp.where(row <= col, fc, 0.0), axis=0, keepdims=True)
        m_prev = m_scr[hh, 0:1, 0:1]

        log_d = jnp.where(tri, b_col - b_row + ir, NEG_BIG)
        log_inter = b_col + m_prev
        m_t = jnp.maximum(log_inter, jnp.max(log_d, axis=1, keepdims=True))
        w_intra = jnp.exp(log_d - m_t)
        w_inter = jnp.exp(log_inter - m_t)

        s = lax.dot_general(qb, kb, (((1,), (1,)), ((), ())), preferred_element_type=F32) * w_intra
        c_state = c_scr[hh]
        n_state = n_scr[hh, 0:1, :]
        num = (jnp.dot(s.astype(BF16), vb, preferred_element_type=F32)
               + w_inter * jnp.dot(qb, c_state.astype(BF16), preferred_element_type=F32))
        den = jnp.sum(s, axis=1, keepdims=True) + w_inter * jnp.sum(q * n_state, axis=1, keepdims=True)
        h = num / jnp.maximum(jnp.abs(den), jnp.exp(-m_t))

        b_last = jnp.sum(fc, axis=0, keepdims=True)
        log_g = b_last - b_col + ic
        m_new = jnp.maximum(b_last + m_prev, jnp.max(log_g, axis=0, keepdims=True))
        decay = jnp.exp(b_last + m_prev - m_new)
        wk = k * jnp.exp(log_g - m_new)
        c_scr[hh] = decay * c_state + lax.dot_general(wk.astype(BF16), vb, (((0,), (0,)), ((), ())),
                                                      preferred_element_type=F32)
        n_scr[hh, 0:1, :] = decay * n_state + jnp.sum(wk, axis=0, keepdims=True)
        m_scr[hh] = jnp.broadcast_to(m_new, m_scr.shape[1:])

        hn = _rms_rows(h, ng_ref[hh])
        y_ref[0, :, sl] = (_sigmoid(o_ref[0, :, sl].astype(F32)) * hn).astype(y_ref.dtype)


def _conv_shift_matrix(L):
    t = np.arange(L)[:, None]
    r = np.arange(L)[None, :]
    return jnp.asarray(np.concatenate([(r == t - (M_CONV - 1) + j) for j in range(M_CONV)], axis=1), BF16)


def _mlstm(z3, gcol, grow, gbias, conv_w, norm_g, batch, seq):
    L = min(MLSTM_CHUNK, seq)
    hd = M_HEAD_DIM
    nh = M_HEADS
    w = nh * hd

    def zspec(col0):
        return pl.BlockSpec((1, L, w), functools.partial(lambda b, c, o: (b, c, o), o=col0 // w))

    return pl.pallas_call(
        _mlstm_kernel,
        grid=(batch, seq // L),
        in_specs=[zspec(Z_MQK), zspec(Z_MQK + w), zspec(Z_MV), zspec(Z_MO),
                  pl.BlockSpec((1, nh, L, 2), lambda b, c: (b, 0, c, 0)),
                  pl.BlockSpec((1, nh, 2, L), lambda b, c: (b, 0, 0, c)),
                  pl.BlockSpec((nh, 1, 2), lambda b, c: (0, 0, 0)),
                  pl.BlockSpec((M_CONV, w), lambda b, c: (0, 0)),
                  pl.BlockSpec((M_CONV, w), lambda b, c: (0, 1)),
                  pl.BlockSpec((nh, 1, hd), lambda b, c: (0, 0, 0)),
                  pl.BlockSpec((L, M_CONV * L), lambda b, c: (0, 0))],
        out_specs=pl.BlockSpec((1, L, w), lambda b, c: (b, c, 0)),
        out_shape=jax.ShapeDtypeStruct((batch, seq, w), BF16),
        scratch_shapes=[pltpu.VMEM((nh, hd, hd), F32), pltpu.VMEM((nh, 8, hd), F32), pltpu.VMEM((nh, 8, 128), F32),
                        pltpu.VMEM((8, w), F32), pltpu.VMEM((8, w), F32)],
        compiler_params=_params("parallel", "arbitrary"),
        name="mlstm",
    )(z3, z3, z3, z3, gcol, grow, gbias, conv_w, conv_w, norm_g.reshape(nh, 1, hd), _conv_shift_matrix(L))


def _xattn_kernel(q_ref, k_ref, v_ref, gq_ref, gk_ref, o_ref):
    hd = C_HEAD_DIM
    for h in range(C_HEADS):
        sl = slice(h * hd, (h + 1) * hd)
        q = _rms_rows(q_ref[0, :, sl].astype(F32), gq_ref[...]) * (hd ** -0.5)
        k = _rms_rows(k_ref[0, :, sl].astype(F32), gk_ref[...])
        s = lax.dot_general(q.astype(BF16), k.astype(BF16), (((1,), (1,)), ((), ())), preferred_element_type=F32)
        e = jnp.exp(s - jnp.max(s, axis=-1, keepdims=True))
        p = e / jnp.sum(e, axis=-1, keepdims=True)
        o = jnp.dot(p.astype(BF16), v_ref[0, :, sl], preferred_element_type=F32)
        o_ref[0, :, sl] = o.astype(o_ref.dtype)


def _xattn(z3, k, v, gq, gk, batch, seq, tq=512):
    tq = min(tq, seq)
    mlen, w = k.shape[1], k.shape[2]
    return pl.pallas_call(
        _xattn_kernel,
        grid=(batch, seq // tq),
        in_specs=[pl.BlockSpec((1, tq, w), lambda b, i: (b, i, Z_CQ // w)),
                  pl.BlockSpec((1, mlen, w), lambda b, i: (b, 0, 0)),
                  pl.BlockSpec((1, mlen, w), lambda b, i: (b, 0, 0)),
                  pl.BlockSpec((1, C_HEAD_DIM), lambda b, i: (0, 0)),
                  pl.BlockSpec((1, C_HEAD_DIM), lambda b, i: (0, 0))],
        out_specs=pl.BlockSpec((1, tq, w), lambda b, i: (b, i, 0)),
        out_shape=jax.ShapeDtypeStruct((batch, seq, w), BF16),
        compiler_params=_params("parallel", "arbitrary"),
        name="memory_cross_attention",
    )(z3, k, v, gq.reshape(1, -1), gk.reshape(1, -1))


def _rope_rows(xn, cos, sin_signed):
    half = N_HEAD_DIM // 2
    rot = jnp.concatenate([xn[:, half:], xn[:, :half]], axis=-1)
    return xn * cos + rot * sin_signed


def _knorm_rope_kernel(x_ref, g_ref, cos_ref, sin_ref, seg_ref, swap_ref, o_ref, *, n_blocks):
    hd = N_HEAD_DIM
    ts = x_ref.shape[1]
    x = x_ref[0].astype(F32)
    xx = x * x
    hi = xx.astype(BF16)
    lo = (xx - hi.astype(F32)).astype(BF16)
    ss = (jnp.dot(hi, seg_ref[...], preferred_element_type=F32)
          + jnp.dot(lo, seg_ref[...], preferred_element_type=F32))
    xn = x * lax.rsqrt(ss * (1.0 / hd) + EPS) * g_ref[...]
    rot = jnp.dot(xn.astype(BF16), swap_ref[...], preferred_element_type=F32)
    reps = N_GROUPS * hd // cos_ref.shape[1]
    cos = jnp.concatenate([cos_ref[...]] * reps, axis=1)
    sin = jnp.concatenate([sin_ref[...]] * reps, axis=1)
    k = (xn * cos + rot * sin).astype(o_ref.dtype)
    if n_blocks:
        pos = pl.program_id(1) * ts + lax.broadcasted_iota(jnp.int32, (ts, n_blocks), 0)
        blk = lax.broadcasted_iota(jnp.int32, (ts, n_blocks), 1)
        onehot = jnp.where(pos // SEL_BLOCK == blk, 1.0, 0.0).astype(o_ref.dtype)
    for g in range(N_GROUPS):
        if n_blocks:
            o_ref[0, g, :, 0:n_blocks] = onehot
        o_ref[0, g, :, n_blocks:n_blocks + hd] = k[:, g * hd:(g + 1) * hd]


def _head_matrices():
    w = N_GROUPS * N_HEAD_DIM
    i = np.arange(w)[:, None]
    j = np.arange(w)[None, :]
    same = (i // N_HEAD_DIM) == (j // N_HEAD_DIM)
    swap = same & ((i % N_HEAD_DIM) == (j % N_HEAD_DIM + N_HEAD_DIM // 2) % N_HEAD_DIM)
    return jnp.asarray(same, BF16), jnp.asarray(swap, BF16)


def _knorm_rope(z3, col0, g, cos2, sin2, n_blocks, ts=1024):
    b, s, _ = z3.shape
    hd, w = N_HEAD_DIM, N_GROUPS * N_HEAD_DIM
    ts = min(ts, s)
    same, swap = _head_matrices()
    const = lambda shape: pl.BlockSpec(shape, lambda i, j: (0, 0))
    return pl.pallas_call(
        functools.partial(_knorm_rope_kernel, n_blocks=n_blocks),
        grid=(b, s // ts),
        in_specs=[pl.BlockSpec((1, ts, w), lambda i, j: (i, j, col0 // w)), const((1, w)),
                  pl.BlockSpec((ts, 2 * hd), lambda i, j: (j, 0)), pl.BlockSpec((ts, 2 * hd), lambda i, j: (j, 0)),
                  const((w, w)), const((w, w))],
        out_specs=pl.BlockSpec((1, N_GROUPS, ts, n_blocks + hd), lambda i, j: (i, 0, j, 0)),
        out_shape=jax.ShapeDtypeStruct((b, N_GROUPS, s, n_blocks + hd), F8 if n_blocks else BF16),
        compiler_params=_params("parallel", "arbitrary"),
        name="key_norm_rope",
    )(z3, jnp.tile(g.reshape(1, hd), (1, N_GROUPS)), cos2, sin2, same, swap)


def _compress_kernel(sub_ref, pe_ref, w1_ref, w2_ref, g_ref, cos_ref, sin_ref, o_ref, *, is_key):
    ns = sub_ref.shape[1]
    sub = sub_ref[0].astype(F32)
    lo = jnp.dot((sub + pe_ref[0:1, :]).astype(BF16), w1_ref[0], preferred_element_type=F32)
    hi = jnp.dot((sub + pe_ref[1:2, :]).astype(BF16), w1_ref[1], preferred_element_type=F32)
    hid = _silu(lo + pltpu.roll(hi, shift=ns - 1, axis=0))
    out = jnp.dot(hid.astype(BF16), w2_ref[...], preferred_element_type=F32)
    if is_key:
        out = _rope_rows(_rms_rows(out, g_ref[...]), cos_ref[...], sin_ref[...])
    o_ref[0] = out.astype(o_ref.dtype)


def _compress(u, pe, w1, w2, g, cos, sin_signed, is_key):
    n, s, hd = u.shape
    ns = s // CMP_STRIDE
    width = CMP_STRIDE * hd
    sub = u.reshape(n, ns, width)
    pe2 = pe.reshape(CMP_LEN // CMP_STRIDE, width)
    w1s = w1.reshape(CMP_LEN // CMP_STRIDE, width, CMP_HIDDEN).astype(BF16)
    return pl.pallas_call(
        functools.partial(_compress_kernel, is_key=is_key),
        grid=(n,),
        in_specs=[pl.BlockSpec((1, ns, width), lambda i: (i, 0, 0)),
                  pl.BlockSpec(pe2.shape, lambda i: (0, 0)),
                  pl.BlockSpec(w1s.shape, lambda i: (0, 0, 0)),
                  pl.BlockSpec(w2.shape, lambda i: (0, 0)),
                  pl.BlockSpec((1, hd), lambda i: (0, 0)),
                  pl.BlockSpec((ns, hd), lambda i: (0, 0)),
                  pl.BlockSpec((ns, hd), lambda i: (0, 0))],
        out_specs=pl.BlockSpec((1, ns, hd), lambda i: (i, 0, 0)),
        out_shape=jax.ShapeDtypeStruct((n, ns, hd), BF16),
        compiler_params=_params("parallel"),
        name="compress_key" if is_key else "compress_value",
    )(sub, pe2, w1s, w2.astype(BF16), g.reshape(1, hd), cos, sin_signed)


def _nsa_kernel(q_ref, gate_ref, kc_ref, vct_ref, ks_ref, vst_ref, kw_ref, vwt_ref, gq_ref, cos_ref, sin_ref,
                wimp_ref, y_ref, s_c, e_c, s_w, e_w, s_a, s_b, e_a, e_b, *, seq):
    tq = q_ref.shape[1]
    hd = N_HEAD_DIM
    nb = seq // SEL_BLOCK
    top = min(SEL_TOP, nb)
    ncol = N_REP * tq
    t0 = pl.program_id(2) * tq

    def lanes(parts):
        return jnp.concatenate(parts, axis=1)

    def head_cols(a, r):
        return a[:, r * tq:(r + 1) * tq]

    def with_ones(v):
        n = v.shape[1]
        tail = jnp.where(lax.broadcasted_iota(jnp.int32, (V_AUG_ROWS, n), 0) == 0, 1.0, 0.0).astype(v.dtype)
        return jnp.concatenate([v, tail], axis=0)

    cb = min(ncol, NSA_CB)

    def chunk_scores(src, c, j, valid_fn):
        sc = src[c * NSA_RC:(c + 1) * NSA_RC, j * cb:(j + 1) * cb]
        if valid_fn is None:
            return sc
        valid = valid_fn(c * NSA_RC + lax.broadcasted_iota(jnp.int32, (NSA_RC, 1), 0))
        return lanes([jnp.where(valid, head_cols(sc, r), NEG_BIG) for r in range(cb // tq)])

    def put_scores(dst, s):
        dst[...] = s
        return jnp.max(s.reshape(s.shape[0] // 8, 8, ncol), axis=0)

    def col_max(src, rows, valid_fn):
        out = []
        for j in range(ncol // cb):
            mx = jnp.full((8, cb), NEG_BIG, F32)
            for c in range(rows // NSA_RC):
                sc = chunk_scores(src, c, j, valid_fn)
                mx = jnp.maximum(mx, jnp.max(sc.reshape(NSA_RC // 8, 8, cb), axis=0))
            out.append(jnp.max(mx, axis=0, keepdims=True))
        return lanes(out)

    def put_weights(src, dst, rows, valid_fn, m):
        for j in range(ncol // cb):
            mj = m[:, j * cb:(j + 1) * cb]
            for c in range(rows // NSA_RC):
                sc = chunk_scores(src, c, j, valid_fn)
                dst[c * NSA_RC:(c + 1) * NSA_RC, j * cb:(j + 1) * cb] = (
                    jnp.exp2((sc - mj).astype(BF16)).astype(dst.dtype))

    q = lanes([q_ref[r * hd:(r + 1) * hd, :].astype(F32) for r in range(N_REP)])
    qn = q * lax.rsqrt(jnp.mean(q * q, axis=0, keepdims=True) + EPS) * gq_ref[...]
    cos = lanes([cos_ref[...]] * N_REP)
    sin = lanes([sin_ref[...]] * N_REP)
    rot = jnp.concatenate([qn[hd // 2:], qn[:hd // 2]], axis=0)
    qr = ((qn * cos + rot * sin) * (hd ** -0.5 * LOG2_E)).astype(BF16)
    t_q = t0 + lax.broadcasted_iota(jnp.int32, (1, tq), 1)
    tcol = lanes([t_q] * N_REP)

    nc = kc_ref.shape[2]
    s_c[...] = jnp.dot(kc_ref[0, 0], qr, preferred_element_type=F32)
    valid_c = lambda n: n * CMP_STRIDE + (CMP_LEN - 1) <= t_q
    put_weights(s_c, e_c, nc, valid_c, col_max(s_c, nc, valid_c))
    both = jnp.dot(jnp.concatenate([vct_ref[0, 0], wimp_ref[...]], axis=0), e_c[...], preferred_element_type=F32)
    inv_c = jnp.where(tcol >= CMP_LEN - 1, 1.0 / both[hd:hd + 1], 0.0)
    o_c = both[0:hd] * inv_c

    imp_h = both[hd + V_AUG_ROWS:] * inv_c
    imp = head_cols(imp_h, 0)
    for r in range(1, N_REP):
        imp = imp + head_cols(imp_h, r)

    span = WINDOW + tq
    ws0 = pl.multiple_of(jnp.maximum(t0 - WINDOW, 0), tq)
    s_w[...] = jnp.dot(kw_ref[0, 0, pl.ds(ws0, span), :], qr, preferred_element_type=F32)
    valid_w = lambda k: (ws0 + k <= t_q) & (ws0 + k > t_q - WINDOW)
    m_w = col_max(s_w, span, valid_w)

    blk = lax.broadcasted_iota(jnp.int32, (nb, tq), 0)
    cur = (t0 + lax.broadcasted_iota(jnp.int32, (nb, tq), 1)) // SEL_BLOCK
    causal_b = blk <= cur
    forced = (blk == 0) | (blk == cur) | (blk == cur - 1)
    score = jnp.where(forced, FORCE_SCORE, jnp.where(causal_b, imp, -1.0))
    taken = -3.0e38

    def pick_by_value(sc):
        for _ in range(top - 2):
            sc = jnp.where(sc == jnp.max(sc, axis=0, keepdims=True), taken, sc)
        return sc

    def pick_by_value_then_index(sc):
        for _ in range(top):
            mx = jnp.max(sc, axis=0, keepdims=True)
            idx = jnp.min(jnp.where(sc == mx, blk, nb), axis=0, keepdims=True)
            sc = jnp.where(blk == idx, taken, sc)
        return sc

    fast = pick_by_value(score)
    n_taken = jnp.sum(jnp.where((fast == taken) & causal_b, 1.0, 0.0), axis=0, keepdims=True)
    n_want = jnp.minimum(cur[0:1, :] + 1, top).astype(F32)
    tied = jnp.max(jnp.abs(n_taken - n_want)) > 0.0
    picked = lax.cond(tied, lambda: pick_by_value_then_index(score), lambda: fast)
    bias = jnp.where((picked == taken) & causal_b, 0.0, BLOCK_BIAS)
    q_aug = jnp.concatenate([lanes([bias.astype(BF16)] * N_REP), qr], axis=0).astype(F8)

    tk = min(NSA_TK, seq)
    last = (t0 + tq - 1) // tk
    s_bufs, e_bufs = (s_a, s_b), (e_a, e_b)

    def scores(t, par):
        ks0 = pl.multiple_of(t * tk, tk)
        return put_scores(s_bufs[par], jnp.dot(ks_ref[0, 0, pl.ds(ks0, tk), :], q_aug, preferred_element_type=F32))

    def weights(t, par, m, mx, masked):
        valid_s = (lambda k: t * tk + k <= t_q) if masked else None
        m_new = jnp.maximum(m, jnp.max(mx, axis=0, keepdims=True))
        put_weights(s_bufs[par], e_bufs[par], tk, valid_s, m_new - SEL_WEIGHT_SHIFT)
        return m_new, jnp.exp2(m - m_new)

    def values(t, par, alpha, acc):
        ks0 = pl.multiple_of(jnp.maximum(t, 0) * tk, tk)
        return alpha * acc + jnp.dot(with_ones(vst_ref[:, pl.ds(ks0, tk)]).astype(F8), e_bufs[par][...],
                                     preferred_element_type=F32)

    def step(t, par, carry, masked, more):
        m, alpha, acc, mx = carry
        mx_next = scores(t + 1, 1 - par) if more else mx
        acc = values(t - 1, 1 - par, alpha, acc)
        m, alpha = weights(t, par, m, mx, masked)
        return m, alpha, acc, mx_next

    def pair_body(j, carry):
        return step(2 * j + 1, 1, step(2 * j, 0, carry, False, True), False, True)

    def tail_even(carry):
        m, alpha, acc, _ = step(last, 0, carry, True, False)
        return values(last, 0, alpha, acc)

    def tail_odd(carry):
        m, alpha, acc, _ = step(last, 1, step(last - 1, 0, carry, False, True), True, False)
        return values(last, 1, alpha, acc)

    mx0 = scores(0, 0)
    e_b[...] = jnp.zeros(e_b.shape, e_b.dtype)
    put_weights(s_w, e_w, span, valid_w, m_w)
    o_w = jnp.dot(with_ones(vwt_ref[:, pl.ds(ws0, span)]), e_w[...], preferred_element_type=F32)
    o_w = o_w[0:hd] / o_w[hd:hd + 1]
    carry = (jnp.full((1, ncol), NEG_BIG, F32), jnp.ones((1, ncol), F32), jnp.zeros((hd + V_AUG_ROWS, ncol), F32),
             mx0)
    carry = lax.fori_loop(0, last // 2, pair_body, carry)
    acc = lax.cond(last % 2 == 0, tail_even, tail_odd, carry)
    o_s = acc[0:hd] / acc[hd:hd + 1]

    g = _sigmoid(gate_ref[0, 0])
    gc, gs, gw = (lanes([g[j, r:r + 1, :] for r in range(N_REP)]) for j in range(3))
    out = gc * o_c + gs * o_s + gw * o_w
    for r in range(N_REP):
        y_ref[r * hd:(r + 1) * hd, :] = head_cols(out, r).astype(y_ref.dtype)


def _nsa(zt, gates, kc, vct, ks_aug, kw, gq, cos_t, sin_t, wimp_t, batch, seq):
    g, hd = N_GROUPS, N_HEAD_DIM
    tq = min(NSA_TQ, seq)
    tk = min(NSA_TK, seq)
    nq = seq // tq
    ncol = N_REP * tq
    nc = kc.shape[2]
    nb = seq // SEL_BLOCK
    full = lambda *shape: pl.BlockSpec((1, 1) + shape, lambda bi, gi, qi: (bi, gi) + (0,) * len(shape))
    vrow = lambda row0: pl.BlockSpec((hd, seq), functools.partial(lambda bi, gi, qi, o: (o + gi, bi), o=row0 // hd))
    qspec = pl.BlockSpec((N_REP * hd, tq), lambda bi, gi, qi: (ZT_Q // (N_REP * hd) + gi, bi * nq + qi))
    return pl.pallas_call(
        functools.partial(_nsa_kernel, seq=seq),
        grid=(batch, g, nq),
        in_specs=[qspec,
                  pl.BlockSpec((1, 1, 3, N_REP, tq), lambda bi, gi, qi: (bi, gi, 0, 0, qi)),
                  full(nc, hd), full(hd + V_AUG_ROWS, nc), full(seq, nb + hd), vrow(ZT_VS), full(seq, hd),
                  vrow(ZT_VW),
                  pl.BlockSpec((hd, 1), lambda bi, gi, qi: (0, 0)),
                  pl.BlockSpec((hd, tq), lambda bi, gi, qi: (0, qi)),
                  pl.BlockSpec((hd, tq), lambda bi, gi, qi: (0, qi)),
                  pl.BlockSpec((nb, nc), lambda bi, gi, qi: (0, 0))],
        out_specs=pl.BlockSpec((N_REP * hd, tq), lambda bi, gi, qi: (gi, bi * nq + qi)),
        out_shape=jax.ShapeDtypeStruct((N_HEADS * hd, batch * seq), BF16),
        scratch_shapes=[pltpu.VMEM((nc, ncol), F32), pltpu.VMEM((nc, ncol), BF16),
                        pltpu.VMEM((WINDOW + tq, ncol), F32), pltpu.VMEM((WINDOW + tq, ncol), BF16),
                        pltpu.VMEM((tk, ncol), F32), pltpu.VMEM((tk, ncol), F32),
                        pltpu.VMEM((tk, ncol), F8), pltpu.VMEM((tk, ncol), F8)],
        compiler_params=_params("parallel", "parallel", "arbitrary"),
        name="nsa_attention",
    )(zt, gates, kc, vct, ks_aug, zt, kw, zt, gq.reshape(hd, 1), cos_t, sin_t, wimp_t)


def _rope_tables(pos):
    half = N_HEAD_DIM // 2
    inv = jnp.power(ROPE_THETA, -jnp.arange(half, dtype=F32) * 2.0 / N_HEAD_DIM)
    ang = pos.astype(F32)[:, None] * inv[None, :]
    cos, sin = jnp.cos(ang), jnp.sin(ang)
    return jnp.concatenate([cos, cos], axis=-1), jnp.concatenate([-sin, sin], axis=-1)


def _importance_matrix(nb, nc_pad):
    per = SEL_BLOCK // CMP_STRIDE
    j = np.arange(nb)[:, None]
    n = np.arange(nc_pad)[None, :]
    w = np.zeros((nb, nc_pad), np.float32)
    for d in range(CMP_LEN // CMP_STRIDE):
        w += ((n + d >= per * j) & (n + d <= per * j + per - 1)).astype(np.float32)
    return jnp.asarray(w, BF16)


def _split_w_in(w_in):
    d = w_in.shape[0]
    sizes = (2048, 1024, 1024, 4, 4, 1024, 256, 256, 256, 256, 256, 256, 48, 1024, 2048, 2048, 2048)
    offs = np.concatenate([[0], np.cumsum(sizes)])
    part = lambda i: w_in[:, offs[i]:offs[i + 1]]
    tok = jnp.concatenate([part(i) for i in (0, 1, 2, 13, 14, 15, 16, 6, 7, 8, 10)], axis=1).astype(BF16)
    feat = jnp.concatenate([part(i) for i in (5, 9, 11)], axis=1).T.astype(BF16)
    small = jnp.concatenate([part(3), part(4), part(12), jnp.zeros((d, 128 - 56), w_in.dtype)], axis=1).astype(BF16)
    return tok, feat, small


def _layer(x, mem, layer, big, norm_mix_g, norm_mem_g, norm_ffn_g, w_in, m_conv_w, m_i_bias, m_f_bias, m_norm_g,
           n_q_norm_g, n_kc_norm_g, n_ks_norm_g, n_kw_norm_g, n_cmp_pe_k, n_cmp_w1_k, n_cmp_w2_k,
           n_cmp_pe_v, n_cmp_w1_v, n_cmp_w2_v, c_q_norm_g, c_k_norm_g, w_mem_k, w_mem_v):
    B, S, D = x.shape
    M = B * S
    G, R, hd = N_GROUPS, N_REP, N_HEAD_DIM
    x2 = x.reshape(M, D)

    w_tok, w_feat, w_small = _split_w_in(w_in)
    z, zs, h = _in_proj(x2, norm_mix_g, w_tok, w_small)
    zt = _mm_feature_major(w_feat, h)
    z3 = z.reshape(B, S, Z_WIDTH)
    zs3 = zs.reshape(B, S, 128)

    gif = jnp.stack([zs3[..., 0:4], zs3[..., 4:8]], axis=-1)
    gcol = gif.transpose(0, 2, 1, 3)
    grow = gif.transpose(0, 2, 3, 1)
    gbias = jnp.stack([m_i_bias, m_f_bias], axis=-1).reshape(M_HEADS, 1, 2).astype(F32)
    y_a = _mlstm(z3, gcol, grow, gbias, m_conv_w, m_norm_g, B, S)

    pos = jnp.arange(S, dtype=jnp.int32)
    cos, sin_s = _rope_tables(pos)
    nc_pad = S // CMP_STRIDE
    cmp_end = jnp.arange(nc_pad, dtype=jnp.int32) * CMP_STRIDE + CMP_LEN - 1
    cos_c, sin_c = _rope_tables(cmp_end)
    nb = S // SEL_BLOCK

    def head_major(col0):
        u = z3[..., col0:col0 + G * hd].reshape(B, S, G, hd)
        return u.transpose(0, 2, 1, 3).reshape(B * G, S, hd)

    kc = _compress(head_major(Z_KC), n_cmp_pe_k, n_cmp_w1_k, n_cmp_w2_k, n_kc_norm_g, cos_c, sin_c, True)
    vc = _compress(head_major(Z_VC), n_cmp_pe_v, n_cmp_w1_v, n_cmp_w2_v, n_kc_norm_g, cos_c, sin_c, False)
    vct = vc.reshape(B, G, nc_pad, hd).transpose(0, 1, 3, 2)
    ones_rows = jnp.concatenate([jnp.ones((B, G, 1, nc_pad), BF16), jnp.zeros((B, G, V_AUG_ROWS - 1, nc_pad), BF16)],
                                axis=2)
    vct = jnp.concatenate([vct, ones_rows], axis=2)
    cos2, sin2 = jnp.concatenate([cos, cos], axis=1), jnp.concatenate([sin_s, sin_s], axis=1)
    ks_aug = _knorm_rope(z3, Z_KS, n_ks_norm_g, cos2, sin2, nb)
    kw = _knorm_rope(z3, Z_KW, n_kw_norm_g, cos2, sin2, 0)
    gates = zs3[..., 8:8 + 3 * N_HEADS].reshape(B, S, G, R, 3).transpose(0, 2, 4, 3, 1)
    y_bt = _nsa(zt, gates, kc.reshape(B, G, nc_pad, hd), vct, ks_aug, kw, n_q_norm_g, cos.T, sin_s.T,
                _importance_matrix(nb, nc_pad), B, S)

    mlen = mem.shape[1]
    mem_n = _rmsnorm(mem.reshape(B * mlen, D), norm_mem_g)
    k_mem = _mm(mem_n, w_mem_k.astype(BF16), BF16).reshape(B, mlen, -1)
    v_mem = _mm(mem_n, w_mem_v.astype(BF16), BF16).reshape(B, mlen, -1)
    y_c = _xattn(z3, k_mem, v_mem, c_q_norm_g, c_k_norm_g, B, S)

    mix = _mix(y_a.reshape(M, -1), y_bt, y_c.reshape(M, -1), big["w_up_a"], big["w_up_b"], big["w_up_c"], layer,
               z, Z_GATES)
    x2 = _mm_residual(mix, big["w_out"], layer, x2, tm=512, tn=D)

    act = _swiglu(x2, norm_ffn_g, big["w_ffn_gate"], big["w_ffn_up"], layer)
    x2 = _mm_residual(act, big["w_ffn_down"], layer, x2, tm=1024, tn=512)
    return x2.reshape(B, S, D)


def kernel(x, mem, norm_mix_g, norm_mem_g, norm_ffn_g, w_in, m_conv_w, m_i_bias, m_f_bias, m_norm_g, n_q_norm_g, n_kc_norm_g, n_ks_norm_g, n_kw_norm_g, n_cmp_pe_k, n_cmp_w1_k, n_cmp_w2_k, n_cmp_pe_v, n_cmp_w1_v, n_cmp_w2_v, c_q_norm_g, c_k_norm_g, w_mem_k, w_mem_v, w_up_a, w_up_b, w_up_c, w_out, w_ffn_gate, w_ffn_up, w_ffn_down):
    per_layer = (norm_mix_g, norm_mem_g, norm_ffn_g, w_in, m_conv_w, m_i_bias, m_f_bias, m_norm_g, n_q_norm_g,
                 n_kc_norm_g, n_ks_norm_g, n_kw_norm_g, n_cmp_pe_k, n_cmp_w1_k, n_cmp_w2_k, n_cmp_pe_v, n_cmp_w1_v,
                 n_cmp_w2_v, c_q_norm_g, c_k_norm_g, w_mem_k, w_mem_v)
    big = {name: w.astype(BF16) for name, w in (
        ("w_up_a", w_up_a), ("w_up_b", w_up_b), ("w_up_c", w_up_c), ("w_out", w_out),
        ("w_ffn_gate", w_ffn_gate), ("w_ffn_up", w_ffn_up), ("w_ffn_down", w_ffn_down))}
    for layer in range(w_in.shape[0]):
        x = _layer(x, mem, layer, big, *(p[layer] for p in per_layer))
    return x
```

```python
import functools

import numpy as np
import jax
import jax.numpy as jnp
from jax import lax
from jax.experimental import pallas as pl
from jax.experimental.pallas import tpu as pltpu

F32 = jnp.float32
BF16 = jnp.bfloat16

EPS = 1e-6
ROPE_THETA = 10000.0
M_HEADS = 4
M_HEAD_DIM = 256
M_CONV = 4
N_HEADS = 16
N_GROUPS = 4
N_REP = N_HEADS // N_GROUPS
N_HEAD_DIM = 64
CMP_LEN = 32
CMP_STRIDE = 16
CMP_HIDDEN = 128
SEL_BLOCK = 64
SEL_TOP = 16
WINDOW = 512
FORCE_SCORE = 1.0e4
C_HEADS = 4
C_HEAD_DIM = 256

LOG2_E = 1.4426950408889634
V_AUG_ROWS = 16
NEG_BIG = -1.0e30
F8 = jnp.float8_e4m3fn
F8_MAX = 448.0
BLOCK_BIAS = -F8_MAX
SEL_WEIGHT_SHIFT = 8.0
V7X_VMEM_BYTES = 64 * 1024 * 1024
VMEM_LIMIT = V7X_VMEM_BYTES - 8 * 1024 * 1024

MLSTM_CHUNK = 256
NSA_TQ = 256
NSA_TK = 512
NSA_RC = 32
NSA_CB = 1024

Z_MQK, Z_MV, Z_MO, Z_CQ, Z_GATES, Z_KC, Z_VC, Z_KS, Z_KW, Z_WIDTH = (
    0, 2048, 3072, 4096, 5120, 11264, 11520, 11776, 12032, 12288)
ZT_Q, ZT_VS, ZT_VW, ZT_ROWS = 0, 1024, 1280, 1536


def _params(*sem):
    return pltpu.CompilerParams(dimension_semantics=sem, vmem_limit_bytes=VMEM_LIMIT)


def _sigmoid(x):
    return 1.0 / (1.0 + jnp.exp(-x))


def _silu(x):
    return x * _sigmoid(x)


def _log_sigmoid(x):
    return jnp.minimum(x, 0.0) - jnp.log1p(jnp.exp(-jnp.abs(x)))


def _rms_rows(x, g):
    return x * lax.rsqrt(jnp.mean(x * x, axis=-1, keepdims=True) + EPS) * g


def _rmsnorm_kernel(x_ref, g_ref, o_ref):
    o_ref[...] = _rms_rows(x_ref[...], g_ref[...]).astype(o_ref.dtype)


def _rmsnorm(x2d, g, tm=512):
    m, d = x2d.shape
    tm = min(tm, m)
    return pl.pallas_call(
        _rmsnorm_kernel,
        grid=(m // tm,),
        in_specs=[pl.BlockSpec((tm, d), lambda i: (i, 0)), pl.BlockSpec((1, d), lambda i: (0, 0))],
        out_specs=pl.BlockSpec((tm, d), lambda i: (i, 0)),
        out_shape=jax.ShapeDtypeStruct((m, d), BF16),
        compiler_params=_params("parallel"),
        name="rmsnorm",
    )(x2d, g.reshape(1, d))


def _mm_kernel(a_ref, w_ref, o_ref):
    o_ref[...] = jnp.dot(a_ref[...], w_ref[...], preferred_element_type=F32).astype(o_ref.dtype)


def _mm(a, w, out_dtype, tm=1024, tn=512):
    m, k = a.shape
    n = w.shape[1]
    tm, tn = min(tm, m), min(tn, n)
    return pl.pallas_call(
        _mm_kernel,
        grid=(m // tm, n // tn),
        in_specs=[pl.BlockSpec((tm, k), lambda i, j: (i, 0)), pl.BlockSpec((k, tn), lambda i, j: (0, j))],
        out_specs=pl.BlockSpec((tm, tn), lambda i, j: (i, j)),
        out_shape=jax.ShapeDtypeStruct((m, n), out_dtype),
        compiler_params=_params("parallel", "arbitrary"),
        name="matmul",
    )(a, w)


def _in_proj_kernel(x_ref, g_ref, w_ref, ws_ref, z_ref, zs_ref, h_ref):
    @pl.when(pl.program_id(1) == 0)
    def _():
        h = _rms_rows(x_ref[...], g_ref[...]).astype(BF16)
        h_ref[...] = h
        zs_ref[...] = jnp.dot(h, ws_ref[...], preferred_element_type=F32)

    z_ref[...] = jnp.dot(h_ref[...], w_ref[...], preferred_element_type=F32).astype(z_ref.dtype)


def _in_proj(x2d, g, w, w_small, tm=1024, tn=1024):
    m, d = x2d.shape
    n, ns = w.shape[1], w_small.shape[1]
    tm, tn = min(tm, m), min(tn, n)
    return pl.pallas_call(
        _in_proj_kernel,
        grid=(m // tm, n // tn),
        in_specs=[pl.BlockSpec((tm, d), lambda i, j: (i, 0)), pl.BlockSpec((1, d), lambda i, j: (0, 0)),
                  pl.BlockSpec((d, tn), lambda i, j: (0, j)), pl.BlockSpec((d, ns), lambda i, j: (0, 0))],
        out_specs=[pl.BlockSpec((tm, tn), lambda i, j: (i, j)), pl.BlockSpec((tm, ns), lambda i, j: (i, 0)),
                   pl.BlockSpec((tm, d), lambda i, j: (i, 0))],
        out_shape=[jax.ShapeDtypeStruct((m, n), BF16), jax.ShapeDtypeStruct((m, ns), F32),
                   jax.ShapeDtypeStruct((m, d), BF16)],
        compiler_params=_params("parallel", "arbitrary"),
        name="in_proj",
    )(x2d, g.reshape(1, d), w, w_small)


def _mm_nt_kernel(w_ref, h_ref, o_ref):
    o_ref[...] = lax.dot_general(w_ref[...], h_ref[...], (((1,), (1,)), ((), ())),
                                 preferred_element_type=F32).astype(o_ref.dtype)


def _mm_feature_major(w_t, h, tm=1024, tn=1536):
    n, k = w_t.shape
    m = h.shape[0]
    tm, tn = min(tm, m), min(tn, n)
    return pl.pallas_call(
        _mm_nt_kernel,
        grid=(m // tm, n // tn),
        in_specs=[pl.BlockSpec((tn, k), lambda i, j: (j, 0)), pl.BlockSpec((tm, k), lambda i, j: (i, 0))],
        out_specs=pl.BlockSpec((tn, tm), lambda i, j: (j, i)),
        out_shape=jax.ShapeDtypeStruct((n, m), BF16),
        compiler_params=_params("parallel", "arbitrary"),
        name="in_proj_feature_major",
    )(w_t, h)


def _mm_res_kernel(a_ref, w_ref, x_ref, o_ref):
    o_ref[...] = x_ref[...] + jnp.dot(a_ref[...], w_ref[...], preferred_element_type=F32)


def _layer_block(layer, block, index_map):
    return pl.BlockSpec((None,) + block, lambda *idx: (layer,) + index_map(*idx))


def _mm_residual(a, w, layer, x, tm=512, tn=512):
    m, k = a.shape
    n = w.shape[2]
    tm, tn = min(tm, m), min(tn, n)
    return pl.pallas_call(
        _mm_res_kernel,
        grid=(m // tm, n // tn),
        in_specs=[pl.BlockSpec((tm, k), lambda i, j: (i, 0)), _layer_block(layer, (k, tn), lambda i, j: (0, j)),
                  pl.BlockSpec((tm, tn), lambda i, j: (i, j))],
        out_specs=pl.BlockSpec((tm, tn), lambda i, j: (i, j)),
        out_shape=jax.ShapeDtypeStruct((m, n), F32),
        compiler_params=_params("parallel", "arbitrary"),
        name="matmul_residual",
    )(a, w, x)


def _swiglu_kernel(x_ref, g_ref, wg_ref, wu_ref, o_ref, h_scr):
    @pl.when(pl.program_id(1) == 0)
    def _():
        h_scr[...] = _rms_rows(x_ref[...], g_ref[...]).astype(BF16)

    h = h_scr[...]
    gate = jnp.dot(h, wg_ref[...], preferred_element_type=F32)
    up = jnp.dot(h, wu_ref[...], preferred_element_type=F32)
    o_ref[...] = (_silu(gate) * up).astype(o_ref.dtype)


def _swiglu(x2d, g, wg, wu, layer, tm=1024, tn=512):
    m, d = x2d.shape
    n = wg.shape[2]
    tm, tn = min(tm, m), min(tn, n)
    return pl.pallas_call(
        _swiglu_kernel,
        grid=(m // tm, n // tn),
        in_specs=[pl.BlockSpec((tm, d), lambda i, j: (i, 0)), pl.BlockSpec((1, d), lambda i, j: (0, 0)),
                  _layer_block(layer, (d, tn), lambda i, j: (0, j)), _layer_block(layer, (d, tn), lambda i, j: (0, j))],
        out_specs=pl.BlockSpec((tm, tn), lambda i, j: (i, j)),
        out_shape=jax.ShapeDtypeStruct((m, n), BF16),
        scratch_shapes=[pltpu.VMEM((tm, d), BF16)],
        compiler_params=_params("parallel", "arbitrary"),
        name="swiglu",
    )(x2d, g.reshape(1, d), wg, wu)


def _mix_kernel(ya_ref, ybt_ref, yc_ref, wa_ref, wb_ref, wc_ref, ga_ref, gb_ref, gc_ref, o_ref):
    def gated(g_ref, prod):
        return _sigmoid(g_ref[...].astype(F32)) * prod

    a = jnp.dot(ya_ref[...], wa_ref[...], preferred_element_type=F32)
    b = lax.dot_general(ybt_ref[...], wb_ref[...], (((0,), (0,)), ((), ())), preferred_element_type=F32)
    c = jnp.dot(yc_ref[...], wc_ref[...], preferred_element_type=F32)
    o_ref[...] = (gated(ga_ref, a) + gated(gb_ref, b) + gated(gc_ref, c)).astype(o_ref.dtype)


def _mix(ya, ybt, yc, wa, wb, wc, layer, z, gate_col0, tm=1024, tn=512):
    m, k = ya.shape
    n = wa.shape[2]
    tm, tn = min(tm, m), min(tn, n)
    gofs = [(gate_col0 + b * n) // tn for b in range(3)]
    y_spec = pl.BlockSpec((tm, k), lambda i, j: (i, 0))
    w_spec = _layer_block(layer, (k, tn), lambda i, j: (0, j))
    g_specs = [pl.BlockSpec((tm, tn), functools.partial(lambda i, j, o: (i, o + j), o=o)) for o in gofs]
    return pl.pallas_call(
        _mix_kernel,
        grid=(m // tm, n // tn),
        in_specs=[y_spec, pl.BlockSpec((k, tm), lambda i, j: (0, i)), y_spec, w_spec, w_spec, w_spec] + g_specs,
        out_specs=pl.BlockSpec((tm, tn), lambda i, j: (i, j)),
        out_shape=jax.ShapeDtypeStruct((m, n), BF16),
        compiler_params=_params("parallel", "arbitrary"),
        name="gated_mix",
    )(ya, ybt, yc, wa, wb, wc, z, z, z)


def _mlstm_kernel(uq_ref, uk_ref, v_ref, o_ref, gcol_ref, grow_ref, bias_ref, cwq_ref, cwk_ref, ng_ref, shift_ref,
                  y_ref, c_scr, n_scr, m_scr, qtail, ktail):
    c = pl.program_id(1)
    L = uq_ref.shape[1]
    hd = M_HEAD_DIM

    @pl.when(c == 0)
    def _():
        c_scr[...] = jnp.zeros_like(c_scr)
        n_scr[...] = jnp.zeros_like(n_scr)
        m_scr[...] = jnp.zeros_like(m_scr)
        qtail[...] = jnp.zeros_like(qtail)
        ktail[...] = jnp.zeros_like(ktail)

    def conv_silu(u_ref, tail, cw_ref):
        u = u_ref[0]
        cw = cw_ref[...]
        cwb = cw.astype(BF16)
        x = jnp.concatenate([u * cwb[j:j + 1, :] for j in range(M_CONV)], axis=0)
        y = jnp.dot(shift_ref[...], x, preferred_element_type=F32)
        prev = tail[...]
        row = lax.broadcasted_iota(jnp.int32, prev.shape, 0)
        edge = jnp.zeros(prev.shape, F32)
        for t in range(M_CONV - 1):
            acc = cw[0:1, :] * prev[8 - (M_CONV - 1) + t:8 - (M_CONV - 1) + t + 1, :]
            for j in range(1, M_CONV - 1 - t):
                acc = acc + cw[j:j + 1, :] * prev[8 - (M_CONV - 1) + t + j:8 - (M_CONV - 1) + t + j + 1, :]
            edge = jnp.where(row == t, acc, edge)
        tail[...] = u[L - 8:L, :].astype(F32)
        return _silu(jnp.concatenate([y[0:8] + edge, y[8:]], axis=0))

    q_all = conv_silu(uq_ref, qtail, cwq_ref)
    k_all = conv_silu(uk_ref, ktail, cwk_ref) * (hd ** -0.5)
    row = lax.broadcasted_iota(jnp.int32, (L, L), 0)
    col = lax.broadcasted_iota(jnp.int32, (L, L), 1)
    tri = col <= row

    for hh in range(M_HEADS):
        sl = slice(hh * hd, (hh + 1) * hd)
        q, k = q_all[:, sl], k_all[:, sl]
        qb = q.astype(BF16)
        kb = k.astype(BF16)
        vb = v_ref[0, :, sl]

        gcol = gcol_ref[0, hh]
        grow = grow_ref[0, hh]
        bias = bias_ref[hh]
        ic = gcol[:, 0:1] + bias[:, 0:1]
        fc = _log_sigmoid(gcol[:, 1:2] + bias[:, 1:2])
        ir = grow[0:1, :] + bias[:, 0:1]
        fr = _log_sigmoid(grow[1:2, :] + bias[:, 1:2])

        b_col = jnp.sum(jnp.where(tri, fr, 0.0), axis=1, keepdims=True)
        b_row = jnp.sum(jnp.where(row <= col, fc, 0.0), axis=0, keepdims=True)
        m_prev = m_scr[hh, 0:1, 0:1]

        log_d = jnp.where(tri, b_col - b_row + ir, NEG_BIG)
        log_inter = b_col + m_prev
        m_t = jnp.maximum(log_inter, jnp.max(log_d, axis=1, keepdims=True))
        w_intra = jnp.exp(log_d - m_t)
        w_inter = jnp.exp(log_inter - m_t)

        s = lax.dot_general(qb, kb, (((1,), (1,)), ((), ())), preferred_element_type=F32) * w_intra
        c_state = c_scr[hh]
        n_state = n_scr[hh, 0:1, :]
        num = (jnp.dot(s.astype(BF16), vb, preferred_element_type=F32)
               + w_inter * jnp.dot(qb, c_state.astype(BF16), preferred_element_type=F32))
        den = jnp.sum(s, axis=1, keepdims=True) + w_inter * jnp.sum(q * n_state, axis=1, keepdims=True)
        h = num / jnp.maximum(jnp.abs(den), jnp.exp(-m_t))

        b_last = jnp.sum(fc, axis=0, keepdims=True)
        log_g = b_last - b_col + ic
        m_new = jnp.maximum(b_last + m_prev, jnp.max(log_g, axis=0, keepdims=True))
        decay = jnp.exp(b_last + m_prev - m_new)
        wk = k * jnp.exp(log_g - m_new)
        c_scr[hh] = decay * c_state + lax.dot_general(wk.astype(BF16), vb, (((0,), (0,)), ((), ())),
                                                      preferred_element_type=F32)
        n_scr[hh, 0:1, :] = decay * n_state + jnp.sum(wk, axis=0, keepdims=True)
        m_scr[hh] = jnp.broadcast_to(m_new, m_scr.shape[1:])

        hn = _rms_rows(h, ng_ref[hh])
        y_ref[0, :, sl] = (_sigmoid(o_ref[0, :, sl].astype(F32)) * hn).astype(y_ref.dtype)


def _conv_shift_matrix(L):
    t = np.arange(L)[:, None]
    r = np.arange(L)[None, :]
    return jnp.asarray(np.concatenate([(r == t - (M_CONV - 1) + j) for j in range(M_CONV)], axis=1), BF16)


def _mlstm(z3, gcol, grow, gbias, conv_w, norm_g, batch, seq):
    L = min(MLSTM_CHUNK, seq)
    hd = M_HEAD_DIM
    nh = M_HEADS
    w = nh * hd

    def zspec(col0):
        return pl.BlockSpec((1, L, w), functools.partial(lambda b, c, o: (b, c, o), o=col0 // w))

    return pl.pallas_call(
        _mlstm_kernel,
        grid=(batch, seq // L),
        in_specs=[zspec(Z_MQK), zspec(Z_MQK + w), zspec(Z_MV), zspec(Z_MO),
                  pl.BlockSpec((1, nh, L, 2), lambda b, c: (b, 0, c, 0)),
                  pl.BlockSpec((1, nh, 2, L), lambda b, c: (b, 0, 0, c)),
                  pl.BlockSpec((nh, 1, 2), lambda b, c: (0, 0, 0)),
                  pl.BlockSpec((M_CONV, w), lambda b, c: (0, 0)),
                  pl.BlockSpec((M_CONV, w), lambda b, c: (0, 1)),
                  pl.BlockSpec((nh, 1, hd), lambda b, c: (0, 0, 0)),
                  pl.BlockSpec((L, M_CONV * L), lambda b, c: (0, 0))],
        out_specs=pl.BlockSpec((1, L, w), lambda b, c: (b, c, 0)),
        out_shape=jax.ShapeDtypeStruct((batch, seq, w), BF16),
        scratch_shapes=[pltpu.VMEM((nh, hd, hd), F32), pltpu.VMEM((nh, 8, hd), F32), pltpu.VMEM((nh, 8, 128), F32),
                        pltpu.VMEM((8, w), F32), pltpu.VMEM((8, w), F32)],
        compiler_params=_params("parallel", "arbitrary"),
        name="mlstm",
    )(z3, z3, z3, z3, gcol, grow, gbias, conv_w, conv_w, norm_g.reshape(nh, 1, hd), _conv_shift_matrix(L))


def _xattn_kernel(q_ref, k_ref, v_ref, gq_ref, gk_ref, o_ref):
    hd = C_HEAD_DIM
    for h in range(C_HEADS):
        sl = slice(h * hd, (h + 1) * hd)
        q = _rms_rows(q_ref[0, :, sl].astype(F32), gq_ref[...]) * (hd ** -0.5)
        k = _rms_rows(k_ref[0, :, sl].astype(F32), gk_ref[...])
        s = lax.dot_general(q.astype(BF16), k.astype(BF16), (((1,), (1,)), ((), ())), preferred_element_type=F32)
        e = jnp.exp(s - jnp.max(s, axis=-1, keepdims=True))
        p = e / jnp.sum(e, axis=-1, keepdims=True)
        o = jnp.dot(p.astype(BF16), v_ref[0, :, sl], preferred_element_type=F32)
        o_ref[0, :, sl] = o.astype(o_ref.dtype)


def _xattn(z3, k, v, gq, gk, batch, seq, tq=512):
    tq = min(tq, seq)
    mlen, w = k.shape[1], k.shape[2]
    return pl.pallas_call(
        _xattn_kernel,
        grid=(batch, seq // tq),
        in_specs=[pl.BlockSpec((1, tq, w), lambda b, i: (b, i, Z_CQ // w)),
                  pl.BlockSpec((1, mlen, w), lambda b, i: (b, 0, 0)),
                  pl.BlockSpec((1, mlen, w), lambda b, i: (b, 0, 0)),
                  pl.BlockSpec((1, C_HEAD_DIM), lambda b, i: (0, 0)),
                  pl.BlockSpec((1, C_HEAD_DIM), lambda b, i: (0, 0))],
        out_specs=pl.BlockSpec((1, tq, w), lambda b, i: (b, i, 0)),
        out_shape=jax.ShapeDtypeStruct((batch, seq, w), BF16),
        compiler_params=_params("parallel", "arbitrary"),
        name="memory_cross_attention",
    )(z3, k, v, gq.reshape(1, -1), gk.reshape(1, -1))


def _rope_rows(xn, cos, sin_signed):
    half = N_HEAD_DIM // 2
    rot = jnp.concatenate([xn[:, half:], xn[:, :half]], axis=-1)
    return xn * cos + rot * sin_signed


def _knorm_rope_kernel(x_ref, g_ref, cos_ref, sin_ref, seg_ref, swap_ref, o_ref, *, n_blocks):
    hd = N_HEAD_DIM
    ts = x_ref.shape[1]
    x = x_ref[0].astype(F32)
    xx = x * x
    hi = xx.astype(BF16)
    lo = (xx - hi.astype(F32)).astype(BF16)
    ss = (jnp.dot(hi, seg_ref[...], preferred_element_type=F32)
          + jnp.dot(lo, seg_ref[...], preferred_element_type=F32))
    xn = x * lax.rsqrt(ss * (1.0 / hd) + EPS) * g_ref[...]
    rot = jnp.dot(xn.astype(BF16), swap_ref[...], preferred_element_type=F32)
    reps = N_GROUPS * hd // cos_ref.shape[1]
    cos = jnp.concatenate([cos_ref[...]] * reps, axis=1)
    sin = jnp.concatenate([sin_ref[...]] * reps, axis=1)
    k = xn * cos + rot * sin
    if n_blocks:
        k = jnp.clip(k, -F8_MAX, F8_MAX)
    k = k.astype(o_ref.dtype)
    if n_blocks:
        pos = pl.program_id(1) * ts + lax.broadcasted_iota(jnp.int32, (ts, n_blocks), 0)
        blk = lax.broadcasted_iota(jnp.int32, (ts, n_blocks), 1)
        onehot = jnp.where(pos // SEL_BLOCK == blk, F8_MAX, 0.0).astype(o_ref.dtype)
    for g in range(N_GROUPS):
        if n_blocks:
            o_ref[0, g, :, 0:n_blocks] = onehot
        o_ref[0, g, :, n_blocks:n_blocks + hd] = k[:, g * hd:(g + 1) * hd]


def _head_matrices():
    w = N_GROUPS * N_HEAD_DIM
    i = np.arange(w)[:, None]
    j = np.arange(w)[None, :]
    same = (i // N_HEAD_DIM) == (j // N_HEAD_DIM)
    swap = same & ((i % N_HEAD_DIM) == (j % N_HEAD_DIM + N_HEAD_DIM // 2) % N_HEAD_DIM)
    return jnp.asarray(same, BF16), jnp.asarray(swap, BF16)


def _knorm_rope(z3, col0, g, cos2, sin2, n_blocks, ts=1024):
    b, s, _ = z3.shape
    hd, w = N_HEAD_DIM, N_GROUPS * N_HEAD_DIM
    ts = min(ts, s)
    same, swap = _head_matrices()
    const = lambda shape: pl.BlockSpec(shape, lambda i, j: (0, 0))
    return pl.pallas_call(
        functools.partial(_knorm_rope_kernel, n_blocks=n_blocks),
        grid=(b, s // ts),
        in_specs=[pl.BlockSpec((1, ts, w), lambda i, j: (i, j, col0 // w)), const((1, w)),
                  pl.BlockSpec((ts, 2 * hd), lambda i, j: (j, 0)), pl.BlockSpec((ts, 2 * hd), lambda i, j: (j, 0)),
                  const((w, w)), const((w, w))],
        out_specs=pl.BlockSpec((1, N_GROUPS, ts, n_blocks + hd), lambda i, j: (i, 0, j, 0)),
        out_shape=jax.ShapeDtypeStruct((b, N_GROUPS, s, n_blocks + hd), F8 if n_blocks else BF16),
        compiler_params=_params("parallel", "arbitrary"),
        name="key_norm_rope",
    )(z3, jnp.tile(g.reshape(1, hd), (1, N_GROUPS)), cos2, sin2, same, swap)


def _compress_kernel(sub_ref, pe_ref, w1_ref, w2_ref, g_ref, cos_ref, sin_ref, o_ref, *, is_key):
    ns = sub_ref.shape[1]
    sub = sub_ref[0].astype(F32)
    lo = jnp.dot((sub + pe_ref[0:1, :]).astype(BF16), w1_ref[0], preferred_element_type=F32)
    hi = jnp.dot((sub + pe_ref[1:2, :]).astype(BF16), w1_ref[1], preferred_element_type=F32)
    hid = _silu(lo + pltpu.roll(hi, shift=ns - 1, axis=0))
    out = jnp.dot(hid.astype(BF16), w2_ref[...], preferred_element_type=F32)
    if is_key:
        out = _rope_rows(_rms_rows(out, g_ref[...]), cos_ref[...], sin_ref[...])
    o_ref[0] = out.astype(o_ref.dtype)


def _compress(u, pe, w1, w2, g, cos, sin_signed, is_key):
    n, s, hd = u.shape
    ns = s // CMP_STRIDE
    width = CMP_STRIDE * hd
    sub = u.reshape(n, ns, width)
    pe2 = pe.reshape(CMP_LEN // CMP_STRIDE, width)
    w1s = w1.reshape(CMP_LEN // CMP_STRIDE, width, CMP_HIDDEN).astype(BF16)
    return pl.pallas_call(
        functools.partial(_compress_kernel, is_key=is_key),
        grid=(n,),
        in_specs=[pl.BlockSpec((1, ns, width), lambda i: (i, 0, 0)),
                  pl.BlockSpec(pe2.shape, lambda i: (0, 0)),
                  pl.BlockSpec(w1s.shape, lambda i: (0, 0, 0)),
                  pl.BlockSpec(w2.shape, lambda i: (0, 0)),
                  pl.BlockSpec((1, hd), lambda i: (0, 0)),
                  pl.BlockSpec((ns, hd), lambda i: (0, 0)),
                  pl.BlockSpec((ns, hd), lambda i: (0, 0))],
        out_specs=pl.BlockSpec((1, ns, hd), lambda i: (i, 0, 0)),
        out_shape=jax.ShapeDtypeStruct((n, ns, hd), BF16),
        compiler_params=_params("parallel"),
        name="compress_key" if is_key else "compress_value",
    )(sub, pe2, w1s, w2.astype(BF16), g.reshape(1, hd), cos, sin_signed)


def _nsa_kernel(q_ref, gate_ref, kc_ref, vct_ref, ks_ref, vst_ref, kw_ref, vwt_ref, gq_ref, cos_ref, sin_ref,
                wimp_ref, y_ref, s_c, e_c, s_w, e_w, s_a, s_b, e_a, e_b, *, seq):
    tq = q_ref.shape[1]
    hd = N_HEAD_DIM
    nb = seq // SEL_BLOCK
    top = min(SEL_TOP, nb)
    ncol = N_REP * tq
    t0 = pl.program_id(2) * tq

    def lanes(parts):
        return jnp.concatenate(parts, axis=1)

    def head_cols(a, r):
        return a[:, r * tq:(r + 1) * tq]

    def with_ones(v):
        n = v.shape[1]
        tail = jnp.where(lax.broadcasted_iota(jnp.int32, (V_AUG_ROWS, n), 0) == 0, 1.0, 0.0).astype(v.dtype)
        return jnp.concatenate([v, tail], axis=0)

    cb = min(ncol, NSA_CB)

    def chunk_scores(src, c, j, valid_fn):
        sc = src[c * NSA_RC:(c + 1) * NSA_RC, j * cb:(j + 1) * cb]
        if valid_fn is None:
            return sc
        valid = valid_fn(c * NSA_RC + lax.broadcasted_iota(jnp.int32, (NSA_RC, 1), 0))
        return lanes([jnp.where(valid, head_cols(sc, r), NEG_BIG) for r in range(cb // tq)])

    def put_scores(dst, s):
        dst[...] = s
        return jnp.max(s.reshape(s.shape[0] // 8, 8, ncol), axis=0)

    def col_max(src, rows, valid_fn):
        out = []
        for j in range(ncol // cb):
            mx = jnp.full((8, cb), NEG_BIG, F32)
            for c in range(rows // NSA_RC):
                sc = chunk_scores(src, c, j, valid_fn)
                mx = jnp.maximum(mx, jnp.max(sc.reshape(NSA_RC // 8, 8, cb), axis=0))
            out.append(jnp.max(mx, axis=0, keepdims=True))
        return lanes(out)

    def put_weights(src, dst, rows, valid_fn, m):
        for j in range(ncol // cb):
            mj = m[:, j * cb:(j + 1) * cb]
            for c in range(rows // NSA_RC):
                sc = chunk_scores(src, c, j, valid_fn)
                dst[c * NSA_RC:(c + 1) * NSA_RC, j * cb:(j + 1) * cb] = (
                    jnp.exp2((sc - mj).astype(BF16)).astype(dst.dtype))

    q = lanes([q_ref[r * hd:(r + 1) * hd, :].astype(F32) for r in range(N_REP)])
    qn = q * lax.rsqrt(jnp.mean(q * q, axis=0, keepdims=True) + EPS) * gq_ref[...]
    cos = lanes([cos_ref[...]] * N_REP)
    sin = lanes([sin_ref[...]] * N_REP)
    rot = jnp.concatenate([qn[hd // 2:], qn[:hd // 2]], axis=0)
    qr = ((qn * cos + rot * sin) * (hd ** -0.5 * LOG2_E)).astype(BF16)
    t_q = t0 + lax.broadcasted_iota(jnp.int32, (1, tq), 1)
    tcol = lanes([t_q] * N_REP)

    nc = kc_ref.shape[2]
    s_c[...] = jnp.dot(kc_ref[0, 0], qr, preferred_element_type=F32)
    valid_c = lambda n: n * CMP_STRIDE + (CMP_LEN - 1) <= t_q
    put_weights(s_c, e_c, nc, valid_c, col_max(s_c, nc, valid_c))
    both = jnp.dot(jnp.concatenate([vct_ref[0, 0], wimp_ref[...]], axis=0), e_c[...], preferred_element_type=F32)
    inv_c = jnp.where(tcol >= CMP_LEN - 1, 1.0 / both[hd:hd + 1], 0.0)
    o_c = both[0:hd] * inv_c

    imp_h = both[hd + V_AUG_ROWS:] * inv_c
    imp = head_cols(imp_h, 0)
    for r in range(1, N_REP):
        imp = imp + head_cols(imp_h, r)

    span = WINDOW + tq
    ws0 = pl.multiple_of(jnp.maximum(t0 - WINDOW, 0), tq)
    s_w[...] = jnp.dot(kw_ref[0, 0, pl.ds(ws0, span), :], qr, preferred_element_type=F32)
    valid_w = lambda k: (ws0 + k <= t_q) & (ws0 + k > t_q - WINDOW)
    m_w = col_max(s_w, span, valid_w)

    blk = lax.broadcasted_iota(jnp.int32, (nb, tq), 0)
    cur = (t0 + lax.broadcasted_iota(jnp.int32, (nb, tq), 1)) // SEL_BLOCK
    causal_b = blk <= cur
    forced = (blk == 0) | (blk == cur) | (blk == cur - 1)
    score = jnp.where(forced, FORCE_SCORE, jnp.where(causal_b, imp, -1.0))
    taken = -3.0e38

    def pick_by_value(sc):
        for _ in range(top - 2):
            sc = jnp.where(sc == jnp.max(sc, axis=0, keepdims=True), taken, sc)
        return sc

    def pick_by_value_then_index(sc):
        for _ in range(top):
            mx = jnp.max(sc, axis=0, keepdims=True)
            idx = jnp.min(jnp.where(sc == mx, blk, nb), axis=0, keepdims=True)
            sc = jnp.where(blk == idx, taken, sc)
        return sc

    fast = pick_by_value(score)
    n_taken = jnp.sum(jnp.where((fast == taken) & causal_b, 1.0, 0.0), axis=0, keepdims=True)
    n_want = jnp.minimum(cur[0:1, :] + 1, top).astype(F32)
    tied = jnp.max(jnp.abs(n_taken - n_want)) > 0.0
    picked = lax.cond(tied, lambda: pick_by_value_then_index(score), lambda: fast)
    bias = jnp.where((picked == taken) & causal_b, 0.0, BLOCK_BIAS)
    q_aug = jnp.concatenate([lanes([bias.astype(BF16)] * N_REP), jnp.clip(qr, -F8_MAX, F8_MAX)],
                            axis=0).astype(F8)

    tk = min(NSA_TK, seq)
    last = (t0 + tq - 1) // tk
    s_bufs, e_bufs = (s_a, s_b), (e_a, e_b)

    def scores(t, par):
        ks0 = pl.multiple_of(t * tk, tk)
        return put_scores(s_bufs[par], jnp.dot(ks_ref[0, 0, pl.ds(ks0, tk), :], q_aug, preferred_element_type=F32))

    def weights(t, par, m, mx, masked):
        valid_s = (lambda k: t * tk + k <= t_q) if masked else None
        m_new = jnp.maximum(m, jnp.max(mx, axis=0, keepdims=True))
        put_weights(s_bufs[par], e_bufs[par], tk, valid_s, m_new - SEL_WEIGHT_SHIFT)
        return m_new, jnp.exp2(m - m_new)

    def values(t, par, alpha, acc):
        ks0 = pl.multiple_of(jnp.maximum(t, 0) * tk, tk)
        v = jnp.clip(vst_ref[:, pl.ds(ks0, tk)], -F8_MAX, F8_MAX)
        return alpha * acc + jnp.dot(with_ones(v).astype(F8), e_bufs[par][...],
                                     preferred_element_type=F32)

    def step(t, par, carry, masked, more):
        m, alpha, acc, mx = carry
        mx_next = scores(t + 1, 1 - par) if more else mx
        acc = values(t - 1, 1 - par, alpha, acc)
        m, alpha = weights(t, par, m, mx, masked)
        return m, alpha, acc, mx_next

    def pair_body(j, carry):
        return step(2 * j + 1, 1, step(2 * j, 0, carry, False, True), False, True)

    def tail_even(carry):
        m, alpha, acc, _ = step(last, 0, carry, True, False)
        return values(last, 0, alpha, acc)

    def tail_odd(carry):
        m, alpha, acc, _ = step(last, 1, step(last - 1, 0, carry, False, True), True, False)
        return values(last, 1, alpha, acc)

    mx0 = scores(0, 0)
    e_b[...] = jnp.zeros(e_b.shape, e_b.dtype)
    put_weights(s_w, e_w, span, valid_w, m_w)
    o_w = jnp.dot(with_ones(vwt_ref[:, pl.ds(ws0, span)]), e_w[...], preferred_element_type=F32)
    o_w = o_w[0:hd] / o_w[hd:hd + 1]
    carry = (jnp.full((1, ncol), NEG_BIG, F32), jnp.ones((1, ncol), F32), jnp.zeros((hd + V_AUG_ROWS, ncol), F32),
             mx0)
    carry = lax.fori_loop(0, last // 2, pair_body, carry)
    acc = lax.cond(last % 2 == 0, tail_even, tail_odd, carry)
    o_s = acc[0:hd] / acc[hd:hd + 1]

    g = _sigmoid(gate_ref[0, 0])
    gc, gs, gw = (lanes([g[j, r:r + 1, :] for r in range(N_REP)]) for j in range(3))
    out = gc * o_c + gs * o_s + gw * o_w
    for r in range(N_REP):
        y_ref[r * hd:(r + 1) * hd, :] = head_cols(out, r).astype(y_ref.dtype)


def _nsa(zt, gates, kc, vct, ks_aug, kw, gq, cos_t, sin_t, wimp_t, batch, seq):
    g, hd = N_GROUPS, N_HEAD_DIM
    tq = min(NSA_TQ, seq)
    tk = min(NSA_TK, seq)
    nq = seq // tq
    ncol = N_REP * tq
    nc = kc.shape[2]
    nb = seq // SEL_BLOCK
    full = lambda *shape: pl.BlockSpec((1, 1) + shape, lambda bi, gi, qi: (bi, gi) + (0,) * len(shape))
    vrow = lambda row0: pl.BlockSpec((hd, seq), functools.partial(lambda bi, gi, qi, o: (o + gi, bi), o=row0 // hd))
    qspec = pl.BlockSpec((N_REP * hd, tq), lambda bi, gi, qi: (ZT_Q // (N_REP * hd) + gi, bi * nq + qi))
    return pl.pallas_call(
        functools.partial(_nsa_kernel, seq=seq),
        grid=(batch, g, nq),
        in_specs=[qspec,
                  pl.BlockSpec((1, 1, 3, N_REP, tq), lambda bi, gi, qi: (bi, gi, 0, 0, qi)),
                  full(nc, hd), full(hd + V_AUG_ROWS, nc), full(seq, nb + hd), vrow(ZT_VS), full(seq, hd),
                  vrow(ZT_VW),
                  pl.BlockSpec((hd, 1), lambda bi, gi, qi: (0, 0)),
                  pl.BlockSpec((hd, tq), lambda bi, gi, qi: (0, qi)),
                  pl.BlockSpec((hd, tq), lambda bi, gi, qi: (0, qi)),
                  pl.BlockSpec((nb, nc), lambda bi, gi, qi: (0, 0))],
        out_specs=pl.BlockSpec((N_REP * hd, tq), lambda bi, gi, qi: (gi, bi * nq + qi)),
        out_shape=jax.ShapeDtypeStruct((N_HEADS * hd, batch * seq), BF16),
        scratch_shapes=[pltpu.VMEM((nc, ncol), F32), pltpu.VMEM((nc, ncol), BF16),
                        pltpu.VMEM((WINDOW + tq, ncol), F32), pltpu.VMEM((WINDOW + tq, ncol), BF16),
                        pltpu.VMEM((tk, ncol), F32), pltpu.VMEM((tk, ncol), F32),
                        pltpu.VMEM((tk, ncol), F8), pltpu.VMEM((tk, ncol), F8)],
        compiler_params=_params("parallel", "parallel", "arbitrary"),
        name="nsa_attention",
    )(zt, gates, kc, vct, ks_aug, zt, kw, zt, gq.reshape(hd, 1), cos_t, sin_t, wimp_t)


def _rope_tables(pos):
    half = N_HEAD_DIM // 2
    inv = jnp.power(ROPE_THETA, -jnp.arange(half, dtype=F32) * 2.0 / N_HEAD_DIM)
    ang = pos.astype(F32)[:, None] * inv[None, :]
    cos, sin = jnp.cos(ang), jnp.sin(ang)
    return jnp.concatenate([cos, cos], axis=-1), jnp.concatenate([-sin, sin], axis=-1)


def _importance_matrix(nb, nc_pad):
    per = SEL_BLOCK // CMP_STRIDE
    j = np.arange(nb)[:, None]
    n = np.arange(nc_pad)[None, :]
    w = np.zeros((nb, nc_pad), np.float32)
    for d in range(CMP_LEN // CMP_STRIDE):
        w += ((n + d >= per * j) & (n + d <= per * j + per - 1)).astype(np.float32)
    return jnp.asarray(w, BF16)


def _split_w_in(w_in):
    d = w_in.shape[0]
    sizes = (2048, 1024, 1024, 4, 4, 1024, 256, 256, 256, 256, 256, 256, 48, 1024, 2048, 2048, 2048)
    offs = np.concatenate([[0], np.cumsum(sizes)])
    part = lambda i: w_in[:, offs[i]:offs[i + 1]]
    tok = jnp.concatenate([part(i) for i in (0, 1, 2, 13, 14, 15, 16, 6, 7, 8, 10)], axis=1).astype(BF16)
    feat = jnp.concatenate([part(i) for i in (5, 9, 11)], axis=1).T.astype(BF16)
    small = jnp.concatenate([part(3), part(4), part(12), jnp.zeros((d, 128 - 56), w_in.dtype)], axis=1).astype(BF16)
    return tok, feat, small


def _layer(x, mem, layer, big, norm_mix_g, norm_mem_g, norm_ffn_g, w_in, m_conv_w, m_i_bias, m_f_bias, m_norm_g,
           n_q_norm_g, n_kc_norm_g, n_ks_norm_g, n_kw_norm_g, n_cmp_pe_k, n_cmp_w1_k, n_cmp_w2_k,
           n_cmp_pe_v, n_cmp_w1_v, n_cmp_w2_v, c_q_norm_g, c_k_norm_g, w_mem_k, w_mem_v):
    B, S, D = x.shape
    M = B * S
    G, R, hd = N_GROUPS, N_REP, N_HEAD_DIM
    x2 = x.reshape(M, D)

    w_tok, w_feat, w_small = _split_w_in(w_in)
    z, zs, h = _in_proj(x2, norm_mix_g, w_tok, w_small)
    zt = _mm_feature_major(w_feat, h)
    z3 = z.reshape(B, S, Z_WIDTH)
    zs3 = zs.reshape(B, S, 128)

    gif = jnp.stack([zs3[..., 0:4], zs3[..., 4:8]], axis=-1)
    gcol = gif.transpose(0, 2, 1, 3)
    grow = gif.transpose(0, 2, 3, 1)
    gbias = jnp.stack([m_i_bias, m_f_bias], axis=-1).reshape(M_HEADS, 1, 2).astype(F32)
    y_a = _mlstm(z3, gcol, grow, gbias, m_conv_w, m_norm_g, B, S)

    pos = jnp.arange(S, dtype=jnp.int32)
    cos, sin_s = _rope_tables(pos)
    nc_pad = S // CMP_STRIDE
    cmp_end = jnp.arange(nc_pad, dtype=jnp.int32) * CMP_STRIDE + CMP_LEN - 1
    cos_c, sin_c = _rope_tables(cmp_end)
    nb = S // SEL_BLOCK

    def head_major(col0):
        u = z3[..., col0:col0 + G * hd].reshape(B, S, G, hd)
        return u.transpose(0, 2, 1, 3).reshape(B * G, S, hd)

    kc = _compress(head_major(Z_KC), n_cmp_pe_k, n_cmp_w1_k, n_cmp_w2_k, n_kc_norm_g, cos_c, sin_c, True)
    vc = _compress(head_major(Z_VC), n_cmp_pe_v, n_cmp_w1_v, n_cmp_w2_v, n_kc_norm_g, cos_c, sin_c, False)
    vct = vc.reshape(B, G, nc_pad, hd).transpose(0, 1, 3, 2)
    ones_rows = jnp.concatenate([jnp.ones((B, G, 1, nc_pad), BF16), jnp.zeros((B, G, V_AUG_ROWS - 1, nc_pad), BF16)],
                                axis=2)
    vct = jnp.concatenate([vct, ones_rows], axis=2)
    cos2, sin2 = jnp.concatenate([cos, cos], axis=1), jnp.concatenate([sin_s, sin_s], axis=1)
    ks_aug = _knorm_rope(z3, Z_KS, n_ks_norm_g, cos2, sin2, nb)
    kw = _knorm_rope(z3, Z_KW, n_kw_norm_g, cos2, sin2, 0)
    gates = zs3[..., 8:8 + 3 * N_HEADS].reshape(B, S, G, R, 3).transpose(0, 2, 4, 3, 1)
    y_bt = _nsa(zt, gates, kc.reshape(B, G, nc_pad, hd), vct, ks_aug, kw, n_q_norm_g, cos.T, sin_s.T,
                _importance_matrix(nb, nc_pad), B, S)

    mlen = mem.shape[1]
    mem_n = _rmsnorm(mem.reshape(B * mlen, D), norm_mem_g)
    k_mem = _mm(mem_n, w_mem_k.astype(BF16), BF16).reshape(B, mlen, -1)
    v_mem = _mm(mem_n, w_mem_v.astype(BF16), BF16).reshape(B, mlen, -1)
    y_c = _xattn(z3, k_mem, v_mem, c_q_norm_g, c_k_norm_g, B, S)

    mix = _mix(y_a.reshape(M, -1), y_bt, y_c.reshape(M, -1), big["w_up_a"], big["w_up_b"], big["w_up_c"], layer,
               z, Z_GATES)
    x2 = _mm_residual(mix, big["w_out"], layer, x2, tm=512, tn=D)

    act = _swiglu(x2, norm_ffn_g, big["w_ffn_gate"], big["w_ffn_up"], layer)
    x2 = _mm_residual(act, big["w_ffn_down"], layer, x2, tm=1024, tn=512)
    return x2.reshape(B, S, D)


def kernel(x, mem, norm_mix_g, norm_mem_g, norm_ffn_g, w_in, m_conv_w, m_i_bias, m_f_bias, m_norm_g, n_q_norm_g, n_kc_norm_g, n_ks_norm_g, n_kw_norm_g, n_cmp_pe_k, n_cmp_w1_k, n_cmp_w2_k, n_cmp_pe_v, n_cmp_w1_v, n_cmp_w2_v, c_q_norm_g, c_k_norm_g, w_mem_k, w_mem_v, w_up_a, w_up_b, w_up_c, w_out, w_ffn_gate, w_ffn_up, w_ffn_down):
    per_layer = (norm_mix_g, norm_mem_g, norm_ffn_g, w_in, m_conv_w, m_i_bias, m_f_bias, m_norm_g, n_q_norm_g,
                 n_kc_norm_g, n_ks_norm_g, n_kw_norm_g, n_cmp_pe_k, n_cmp_w1_k, n_cmp_w2_k, n_cmp_pe_v, n_cmp_w1_v,
                 n_cmp_w2_v, c_q_norm_g, c_k_norm_g, w_mem_k, w_mem_v)
    big = {name: w.astype(BF16) for name, w in (
        ("w_up_a", w_up_a), ("w_up_b", w_up_b), ("w_up_c", w_up_c), ("w_out", w_out),
        ("w_ffn_gate", w_ffn_gate), ("w_ffn_up", w_ffn_up), ("w_ffn_down", w_ffn_down))}
    for layer in range(w_in.shape[0]):
        x = _layer(x, mem, layer, big, *(p[layer] for p in per_layer))
    return x
```

```python
import functools

import numpy as np
import jax
import jax.numpy as jnp
from jax import lax
from jax.experimental import pallas as pl
from jax.experimental.pallas import tpu as pltpu

F32 = jnp.float32
BF16 = jnp.bfloat16

EPS = 1e-6
ROPE_THETA = 10000.0
M_HEADS = 4
M_HEAD_DIM = 256
M_CONV = 4
N_HEADS = 16
N_GROUPS = 4
N_REP = N_HEADS // N_GROUPS
N_HEAD_DIM = 64
CMP_LEN = 32
CMP_STRIDE = 16
CMP_HIDDEN = 128
SEL_BLOCK = 64
SEL_TOP = 16
WINDOW = 512
FORCE_SCORE = 1.0e4
C_HEADS = 4
C_HEAD_DIM = 256

LOG2_E = 1.4426950408889634
V_AUG_ROWS = 16
NEG_BIG = -1.0e30
F8 = jnp.float8_e4m3fn
F8_MAX = 448.0
BLOCK_BIAS = -F8_MAX
SEL_WEIGHT_SHIFT = 8.0
V7X_VMEM_BYTES = 64 * 1024 * 1024
VMEM_LIMIT = V7X_VMEM_BYTES - 8 * 1024 * 1024

MLSTM_CHUNK = 256
NSA_TQ = 256
NSA_TK = 512
NSA_RC = 32
NSA_CB = 1024

Z_MQK, Z_MV, Z_MO, Z_CQ, Z_GATES, Z_KC, Z_VC, Z_KS, Z_KW, Z_WIDTH = (
    0, 2048, 3072, 4096, 5120, 11264, 11520, 11776, 12032, 12288)
ZT_Q, ZT_VS, ZT_VW, ZT_ROWS = 0, 1024, 1280, 1536


def _params(*sem):
    return pltpu.CompilerParams(dimension_semantics=sem, vmem_limit_bytes=VMEM_LIMIT)


def _sigmoid(x):
    return 1.0 / (1.0 + jnp.exp(-x))


def _silu(x):
    return x * _sigmoid(x)


def _log_sigmoid(x):
    return jnp.minimum(x, 0.0) - jnp.log1p(jnp.exp(-jnp.abs(x)))


def _rms_rows(x, g):
    return x * lax.rsqrt(jnp.mean(x * x, axis=-1, keepdims=True) + EPS) * g


def _rmsnorm_kernel(x_ref, g_ref, o_ref):
    o_ref[...] = _rms_rows(x_ref[...], g_ref[...]).astype(o_ref.dtype)


def _rmsnorm(x2d, g, tm=512):
    m, d = x2d.shape
    tm = min(tm, m)
    return pl.pallas_call(
        _rmsnorm_kernel,
        grid=(m // tm,),
        in_specs=[pl.BlockSpec((tm, d), lambda i: (i, 0)), pl.BlockSpec((1, d), lambda i: (0, 0))],
        out_specs=pl.BlockSpec((tm, d), lambda i: (i, 0)),
        out_shape=jax.ShapeDtypeStruct((m, d), BF16),
        compiler_params=_params("parallel"),
        name="rmsnorm",
    )(x2d, g.reshape(1, d))


def _mm_kernel(a_ref, w_ref, o_ref):
    o_ref[...] = jnp.dot(a_ref[...], w_ref[...], preferred_element_type=F32).astype(o_ref.dtype)


def _mm(a, w, out_dtype, tm=1024, tn=512):
    m, k = a.shape
    n = w.shape[1]
    tm, tn = min(tm, m), min(tn, n)
    return pl.pallas_call(
        _mm_kernel,
        grid=(m // tm, n // tn),
        in_specs=[pl.BlockSpec((tm, k), lambda i, j: (i, 0)), pl.BlockSpec((k, tn), lambda i, j: (0, j))],
        out_specs=pl.BlockSpec((tm, tn), lambda i, j: (i, j)),
        out_shape=jax.ShapeDtypeStruct((m, n), out_dtype),
        compiler_params=_params("parallel", "arbitrary"),
        name="matmul",
    )(a, w)


def _in_proj_kernel(x_ref, g_ref, w_ref, ws_ref, z_ref, zs_ref, h_ref):
    @pl.when(pl.program_id(1) == 0)
    def _():
        h = _rms_rows(x_ref[...], g_ref[...]).astype(BF16)
        h_ref[...] = h
        zs_ref[...] = jnp.dot(h, ws_ref[...], preferred_element_type=F32)

    z_ref[...] = jnp.dot(h_ref[...], w_ref[...], preferred_element_type=F32).astype(z_ref.dtype)


def _in_proj(x2d, g, w, w_small, tm=1024, tn=1024):
    m, d = x2d.shape
    n, ns = w.shape[1], w_small.shape[1]
    tm, tn = min(tm, m), min(tn, n)
    return pl.pallas_call(
        _in_proj_kernel,
        grid=(m // tm, n // tn),
        in_specs=[pl.BlockSpec((tm, d), lambda i, j: (i, 0)), pl.BlockSpec((1, d), lambda i, j: (0, 0)),
                  pl.BlockSpec((d, tn), lambda i, j: (0, j)), pl.BlockSpec((d, ns), lambda i, j: (0, 0))],
        out_specs=[pl.BlockSpec((tm, tn), lambda i, j: (i, j)), pl.BlockSpec((tm, ns), lambda i, j: (i, 0)),
                   pl.BlockSpec((tm, d), lambda i, j: (i, 0))],
        out_shape=[jax.ShapeDtypeStruct((m, n), BF16), jax.ShapeDtypeStruct((m, ns), F32),
                   jax.ShapeDtypeStruct((m, d), BF16)],
        compiler_params=_params("parallel", "arbitrary"),
        name="in_proj",
    )(x2d, g.reshape(1, d), w, w_small)


def _mm_nt_kernel(w_ref, h_ref, o_ref):
    o_ref[...] = lax.dot_general(w_ref[...], h_ref[...], (((1,), (1,)), ((), ())),
                                 preferred_element_type=F32).astype(o_ref.dtype)


def _mm_feature_major(w_t, h, tm=1024, tn=1536):
    n, k = w_t.shape
    m = h.shape[0]
    tm, tn = min(tm, m), min(tn, n)
    return pl.pallas_call(
        _mm_nt_kernel,
        grid=(m // tm, n // tn),
        in_specs=[pl.BlockSpec((tn, k), lambda i, j: (j, 0)), pl.BlockSpec((tm, k), lambda i, j: (i, 0))],
        out_specs=pl.BlockSpec((tn, tm), lambda i, j: (j, i)),
        out_shape=jax.ShapeDtypeStruct((n, m), BF16),
        compiler_params=_params("parallel", "arbitrary"),
        name="in_proj_feature_major",
    )(w_t, h)


def _mm_res_kernel(a_ref, w_ref, x_ref, o_ref):
    o_ref[...] = x_ref[...] + jnp.dot(a_ref[...], w_ref[...], preferred_element_type=F32)


def _layer_block(layer, block, index_map):
    return pl.BlockSpec((None,) + block, lambda *idx: (layer,) + index_map(*idx))


def _mm_residual(a, w, layer, x, tm=512, tn=512):
    m, k = a.shape
    n = w.shape[2]
    tm, tn = min(tm, m), min(tn, n)
    return pl.pallas_call(
        _mm_res_kernel,
        grid=(m // tm, n // tn),
        in_specs=[pl.BlockSpec((tm, k), lambda i, j: (i, 0)), _layer_block(layer, (k, tn), lambda i, j: (0, j)),
                  pl.BlockSpec((tm, tn), lambda i, j: (i, j))],
        out_specs=pl.BlockSpec((tm, tn), lambda i, j: (i, j)),
        out_shape=jax.ShapeDtypeStruct((m, n), F32),
        compiler_params=_params("parallel", "arbitrary"),
        name="matmul_residual",
    )(a, w, x)


def _swiglu_kernel(x_ref, g_ref, wg_ref, wu_ref, o_ref, h_scr):
    @pl.when(pl.program_id(1) == 0)
    def _():
        h_scr[...] = _rms_rows(x_ref[...], g_ref[...]).astype(BF16)

    h = h_scr[...]
    gate = jnp.dot(h, wg_ref[...], preferred_element_type=F32)
    up = jnp.dot(h, wu_ref[...], preferred_element_type=F32)
    o_ref[...] = (_silu(gate) * up).astype(o_ref.dtype)


def _swiglu(x2d, g, wg, wu, layer, tm=1024, tn=512):
    m, d = x2d.shape
    n = wg.shape[2]
    tm, tn = min(tm, m), min(tn, n)
    return pl.pallas_call(
        _swiglu_kernel,
        grid=(m // tm, n // tn),
        in_specs=[pl.BlockSpec((tm, d), lambda i, j: (i, 0)), pl.BlockSpec((1, d), lambda i, j: (0, 0)),
                  _layer_block(layer, (d, tn), lambda i, j: (0, j)), _layer_block(layer, (d, tn), lambda i, j: (0, j))],
        out_specs=pl.BlockSpec((tm, tn), lambda i, j: (i, j)),
        out_shape=jax.ShapeDtypeStruct((m, n), BF16),
        scratch_shapes=[pltpu.VMEM((tm, d), BF16)],
        compiler_params=_params("parallel", "arbitrary"),
        name="swiglu",
    )(x2d, g.reshape(1, d), wg, wu)


def _mix_kernel(ya_ref, ybt_ref, yc_ref, wa_ref, wb_ref, wc_ref, ga_ref, gb_ref, gc_ref, o_ref):
    def gated(g_ref, prod):
        return _sigmoid(g_ref[...].astype(F32)) * prod

    a = jnp.dot(ya_ref[...], wa_ref[...], preferred_element_type=F32)
    b = lax.dot_general(ybt_ref[...], wb_ref[...], (((0,), (0,)), ((), ())), preferred_element_type=F32)
    c = jnp.dot(yc_ref[...], wc_ref[...], preferred_element_type=F32)
    o_ref[...] = (gated(ga_ref, a) + gated(gb_ref, b) + gated(gc_ref, c)).astype(o_ref.dtype)


def _mix(ya, ybt, yc, wa, wb, wc, layer, z, gate_col0, tm=1024, tn=512):
    m, k = ya.shape
    n = wa.shape[2]
    tm, tn = min(tm, m), min(tn, n)
    gofs = [(gate_col0 + b * n) // tn for b in range(3)]
    y_spec = pl.BlockSpec((tm, k), lambda i, j: (i, 0))
    w_spec = _layer_block(layer, (k, tn), lambda i, j: (0, j))
    g_specs = [pl.BlockSpec((tm, tn), functools.partial(lambda i, j, o: (i, o + j), o=o)) for o in gofs]
    return pl.pallas_call(
        _mix_kernel,
        grid=(m // tm, n // tn),
        in_specs=[y_spec, pl.BlockSpec((k, tm), lambda i, j: (0, i)), y_spec, w_spec, w_spec, w_spec] + g_specs,
        out_specs=pl.BlockSpec((tm, tn), lambda i, j: (i, j)),
        out_shape=jax.ShapeDtypeStruct((m, n), BF16),
        compiler_params=_params("parallel", "arbitrary"),
        name="gated_mix",
    )(ya, ybt, yc, wa, wb, wc, z, z, z)


def _mlstm_kernel(uq_ref, uk_ref, v_ref, o_ref, gcol_ref, grow_ref, bias_ref, cwq_ref, cwk_ref, ng_ref, shift_ref,
                  y_ref, c_scr, n_scr, m_scr, qtail, ktail):
    c = pl.program_id(1)
    L = uq_ref.shape[1]
    hd = M_HEAD_DIM

    @pl.when(c == 0)
    def _():
        c_scr[...] = jnp.zeros_like(c_scr)
        n_scr[...] = jnp.zeros_like(n_scr)
        m_scr[...] = jnp.zeros_like(m_scr)
        qtail[...] = jnp.zeros_like(qtail)
        ktail[...] = jnp.zeros_like(ktail)

    def conv_silu(u_ref, tail, cw_ref):
        u = u_ref[0]
        cw = cw_ref[...]
        cwb = cw.astype(BF16)
        x = jnp.concatenate([u * cwb[j:j + 1, :] for j in range(M_CONV)], axis=0)
        y = jnp.dot(shift_ref[...], x, preferred_element_type=F32)
        prev = tail[...]
        row = lax.broadcasted_iota(jnp.int32, prev.shape, 0)
        edge = jnp.zeros(prev.shape, F32)
        for t in range(M_CONV - 1):
            acc = cw[0:1, :] * prev[8 - (M_CONV - 1) + t:8 - (M_CONV - 1) + t + 1, :]
            for j in range(1, M_CONV - 1 - t):
                acc = acc + cw[j:j + 1, :] * prev[8 - (M_CONV - 1) + t + j:8 - (M_CONV - 1) + t + j + 1, :]
            edge = jnp.where(row == t, acc, edge)
        tail[...] = u[L - 8:L, :].astype(F32)
        return _silu(jnp.concatenate([y[0:8] + edge, y[8:]], axis=0))

    q_all = conv_silu(uq_ref, qtail, cwq_ref)
    k_all = conv_silu(uk_ref, ktail, cwk_ref) * (hd ** -0.5)
    row = lax.broadcasted_iota(jnp.int32, (L, L), 0)
    col = lax.broadcasted_iota(jnp.int32, (L, L), 1)
    tri = col <= row
    triu = jnp.where(row <= col, 1.0, 0.0).astype(BF16)

    for hh in range(M_HEADS):
        sl = slice(hh * hd, (hh + 1) * hd)
        q, k = q_all[:, sl], k_all[:, sl]
        qb = q.astype(BF16)
        kb = k.astype(BF16)
        vb = v_ref[0, :, sl]

        gcol = gcol_ref[0, hh]
        grow = grow_ref[0, hh]
        bias = bias_ref[hh]
        ic = gcol[:, 0:1] + bias[:, 0:1]
        ir = grow[0:1, :] + bias[:, 0:1]
        fr = _log_sigmoid(grow[1:2, :] + bias[:, 1:2])

        b_col = jnp.sum(jnp.where(tri, fr, 0.0), axis=1, keepdims=True)
        f1 = fr.astype(BF16)
        r1 = fr - f1.astype(F32)
        f2 = r1.astype(BF16)
        f3 = (r1 - f2.astype(F32)).astype(BF16)
        terms = jnp.concatenate([f1, f2, f3, jnp.zeros((13, L), BF16)], axis=0)
        pre = jnp.dot(terms, triu, preferred_element_type=F32)
        b_row = pre[0:1] + pre[1:2] + pre[2:3]
        m_prev = m_scr[hh, 0:1, 0:1]

        log_d = jnp.where(tri, b_col - b_row + ir, NEG_BIG)
        log_inter = b_col + m_prev
        m_t = jnp.maximum(log_inter, jnp.max(log_d, axis=1, keepdims=True))
        w_intra = jnp.exp(log_d - m_t)
        w_inter = jnp.exp(log_inter - m_t)

        s = lax.dot_general(qb, kb, (((1,), (1,)), ((), ())), preferred_element_type=F32) * w_intra
        c_state = c_scr[hh]
        n_state = n_scr[hh, 0:1, :]
        num = (jnp.dot(s.astype(BF16), vb, preferred_element_type=F32)
               + w_inter * jnp.dot(qb, c_state.astype(BF16), preferred_element_type=F32))
        den = jnp.sum(s, axis=1, keepdims=True) + w_inter * jnp.sum(q * n_state, axis=1, keepdims=True)
        h = num / jnp.maximum(jnp.abs(den), jnp.exp(-m_t))

        b_last = b_row[:, L - 1:L]
        log_g = b_last - b_col + ic
        m_new = jnp.maximum(b_last + m_prev, jnp.max(log_g, axis=0, keepdims=True))
        decay = jnp.exp(b_last + m_prev - m_new)
        wk = k * jnp.exp(log_g - m_new)
        c_scr[hh] = decay * c_state + lax.dot_general(wk.astype(BF16), vb, (((0,), (0,)), ((), ())),
                                                      preferred_element_type=F32)
        n_scr[hh, 0:1, :] = decay * n_state + jnp.sum(wk, axis=0, keepdims=True)
        m_scr[hh] = jnp.broadcast_to(m_new, m_scr.shape[1:])

        hn = _rms_rows(h, ng_ref[hh])
        y_ref[0, :, sl] = (_sigmoid(o_ref[0, :, sl].astype(F32)) * hn).astype(y_ref.dtype)


def _conv_shift_matrix(L):
    t = np.arange(L)[:, None]
    r = np.arange(L)[None, :]
    return jnp.asarray(np.concatenate([(r == t - (M_CONV - 1) + j) for j in range(M_CONV)], axis=1), BF16)


def _mlstm(z3, gcol, grow, gbias, conv_w, norm_g, batch, seq):
    L = min(MLSTM_CHUNK, seq)
    hd = M_HEAD_DIM
    nh = M_HEADS
    w = nh * hd

    def zspec(col0):
        return pl.BlockSpec((1, L, w), functools.partial(lambda b, c, o: (b, c, o), o=col0 // w))

    return pl.pallas_call(
        _mlstm_kernel,
        grid=(batch, seq // L),
        in_specs=[zspec(Z_MQK), zspec(Z_MQK + w), zspec(Z_MV), zspec(Z_MO),
                  pl.BlockSpec((1, nh, L, 2), lambda b, c: (b, 0, c, 0)),
                  pl.BlockSpec((1, nh, 2, L), lambda b, c: (b, 0, 0, c)),
                  pl.BlockSpec((nh, 1, 2), lambda b, c: (0, 0, 0)),
                  pl.BlockSpec((M_CONV, w), lambda b, c: (0, 0)),
                  pl.BlockSpec((M_CONV, w), lambda b, c: (0, 1)),
                  pl.BlockSpec((nh, 1, hd), lambda b, c: (0, 0, 0)),
                  pl.BlockSpec((L, M_CONV * L), lambda b, c: (0, 0))],
        out_specs=pl.BlockSpec((1, L, w), lambda b, c: (b, c, 0)),
        out_shape=jax.ShapeDtypeStruct((batch, seq, w), BF16),
        scratch_shapes=[pltpu.VMEM((nh, hd, hd), F32), pltpu.VMEM((nh, 8, hd), F32), pltpu.VMEM((nh, 8, 128), F32),
                        pltpu.VMEM((8, w), F32), pltpu.VMEM((8, w), F32)],
        compiler_params=_params("parallel", "arbitrary"),
        name="mlstm",
    )(z3, z3, z3, z3, gcol, grow, gbias, conv_w, conv_w, norm_g.reshape(nh, 1, hd), _conv_shift_matrix(L))


def _xattn_kernel(q_ref, k_ref, v_ref, gq_ref, gk_ref, o_ref):
    hd = C_HEAD_DIM
    for h in range(C_HEADS):
        sl = slice(h * hd, (h + 1) * hd)
        q = _rms_rows(q_ref[0, :, sl].astype(F32), gq_ref[...]) * (hd ** -0.5)
        k = _rms_rows(k_ref[0, :, sl].astype(F32), gk_ref[...])
        s = lax.dot_general(q.astype(BF16), k.astype(BF16), (((1,), (1,)), ((), ())), preferred_element_type=F32)
        e = jnp.exp(s - jnp.max(s, axis=-1, keepdims=True))
        p = e / jnp.sum(e, axis=-1, keepdims=True)
        o = jnp.dot(p.astype(BF16), v_ref[0, :, sl], preferred_element_type=F32)
        o_ref[0, :, sl] = o.astype(o_ref.dtype)


def _xattn(z3, k, v, gq, gk, batch, seq, tq=512):
    tq = min(tq, seq)
    mlen, w = k.shape[1], k.shape[2]
    return pl.pallas_call(
        _xattn_kernel,
        grid=(batch, seq // tq),
        in_specs=[pl.BlockSpec((1, tq, w), lambda b, i: (b, i, Z_CQ // w)),
                  pl.BlockSpec((1, mlen, w), lambda b, i: (b, 0, 0)),
                  pl.BlockSpec((1, mlen, w), lambda b, i: (b, 0, 0)),
                  pl.BlockSpec((1, C_HEAD_DIM), lambda b, i: (0, 0)),
                  pl.BlockSpec((1, C_HEAD_DIM), lambda b, i: (0, 0))],
        out_specs=pl.BlockSpec((1, tq, w), lambda b, i: (b, i, 0)),
        out_shape=jax.ShapeDtypeStruct((batch, seq, w), BF16),
        compiler_params=_params("parallel", "arbitrary"),
        name="memory_cross_attention",
    )(z3, k, v, gq.reshape(1, -1), gk.reshape(1, -1))


def _rope_rows(xn, cos, sin_signed):
    half = N_HEAD_DIM // 2
    rot = jnp.concatenate([xn[:, half:], xn[:, :half]], axis=-1)
    return xn * cos + rot * sin_signed


def _knorm_rope_kernel(x_ref, g_ref, cos_ref, sin_ref, seg_ref, swap_ref, o_ref, *, n_blocks):
    hd = N_HEAD_DIM
    ts = x_ref.shape[1]
    x = x_ref[0].astype(F32)
    xx = x * x
    hi = xx.astype(BF16)
    lo = (xx - hi.astype(F32)).astype(BF16)
    ss = (jnp.dot(hi, seg_ref[...], preferred_element_type=F32)
          + jnp.dot(lo, seg_ref[...], preferred_element_type=F32))
    xn = x * lax.rsqrt(ss * (1.0 / hd) + EPS) * g_ref[...]
    rot = jnp.dot(xn.astype(BF16), swap_ref[...], preferred_element_type=F32)
    reps = N_GROUPS * hd // cos_ref.shape[1]
    cos = jnp.concatenate([cos_ref[...]] * reps, axis=1)
    sin = jnp.concatenate([sin_ref[...]] * reps, axis=1)
    k = xn * cos + rot * sin
    if n_blocks:
        k = jnp.clip(k, -F8_MAX, F8_MAX)
    k = k.astype(o_ref.dtype)
    if n_blocks:
        pos = pl.program_id(1) * ts + lax.broadcasted_iota(jnp.int32, (ts, n_blocks), 0)
        blk = lax.broadcasted_iota(jnp.int32, (ts, n_blocks), 1)
        onehot = jnp.where(pos // SEL_BLOCK == blk, F8_MAX, 0.0).astype(o_ref.dtype)
    for g in range(N_GROUPS):
        if n_blocks:
            o_ref[0, g, :, 0:n_blocks] = onehot
        o_ref[0, g, :, n_blocks:n_blocks + hd] = k[:, g * hd:(g + 1) * hd]


def _head_matrices():
    w = N_GROUPS * N_HEAD_DIM
    i = np.arange(w)[:, None]
    j = np.arange(w)[None, :]
    same = (i // N_HEAD_DIM) == (j // N_HEAD_DIM)
    swap = same & ((i % N_HEAD_DIM) == (j % N_HEAD_DIM + N_HEAD_DIM // 2) % N_HEAD_DIM)
    return jnp.asarray(same, BF16), jnp.asarray(swap, BF16)


def _knorm_rope(z3, col0, g, cos2, sin2, n_blocks, ts=1024):
    b, s, _ = z3.shape
    hd, w = N_HEAD_DIM, N_GROUPS * N_HEAD_DIM
    ts = min(ts, s)
    same, swap = _head_matrices()
    const = lambda shape: pl.BlockSpec(shape, lambda i, j: (0, 0))
    return pl.pallas_call(
        functools.partial(_knorm_rope_kernel, n_blocks=n_blocks),
        grid=(b, s // ts),
        in_specs=[pl.BlockSpec((1, ts, w), lambda i, j: (i, j, col0 // w)), const((1, w)),
                  pl.BlockSpec((ts, 2 * hd), lambda i, j: (j, 0)), pl.BlockSpec((ts, 2 * hd), lambda i, j: (j, 0)),
                  const((w, w)), const((w, w))],
        out_specs=pl.BlockSpec((1, N_GROUPS, ts, n_blocks + hd), lambda i, j: (i, 0, j, 0)),
        out_shape=jax.ShapeDtypeStruct((b, N_GROUPS, s, n_blocks + hd), F8 if n_blocks else BF16),
        compiler_params=_params("parallel", "arbitrary"),
        name="key_norm_rope",
    )(z3, jnp.tile(g.reshape(1, hd), (1, N_GROUPS)), cos2, sin2, same, swap)


def _compress_kernel(sub_ref, pe_ref, w1_ref, w2_ref, g_ref, cos_ref, sin_ref, o_ref, *, is_key):
    ns = sub_ref.shape[1]
    sub = sub_ref[0].astype(F32)
    lo = jnp.dot((sub + pe_ref[0:1, :]).astype(BF16), w1_ref[0], preferred_element_type=F32)
    hi = jnp.dot((sub + pe_ref[1:2, :]).astype(BF16), w1_ref[1], preferred_element_type=F32)
    hid = _silu(lo + pltpu.roll(hi, shift=ns - 1, axis=0))
    out = jnp.dot(hid.astype(BF16), w2_ref[...], preferred_element_type=F32)
    if is_key:
        out = _rope_rows(_rms_rows(out, g_ref[...]), cos_ref[...], sin_ref[...])
    o_ref[0] = out.astype(o_ref.dtype)


def _compress(u, pe, w1, w2, g, cos, sin_signed, is_key):
    n, s, hd = u.shape
    ns = s // CMP_STRIDE
    width = CMP_STRIDE * hd
    sub = u.reshape(n, ns, width)
    pe2 = pe.reshape(CMP_LEN // CMP_STRIDE, width)
    w1s = w1.reshape(CMP_LEN // CMP_STRIDE, width, CMP_HIDDEN).astype(BF16)
    return pl.pallas_call(
        functools.partial(_compress_kernel, is_key=is_key),
        grid=(n,),
        in_specs=[pl.BlockSpec((1, ns, width), lambda i: (i, 0, 0)),
                  pl.BlockSpec(pe2.shape, lambda i: (0, 0)),
                  pl.BlockSpec(w1s.shape, lambda i: (0, 0, 0)),
                  pl.BlockSpec(w2.shape, lambda i: (0, 0)),
                  pl.BlockSpec((1, hd), lambda i: (0, 0)),
                  pl.BlockSpec((ns, hd), lambda i: (0, 0)),
                  pl.BlockSpec((ns, hd), lambda i: (0, 0))],
        out_specs=pl.BlockSpec((1, ns, hd), lambda i: (i, 0, 0)),
        out_shape=jax.ShapeDtypeStruct((n, ns, hd), BF16),
        compiler_params=_params("parallel"),
        name="compress_key" if is_key else "compress_value",
    )(sub, pe2, w1s, w2.astype(BF16), g.reshape(1, hd), cos, sin_signed)


def _nsa_kernel(q_ref, gate_ref, kc_ref, vct_ref, ks_ref, vst_ref, kw_ref, vwt_ref, gq_ref, cos_ref, sin_ref,
                wimp_ref, y_ref, s_c, e_c, s_w, e_w, s_a, s_b, e_a, e_b, *, seq):
    tq = q_ref.shape[1]
    hd = N_HEAD_DIM
    nb = seq // SEL_BLOCK
    top = min(SEL_TOP, nb)
    ncol = N_REP * tq
    t0 = pl.program_id(2) * tq

    def lanes(parts):
        return jnp.concatenate(parts, axis=1)

    def head_cols(a, r):
        return a[:, r * tq:(r + 1) * tq]

    def with_ones(v):
        n = v.shape[1]
        tail = jnp.where(lax.broadcasted_iota(jnp.int32, (V_AUG_ROWS, n), 0) == 0, 1.0, 0.0).astype(v.dtype)
        return jnp.concatenate([v, tail], axis=0)

    cb = min(ncol, NSA_CB)

    def chunk_scores(src, c, j, valid_fn):
        sc = src[c * NSA_RC:(c + 1) * NSA_RC, j * cb:(j + 1) * cb]
        if valid_fn is None:
            return sc
        valid = valid_fn(c * NSA_RC + lax.broadcasted_iota(jnp.int32, (NSA_RC, 1), 0))
        return lanes([jnp.where(valid, head_cols(sc, r), NEG_BIG) for r in range(cb // tq)])

    def put_scores(dst, s):
        dst[...] = s
        return jnp.max(s.reshape(s.shape[0] // 8, 8, ncol), axis=0)

    def col_max(src, rows, valid_fn):
        out = []
        for j in range(ncol // cb):
            mx = jnp.full((8, cb), NEG_BIG, F32)
            for c in range(rows // NSA_RC):
                sc = chunk_scores(src, c, j, valid_fn)
                mx = jnp.maximum(mx, jnp.max(sc.reshape(NSA_RC // 8, 8, cb), axis=0))
            out.append(jnp.max(mx, axis=0, keepdims=True))
        return lanes(out)

    def put_weights(src, dst, rows, valid_fn, m):
        for j in range(ncol // cb):
            mj = m[:, j * cb:(j + 1) * cb]
            for c in range(rows // NSA_RC):
                sc = chunk_scores(src, c, j, valid_fn)
                dst[c * NSA_RC:(c + 1) * NSA_RC, j * cb:(j + 1) * cb] = (
                    jnp.exp2((sc - mj).astype(BF16)).astype(dst.dtype))

    q = lanes([q_ref[r * hd:(r + 1) * hd, :].astype(F32) for r in range(N_REP)])
    qn = q * lax.rsqrt(jnp.mean(q * q, axis=0, keepdims=True) + EPS) * gq_ref[...]
    cos = lanes([cos_ref[...]] * N_REP)
    sin = lanes([sin_ref[...]] * N_REP)
    rot = jnp.concatenate([qn[hd // 2:], qn[:hd // 2]], axis=0)
    qr = ((qn * cos + rot * sin) * (hd ** -0.5 * LOG2_E)).astype(BF16)
    t_q = t0 + lax.broadcasted_iota(jnp.int32, (1, tq), 1)
    tcol = lanes([t_q] * N_REP)

    nc = kc_ref.shape[2]
    s_c[...] = jnp.dot(kc_ref[0, 0], qr, preferred_element_type=F32)
    valid_c = lambda n: n * CMP_STRIDE + (CMP_LEN - 1) <= t_q
    put_weights(s_c, e_c, nc, valid_c, col_max(s_c, nc, valid_c))
    both = jnp.dot(jnp.concatenate([vct_ref[0, 0], wimp_ref[...]], axis=0), e_c[...], preferred_element_type=F32)
    inv_c = jnp.where(tcol >= CMP_LEN - 1, 1.0 / both[hd:hd + 1], 0.0)
    o_c = both[0:hd] * inv_c

    imp_h = both[hd + V_AUG_ROWS:] * inv_c
    imp = head_cols(imp_h, 0)
    for r in range(1, N_REP):
        imp = imp + head_cols(imp_h, r)

    span = WINDOW + tq
    ws0 = pl.multiple_of(jnp.maximum(t0 - WINDOW, 0), tq)
    s_w[...] = jnp.dot(kw_ref[0, 0, pl.ds(ws0, span), :], qr, preferred_element_type=F32)
    valid_w = lambda k: (ws0 + k <= t_q) & (ws0 + k > t_q - WINDOW)
    m_w = col_max(s_w, span, valid_w)

    blk = lax.broadcasted_iota(jnp.int32, (nb, tq), 0)
    cur = (t0 + lax.broadcasted_iota(jnp.int32, (nb, tq), 1)) // SEL_BLOCK
    causal_b = blk <= cur
    forced = (blk == 0) | (blk == cur) | (blk == cur - 1)
    score = jnp.where(forced, FORCE_SCORE, jnp.where(causal_b, imp, -1.0))
    taken = -3.0e38

    def pick_by_value(sc):
        for _ in range(top - 2):
            sc = jnp.where(sc == jnp.max(sc, axis=0, keepdims=True), taken, sc)
        return sc

    def pick_by_value_then_index(sc):
        for _ in range(top):
            mx = jnp.max(sc, axis=0, keepdims=True)
            idx = jnp.min(jnp.where(sc == mx, blk, nb), axis=0, keepdims=True)
            sc = jnp.where(blk == idx, taken, sc)
        return sc

    fast = pick_by_value(score)
    n_taken = jnp.sum(jnp.where((fast == taken) & causal_b, 1.0, 0.0), axis=0, keepdims=True)
    n_want = jnp.minimum(cur[0:1, :] + 1, top).astype(F32)
    tied = jnp.max(jnp.abs(n_taken - n_want)) > 0.0
    picked = lax.cond(tied, lambda: pick_by_value_then_index(score), lambda: fast)
    bias = jnp.where((picked == taken) & causal_b, 0.0, BLOCK_BIAS)
    q_aug = jnp.concatenate([lanes([bias.astype(BF16)] * N_REP), jnp.clip(qr, -F8_MAX, F8_MAX)],
                            axis=0).astype(F8)

    tk = min(NSA_TK, seq)
    last = (t0 + tq - 1) // tk
    s_bufs, e_bufs = (s_a, s_b), (e_a, e_b)

    def scores(t, par):
        ks0 = pl.multiple_of(t * tk, tk)
        return put_scores(s_bufs[par], jnp.dot(ks_ref[0, 0, pl.ds(ks0, tk), :], q_aug, preferred_element_type=F32))

    def weights(t, par, m, mx, masked):
        valid_s = (lambda k: t * tk + k <= t_q) if masked else None
        m_new = jnp.maximum(m, jnp.max(mx, axis=0, keepdims=True))
        put_weights(s_bufs[par], e_bufs[par], tk, valid_s, m_new - SEL_WEIGHT_SHIFT)
        return m_new, jnp.exp2(m - m_new)

    def values(t, par, alpha, acc):
        ks0 = pl.multiple_of(jnp.maximum(t, 0) * tk, tk)
        v = jnp.clip(vst_ref[:, pl.ds(ks0, tk)], -F8_MAX, F8_MAX)
        return alpha * acc + jnp.dot(with_ones(v).astype(F8), e_bufs[par][...],
                                     preferred_element_type=F32)

    def step(t, par, carry, masked, more):
        m, alpha, acc, mx = carry
        mx_next = scores(t + 1, 1 - par) if more else mx
        acc = values(t - 1, 1 - par, alpha, acc)
        m, alpha = weights(t, par, m, mx, masked)
        return m, alpha, acc, mx_next

    def pair_body(j, carry):
        return step(2 * j + 1, 1, step(2 * j, 0, carry, False, True), False, True)

    def tail_even(carry):
        m, alpha, acc, _ = step(last, 0, carry, True, False)
        return values(last, 0, alpha, acc)

    def tail_odd(carry):
        m, alpha, acc, _ = step(last, 1, step(last - 1, 0, carry, False, True), True, False)
        return values(last, 1, alpha, acc)

    mx0 = scores(0, 0)
    e_b[...] = jnp.zeros(e_b.shape, e_b.dtype)
    put_weights(s_w, e_w, span, valid_w, m_w)
    o_w = jnp.dot(with_ones(vwt_ref[:, pl.ds(ws0, span)]), e_w[...], preferred_element_type=F32)
    o_w = o_w[0:hd] / o_w[hd:hd + 1]
    carry = (jnp.full((1, ncol), NEG_BIG, F32), jnp.ones((1, ncol), F32), jnp.zeros((hd + V_AUG_ROWS, ncol), F32),
             mx0)
    carry = lax.fori_loop(0, last // 2, pair_body, carry)
    acc = lax.cond(last % 2 == 0, tail_even, tail_odd, carry)
    o_s = acc[0:hd] / acc[hd:hd + 1]

    g = _sigmoid(gate_ref[0, 0])
    gc, gs, gw = (lanes([g[j, r:r + 1, :] for r in range(N_REP)]) for j in range(3))
    out = gc * o_c + gs * o_s + gw * o_w
    for r in range(N_REP):
        y_ref[r * hd:(r + 1) * hd, :] = head_cols(out, r).astype(y_ref.dtype)


def _nsa(zt, gates, kc, vct, ks_aug, kw, gq, cos_t, sin_t, wimp_t, batch, seq):
    g, hd = N_GROUPS, N_HEAD_DIM
    tq = min(NSA_TQ, seq)
    tk = min(NSA_TK, seq)
    nq = seq // tq
    ncol = N_REP * tq
    nc = kc.shape[2]
    nb = seq // SEL_BLOCK
    full = lambda *shape: pl.BlockSpec((1, 1) + shape, lambda bi, gi, qi: (bi, gi) + (0,) * len(shape))
    vrow = lambda row0: pl.BlockSpec((hd, seq), functools.partial(lambda bi, gi, qi, o: (o + gi, bi), o=row0 // hd))
    qspec = pl.BlockSpec((N_REP * hd, tq), lambda bi, gi, qi: (ZT_Q // (N_REP * hd) + gi, bi * nq + qi))
    return pl.pallas_call(
        functools.partial(_nsa_kernel, seq=seq),
        grid=(batch, g, nq),
        in_specs=[qspec,
                  pl.BlockSpec((1, 1, 3, N_REP, tq), lambda bi, gi, qi: (bi, gi, 0, 0, qi)),
                  full(nc, hd), full(hd + V_AUG_ROWS, nc), full(seq, nb + hd), vrow(ZT_VS), full(seq, hd),
                  vrow(ZT_VW),
                  pl.BlockSpec((hd, 1), lambda bi, gi, qi: (0, 0)),
                  pl.BlockSpec((hd, tq), lambda bi, gi, qi: (0, qi)),
                  pl.BlockSpec((hd, tq), lambda bi, gi, qi: (0, qi)),
                  pl.BlockSpec((nb, nc), lambda bi, gi, qi: (0, 0))],
        out_specs=pl.BlockSpec((N_REP * hd, tq), lambda bi, gi, qi: (gi, bi * nq + qi)),
        out_shape=jax.ShapeDtypeStruct((N_HEADS * hd, batch * seq), BF16),
        scratch_shapes=[pltpu.VMEM((nc, ncol), F32), pltpu.VMEM((nc, ncol), BF16),
                        pltpu.VMEM((WINDOW + tq, ncol), F32), pltpu.VMEM((WINDOW + tq, ncol), BF16),
                        pltpu.VMEM((tk, ncol), F32), pltpu.VMEM((tk, ncol), F32),
                        pltpu.VMEM((tk, ncol), F8), pltpu.VMEM((tk, ncol), F8)],
        compiler_params=_params("parallel", "parallel", "arbitrary"),
        name="nsa_attention",
    )(zt, gates, kc, vct, ks_aug, zt, kw, zt, gq.reshape(hd, 1), cos_t, sin_t, wimp_t)


def _rope_tables(pos):
    half = N_HEAD_DIM // 2
    inv = jnp.power(ROPE_THETA, -jnp.arange(half, dtype=F32) * 2.0 / N_HEAD_DIM)
    ang = pos.astype(F32)[:, None] * inv[None, :]
    cos, sin = jnp.cos(ang), jnp.sin(ang)
    return jnp.concatenate([cos, cos], axis=-1), jnp.concatenate([-sin, sin], axis=-1)


def _importance_matrix(nb, nc_pad):
    per = SEL_BLOCK // CMP_STRIDE
    j = np.arange(nb)[:, None]
    n = np.arange(nc_pad)[None, :]
    w = np.zeros((nb, nc_pad), np.float32)
    for d in range(CMP_LEN // CMP_STRIDE):
        w += ((n + d >= per * j) & (n + d <= per * j + per - 1)).astype(np.float32)
    return jnp.asarray(w, BF16)


def _split_w_in(w_in):
    d = w_in.shape[0]
    sizes = (2048, 1024, 1024, 4, 4, 1024, 256, 256, 256, 256, 256, 256, 48, 1024, 2048, 2048, 2048)
    offs = np.concatenate([[0], np.cumsum(sizes)])
    part = lambda i: w_in[:, offs[i]:offs[i + 1]]
    tok = jnp.concatenate([part(i) for i in (0, 1, 2, 13, 14, 15, 16, 6, 7, 8, 10)], axis=1).astype(BF16)
    feat = jnp.concatenate([part(i) for i in (5, 9, 11)], axis=1).T.astype(BF16)
    small = jnp.concatenate([part(3), part(4), part(12), jnp.zeros((d, 128 - 56), w_in.dtype)], axis=1).astype(BF16)
    return tok, feat, small


def _layer(x, mem, layer, big, norm_mix_g, norm_mem_g, norm_ffn_g, w_in, m_conv_w, m_i_bias, m_f_bias, m_norm_g,
           n_q_norm_g, n_kc_norm_g, n_ks_norm_g, n_kw_norm_g, n_cmp_pe_k, n_cmp_w1_k, n_cmp_w2_k,
           n_cmp_pe_v, n_cmp_w1_v, n_cmp_w2_v, c_q_norm_g, c_k_norm_g, w_mem_k, w_mem_v):
    B, S, D = x.shape
    M = B * S
    G, R, hd = N_GROUPS, N_REP, N_HEAD_DIM
    x2 = x.reshape(M, D)

    w_tok, w_feat, w_small = _split_w_in(w_in)
    z, zs, h = _in_proj(x2, norm_mix_g, w_tok, w_small)
    zt = _mm_feature_major(w_feat, h)
    z3 = z.reshape(B, S, Z_WIDTH)
    zs3 = zs.reshape(B, S, 128)

    gif = jnp.stack([zs3[..., 0:4], zs3[..., 4:8]], axis=-1)
    gcol = gif.transpose(0, 2, 1, 3)
    grow = gif.transpose(0, 2, 3, 1)
    gbias = jnp.stack([m_i_bias, m_f_bias], axis=-1).reshape(M_HEADS, 1, 2).astype(F32)
    y_a = _mlstm(z3, gcol, grow, gbias, m_conv_w, m_norm_g, B, S)

    pos = jnp.arange(S, dtype=jnp.int32)
    cos, sin_s = _rope_tables(pos)
    nc_pad = S // CMP_STRIDE
    cmp_end = jnp.arange(nc_pad, dtype=jnp.int32) * CMP_STRIDE + CMP_LEN - 1
    cos_c, sin_c = _rope_tables(cmp_end)
    nb = S // SEL_BLOCK

    def head_major(col0):
        u = z3[..., col0:col0 + G * hd].reshape(B, S, G, hd)
        return u.transpose(0, 2, 1, 3).reshape(B * G, S, hd)

    kc = _compress(head_major(Z_KC), n_cmp_pe_k, n_cmp_w1_k, n_cmp_w2_k, n_kc_norm_g, cos_c, sin_c, True)
    vc = _compress(head_major(Z_VC), n_cmp_pe_v, n_cmp_w1_v, n_cmp_w2_v, n_kc_norm_g, cos_c, sin_c, False)
    vct = vc.reshape(B, G, nc_pad, hd).transpose(0, 1, 3, 2)
    ones_rows = jnp.concatenate([jnp.ones((B, G, 1, nc_pad), BF16), jnp.zeros((B, G, V_AUG_ROWS - 1, nc_pad), BF16)],
                                axis=2)
    vct = jnp.concatenate([vct, ones_rows], axis=2)
    cos2, sin2 = jnp.concatenate([cos, cos], axis=1), jnp.concatenate([sin_s, sin_s], axis=1)
    ks_aug = _knorm_rope(z3, Z_KS, n_ks_norm_g, cos2, sin2, nb)
    kw = _knorm_rope(z3, Z_KW, n_kw_norm_g, cos2, sin2, 0)
    gates = zs3[..., 8:8 + 3 * N_HEADS].reshape(B, S, G, R, 3).transpose(0, 2, 4, 3, 1)
    y_bt = _nsa(zt, gates, kc.reshape(B, G, nc_pad, hd), vct, ks_aug, kw, n_q_norm_g, cos.T, sin_s.T,
                _importance_matrix(nb, nc_pad), B, S)

    mlen = mem.shape[1]
    mem_n = _rmsnorm(mem.reshape(B * mlen, D), norm_mem_g)
    k_mem = _mm(mem_n, w_mem_k.astype(BF16), BF16).reshape(B, mlen, -1)
    v_mem = _mm(mem_n, w_mem_v.astype(BF16), BF16).reshape(B, mlen, -1)
    y_c = _xattn(z3, k_mem, v_mem, c_q_norm_g, c_k_norm_g, B, S)

    mix = _mix(y_a.reshape(M, -1), y_bt, y_c.reshape(M, -1), big["w_up_a"], big["w_up_b"], big["w_up_c"], layer,
               z, Z_GATES)
    x2 = _mm_residual(mix, big["w_out"], layer, x2, tm=512, tn=D)

    act = _swiglu(x2, norm_ffn_g, big["w_ffn_gate"], big["w_ffn_up"], layer)
    x2 = _mm_residual(act, big["w_ffn_down"], layer, x2, tm=1024, tn=512)
    return x2.reshape(B, S, D)


def kernel(x, mem, norm_mix_g, norm_mem_g, norm_ffn_g, w_in, m_conv_w, m_i_bias, m_f_bias, m_norm_g, n_q_norm_g, n_kc_norm_g, n_ks_norm_g, n_kw_norm_g, n_cmp_pe_k, n_cmp_w1_k, n_cmp_w2_k, n_cmp_pe_v, n_cmp_w1_v, n_cmp_w2_v, c_q_norm_g, c_k_norm_g, w_mem_k, w_mem_v, w_up_a, w_up_b, w_up_c, w_out, w_ffn_gate, w_ffn_up, w_ffn_down):
    per_layer = (norm_mix_g, norm_mem_g, norm_ffn_g, w_in, m_conv_w, m_i_bias, m_f_bias, m_norm_g, n_q_norm_g,
                 n_kc_norm_g, n_ks_norm_g, n_kw_norm_g, n_cmp_pe_k, n_cmp_w1_k, n_cmp_w2_k, n_cmp_pe_v, n_cmp_w1_v,
                 n_cmp_w2_v, c_q_norm_g, c_k_norm_g, w_mem_k, w_mem_v)
    big = {name: w.astype(BF16) for name, w in (
        ("w_up_a", w_up_a), ("w_up_b", w_up_b), ("w_up_c", w_up_c), ("w_out", w_out),
        ("w_ffn_gate", w_ffn_gate), ("w_ffn_up", w_ffn_up), ("w_ffn_down", w_ffn_down))}
    for layer in range(w_in.shape[0]):
        x = _layer(x, mem, layer, big, *(p[layer] for p in per_layer))
    return x
```

```python
import functools

import numpy as np
import jax
import jax.numpy as jnp
from jax import lax
from jax.experimental import pallas as pl
from jax.experimental.pallas import tpu as pltpu

F32 = jnp.float32
BF16 = jnp.bfloat16

EPS = 1e-6
ROPE_THETA = 10000.0
M_HEADS = 4
M_HEAD_DIM = 256
M_CONV = 4
N_HEADS = 16
N_GROUPS = 4
N_REP = N_HEADS // N_GROUPS
N_HEAD_DIM = 64
CMP_LEN = 32
CMP_STRIDE = 16
CMP_HIDDEN = 128
SEL_BLOCK = 64
SEL_TOP = 16
WINDOW = 512
FORCE_SCORE = 1.0e4
C_HEADS = 4
C_HEAD_DIM = 256

LOG2_E = 1.4426950408889634
V_AUG_ROWS = 16
NEG_BIG = -1.0e30
F8 = jnp.float8_e4m3fn
F8_MAX = 448.0
BLOCK_BIAS = -F8_MAX
SEL_WEIGHT_SHIFT = 8.0
V7X_VMEM_BYTES = 64 * 1024 * 1024
VMEM_LIMIT = V7X_VMEM_BYTES - 8 * 1024 * 1024

MLSTM_CHUNK = 256
MLSTM_BATCH_ROWS = 2
NSA_TQ = 256
NSA_TK = 512
NSA_GROUPS = 2
NSA_RC = 32
NSA_CB = 1024

Z_MQK, Z_MV, Z_MO, Z_CQ, Z_GATES, Z_KC, Z_VC, Z_KS, Z_KW, Z_WIDTH = (
    0, 2048, 3072, 4096, 5120, 11264, 11520, 11776, 12032, 12288)
ZT_Q, ZT_VS, ZT_VW, ZT_ROWS = 0, 1024, 1280, 1536


def _params(*sem):
    return pltpu.CompilerParams(dimension_semantics=sem, vmem_limit_bytes=VMEM_LIMIT)


def _sigmoid(x):
    return 1.0 / (1.0 + jnp.exp(-x))


def _silu(x):
    return x * _sigmoid(x)


def _log_sigmoid(x):
    return jnp.minimum(x, 0.0) - jnp.log1p(jnp.exp(-jnp.abs(x)))


def _rms_rows(x, g):
    return x * lax.rsqrt(jnp.mean(x * x, axis=-1, keepdims=True) + EPS) * g


def _rmsnorm_kernel(x_ref, g_ref, o_ref):
    o_ref[...] = _rms_rows(x_ref[...], g_ref[...]).astype(o_ref.dtype)


def _rmsnorm(x2d, g, tm=512):
    m, d = x2d.shape
    tm = min(tm, m)
    return pl.pallas_call(
        _rmsnorm_kernel,
        grid=(m // tm,),
        in_specs=[pl.BlockSpec((tm, d), lambda i: (i, 0)), pl.BlockSpec((1, d), lambda i: (0, 0))],
        out_specs=pl.BlockSpec((tm, d), lambda i: (i, 0)),
        out_shape=jax.ShapeDtypeStruct((m, d), BF16),
        compiler_params=_params("parallel"),
        name="rmsnorm",
    )(x2d, g.reshape(1, d))


def _mm_kernel(a_ref, w_ref, o_ref):
    o_ref[...] = jnp.dot(a_ref[...], w_ref[...], preferred_element_type=F32).astype(o_ref.dtype)


def _mm(a, w, out_dtype, tm=1024, tn=512):
    m, k = a.shape
    n = w.shape[1]
    tm, tn = min(tm, m), min(tn, n)
    return pl.pallas_call(
        _mm_kernel,
        grid=(m // tm, n // tn),
        in_specs=[pl.BlockSpec((tm, k), lambda i, j: (i, 0)), pl.BlockSpec((k, tn), lambda i, j: (0, j))],
        out_specs=pl.BlockSpec((tm, tn), lambda i, j: (i, j)),
        out_shape=jax.ShapeDtypeStruct((m, n), out_dtype),
        compiler_params=_params("parallel", "arbitrary"),
        name="matmul",
    )(a, w)


def _in_proj_kernel(x_ref, g_ref, w_ref, ws_ref, z_ref, zs_ref, h_ref):
    @pl.when(pl.program_id(1) == 0)
    def _():
        h = _rms_rows(x_ref[...], g_ref[...]).astype(BF16)
        h_ref[...] = h
        zs_ref[...] = jnp.dot(h, ws_ref[...], preferred_element_type=F32)

    z_ref[...] = jnp.dot(h_ref[...], w_ref[...], preferred_element_type=F32).astype(z_ref.dtype)


def _in_proj(x2d, g, w, w_small, tm=1024, tn=1024):
    m, d = x2d.shape
    n, ns = w.shape[1], w_small.shape[1]
    tm, tn = min(tm, m), min(tn, n)
    return pl.pallas_call(
        _in_proj_kernel,
        grid=(m // tm, n // tn),
        in_specs=[pl.BlockSpec((tm, d), lambda i, j: (i, 0)), pl.BlockSpec((1, d), lambda i, j: (0, 0)),
                  pl.BlockSpec((d, tn), lambda i, j: (0, j)), pl.BlockSpec((d, ns), lambda i, j: (0, 0))],
        out_specs=[pl.BlockSpec((tm, tn), lambda i, j: (i, j)), pl.BlockSpec((tm, ns), lambda i, j: (i, 0)),
                   pl.BlockSpec((tm, d), lambda i, j: (i, 0))],
        out_shape=[jax.ShapeDtypeStruct((m, n), BF16), jax.ShapeDtypeStruct((m, ns), F32),
                   jax.ShapeDtypeStruct((m, d), BF16)],
        compiler_params=_params("parallel", "arbitrary"),
        name="in_proj",
    )(x2d, g.reshape(1, d), w, w_small)


def _mm_nt_kernel(w_ref, h_ref, o_ref):
    o_ref[...] = lax.dot_general(w_ref[...], h_ref[...], (((1,), (1,)), ((), ())),
                                 preferred_element_type=F32).astype(o_ref.dtype)


def _mm_feature_major(w_t, h, tm=1024, tn=1536):
    n, k = w_t.shape
    m = h.shape[0]
    tm, tn = min(tm, m), min(tn, n)
    return pl.pallas_call(
        _mm_nt_kernel,
        grid=(m // tm, n // tn),
        in_specs=[pl.BlockSpec((tn, k), lambda i, j: (j, 0)), pl.BlockSpec((tm, k), lambda i, j: (i, 0))],
        out_specs=pl.BlockSpec((tn, tm), lambda i, j: (j, i)),
        out_shape=jax.ShapeDtypeStruct((n, m), BF16),
        compiler_params=_params("parallel", "arbitrary"),
        name="in_proj_feature_major",
    )(w_t, h)


def _mm_res_kernel(a_ref, w_ref, x_ref, o_ref):
    o_ref[...] = x_ref[...] + jnp.dot(a_ref[...], w_ref[...], preferred_element_type=F32)


def _layer_block(layer, block, index_map):
    return pl.BlockSpec((None,) + block, lambda *idx: (layer,) + index_map(*idx))


def _mm_residual(a, w, layer, x, tm=512, tn=512):
    m, k = a.shape
    n = w.shape[2]
    tm, tn = min(tm, m), min(tn, n)
    return pl.pallas_call(
        _mm_res_kernel,
        grid=(m // tm, n // tn),
        in_specs=[pl.BlockSpec((tm, k), lambda i, j: (i, 0)), _layer_block(layer, (k, tn), lambda i, j: (0, j)),
                  pl.BlockSpec((tm, tn), lambda i, j: (i, j))],
        out_specs=pl.BlockSpec((tm, tn), lambda i, j: (i, j)),
        out_shape=jax.ShapeDtypeStruct((m, n), F32),
        compiler_params=_params("parallel", "arbitrary"),
        name="matmul_residual",
    )(a, w, x)


def _swiglu_kernel(x_ref, g_ref, wg_ref, wu_ref, o_ref, h_scr):
    @pl.when(pl.program_id(1) == 0)
    def _():
        h_scr[...] = _rms_rows(x_ref[...], g_ref[...]).astype(BF16)

    h = h_scr[...]
    gate = jnp.dot(h, wg_ref[...], preferred_element_type=F32)
    up = jnp.dot(h, wu_ref[...], preferred_element_type=F32)
    o_ref[...] = (_silu(gate) * up).astype(o_ref.dtype)


def _swiglu(x2d, g, wg, wu, layer, tm=1024, tn=512):
    m, d = x2d.shape
    n = wg.shape[2]
    tm, tn = min(tm, m), min(tn, n)
    return pl.pallas_call(
        _swiglu_kernel,
        grid=(m // tm, n // tn),
        in_specs=[pl.BlockSpec((tm, d), lambda i, j: (i, 0)), pl.BlockSpec((1, d), lambda i, j: (0, 0)),
                  _layer_block(layer, (d, tn), lambda i, j: (0, j)), _layer_block(layer, (d, tn), lambda i, j: (0, j))],
        out_specs=pl.BlockSpec((tm, tn), lambda i, j: (i, j)),
        out_shape=jax.ShapeDtypeStruct((m, n), BF16),
        scratch_shapes=[pltpu.VMEM((tm, d), BF16)],
        compiler_params=_params("parallel", "arbitrary"),
        name="swiglu",
    )(x2d, g.reshape(1, d), wg, wu)


def _mix_kernel(ya_ref, ybt_ref, yc_ref, wa_ref, wb_ref, wc_ref, ga_ref, gb_ref, gc_ref, o_ref):
    def gated(g_ref, prod):
        return _sigmoid(g_ref[...].astype(F32)) * prod

    a = jnp.dot(ya_ref[...], wa_ref[...], preferred_element_type=F32)
    b = lax.dot_general(ybt_ref[...], wb_ref[...], (((0,), (0,)), ((), ())), preferred_element_type=F32)
    c = jnp.dot(yc_ref[...], wc_ref[...], preferred_element_type=F32)
    o_ref[...] = (gated(ga_ref, a) + gated(gb_ref, b) + gated(gc_ref, c)).astype(o_ref.dtype)


def _mix(ya, ybt, yc, wa, wb, wc, layer, z, gate_col0, tm=1024, tn=512):
    m, k = ya.shape
    n = wa.shape[2]
    tm, tn = min(tm, m), min(tn, n)
    gofs = [(gate_col0 + b * n) // tn for b in range(3)]
    y_spec = pl.BlockSpec((tm, k), lambda i, j: (i, 0))
    w_spec = _layer_block(layer, (k, tn), lambda i, j: (0, j))
    g_specs = [pl.BlockSpec((tm, tn), functools.partial(lambda i, j, o: (i, o + j), o=o)) for o in gofs]
    return pl.pallas_call(
        _mix_kernel,
        grid=(m // tm, n // tn),
        in_specs=[y_spec, pl.BlockSpec((k, tm), lambda i, j: (0, i)), y_spec, w_spec, w_spec, w_spec] + g_specs,
        out_specs=pl.BlockSpec((tm, tn), lambda i, j: (i, j)),
        out_shape=jax.ShapeDtypeStruct((m, n), BF16),
        compiler_params=_params("parallel", "arbitrary"),
        name="gated_mix",
    )(ya, ybt, yc, wa, wb, wc, z, z, z)


def _mlstm_kernel(uq_ref, uk_ref, v_ref, o_ref, gcol_ref, grow_ref, bias_ref, cwq_ref, cwk_ref, ng_ref, shift_ref,
                  y_ref, c_scr, n_scr, m_scr, qtail, ktail):
    c = pl.program_id(1)
    nbp = uq_ref.shape[0]
    L = uq_ref.shape[1]
    hd = M_HEAD_DIM

    @pl.when(c == 0)
    def _():
        c_scr[...] = jnp.zeros_like(c_scr)
        n_scr[...] = jnp.zeros_like(n_scr)
        m_scr[...] = jnp.zeros_like(m_scr)
        qtail[...] = jnp.zeros_like(qtail)
        ktail[...] = jnp.zeros_like(ktail)

    def conv_silu(u_ref, bb, tail_all, cw_ref):
        tail = tail_all.at[bb * 8:(bb + 1) * 8, :]
        u = u_ref[bb]
        cw = cw_ref[...]
        cwb = cw.astype(BF16)
        x = jnp.concatenate([u * cwb[j:j + 1, :] for j in range(M_CONV)], axis=0)
        y = jnp.dot(shift_ref[...], x, preferred_element_type=F32)
        prev = tail[...]
        row = lax.broadcasted_iota(jnp.int32, prev.shape, 0)
        edge = jnp.zeros(prev.shape, F32)
        for t in range(M_CONV - 1):
            acc = cw[0:1, :] * prev[8 - (M_CONV - 1) + t:8 - (M_CONV - 1) + t + 1, :]
            for j in range(1, M_CONV - 1 - t):
                acc = acc + cw[j:j + 1, :] * prev[8 - (M_CONV - 1) + t + j:8 - (M_CONV - 1) + t + j + 1, :]
            edge = jnp.where(row == t, acc, edge)
        tail[...] = u[L - 8:L, :].astype(F32)
        return _silu(jnp.concatenate([y[0:8] + edge, y[8:]], axis=0))

    q_alls = [conv_silu(uq_ref, bb, qtail, cwq_ref) for bb in range(nbp)]
    k_alls = [conv_silu(uk_ref, bb, ktail, cwk_ref) * (hd ** -0.5) for bb in range(nbp)]
    row = lax.broadcasted_iota(jnp.int32, (L, L), 0)
    col = lax.broadcasted_iota(jnp.int32, (L, L), 1)
    tri = col <= row
    triu = jnp.where(row <= col, 1.0, 0.0).astype(BF16)

    for sh in range(nbp * M_HEADS):
        bb, hh = divmod(sh, M_HEADS)
        sl = slice(hh * hd, (hh + 1) * hd)
        q, k = q_alls[bb][:, sl], k_alls[bb][:, sl]
        qb = q.astype(BF16)
        kb = k.astype(BF16)
        vb = v_ref[bb, :, sl]

        gcol = gcol_ref[bb, hh]
        grow = grow_ref[bb, hh]
        bias = bias_ref[hh]
        ic = gcol[:, 0:1] + bias[:, 0:1]
        ir = grow[0:1, :] + bias[:, 0:1]
        fr = _log_sigmoid(grow[1:2, :] + bias[:, 1:2])

        b_col = jnp.sum(jnp.where(tri, fr, 0.0), axis=1, keepdims=True)
        f1 = fr.astype(BF16)
        r1 = fr - f1.astype(F32)
        f2 = r1.astype(BF16)
        f3 = (r1 - f2.astype(F32)).astype(BF16)
        terms = jnp.concatenate([f1, f2, f3, jnp.zeros((13, L), BF16)], axis=0)
        pre = jnp.dot(terms, triu, preferred_element_type=F32)
        b_row = pre[0:1] + pre[1:2] + pre[2:3]
        m_prev = m_scr[sh, 0:1, 0:1]

        log_d = jnp.where(tri, b_col - b_row + ir, NEG_BIG)
        log_inter = b_col + m_prev
        m_t = jnp.maximum(log_inter, jnp.max(log_d, axis=1, keepdims=True))
        w_intra = jnp.exp(log_d - m_t)
        w_inter = jnp.exp(log_inter - m_t)

        s = lax.dot_general(qb, kb, (((1,), (1,)), ((), ())), preferred_element_type=F32) * w_intra
        c_state = c_scr[sh]
        n_state = n_scr[sh, 0:1, :]
        num = (jnp.dot(s.astype(BF16), vb, preferred_element_type=F32)
               + w_inter * jnp.dot(qb, c_state.astype(BF16), preferred_element_type=F32))
        den = jnp.sum(s, axis=1, keepdims=True) + w_inter * jnp.sum(q * n_state, axis=1, keepdims=True)
        h = num / jnp.maximum(jnp.abs(den), jnp.exp(-m_t))

        b_last = b_row[:, L - 1:L]
        log_g = b_last - b_col + ic
        m_new = jnp.maximum(b_last + m_prev, jnp.max(log_g, axis=0, keepdims=True))
        decay = jnp.exp(b_last + m_prev - m_new)
        wk = k * jnp.exp(log_g - m_new)
        c_scr[sh] = decay * c_state + lax.dot_general(wk.astype(BF16), vb, (((0,), (0,)), ((), ())),
                                                      preferred_element_type=F32)
        n_scr[sh, 0:1, :] = decay * n_state + jnp.sum(wk, axis=0, keepdims=True)
        m_scr[sh] = jnp.broadcast_to(m_new, m_scr.shape[1:])

        hn = _rms_rows(h, ng_ref[hh])
        y_ref[bb, :, sl] = (_sigmoid(o_ref[bb, :, sl].astype(F32)) * hn).astype(y_ref.dtype)


def _conv_shift_matrix(L):
    t = np.arange(L)[:, None]
    r = np.arange(L)[None, :]
    return jnp.asarray(np.concatenate([(r == t - (M_CONV - 1) + j) for j in range(M_CONV)], axis=1), BF16)


def _mlstm(z3, gcol, grow, gbias, conv_w, norm_g, batch, seq):
    L = min(MLSTM_CHUNK, seq)
    hd = M_HEAD_DIM
    nh = M_HEADS
    w = nh * hd

    nbp = MLSTM_BATCH_ROWS if batch % MLSTM_BATCH_ROWS == 0 else 1

    def zspec(col0):
        return pl.BlockSpec((nbp, L, w), functools.partial(lambda b, c, o: (b, c, o), o=col0 // w))

    return pl.pallas_call(
        _mlstm_kernel,
        grid=(batch // nbp, seq // L),
        in_specs=[zspec(Z_MQK), zspec(Z_MQK + w), zspec(Z_MV), zspec(Z_MO),
                  pl.BlockSpec((nbp, nh, L, 2), lambda b, c: (b, 0, c, 0)),
                  pl.BlockSpec((nbp, nh, 2, L), lambda b, c: (b, 0, 0, c)),
                  pl.BlockSpec((nh, 1, 2), lambda b, c: (0, 0, 0)),
                  pl.BlockSpec((M_CONV, w), lambda b, c: (0, 0)),
                  pl.BlockSpec((M_CONV, w), lambda b, c: (0, 1)),
                  pl.BlockSpec((nh, 1, hd), lambda b, c: (0, 0, 0)),
                  pl.BlockSpec((L, M_CONV * L), lambda b, c: (0, 0))],
        out_specs=pl.BlockSpec((nbp, L, w), lambda b, c: (b, c, 0)),
        out_shape=jax.ShapeDtypeStruct((batch, seq, w), BF16),
        scratch_shapes=[pltpu.VMEM((nbp * nh, hd, hd), F32), pltpu.VMEM((nbp * nh, 8, hd), F32),
                        pltpu.VMEM((nbp * nh, 8, 128), F32),
                        pltpu.VMEM((nbp * 8, w), F32), pltpu.VMEM((nbp * 8, w), F32)],
        compiler_params=_params("parallel", "arbitrary"),
        name="mlstm",
    )(z3, z3, z3, z3, gcol, grow, gbias, conv_w, conv_w, norm_g.reshape(nh, 1, hd), _conv_shift_matrix(L))


def _xattn_kernel(q_ref, k_ref, v_ref, gq_ref, gk_ref, o_ref):
    hd = C_HEAD_DIM
    for h in range(C_HEADS):
        sl = slice(h * hd, (h + 1) * hd)
        q = _rms_rows(q_ref[0, :, sl].astype(F32), gq_ref[...]) * (hd ** -0.5)
        k = _rms_rows(k_ref[0, :, sl].astype(F32), gk_ref[...])
        s = lax.dot_general(q.astype(BF16), k.astype(BF16), (((1,), (1,)), ((), ())), preferred_element_type=F32)
        e = jnp.exp(s - jnp.max(s, axis=-1, keepdims=True))
        p = e / jnp.sum(e, axis=-1, keepdims=True)
        o = jnp.dot(p.astype(BF16), v_ref[0, :, sl], preferred_element_type=F32)
        o_ref[0, :, sl] = o.astype(o_ref.dtype)


def _xattn(z3, k, v, gq, gk, batch, seq, tq=512):
    tq = min(tq, seq)
    mlen, w = k.shape[1], k.shape[2]
    return pl.pallas_call(
        _xattn_kernel,
        grid=(batch, seq // tq),
        in_specs=[pl.BlockSpec((1, tq, w), lambda b, i: (b, i, Z_CQ // w)),
                  pl.BlockSpec((1, mlen, w), lambda b, i: (b, 0, 0)),
                  pl.BlockSpec((1, mlen, w), lambda b, i: (b, 0, 0)),
                  pl.BlockSpec((1, C_HEAD_DIM), lambda b, i: (0, 0)),
                  pl.BlockSpec((1, C_HEAD_DIM), lambda b, i: (0, 0))],
        out_specs=pl.BlockSpec((1, tq, w), lambda b, i: (b, i, 0)),
        out_shape=jax.ShapeDtypeStruct((batch, seq, w), BF16),
        compiler_params=_params("parallel", "arbitrary"),
        name="memory_cross_attention",
    )(z3, k, v, gq.reshape(1, -1), gk.reshape(1, -1))


def _rope_rows(xn, cos, sin_signed):
    half = N_HEAD_DIM // 2
    rot = jnp.concatenate([xn[:, half:], xn[:, :half]], axis=-1)
    return xn * cos + rot * sin_signed


def _knorm_rope_kernel(x_ref, g_ref, cos_ref, sin_ref, seg_ref, swap_ref, o_ref, *, n_blocks):
    hd = N_HEAD_DIM
    ts = x_ref.shape[1]
    x = x_ref[0].astype(F32)
    xx = x * x
    hi = xx.astype(BF16)
    lo = (xx - hi.astype(F32)).astype(BF16)
    ss = (jnp.dot(hi, seg_ref[...], preferred_element_type=F32)
          + jnp.dot(lo, seg_ref[...], preferred_element_type=F32))
    xn = x * lax.rsqrt(ss * (1.0 / hd) + EPS) * g_ref[...]
    rot = jnp.dot(xn.astype(BF16), swap_ref[...], preferred_element_type=F32)
    reps = N_GROUPS * hd // cos_ref.shape[1]
    cos = jnp.concatenate([cos_ref[...]] * reps, axis=1)
    sin = jnp.concatenate([sin_ref[...]] * reps, axis=1)
    k = xn * cos + rot * sin
    if n_blocks:
        k = jnp.clip(k, -F8_MAX, F8_MAX)
    k = k.astype(o_ref.dtype)
    if n_blocks:
        pos = pl.program_id(1) * ts + lax.broadcasted_iota(jnp.int32, (ts, n_blocks), 0)
        blk = lax.broadcasted_iota(jnp.int32, (ts, n_blocks), 1)
        onehot = jnp.where(pos // SEL_BLOCK == blk, F8_MAX, 0.0).astype(o_ref.dtype)
    for g in range(N_GROUPS):
        if n_blocks:
            o_ref[0, g, :, 0:n_blocks] = onehot
        o_ref[0, g, :, n_blocks:n_blocks + hd] = k[:, g * hd:(g + 1) * hd]


def _head_matrices():
    w = N_GROUPS * N_HEAD_DIM
    i = np.arange(w)[:, None]
    j = np.arange(w)[None, :]
    same = (i // N_HEAD_DIM) == (j // N_HEAD_DIM)
    swap = same & ((i % N_HEAD_DIM) == (j % N_HEAD_DIM + N_HEAD_DIM // 2) % N_HEAD_DIM)
    return jnp.asarray(same, BF16), jnp.asarray(swap, BF16)


def _knorm_rope(z3, col0, g, cos2, sin2, n_blocks, ts=1024):
    b, s, _ = z3.shape
    hd, w = N_HEAD_DIM, N_GROUPS * N_HEAD_DIM
    ts = min(ts, s)
    same, swap = _head_matrices()
    const = lambda shape: pl.BlockSpec(shape, lambda i, j: (0, 0))
    return pl.pallas_call(
        functools.partial(_knorm_rope_kernel, n_blocks=n_blocks),
        grid=(b, s // ts),
        in_specs=[pl.BlockSpec((1, ts, w), lambda i, j: (i, j, col0 // w)), const((1, w)),
                  pl.BlockSpec((ts, 2 * hd), lambda i, j: (j, 0)), pl.BlockSpec((ts, 2 * hd), lambda i, j: (j, 0)),
                  const((w, w)), const((w, w))],
        out_specs=pl.BlockSpec((1, N_GROUPS, ts, n_blocks + hd), lambda i, j: (i, 0, j, 0)),
        out_shape=jax.ShapeDtypeStruct((b, N_GROUPS, s, n_blocks + hd), F8 if n_blocks else BF16),
        compiler_params=_params("parallel", "arbitrary"),
        name="key_norm_rope",
    )(z3, jnp.tile(g.reshape(1, hd), (1, N_GROUPS)), cos2, sin2, same, swap)


def _compress_kernel(sub_ref, pe_ref, w1_ref, w2_ref, g_ref, cos_ref, sin_ref, o_ref, *, is_key):
    ns = sub_ref.shape[1]
    sub = sub_ref[0].astype(F32)
    lo = jnp.dot((sub + pe_ref[0:1, :]).astype(BF16), w1_ref[0], preferred_element_type=F32)
    hi = jnp.dot((sub + pe_ref[1:2, :]).astype(BF16), w1_ref[1], preferred_element_type=F32)
    hid = _silu(lo + pltpu.roll(hi, shift=ns - 1, axis=0))
    out = jnp.dot(hid.astype(BF16), w2_ref[...], preferred_element_type=F32)
    if is_key:
        out = _rope_rows(_rms_rows(out, g_ref[...]), cos_ref[...], sin_ref[...])
    o_ref[0] = out.astype(o_ref.dtype)


def _compress(u, pe, w1, w2, g, cos, sin_signed, is_key):
    n, s, hd = u.shape
    ns = s // CMP_STRIDE
    width = CMP_STRIDE * hd
    sub = u.reshape(n, ns, width)
    pe2 = pe.reshape(CMP_LEN // CMP_STRIDE, width)
    w1s = w1.reshape(CMP_LEN // CMP_STRIDE, width, CMP_HIDDEN).astype(BF16)
    return pl.pallas_call(
        functools.partial(_compress_kernel, is_key=is_key),
        grid=(n,),
        in_specs=[pl.BlockSpec((1, ns, width), lambda i: (i, 0, 0)),
                  pl.BlockSpec(pe2.shape, lambda i: (0, 0)),
                  pl.BlockSpec(w1s.shape, lambda i: (0, 0, 0)),
                  pl.BlockSpec(w2.shape, lambda i: (0, 0)),
                  pl.BlockSpec((1, hd), lambda i: (0, 0)),
                  pl.BlockSpec((ns, hd), lambda i: (0, 0)),
                  pl.BlockSpec((ns, hd), lambda i: (0, 0))],
        out_specs=pl.BlockSpec((1, ns, hd), lambda i: (i, 0, 0)),
        out_shape=jax.ShapeDtypeStruct((n, ns, hd), BF16),
        compiler_params=_params("parallel"),
        name="compress_key" if is_key else "compress_value",
    )(sub, pe2, w1s, w2.astype(BF16), g.reshape(1, hd), cos, sin_signed)


def _nsa_kernel(q_ref, gate_ref, kc_ref, vct_ref, ks_ref, vst_ref, kw_ref, vwt_ref, gq_ref, cos_ref, sin_ref,
                wimp_ref, y_ref, s_c, e_c, s_w, e_w, s_a, s_b, e_a, e_b, *, seq):
    tq = q_ref.shape[1]
    gp = kc_ref.shape[1]
    hd = N_HEAD_DIM
    nb = seq // SEL_BLOCK
    top = min(SEL_TOP, nb)
    ncol = N_REP * tq
    t0 = pl.program_id(2) * tq

    def lanes(parts):
        return jnp.concatenate(parts, axis=1)

    def head_cols(a, r):
        return a[:, r * tq:(r + 1) * tq]

    def with_ones(v):
        n = v.shape[1]
        tail = jnp.where(lax.broadcasted_iota(jnp.int32, (V_AUG_ROWS, n), 0) == 0, 1.0, 0.0).astype(v.dtype)
        return jnp.concatenate([v, tail], axis=0)

    cb = min(ncol, NSA_CB)

    def chunk_scores(src, c, j, valid_fn):
        sc = src[c * NSA_RC:(c + 1) * NSA_RC, j * cb:(j + 1) * cb]
        if valid_fn is None:
            return sc
        valid = valid_fn(c * NSA_RC + lax.broadcasted_iota(jnp.int32, (NSA_RC, 1), 0))
        return lanes([jnp.where(valid, head_cols(sc, r), NEG_BIG) for r in range(cb // tq)])

    def put_scores(dst, s):
        dst[...] = s
        return jnp.max(s.reshape(s.shape[0] // 8, 8, ncol), axis=0)

    def col_max(src, rows, valid_fn):
        out = []
        for j in range(ncol // cb):
            mx = jnp.full((8, cb), NEG_BIG, F32)
            for c in range(rows // NSA_RC):
                sc = chunk_scores(src, c, j, valid_fn)
                mx = jnp.maximum(mx, jnp.max(sc.reshape(NSA_RC // 8, 8, cb), axis=0))
            out.append(jnp.max(mx, axis=0, keepdims=True))
        return lanes(out)

    def put_weights(src, dst, rows, valid_fn, m):
        for j in range(ncol // cb):
            mj = m[:, j * cb:(j + 1) * cb]
            for c in range(rows // NSA_RC):
                sc = chunk_scores(src, c, j, valid_fn)
                dst[c * NSA_RC:(c + 1) * NSA_RC, j * cb:(j + 1) * cb] = (
                    jnp.exp2((sc - mj).astype(BF16)).astype(dst.dtype))

    cos = lanes([cos_ref[...]] * N_REP)
    sin = lanes([sin_ref[...]] * N_REP)
    t_q = t0 + lax.broadcasted_iota(jnp.int32, (1, tq), 1)
    tcol = lanes([t_q] * N_REP)
    nc = kc_ref.shape[2]
    valid_c = lambda n: n * CMP_STRIDE + (CMP_LEN - 1) <= t_q
    span = WINDOW + tq
    ws0 = pl.multiple_of(jnp.maximum(t0 - WINDOW, 0), tq)
    valid_w = lambda k: (ws0 + k <= t_q) & (ws0 + k > t_q - WINDOW)
    blk = lax.broadcasted_iota(jnp.int32, (nb, tq), 0)
    cur = (t0 + lax.broadcasted_iota(jnp.int32, (nb, tq), 1)) // SEL_BLOCK
    causal_b = blk <= cur
    forced = (blk == 0) | (blk == cur) | (blk == cur - 1)
    taken = -3.0e38

    def prefix(u):
        q0 = u * N_REP * hd
        q = lanes([q_ref[q0 + r * hd:q0 + (r + 1) * hd, :].astype(F32) for r in range(N_REP)])
        qn = q * lax.rsqrt(jnp.mean(q * q, axis=0, keepdims=True) + EPS) * gq_ref[...]
        rot = jnp.concatenate([qn[hd // 2:], qn[:hd // 2]], axis=0)
        qr = ((qn * cos + rot * sin) * (hd ** -0.5 * LOG2_E)).astype(BF16)

        s_c.at[u][...] = jnp.dot(kc_ref[0, u], qr, preferred_element_type=F32)
        put_weights(s_c.at[u], e_c.at[u], nc, valid_c, col_max(s_c.at[u], nc, valid_c))
        both = jnp.dot(jnp.concatenate([vct_ref[0, u], wimp_ref[...]], axis=0), e_c.at[u][...],
                       preferred_element_type=F32)
        inv_c = jnp.where(tcol >= CMP_LEN - 1, 1.0 / both[hd:hd + 1], 0.0)
        o_c = both[0:hd] * inv_c

        imp_h = both[hd + V_AUG_ROWS:] * inv_c
        imp = head_cols(imp_h, 0)
        for r in range(1, N_REP):
            imp = imp + head_cols(imp_h, r)

        s_w.at[u][...] = jnp.dot(kw_ref[0, u, pl.ds(ws0, span), :], qr, preferred_element_type=F32)
        m_w = col_max(s_w.at[u], span, valid_w)
        score = jnp.where(forced, FORCE_SCORE, jnp.where(causal_b, imp, -1.0))
        return qr, o_c, m_w, score

    def pick_by_value(sc):
        for _ in range(top - 2):
            sc = jnp.where(sc == jnp.max(sc, axis=0, keepdims=True), taken, sc)
        return sc

    def pick_by_value_then_index(sc):
        for _ in range(top):
            mx = jnp.max(sc, axis=0, keepdims=True)
            idx = jnp.min(jnp.where(sc == mx, blk, nb), axis=0, keepdims=True)
            sc = jnp.where(blk == idx, taken, sc)
        return sc

    pre = [prefix(u) for u in range(gp)]
    fast = [pick_by_value(p[3]) for p in pre]
    n_want = jnp.minimum(cur[0:1, :] + 1, top).astype(F32)
    miss = [jnp.abs(jnp.sum(jnp.where((f == taken) & causal_b, 1.0, 0.0), axis=0, keepdims=True) - n_want)
            for f in fast]
    tied = jnp.max(functools.reduce(jnp.maximum, miss)) > 0.0
    picked = lax.cond(tied, lambda: tuple(pick_by_value_then_index(p[3]) for p in pre), lambda: tuple(fast))
    q_aug = []
    for u in range(gp):
        bias = jnp.where((picked[u] == taken) & causal_b, 0.0, BLOCK_BIAS)
        q_aug.append(jnp.concatenate([lanes([bias.astype(BF16)] * N_REP), jnp.clip(pre[u][0], -F8_MAX, F8_MAX)],
                                     axis=0).astype(F8))

    tk = min(NSA_TK, seq)
    last = (t0 + tq - 1) // tk
    s_bufs, e_bufs = (s_a, s_b), (e_a, e_b)

    def scores(u, t, par):
        ks0 = pl.multiple_of(t * tk, tk)
        return put_scores(s_bufs[par].at[u],
                          jnp.dot(ks_ref[0, u, pl.ds(ks0, tk), :], q_aug[u], preferred_element_type=F32))

    def weights(u, t, par, m, mx, masked):
        valid_s = (lambda k: t * tk + k <= t_q) if masked else None
        m_new = jnp.maximum(m, jnp.max(mx, axis=0, keepdims=True))
        put_weights(s_bufs[par].at[u], e_bufs[par].at[u], tk, valid_s, m_new - SEL_WEIGHT_SHIFT)
        return m_new, jnp.exp2(m - m_new)

    def values(u, t, par, alpha, acc):
        ks0 = pl.multiple_of(jnp.maximum(t, 0) * tk, tk)
        v = jnp.clip(vst_ref[u * hd:(u + 1) * hd, pl.ds(ks0, tk)], -F8_MAX, F8_MAX)
        return alpha * acc + jnp.dot(with_ones(v).astype(F8), e_bufs[par].at[u][...],
                                     preferred_element_type=F32)

    def step(t, par, carries, masked, more):
        mx_next = [scores(u, t + 1, 1 - par) if more else c[3] for u, c in enumerate(carries)]
        acc = [values(u, t - 1, 1 - par, c[1], c[2]) for u, c in enumerate(carries)]
        ma = [weights(u, t, par, c[0], c[3], masked) for u, c in enumerate(carries)]
        return tuple((ma[u][0], ma[u][1], acc[u], mx_next[u]) for u in range(gp))

    def pair_body(j, carries):
        return step(2 * j + 1, 1, step(2 * j, 0, carries, False, True), False, True)

    def tail_even(carries):
        carries = step(last, 0, carries, True, False)
        return tuple(values(u, last, 0, c[1], c[2]) for u, c in enumerate(carries))

    def tail_odd(carries):
        carries = step(last, 1, step(last - 1, 0, carries, False, True), True, False)
        return tuple(values(u, last, 1, c[1], c[2]) for u, c in enumerate(carries))

    mx0 = [scores(u, 0, 0) for u in range(gp)]
    e_b[...] = jnp.zeros(e_b.shape, e_b.dtype)
    o_w = []
    for u in range(gp):
        put_weights(s_w.at[u], e_w.at[u], span, valid_w, pre[u][2])
        ow = jnp.dot(with_ones(vwt_ref[u * hd:(u + 1) * hd, pl.ds(ws0, span)]), e_w.at[u][...],
                     preferred_element_type=F32)
        o_w.append(ow[0:hd] / ow[hd:hd + 1])
    carries = tuple((jnp.full((1, ncol), NEG_BIG, F32), jnp.ones((1, ncol), F32),
                     jnp.zeros((hd + V_AUG_ROWS, ncol), F32), mx0[u]) for u in range(gp))
    carries = lax.fori_loop(0, last // 2, pair_body, carries)
    accs = lax.cond(last % 2 == 0, tail_even, tail_odd, carries)

    for u in range(gp):
        o_s = accs[u][0:hd] / accs[u][hd:hd + 1]
        g = _sigmoid(gate_ref[0, u])
        gc, gs, gw = (lanes([g[j, r:r + 1, :] for r in range(N_REP)]) for j in range(3))
        out = gc * pre[u][1] + gs * o_s + gw * o_w[u]
        y0 = u * N_REP * hd
        for r in range(N_REP):
            y_ref[y0 + r * hd:y0 + (r + 1) * hd, :] = head_cols(out, r).astype(y_ref.dtype)


def _nsa(zt, gates, kc, vct, ks_aug, kw, gq, cos_t, sin_t, wimp_t, batch, seq):
    g, hd, gp = N_GROUPS, N_HEAD_DIM, NSA_GROUPS
    tq = min(NSA_TQ, seq)
    tk = min(NSA_TK, seq)
    nq = seq // tq
    ncol = N_REP * tq
    nc = kc.shape[2]
    nb = seq // SEL_BLOCK
    full = lambda *shape: pl.BlockSpec((1, gp) + shape, lambda bi, gi, qi: (bi, gi) + (0,) * len(shape))
    vrow = lambda row0: pl.BlockSpec((gp * hd, seq),
                                     functools.partial(lambda bi, gi, qi, o: (o + gi, bi), o=row0 // (gp * hd)))
    qrows = gp * N_REP * hd
    qspec = pl.BlockSpec((qrows, tq), lambda bi, gi, qi: (ZT_Q // qrows + gi, bi * nq + qi))
    return pl.pallas_call(
        functools.partial(_nsa_kernel, seq=seq),
        grid=(batch, g // gp, nq),
        in_specs=[qspec,
                  pl.BlockSpec((1, gp, 3, N_REP, tq), lambda bi, gi, qi: (bi, gi, 0, 0, qi)),
                  full(nc, hd), full(hd + V_AUG_ROWS, nc), full(seq, nb + hd), vrow(ZT_VS), full(seq, hd),
                  vrow(ZT_VW),
                  pl.BlockSpec((hd, 1), lambda bi, gi, qi: (0, 0)),
                  pl.BlockSpec((hd, tq), lambda bi, gi, qi: (0, qi)),
                  pl.BlockSpec((hd, tq), lambda bi, gi, qi: (0, qi)),
                  pl.BlockSpec((nb, nc), lambda bi, gi, qi: (0, 0))],
        out_specs=pl.BlockSpec((qrows, tq), lambda bi, gi, qi: (gi, bi * nq + qi)),
        out_shape=jax.ShapeDtypeStruct((N_HEADS * hd, batch * seq), BF16),
        scratch_shapes=[pltpu.VMEM((gp, nc, ncol), F32), pltpu.VMEM((gp, nc, ncol), BF16),
                        pltpu.VMEM((gp, WINDOW + tq, ncol), F32), pltpu.VMEM((gp, WINDOW + tq, ncol), BF16),
                        pltpu.VMEM((gp, tk, ncol), F32), pltpu.VMEM((gp, tk, ncol), F32),
                        pltpu.VMEM((gp, tk, ncol), F8), pltpu.VMEM((gp, tk, ncol), F8)],
        compiler_params=_params("parallel", "parallel", "arbitrary"),
        name="nsa_attention",
    )(zt, gates, kc, vct, ks_aug, zt, kw, zt, gq.reshape(hd, 1), cos_t, sin_t, wimp_t)


def _rope_tables(pos):
    half = N_HEAD_DIM // 2
    inv = jnp.power(ROPE_THETA, -jnp.arange(half, dtype=F32) * 2.0 / N_HEAD_DIM)
    ang = pos.astype(F32)[:, None] * inv[None, :]
    cos, sin = jnp.cos(ang), jnp.sin(ang)
    return jnp.concatenate([cos, cos], axis=-1), jnp.concatenate([-sin, sin], axis=-1)


def _importance_matrix(nb, nc_pad):
    per = SEL_BLOCK // CMP_STRIDE
    j = np.arange(nb)[:, None]
    n = np.arange(nc_pad)[None, :]
    w = np.zeros((nb, nc_pad), np.float32)
    for d in range(CMP_LEN // CMP_STRIDE):
        w += ((n + d >= per * j) & (n + d <= per * j + per - 1)).astype(np.float32)
    return jnp.asarray(w, BF16)


def _split_w_in(w_in):
    d = w_in.shape[0]
    sizes = (2048, 1024, 1024, 4, 4, 1024, 256, 256, 256, 256, 256, 256, 48, 1024, 2048, 2048, 2048)
    offs = np.concatenate([[0], np.cumsum(sizes)])
    part = lambda i: w_in[:, offs[i]:offs[i + 1]]
    tok = jnp.concatenate([part(i) for i in (0, 1, 2, 13, 14, 15, 16, 6, 7, 8, 10)], axis=1).astype(BF16)
    feat = jnp.concatenate([part(i) for i in (5, 9, 11)], axis=1).T.astype(BF16)
    small = jnp.concatenate([part(3), part(4), part(12), jnp.zeros((d, 128 - 56), w_in.dtype)], axis=1).astype(BF16)
    return tok, feat, small


def _layer(x, mem, layer, big, norm_mix_g, norm_mem_g, norm_ffn_g, w_in, m_conv_w, m_i_bias, m_f_bias, m_norm_g,
           n_q_norm_g, n_kc_norm_g, n_ks_norm_g, n_kw_norm_g, n_cmp_pe_k, n_cmp_w1_k, n_cmp_w2_k,
           n_cmp_pe_v, n_cmp_w1_v, n_cmp_w2_v, c_q_norm_g, c_k_norm_g, w_mem_k, w_mem_v):
    B, S, D = x.shape
    M = B * S
    G, R, hd = N_GROUPS, N_REP, N_HEAD_DIM
    x2 = x.reshape(M, D)

    w_tok, w_feat, w_small = _split_w_in(w_in)
    z, zs, h = _in_proj(x2, norm_mix_g, w_tok, w_small)
    zt = _mm_feature_major(w_feat, h)
    z3 = z.reshape(B, S, Z_WIDTH)
    zs3 = zs.reshape(B, S, 128)

    gif = jnp.stack([zs3[..., 0:4], zs3[..., 4:8]], axis=-1)
    gcol = gif.transpose(0, 2, 1, 3)
    grow = gif.transpose(0, 2, 3, 1)
    gbias = jnp.stack([m_i_bias, m_f_bias], axis=-1).reshape(M_HEADS, 1, 2).astype(F32)
    y_a = _mlstm(z3, gcol, grow, gbias, m_conv_w, m_norm_g, B, S)

    pos = jnp.arange(S, dtype=jnp.int32)
    cos, sin_s = _rope_tables(pos)
    nc_pad = S // CMP_STRIDE
    cmp_end = jnp.arange(nc_pad, dtype=jnp.int32) * CMP_STRIDE + CMP_LEN - 1
    cos_c, sin_c = _rope_tables(cmp_end)
    nb = S // SEL_BLOCK

    def head_major(col0):
        u = z3[..., col0:col0 + G * hd].reshape(B, S, G, hd)
        return u.transpose(0, 2, 1, 3).reshape(B * G, S, hd)

    kc = _compress(head_major(Z_KC), n_cmp_pe_k, n_cmp_w1_k, n_cmp_w2_k, n_kc_norm_g, cos_c, sin_c, True)
    vc = _compress(head_major(Z_VC), n_cmp_pe_v, n_cmp_w1_v, n_cmp_w2_v, n_kc_norm_g, cos_c, sin_c, False)
    vct = vc.reshape(B, G, nc_pad, hd).transpose(0, 1, 3, 2)
    ones_rows = jnp.concatenate([jnp.ones((B, G, 1, nc_pad), BF16), jnp.zeros((B, G, V_AUG_ROWS - 1, nc_pad), BF16)],
                                axis=2)
    vct = jnp.concatenate([vct, ones_rows], axis=2)
    cos2, sin2 = jnp.concatenate([cos, cos], axis=1), jnp.concatenate([sin_s, sin_s], axis=1)
    ks_aug = _knorm_rope(z3, Z_KS, n_ks_norm_g, cos2, sin2, nb)
    kw = _knorm_rope(z3, Z_KW, n_kw_norm_g, cos2, sin2, 0)
    gates = zs3[..., 8:8 + 3 * N_HEADS].reshape(B, S, G, R, 3).transpose(0, 2, 4, 3, 1)
    y_bt = _nsa(zt, gates, kc.reshape(B, G, nc_pad, hd), vct, ks_aug, kw, n_q_norm_g, cos.T, sin_s.T,
                _importance_matrix(nb, nc_pad), B, S)

    mlen = mem.shape[1]
    mem_n = _rmsnorm(mem.reshape(B * mlen, D), norm_mem_g)
    k_mem = _mm(mem_n, w_mem_k.astype(BF16), BF16).reshape(B, mlen, -1)
    v_mem = _mm(mem_n, w_mem_v.astype(BF16), BF16).reshape(B, mlen, -1)
    y_c = _xattn(z3, k_mem, v_mem, c_q_norm_g, c_k_norm_g, B, S)

    mix = _mix(y_a.reshape(M, -1), y_bt, y_c.reshape(M, -1), big["w_up_a"], big["w_up_b"], big["w_up_c"], layer,
               z, Z_GATES)
    x2 = _mm_residual(mix, big["w_out"], layer, x2, tm=512, tn=D)

    act = _swiglu(x2, norm_ffn_g, big["w_ffn_gate"], big["w_ffn_up"], layer)
    x2 = _mm_residual(act, big["w_ffn_down"], layer, x2, tm=1024, tn=512)
    return x2.reshape(B, S, D)


def kernel(x, mem, norm_mix_g, norm_mem_g, norm_ffn_g, w_in, m_conv_w, m_i_bias, m_f_bias, m_norm_g, n_q_norm_g, n_kc_norm_g, n_ks_norm_g, n_kw_norm_g, n_cmp_pe_k, n_cmp_w1_k, n_cmp_w2_k, n_cmp_pe_v, n_cmp_w1_v, n_cmp_w2_v, c_q_norm_g, c_k_norm_g, w_mem_k, w_mem_v, w_up_a, w_up_b, w_up_c, w_out, w_ffn_gate, w_ffn_up, w_ffn_down):
    per_layer = (norm_mix_g, norm_mem_g, norm_ffn_g, w_in, m_conv_w, m_i_bias, m_f_bias, m_norm_g, n_q_norm_g,
                 n_kc_norm_g, n_ks_norm_g, n_kw_norm_g, n_cmp_pe_k, n_cmp_w1_k, n_cmp_w2_k, n_cmp_pe_v, n_cmp_w1_v,
                 n_cmp_w2_v, c_q_norm_g, c_k_norm_g, w_mem_k, w_mem_v)
    big = {name: w.astype(BF16) for name, w in (
        ("w_up_a", w_up_a), ("w_up_b", w_up_b), ("w_up_c", w_up_c), ("w_out", w_out),
        ("w_ffn_gate", w_ffn_gate), ("w_ffn_up", w_ffn_up), ("w_ffn_down", w_ffn_down))}
    for layer in range(w_in.shape[0]):
        x = _layer(x, mem, layer, big, *(p[layer] for p in per_layer))
    return x
```

```python
import functools

import numpy as np
import jax
import jax.numpy as jnp
from jax import lax
from jax.experimental import pallas as pl
from jax.experimental.pallas import tpu as pltpu

F32 = jnp.float32
BF16 = jnp.bfloat16

EPS = 1e-6
ROPE_THETA = 10000.0
M_HEADS = 4
M_HEAD_DIM = 256
M_CONV = 4
N_HEADS = 16
N_GROUPS = 4
N_REP = N_HEADS // N_GROUPS
N_HEAD_DIM = 64
CMP_LEN = 32
CMP_STRIDE = 16
CMP_HIDDEN = 128
SEL_BLOCK = 64
SEL_TOP = 16
WINDOW = 512
FORCE_SCORE = 1.0e4
C_HEADS = 4
C_HEAD_DIM = 256

LOG2_E = 1.4426950408889634
V_AUG_ROWS = 16
NEG_BIG = -1.0e30
F8 = jnp.float8_e4m3fn
F8_MAX = 448.0
BLOCK_BIAS = -F8_MAX
SEL_WEIGHT_SHIFT = 8.0
V7X_VMEM_BYTES = 64 * 1024 * 1024
VMEM_LIMIT = V7X_VMEM_BYTES - 8 * 1024 * 1024

MLSTM_CHUNK = 256
MLSTM_BATCH_ROWS = 2
NSA_TQ = 256
NSA_TK = 512
NSA_GROUPS = 2
NSA_RC = 32
NSA_CB = 1024

Z_MQK, Z_MV, Z_MO, Z_CQ, Z_GATES, Z_KC, Z_VC, Z_KS, Z_KW, Z_WIDTH = (
    0, 2048, 3072, 4096, 5120, 11264, 11520, 11776, 12032, 12288)
ZT_Q, ZT_VS, ZT_VW, ZT_ROWS = 0, 1024, 1280, 1536


def _params(*sem):
    return pltpu.CompilerParams(dimension_semantics=sem, vmem_limit_bytes=VMEM_LIMIT)


def _sigmoid(x):
    return 1.0 / (1.0 + jnp.exp(-x))


def _silu(x):
    return x * _sigmoid(x)


def _log_sigmoid(x):
    return jnp.minimum(x, 0.0) - jnp.log1p(jnp.exp(-jnp.abs(x)))


def _rms_rows(x, g):
    return x * lax.rsqrt(jnp.mean(x * x, axis=-1, keepdims=True) + EPS) * g


def _rmsnorm_kernel(x_ref, g_ref, o_ref):
    o_ref[...] = _rms_rows(x_ref[...], g_ref[...]).astype(o_ref.dtype)


def _rmsnorm(x2d, g, tm=512):
    m, d = x2d.shape
    tm = min(tm, m)
    return pl.pallas_call(
        _rmsnorm_kernel,
        grid=(m // tm,),
        in_specs=[pl.BlockSpec((tm, d), lambda i: (i, 0)), pl.BlockSpec((1, d), lambda i: (0, 0))],
        out_specs=pl.BlockSpec((tm, d), lambda i: (i, 0)),
        out_shape=jax.ShapeDtypeStruct((m, d), BF16),
        compiler_params=_params("parallel"),
        name="rmsnorm",
    )(x2d, g.reshape(1, d))


def _mm_kernel(a_ref, w_ref, o_ref):
    o_ref[...] = jnp.dot(a_ref[...], w_ref[...], preferred_element_type=F32).astype(o_ref.dtype)


def _mm(a, w, out_dtype, tm=1024, tn=512):
    m, k = a.shape
    n = w.shape[1]
    tm, tn = min(tm, m), min(tn, n)
    return pl.pallas_call(
        _mm_kernel,
        grid=(m // tm, n // tn),
        in_specs=[pl.BlockSpec((tm, k), lambda i, j: (i, 0)), pl.BlockSpec((k, tn), lambda i, j: (0, j))],
        out_specs=pl.BlockSpec((tm, tn), lambda i, j: (i, j)),
        out_shape=jax.ShapeDtypeStruct((m, n), out_dtype),
        compiler_params=_params("parallel", "arbitrary"),
        name="matmul",
    )(a, w)


def _in_proj_kernel(x_ref, g_ref, w_ref, ws_ref, z_ref, zs_ref, h_ref):
    @pl.when(pl.program_id(1) == 0)
    def _():
        h = _rms_rows(x_ref[...], g_ref[...]).astype(BF16)
        h_ref[...] = h
        zs_ref[...] = jnp.dot(h, ws_ref[...], preferred_element_type=F32)

    z_ref[...] = jnp.dot(h_ref[...], w_ref[...], preferred_element_type=F32).astype(z_ref.dtype)


def _in_proj(x2d, g, w, w_small, tm=1024, tn=1024):
    m, d = x2d.shape
    n, ns = w.shape[1], w_small.shape[1]
    tm, tn = min(tm, m), min(tn, n)
    return pl.pallas_call(
        _in_proj_kernel,
        grid=(m // tm, n // tn),
        in_specs=[pl.BlockSpec((tm, d), lambda i, j: (i, 0)), pl.BlockSpec((1, d), lambda i, j: (0, 0)),
                  pl.BlockSpec((d, tn), lambda i, j: (0, j)), pl.BlockSpec((d, ns), lambda i, j: (0, 0))],
        out_specs=[pl.BlockSpec((tm, tn), lambda i, j: (i, j)), pl.BlockSpec((tm, ns), lambda i, j: (i, 0)),
                   pl.BlockSpec((tm, d), lambda i, j: (i, 0))],
        out_shape=[jax.ShapeDtypeStruct((m, n), BF16), jax.ShapeDtypeStruct((m, ns), F32),
                   jax.ShapeDtypeStruct((m, d), BF16)],
        compiler_params=_params("parallel", "arbitrary"),
        name="in_proj",
    )(x2d, g.reshape(1, d), w, w_small)


def _mm_nt_kernel(w_ref, h_ref, o_ref):
    o_ref[...] = lax.dot_general(w_ref[...], h_ref[...], (((1,), (1,)), ((), ())),
                                 preferred_element_type=F32).astype(o_ref.dtype)


def _mm_feature_major(w_t, h, tm=1024, tn=1536):
    n, k = w_t.shape
    m = h.shape[0]
    tm, tn = min(tm, m), min(tn, n)
    return pl.pallas_call(
        _mm_nt_kernel,
        grid=(m // tm, n // tn),
        in_specs=[pl.BlockSpec((tn, k), lambda i, j: (j, 0)), pl.BlockSpec((tm, k), lambda i, j: (i, 0))],
        out_specs=pl.BlockSpec((tn, tm), lambda i, j: (j, i)),
        out_shape=jax.ShapeDtypeStruct((n, m), BF16),
        compiler_params=_params("parallel", "arbitrary"),
        name="in_proj_feature_major",
    )(w_t, h)


def _mm_res_kernel(a_ref, w_ref, x_ref, o_ref):
    o_ref[...] = x_ref[...] + jnp.dot(a_ref[...], w_ref[...], preferred_element_type=F32)


def _layer_block(layer, block, index_map):
    return pl.BlockSpec((None,) + block, lambda *idx: (layer,) + index_map(*idx))


def _mm_residual(a, w, layer, x, tm=512, tn=512):
    m, k = a.shape
    n = w.shape[2]
    tm, tn = min(tm, m), min(tn, n)
    return pl.pallas_call(
        _mm_res_kernel,
        grid=(m // tm, n // tn),
        in_specs=[pl.BlockSpec((tm, k), lambda i, j: (i, 0)), _layer_block(layer, (k, tn), lambda i, j: (0, j)),
                  pl.BlockSpec((tm, tn), lambda i, j: (i, j))],
        out_specs=pl.BlockSpec((tm, tn), lambda i, j: (i, j)),
        out_shape=jax.ShapeDtypeStruct((m, n), F32),
        compiler_params=_params("parallel", "arbitrary"),
        name="matmul_residual",
    )(a, w, x)


def _swiglu_kernel(x_ref, g_ref, wg_ref, wu_ref, o_ref, h_scr):
    @pl.when(pl.program_id(1) == 0)
    def _():
        h_scr[...] = _rms_rows(x_ref[...], g_ref[...]).astype(BF16)

    h = h_scr[...]
    gate = jnp.dot(h, wg_ref[...], preferred_element_type=F32)
    up = jnp.dot(h, wu_ref[...], preferred_element_type=F32)
    o_ref[...] = (_silu(gate) * up).astype(o_ref.dtype)


def _swiglu(x2d, g, wg, wu, layer, tm=1024, tn=512):
    m, d = x2d.shape
    n = wg.shape[2]
    tm, tn = min(tm, m), min(tn, n)
    return pl.pallas_call(
        _swiglu_kernel,
        grid=(m // tm, n // tn),
        in_specs=[pl.BlockSpec((tm, d), lambda i, j: (i, 0)), pl.BlockSpec((1, d), lambda i, j: (0, 0)),
                  _layer_block(layer, (d, tn), lambda i, j: (0, j)), _layer_block(layer, (d, tn), lambda i, j: (0, j))],
        out_specs=pl.BlockSpec((tm, tn), lambda i, j: (i, j)),
        out_shape=jax.ShapeDtypeStruct((m, n), BF16),
        scratch_shapes=[pltpu.VMEM((tm, d), BF16)],
        compiler_params=_params("parallel", "arbitrary"),
        name="swiglu",
    )(x2d, g.reshape(1, d), wg, wu)


def _mix_kernel(ya_ref, ybt_ref, yc_ref, wa_ref, wb_ref, wc_ref, ga_ref, gb_ref, gc_ref, o_ref):
    def gated(g_ref, prod):
        return _sigmoid(g_ref[...].astype(F32)) * prod

    a = jnp.dot(ya_ref[...], wa_ref[...], preferred_element_type=F32)
    b = lax.dot_general(ybt_ref[...], wb_ref[...], (((0,), (0,)), ((), ())), preferred_element_type=F32)
    c = jnp.dot(yc_ref[...], wc_ref[...], preferred_element_type=F32)
    o_ref[...] = (gated(ga_ref, a) + gated(gb_ref, b) + gated(gc_ref, c)).astype(o_ref.dtype)


def _mix(ya, ybt, yc, wa, wb, wc, layer, z, gate_col0, tm=1024, tn=512):
    m, k = ya.shape
    n = wa.shape[2]
    tm, tn = min(tm, m), min(tn, n)
    gofs = [(gate_col0 + b * n) // tn for b in range(3)]
    y_spec = pl.BlockSpec((tm, k), lambda i, j: (i, 0))
    w_spec = _layer_block(layer, (k, tn), lambda i, j: (0, j))
    g_specs = [pl.BlockSpec((tm, tn), functools.partial(lambda i, j, o: (i, o + j), o=o)) for o in gofs]
    return pl.pallas_call(
        _mix_kernel,
        grid=(m // tm, n // tn),
        in_specs=[y_spec, pl.BlockSpec((k, tm), lambda i, j: (0, i)), y_spec, w_spec, w_spec, w_spec] + g_specs,
        out_specs=pl.BlockSpec((tm, tn), lambda i, j: (i, j)),
        out_shape=jax.ShapeDtypeStruct((m, n), BF16),
        compiler_params=_params("parallel", "arbitrary"),
        name="gated_mix",
    )(ya, ybt, yc, wa, wb, wc, z, z, z)


def _mlstm_kernel(uq_ref, uk_ref, v_ref, o_ref, gcol_ref, grow_ref, bias_ref, cwq_ref, cwk_ref, ng_ref, shift_ref,
                  y_ref, c_scr, n_scr, m_scr, qtail, ktail):
    c = pl.program_id(1)
    nbp = uq_ref.shape[0]
    L = uq_ref.shape[1]
    hd = M_HEAD_DIM

    @pl.when(c == 0)
    def _():
        c_scr[...] = jnp.zeros_like(c_scr)
        n_scr[...] = jnp.zeros_like(n_scr)
        m_scr[...] = jnp.zeros_like(m_scr)
        qtail[...] = jnp.zeros_like(qtail)
        ktail[...] = jnp.zeros_like(ktail)

    def conv_silu(u_ref, bb, tail_all, cw_ref):
        tail = tail_all.at[bb * 8:(bb + 1) * 8, :]
        u = u_ref[bb]
        cw = cw_ref[...]
        cwb = cw.astype(BF16)
        x = jnp.concatenate([u * cwb[j:j + 1, :] for j in range(M_CONV)], axis=0)
        y = jnp.dot(shift_ref[...], x, preferred_element_type=F32)
        prev = tail[...]
        row = lax.broadcasted_iota(jnp.int32, prev.shape, 0)
        edge = jnp.zeros(prev.shape, F32)
        for t in range(M_CONV - 1):
            acc = cw[0:1, :] * prev[8 - (M_CONV - 1) + t:8 - (M_CONV - 1) + t + 1, :]
            for j in range(1, M_CONV - 1 - t):
                acc = acc + cw[j:j + 1, :] * prev[8 - (M_CONV - 1) + t + j:8 - (M_CONV - 1) + t + j + 1, :]
            edge = jnp.where(row == t, acc, edge)
        tail[...] = u[L - 8:L, :].astype(F32)
        return _silu(jnp.concatenate([y[0:8] + edge, y[8:]], axis=0))

    q_alls = [conv_silu(uq_ref, bb, qtail, cwq_ref) for bb in range(nbp)]
    k_alls = [conv_silu(uk_ref, bb, ktail, cwk_ref) * (hd ** -0.5) for bb in range(nbp)]
    row = lax.broadcasted_iota(jnp.int32, (L, L), 0)
    col = lax.broadcasted_iota(jnp.int32, (L, L), 1)
    tri = col <= row
    triu = jnp.where(row <= col, 1.0, 0.0).astype(BF16)

    for sh in range(nbp * M_HEADS):
        bb, hh = divmod(sh, M_HEADS)
        sl = slice(hh * hd, (hh + 1) * hd)
        q, k = q_alls[bb][:, sl], k_alls[bb][:, sl]
        qb = q.astype(BF16)
        kb = k.astype(BF16)
        vb = v_ref[bb, :, sl]

        gcol = gcol_ref[bb, hh]
        grow = grow_ref[bb, hh]
        bias = bias_ref[hh]
        ic = gcol[:, 0:1] + bias[:, 0:1]
        ir = grow[0:1, :] + bias[:, 0:1]
        fr = _log_sigmoid(grow[1:2, :] + bias[:, 1:2])

        b_col = jnp.sum(jnp.where(tri, fr, 0.0), axis=1, keepdims=True)
        f1 = fr.astype(BF16)
        r1 = fr - f1.astype(F32)
        f2 = r1.astype(BF16)
        f3 = (r1 - f2.astype(F32)).astype(BF16)
        terms = jnp.concatenate([f1, f2, f3, jnp.zeros((13, L), BF16)], axis=0)
        pre = jnp.dot(terms, triu, preferred_element_type=F32)
        b_row = pre[0:1] + pre[1:2] + pre[2:3]
        m_prev = m_scr[sh, 0:1, 0:1]

        log_d = jnp.where(tri, b_col - b_row + ir, NEG_BIG)
        log_inter = b_col + m_prev
        m_t = jnp.maximum(log_inter, jnp.max(log_d, axis=1, keepdims=True))
        w_intra = jnp.exp(log_d - m_t)
        w_inter = jnp.exp(log_inter - m_t)

        s = lax.dot_general(qb, kb, (((1,), (1,)), ((), ())), preferred_element_type=F32) * w_intra
        c_state = c_scr[sh]
        n_state = n_scr[sh, 0:1, :]
        num = (jnp.dot(s.astype(BF16), vb, preferred_element_type=F32)
               + w_inter * jnp.dot(qb, c_state.astype(BF16), preferred_element_type=F32))
        den = jnp.sum(s, axis=1, keepdims=True) + w_inter * jnp.sum(q * n_state, axis=1, keepdims=True)
        h = num / jnp.maximum(jnp.abs(den), jnp.exp(-m_t))

        b_last = b_row[:, L - 1:L]
        log_g = b_last - b_col + ic
        m_new = jnp.maximum(b_last + m_prev, jnp.max(log_g, axis=0, keepdims=True))
        decay = jnp.exp(b_last + m_prev - m_new)
        wk = k * jnp.exp(log_g - m_new)
        c_scr[sh] = decay * c_state + lax.dot_general(wk.astype(BF16), vb, (((0,), (0,)), ((), ())),
                                                      preferred_element_type=F32)
        n_scr[sh, 0:1, :] = decay * n_state + jnp.sum(wk, axis=0, keepdims=True)
        m_scr[sh] = jnp.broadcast_to(m_new, m_scr.shape[1:])

        hn = _rms_rows(h, ng_ref[hh])
        y_ref[bb, :, sl] = (_sigmoid(o_ref[bb, :, sl].astype(F32)) * hn).astype(y_ref.dtype)


def _conv_shift_matrix(L):
    t = np.arange(L)[:, None]
    r = np.arange(L)[None, :]
    return jnp.asarray(np.concatenate([(r == t - (M_CONV - 1) + j) for j in range(M_CONV)], axis=1), BF16)


def _mlstm(z3, gcol, grow, gbias, conv_w, norm_g, batch, seq):
    L = min(MLSTM_CHUNK, seq)
    hd = M_HEAD_DIM
    nh = M_HEADS
    w = nh * hd

    nbp = MLSTM_BATCH_ROWS if batch % MLSTM_BATCH_ROWS == 0 else 1

    def zspec(col0):
        return pl.BlockSpec((nbp, L, w), functools.partial(lambda b, c, o: (b, c, o), o=col0 // w))

    return pl.pallas_call(
        _mlstm_kernel,
        grid=(batch // nbp, seq // L),
        in_specs=[zspec(Z_MQK), zspec(Z_MQK + w), zspec(Z_MV), zspec(Z_MO),
                  pl.BlockSpec((nbp, nh, L, 2), lambda b, c: (b, 0, c, 0)),
                  pl.BlockSpec((nbp, nh, 2, L), lambda b, c: (b, 0, 0, c)),
                  pl.BlockSpec((nh, 1, 2), lambda b, c: (0, 0, 0)),
                  pl.BlockSpec((M_CONV, w), lambda b, c: (0, 0)),
                  pl.BlockSpec((M_CONV, w), lambda b, c: (0, 1)),
                  pl.BlockSpec((nh, 1, hd), lambda b, c: (0, 0, 0)),
                  pl.BlockSpec((L, M_CONV * L), lambda b, c: (0, 0))],
        out_specs=pl.BlockSpec((nbp, L, w), lambda b, c: (b, c, 0)),
        out_shape=jax.ShapeDtypeStruct((batch, seq, w), BF16),
        scratch_shapes=[pltpu.VMEM((nbp * nh, hd, hd), F32), pltpu.VMEM((nbp * nh, 8, hd), F32),
                        pltpu.VMEM((nbp * nh, 8, 128), F32),
                        pltpu.VMEM((nbp * 8, w), F32), pltpu.VMEM((nbp * 8, w), F32)],
        compiler_params=_params("parallel", "arbitrary"),
        name="mlstm",
    )(z3, z3, z3, z3, gcol, grow, gbias, conv_w, conv_w, norm_g.reshape(nh, 1, hd), _conv_shift_matrix(L))


def _xattn_kernel(q_ref, k_ref, v_ref, gq_ref, gk_ref, o_ref, kn_scr):
    hd = C_HEAD_DIM

    @pl.when(pl.program_id(1) == 0)
    def _():
        for h in range(C_HEADS):
            sl = slice(h * hd, (h + 1) * hd)
            kn_scr[:, sl] = _rms_rows(k_ref[0, :, sl].astype(F32), gk_ref[...]).astype(BF16)

    for h in range(C_HEADS):
        sl = slice(h * hd, (h + 1) * hd)
        q = _rms_rows(q_ref[0, :, sl].astype(F32), gq_ref[...]) * (hd ** -0.5)
        s = lax.dot_general(q.astype(BF16), kn_scr[:, sl], (((1,), (1,)), ((), ())), preferred_element_type=F32)
        e = jnp.exp(s - jnp.max(s, axis=-1, keepdims=True))
        p = e / jnp.sum(e, axis=-1, keepdims=True)
        o = jnp.dot(p.astype(BF16), v_ref[0, :, sl], preferred_element_type=F32)
        o_ref[0, :, sl] = o.astype(o_ref.dtype)


def _xattn(z3, k, v, gq, gk, batch, seq, tq=512):
    tq = min(tq, seq)
    mlen, w = k.shape[1], k.shape[2]
    return pl.pallas_call(
        _xattn_kernel,
        grid=(batch, seq // tq),
        in_specs=[pl.BlockSpec((1, tq, w), lambda b, i: (b, i, Z_CQ // w)),
                  pl.BlockSpec((1, mlen, w), lambda b, i: (b, 0, 0)),
                  pl.BlockSpec((1, mlen, w), lambda b, i: (b, 0, 0)),
                  pl.BlockSpec((1, C_HEAD_DIM), lambda b, i: (0, 0)),
                  pl.BlockSpec((1, C_HEAD_DIM), lambda b, i: (0, 0))],
        out_specs=pl.BlockSpec((1, tq, w), lambda b, i: (b, i, 0)),
        out_shape=jax.ShapeDtypeStruct((batch, seq, w), BF16),
        scratch_shapes=[pltpu.VMEM((mlen, w), BF16)],
        compiler_params=_params("parallel", "arbitrary"),
        name="memory_cross_attention",
    )(z3, k, v, gq.reshape(1, -1), gk.reshape(1, -1))


def _rope_rows(xn, cos, sin_signed):
    half = N_HEAD_DIM // 2
    rot = jnp.concatenate([xn[:, half:], xn[:, :half]], axis=-1)
    return xn * cos + rot * sin_signed


def _knorm_rope_kernel(x_ref, g_ref, cos_ref, sin_ref, seg_ref, swap_ref, o_ref, *, n_blocks):
    hd = N_HEAD_DIM
    ts = x_ref.shape[1]
    x = x_ref[0].astype(F32)
    xx = x * x
    hi = xx.astype(BF16)
    lo = (xx - hi.astype(F32)).astype(BF16)
    ss = (jnp.dot(hi, seg_ref[...], preferred_element_type=F32)
          + jnp.dot(lo, seg_ref[...], preferred_element_type=F32))
    xn = x * lax.rsqrt(ss * (1.0 / hd) + EPS) * g_ref[...]
    rot = jnp.dot(xn.astype(BF16), swap_ref[...], preferred_element_type=F32)
    reps = N_GROUPS * hd // cos_ref.shape[1]
    cos = jnp.concatenate([cos_ref[...]] * reps, axis=1)
    sin = jnp.concatenate([sin_ref[...]] * reps, axis=1)
    k = xn * cos + rot * sin
    if n_blocks:
        k = jnp.clip(k, -F8_MAX, F8_MAX)
    k = k.astype(o_ref.dtype)
    if n_blocks:
        pos = pl.program_id(1) * ts + lax.broadcasted_iota(jnp.int32, (ts, n_blocks), 0)
        blk = lax.broadcasted_iota(jnp.int32, (ts, n_blocks), 1)
        onehot = jnp.where(pos // SEL_BLOCK == blk, F8_MAX, 0.0).astype(o_ref.dtype)
    for g in range(N_GROUPS):
        if n_blocks:
            o_ref[0, g, :, 0:n_blocks] = onehot
        o_ref[0, g, :, n_blocks:n_blocks + hd] = k[:, g * hd:(g + 1) * hd]


def _head_matrices():
    w = N_GROUPS * N_HEAD_DIM
    i = np.arange(w)[:, None]
    j = np.arange(w)[None, :]
    same = (i // N_HEAD_DIM) == (j // N_HEAD_DIM)
    swap = same & ((i % N_HEAD_DIM) == (j % N_HEAD_DIM + N_HEAD_DIM // 2) % N_HEAD_DIM)
    return jnp.asarray(same, BF16), jnp.asarray(swap, BF16)


def _knorm_rope(z3, col0, g, cos2, sin2, n_blocks, ts=1024):
    b, s, _ = z3.shape
    hd, w = N_HEAD_DIM, N_GROUPS * N_HEAD_DIM
    ts = min(ts, s)
    same, swap = _head_matrices()
    const = lambda shape: pl.BlockSpec(shape, lambda i, j: (0, 0))
    return pl.pallas_call(
        functools.partial(_knorm_rope_kernel, n_blocks=n_blocks),
        grid=(b, s // ts),
        in_specs=[pl.BlockSpec((1, ts, w), lambda i, j: (i, j, col0 // w)), const((1, w)),
                  pl.BlockSpec((ts, 2 * hd), lambda i, j: (j, 0)), pl.BlockSpec((ts, 2 * hd), lambda i, j: (j, 0)),
                  const((w, w)), const((w, w))],
        out_specs=pl.BlockSpec((1, N_GROUPS, ts, n_blocks + hd), lambda i, j: (i, 0, j, 0)),
        out_shape=jax.ShapeDtypeStruct((b, N_GROUPS, s, n_blocks + hd), F8 if n_blocks else BF16),
        compiler_params=_params("parallel", "arbitrary"),
        name="key_norm_rope",
    )(z3, jnp.tile(g.reshape(1, hd), (1, N_GROUPS)), cos2, sin2, same, swap)


def _compress_kernel(sub_ref, pe_ref, w1_ref, w2_ref, g_ref, cos_ref, sin_ref, o_ref, *, is_key):
    ns = sub_ref.shape[1]
    sub = sub_ref[0].astype(F32)
    lo = jnp.dot((sub + pe_ref[0:1, :]).astype(BF16), w1_ref[0], preferred_element_type=F32)
    hi = jnp.dot((sub + pe_ref[1:2, :]).astype(BF16), w1_ref[1], preferred_element_type=F32)
    hid = _silu(lo + pltpu.roll(hi, shift=ns - 1, axis=0))
    out = jnp.dot(hid.astype(BF16), w2_ref[...], preferred_element_type=F32)
    if is_key:
        out = _rope_rows(_rms_rows(out, g_ref[...]), cos_ref[...], sin_ref[...])
    o_ref[0] = out.astype(o_ref.dtype)


def _compress(u, pe, w1, w2, g, cos, sin_signed, is_key):
    n, s, hd = u.shape
    ns = s // CMP_STRIDE
    width = CMP_STRIDE * hd
    sub = u.reshape(n, ns, width)
    pe2 = pe.reshape(CMP_LEN // CMP_STRIDE, width)
    w1s = w1.reshape(CMP_LEN // CMP_STRIDE, width, CMP_HIDDEN).astype(BF16)
    return pl.pallas_call(
        functools.partial(_compress_kernel, is_key=is_key),
        grid=(n,),
        in_specs=[pl.BlockSpec((1, ns, width), lambda i: (i, 0, 0)),
                  pl.BlockSpec(pe2.shape, lambda i: (0, 0)),
                  pl.BlockSpec(w1s.shape, lambda i: (0, 0, 0)),
                  pl.BlockSpec(w2.shape, lambda i: (0, 0)),
                  pl.BlockSpec((1, hd), lambda i: (0, 0)),
                  pl.BlockSpec((ns, hd), lambda i: (0, 0)),
                  pl.BlockSpec((ns, hd), lambda i: (0, 0))],
        out_specs=pl.BlockSpec((1, ns, hd), lambda i: (i, 0, 0)),
        out_shape=jax.ShapeDtypeStruct((n, ns, hd), BF16),
        compiler_params=_params("parallel"),
        name="compress_key" if is_key else "compress_value",
    )(sub, pe2, w1s, w2.astype(BF16), g.reshape(1, hd), cos, sin_signed)


def _nsa_kernel(q_ref, gate_ref, kc_ref, vct_ref, ks_ref, vst_ref, kw_ref, vwt_ref, gq_ref, cos_ref, sin_ref,
                wimp_ref, y_ref, s_c, e_c, s_w, e_w, s_a, s_b, e_a, e_b, *, seq):
    tq = q_ref.shape[1]
    gp = kc_ref.shape[1]
    hd = N_HEAD_DIM
    nb = seq // SEL_BLOCK
    top = min(SEL_TOP, nb)
    ncol = N_REP * tq
    t0 = pl.program_id(2) * tq

    def lanes(parts):
        return jnp.concatenate(parts, axis=1)

    def head_cols(a, r):
        return a[:, r * tq:(r + 1) * tq]

    def with_ones(v):
        n = v.shape[1]
        tail = jnp.where(lax.broadcasted_iota(jnp.int32, (V_AUG_ROWS, n), 0) == 0, 1.0, 0.0).astype(v.dtype)
        return jnp.concatenate([v, tail], axis=0)

    cb = min(ncol, NSA_CB)

    def chunk_scores(src, c, j, valid_fn):
        sc = src[c * NSA_RC:(c + 1) * NSA_RC, j * cb:(j + 1) * cb]
        if valid_fn is None:
            return sc
        valid = valid_fn(c * NSA_RC + lax.broadcasted_iota(jnp.int32, (NSA_RC, 1), 0))
        return lanes([jnp.where(valid, head_cols(sc, r), NEG_BIG) for r in range(cb // tq)])

    def put_scores(dst, s):
        dst[...] = s
        return jnp.max(s.reshape(s.shape[0] // 8, 8, ncol), axis=0)

    def col_max(src, rows, valid_fn):
        out = []
        for j in range(ncol // cb):
            mx = jnp.full((8, cb), NEG_BIG, F32)
            for c in range(rows // NSA_RC):
                sc = chunk_scores(src, c, j, valid_fn)
                mx = jnp.maximum(mx, jnp.max(sc.reshape(NSA_RC // 8, 8, cb), axis=0))
            out.append(jnp.max(mx, axis=0, keepdims=True))
        return lanes(out)

    def put_weights(src, dst, rows, valid_fn, m):
        for j in range(ncol // cb):
            mj = m[:, j * cb:(j + 1) * cb]
            for c in range(rows // NSA_RC):
                sc = chunk_scores(src, c, j, valid_fn)
                dst[c * NSA_RC:(c + 1) * NSA_RC, j * cb:(j + 1) * cb] = (
                    jnp.exp2((sc - mj).astype(BF16)).astype(dst.dtype))

    cos = lanes([cos_ref[...]] * N_REP)
    sin = lanes([sin_ref[...]] * N_REP)
    t_q = t0 + lax.broadcasted_iota(jnp.int32, (1, tq), 1)
    tcol = lanes([t_q] * N_REP)
    nc = kc_ref.shape[2]
    valid_c = lambda n: n * CMP_STRIDE + (CMP_LEN - 1) <= t_q
    span = WINDOW + tq
    ws0 = pl.multiple_of(jnp.maximum(t0 - WINDOW, 0), tq)
    valid_w = lambda k: (ws0 + k <= t_q) & (ws0 + k > t_q - WINDOW)
    blk = lax.broadcasted_iota(jnp.int32, (nb, tq), 0)
    cur = (t0 + lax.broadcasted_iota(jnp.int32, (nb, tq), 1)) // SEL_BLOCK
    causal_b = blk <= cur
    forced = (blk == 0) | (blk == cur) | (blk == cur - 1)
    taken = -3.0e38

    def prefix(u):
        q0 = u * N_REP * hd
        q = lanes([q_ref[q0 + r * hd:q0 + (r + 1) * hd, :].astype(F32) for r in range(N_REP)])
        qn = q * lax.rsqrt(jnp.mean(q * q, axis=0, keepdims=True) + EPS) * gq_ref[...]
        rot = jnp.concatenate([qn[hd // 2:], qn[:hd // 2]], axis=0)
        qr = ((qn * cos + rot * sin) * (hd ** -0.5 * LOG2_E)).astype(BF16)

        s_c.at[u][...] = jnp.dot(kc_ref[0, u], qr, preferred_element_type=F32)
        put_weights(s_c.at[u], e_c.at[u], nc, valid_c, col_max(s_c.at[u], nc, valid_c))
        both = jnp.dot(jnp.concatenate([vct_ref[0, u], wimp_ref[...]], axis=0), e_c.at[u][...],
                       preferred_element_type=F32)
        inv_c = jnp.where(tcol >= CMP_LEN - 1, 1.0 / both[hd:hd + 1], 0.0)
        o_c = both[0:hd] * inv_c

        imp_h = both[hd + V_AUG_ROWS:] * inv_c
        imp = head_cols(imp_h, 0)
        for r in range(1, N_REP):
            imp = imp + head_cols(imp_h, r)

        s_w.at[u][...] = jnp.dot(kw_ref[0, u, pl.ds(ws0, span), :], qr, preferred_element_type=F32)
        m_w = col_max(s_w.at[u], span, valid_w)
        score = jnp.where(forced, FORCE_SCORE, jnp.where(causal_b, imp, -1.0))
        return qr, o_c, m_w, score

    def pick_by_value(sc):
        for _ in range(top - 2):
            sc = jnp.where(sc == jnp.max(sc, axis=0, keepdims=True), taken, sc)
        return sc

    def pick_by_value_then_index(sc):
        for _ in range(top):
            mx = jnp.max(sc, axis=0, keepdims=True)
            idx = jnp.min(jnp.where(sc == mx, blk, nb), axis=0, keepdims=True)
            sc = jnp.where(blk == idx, taken, sc)
        return sc

    pre = [prefix(u) for u in range(gp)]
    fast = [pick_by_value(p[3]) for p in pre]
    n_want = jnp.minimum(cur[0:1, :] + 1, top).astype(F32)
    miss = [jnp.abs(jnp.sum(jnp.where((f == taken) & causal_b, 1.0, 0.0), axis=0, keepdims=True) - n_want)
            for f in fast]
    tied = jnp.max(functools.reduce(jnp.maximum, miss)) > 0.0
    picked = lax.cond(tied, lambda: tuple(pick_by_value_then_index(p[3]) for p in pre), lambda: tuple(fast))
    q_aug = []
    for u in range(gp):
        bias = jnp.where((picked[u] == taken) & causal_b, 0.0, BLOCK_BIAS)
        q_aug.append(jnp.concatenate([lanes([bias.astype(BF16)] * N_REP), jnp.clip(pre[u][0], -F8_MAX, F8_MAX)],
                                     axis=0).astype(F8))

    tk = min(NSA_TK, seq)
    last = (t0 + tq - 1) // tk
    s_bufs, e_bufs = (s_a, s_b), (e_a, e_b)

    def scores(u, t, par):
        ks0 = pl.multiple_of(t * tk, tk)
        return put_scores(s_bufs[par].at[u],
                          jnp.dot(ks_ref[0, u, pl.ds(ks0, tk), :], q_aug[u], preferred_element_type=F32))

    def weights(u, t, par, m, mx, masked):
        valid_s = (lambda k: t * tk + k <= t_q) if masked else None
        m_new = jnp.maximum(m, jnp.max(mx, axis=0, keepdims=True))
        put_weights(s_bufs[par].at[u], e_bufs[par].at[u], tk, valid_s, m_new - SEL_WEIGHT_SHIFT)
        return m_new, jnp.exp2(m - m_new)

    def values(u, t, par, alpha, acc):
        ks0 = pl.multiple_of(jnp.maximum(t, 0) * tk, tk)
        v = jnp.clip(vst_ref[u * hd:(u + 1) * hd, pl.ds(ks0, tk)], -F8_MAX, F8_MAX)
        return alpha * acc + jnp.dot(with_ones(v).astype(F8), e_bufs[par].at[u][...],
                                     preferred_element_type=F32)

    def step(t, par, carries, masked, more):
        mx_next = [scores(u, t + 1, 1 - par) if more else c[3] for u, c in enumerate(carries)]
        acc = [values(u, t - 1, 1 - par, c[1], c[2]) for u, c in enumerate(carries)]
        ma = [weights(u, t, par, c[0], c[3], masked) for u, c in enumerate(carries)]
        return tuple((ma[u][0], ma[u][1], acc[u], mx_next[u]) for u in range(gp))

    def pair_body(j, carries):
        return step(2 * j + 1, 1, step(2 * j, 0, carries, False, True), False, True)

    def tail_even(carries):
        carries = step(last, 0, carries, True, False)
        return tuple(values(u, last, 0, c[1], c[2]) for u, c in enumerate(carries))

    def tail_odd(carries):
        carries = step(last, 1, step(last - 1, 0, carries, False, True), True, False)
        return tuple(values(u, last, 1, c[1], c[2]) for u, c in enumerate(carries))

    mx0 = [scores(u, 0, 0) for u in range(gp)]
    e_b[...] = jnp.zeros(e_b.shape, e_b.dtype)
    o_w = []
    for u in range(gp):
        put_weights(s_w.at[u], e_w.at[u], span, valid_w, pre[u][2])
        ow = jnp.dot(with_ones(vwt_ref[u * hd:(u + 1) * hd, pl.ds(ws0, span)]), e_w.at[u][...],
                     preferred_element_type=F32)
        o_w.append(ow[0:hd] / ow[hd:hd + 1])
    carries = tuple((jnp.full((1, ncol), NEG_BIG, F32), jnp.ones((1, ncol), F32),
                     jnp.zeros((hd + V_AUG_ROWS, ncol), F32), mx0[u]) for u in range(gp))
    carries = lax.fori_loop(0, last // 2, pair_body, carries)
    accs = lax.cond(last % 2 == 0, tail_even, tail_odd, carries)

    for u in range(gp):
        o_s = accs[u][0:hd] / accs[u][hd:hd + 1]
        g = _sigmoid(gate_ref[0, u])
        gc, gs, gw = (lanes([g[j, r:r + 1, :] for r in range(N_REP)]) for j in range(3))
        out = gc * pre[u][1] + gs * o_s + gw * o_w[u]
        y0 = u * N_REP * hd
        for r in range(N_REP):
            y_ref[y0 + r * hd:y0 + (r + 1) * hd, :] = head_cols(out, r).astype(y_ref.dtype)


def _nsa(zt, gates, kc, vct, ks_aug, kw, gq, cos_t, sin_t, wimp_t, batch, seq):
    g, hd, gp = N_GROUPS, N_HEAD_DIM, NSA_GROUPS
    tq = min(NSA_TQ, seq)
    tk = min(NSA_TK, seq)
    nq = seq // tq
    ncol = N_REP * tq
    nc = kc.shape[2]
    nb = seq // SEL_BLOCK
    full = lambda *shape: pl.BlockSpec((1, gp) + shape, lambda bi, gi, qi: (bi, gi) + (0,) * len(shape))
    vrow = lambda row0: pl.BlockSpec((gp * hd, seq),
                                     functools.partial(lambda bi, gi, qi, o: (o + gi, bi), o=row0 // (gp * hd)))
    qrows = gp * N_REP * hd
    qspec = pl.BlockSpec((qrows, tq), lambda bi, gi, qi: (ZT_Q // qrows + gi, bi * nq + qi))
    return pl.pallas_call(
        functools.partial(_nsa_kernel, seq=seq),
        grid=(batch, g // gp, nq),
        in_specs=[qspec,
                  pl.BlockSpec((1, gp, 3, N_REP, tq), lambda bi, gi, qi: (bi, gi, 0, 0, qi)),
                  full(nc, hd), full(hd + V_AUG_ROWS, nc), full(seq, nb + hd), vrow(ZT_VS), full(seq, hd),
                  vrow(ZT_VW),
                  pl.BlockSpec((hd, 1), lambda bi, gi, qi: (0, 0)),
                  pl.BlockSpec((hd, tq), lambda bi, gi, qi: (0, qi)),
                  pl.BlockSpec((hd, tq), lambda bi, gi, qi: (0, qi)),
                  pl.BlockSpec((nb, nc), lambda bi, gi, qi: (0, 0))],
        out_specs=pl.BlockSpec((qrows, tq), lambda bi, gi, qi: (gi, bi * nq + qi)),
        out_shape=jax.ShapeDtypeStruct((N_HEADS * hd, batch * seq), BF16),
        scratch_shapes=[pltpu.VMEM((gp, nc, ncol), F32), pltpu.VMEM((gp, nc, ncol), BF16),
                        pltpu.VMEM((gp, WINDOW + tq, ncol), F32), pltpu.VMEM((gp, WINDOW + tq, ncol), BF16),
                        pltpu.VMEM((gp, tk, ncol), F32), pltpu.VMEM((gp, tk, ncol), F32),
                        pltpu.VMEM((gp, tk, ncol), F8), pltpu.VMEM((gp, tk, ncol), F8)],
        compiler_params=_params("parallel", "parallel", "arbitrary"),
        name="nsa_attention",
    )(zt, gates, kc, vct, ks_aug, zt, kw, zt, gq.reshape(hd, 1), cos_t, sin_t, wimp_t)


def _rope_tables(pos):
    half = N_HEAD_DIM // 2
    inv = jnp.power(ROPE_THETA, -jnp.arange(half, dtype=F32) * 2.0 / N_HEAD_DIM)
    ang = pos.astype(F32)[:, None] * inv[None, :]
    cos, sin = jnp.cos(ang), jnp.sin(ang)
    return jnp.concatenate([cos, cos], axis=-1), jnp.concatenate([-sin, sin], axis=-1)


def _importance_matrix(nb, nc_pad):
    per = SEL_BLOCK // CMP_STRIDE
    j = np.arange(nb)[:, None]
    n = np.arange(nc_pad)[None, :]
    w = np.zeros((nb, nc_pad), np.float32)
    for d in range(CMP_LEN // CMP_STRIDE):
        w += ((n + d >= per * j) & (n + d <= per * j + per - 1)).astype(np.float32)
    return jnp.asarray(w, BF16)


def _split_w_in(w_in):
    d = w_in.shape[0]
    sizes = (2048, 1024, 1024, 4, 4, 1024, 256, 256, 256, 256, 256, 256, 48, 1024, 2048, 2048, 2048)
    offs = np.concatenate([[0], np.cumsum(sizes)])
    part = lambda i: w_in[:, offs[i]:offs[i + 1]]
    tok = jnp.concatenate([part(i) for i in (0, 1, 2, 13, 14, 15, 16, 6, 7, 8, 10)], axis=1).astype(BF16)
    feat = jnp.concatenate([part(i) for i in (5, 9, 11)], axis=1).T.astype(BF16)
    small = jnp.concatenate([part(3), part(4), part(12), jnp.zeros((d, 128 - 56), w_in.dtype)], axis=1).astype(BF16)
    return tok, feat, small


def _layer(x, mem, layer, big, norm_mix_g, norm_mem_g, norm_ffn_g, w_in, m_conv_w, m_i_bias, m_f_bias, m_norm_g,
           n_q_norm_g, n_kc_norm_g, n_ks_norm_g, n_kw_norm_g, n_cmp_pe_k, n_cmp_w1_k, n_cmp_w2_k,
           n_cmp_pe_v, n_cmp_w1_v, n_cmp_w2_v, c_q_norm_g, c_k_norm_g, w_mem_k, w_mem_v):
    B, S, D = x.shape
    M = B * S
    G, R, hd = N_GROUPS, N_REP, N_HEAD_DIM
    x2 = x.reshape(M, D)

    w_tok, w_feat, w_small = _split_w_in(w_in)
    z, zs, h = _in_proj(x2, norm_mix_g, w_tok, w_small)
    zt = _mm_feature_major(w_feat, h)
    z3 = z.reshape(B, S, Z_WIDTH)
    zs3 = zs.reshape(B, S, 128)

    gif = jnp.stack([zs3[..., 0:4], zs3[..., 4:8]], axis=-1)
    gcol = gif.transpose(0, 2, 1, 3)
    grow = gif.transpose(0, 2, 3, 1)
    gbias = jnp.stack([m_i_bias, m_f_bias], axis=-1).reshape(M_HEADS, 1, 2).astype(F32)
    y_a = _mlstm(z3, gcol, grow, gbias, m_conv_w, m_norm_g, B, S)

    pos = jnp.arange(S, dtype=jnp.int32)
    cos, sin_s = _rope_tables(pos)
    nc_pad = S // CMP_STRIDE
    cmp_end = jnp.arange(nc_pad, dtype=jnp.int32) * CMP_STRIDE + CMP_LEN - 1
    cos_c, sin_c = _rope_tables(cmp_end)
    nb = S // SEL_BLOCK

    def head_major(col0):
        u = z3[..., col0:col0 + G * hd].reshape(B, S, G, hd)
        return u.transpose(0, 2, 1, 3).reshape(B * G, S, hd)

    kc = _compress(head_major(Z_KC), n_cmp_pe_k, n_cmp_w1_k, n_cmp_w2_k, n_kc_norm_g, cos_c, sin_c, True)
    vc = _compress(head_major(Z_VC), n_cmp_pe_v, n_cmp_w1_v, n_cmp_w2_v, n_kc_norm_g, cos_c, sin_c, False)
    vct = vc.reshape(B, G, nc_pad, hd).transpose(0, 1, 3, 2)
    ones_rows = jnp.concatenate([jnp.ones((B, G, 1, nc_pad), BF16), jnp.zeros((B, G, V_AUG_ROWS - 1, nc_pad), BF16)],
                                axis=2)
    vct = jnp.concatenate([vct, ones_rows], axis=2)
    cos2, sin2 = jnp.concatenate([cos, cos], axis=1), jnp.concatenate([sin_s, sin_s], axis=1)
    ks_aug = _knorm_rope(z3, Z_KS, n_ks_norm_g, cos2, sin2, nb)
    kw = _knorm_rope(z3, Z_KW, n_kw_norm_g, cos2, sin2, 0)
    gates = zs3[..., 8:8 + 3 * N_HEADS].reshape(B, S, G, R, 3).transpose(0, 2, 4, 3, 1)
    y_bt = _nsa(zt, gates, kc.reshape(B, G, nc_pad, hd), vct, ks_aug, kw, n_q_norm_g, cos.T, sin_s.T,
                _importance_matrix(nb, nc_pad), B, S)

    mlen = mem.shape[1]
    mem_n = _rmsnorm(mem.reshape(B * mlen, D), norm_mem_g)
    k_mem = _mm(mem_n, w_mem_k.astype(BF16), BF16).reshape(B, mlen, -1)
    v_mem = _mm(mem_n, w_mem_v.astype(BF16), BF16).reshape(B, mlen, -1)
    y_c = _xattn(z3, k_mem, v_mem, c_q_norm_g, c_k_norm_g, B, S)

    mix = _mix(y_a.reshape(M, -1), y_bt, y_c.reshape(M, -1), big["w_up_a"], big["w_up_b"], big["w_up_c"], layer,
               z, Z_GATES)
    x2 = _mm_residual(mix, big["w_out"], layer, x2, tm=512, tn=D)

    act = _swiglu(x2, norm_ffn_g, big["w_ffn_gate"], big["w_ffn_up"], layer)
    x2 = _mm_residual(act, big["w_ffn_down"], layer, x2, tm=1024, tn=512)
    return x2.reshape(B, S, D)


def kernel(x, mem, norm_mix_g, norm_mem_g, norm_ffn_g, w_in, m_conv_w, m_i_bias, m_f_bias, m_norm_g, n_q_norm_g, n_kc_norm_g, n_ks_norm_g, n_kw_norm_g, n_cmp_pe_k, n_cmp_w1_k, n_cmp_w2_k, n_cmp_pe_v, n_cmp_w1_v, n_cmp_w2_v, c_q_norm_g, c_k_norm_g, w_mem_k, w_mem_v, w_up_a, w_up_b, w_up_c, w_out, w_ffn_gate, w_ffn_up, w_ffn_down):
    per_layer = (norm_mix_g, norm_mem_g, norm_ffn_g, w_in, m_conv_w, m_i_bias, m_f_bias, m_norm_g, n_q_norm_g,
                 n_kc_norm_g, n_ks_norm_g, n_kw_norm_g, n_cmp_pe_k, n_cmp_w1_k, n_cmp_w2_k, n_cmp_pe_v, n_cmp_w1_v,
                 n_cmp_w2_v, c_q_norm_g, c_k_norm_g, w_mem_k, w_mem_v)
    big = {name: w.astype(BF16) for name, w in (
        ("w_up_a", w_up_a), ("w_up_b", w_up_b), ("w_up_c", w_up_c), ("w_out", w_out),
        ("w_ffn_gate", w_ffn_gate), ("w_ffn_up", w_ffn_up), ("w_ffn_down", w_ffn_down))}
    for layer in range(w_in.shape[0]):
        x = _layer(x, mem, layer, big, *(p[layer] for p in per_layer))
    return x
```

```python
import functools

import numpy as np
import jax
import jax.numpy as jnp
from jax import lax
from jax.experimental import pallas as pl
from jax.experimental.pallas import tpu as pltpu

F32 = jnp.float32
BF16 = jnp.bfloat16

EPS = 1e-6
ROPE_THETA = 10000.0
M_HEADS = 4
M_HEAD_DIM = 256
M_CONV = 4
N_HEADS = 16
N_GROUPS = 4
N_REP = N_HEADS // N_GROUPS
N_HEAD_DIM = 64
CMP_LEN = 32
CMP_STRIDE = 16
CMP_HIDDEN = 128
SEL_BLOCK = 64
SEL_TOP = 16
WINDOW = 512
FORCE_SCORE = 1.0e4
C_HEADS = 4
C_HEAD_DIM = 256

LOG2_E = 1.4426950408889634
V_AUG_ROWS = 16
NEG_BIG = -1.0e30
F8 = jnp.float8_e4m3fn
F8_MAX = 448.0
BLOCK_BIAS = -F8_MAX
SEL_WEIGHT_SHIFT = 8.0
V7X_VMEM_BYTES = 64 * 1024 * 1024
VMEM_LIMIT = V7X_VMEM_BYTES - 8 * 1024 * 1024

MLSTM_CHUNK = 256
MLSTM_BATCH_ROWS = 2
NSA_TQ = 256
NSA_TK = 512
NSA_GROUPS = 2
NSA_RC = 32
NSA_CB = 1024

Z_MQK, Z_MV, Z_MO, Z_CQ, Z_GATES, Z_KC, Z_VC, Z_KS, Z_KW, Z_WIDTH = (
    0, 2048, 3072, 4096, 5120, 11264, 11520, 11776, 12032, 12288)
ZT_Q, ZT_VS, ZT_VW, ZT_ROWS = 0, 1024, 1280, 1536


def _params(*sem):
    return pltpu.CompilerParams(dimension_semantics=sem, vmem_limit_bytes=VMEM_LIMIT)


def _sigmoid(x):
    return 1.0 / (1.0 + jnp.exp(-x))


def _silu(x):
    return x * _sigmoid(x)


def _log_sigmoid(x):
    return jnp.minimum(x, 0.0) - jnp.log1p(jnp.exp(-jnp.abs(x)))


def _rms_rows(x, g):
    return x * lax.rsqrt(jnp.mean(x * x, axis=-1, keepdims=True) + EPS) * g


def _rmsnorm_kernel(x_ref, g_ref, o_ref):
    o_ref[...] = _rms_rows(x_ref[...], g_ref[...]).astype(o_ref.dtype)


def _rmsnorm(x2d, g, tm=512):
    m, d = x2d.shape
    tm = min(tm, m)
    return pl.pallas_call(
        _rmsnorm_kernel,
        grid=(m // tm,),
        in_specs=[pl.BlockSpec((tm, d), lambda i: (i, 0)), pl.BlockSpec((1, d), lambda i: (0, 0))],
        out_specs=pl.BlockSpec((tm, d), lambda i: (i, 0)),
        out_shape=jax.ShapeDtypeStruct((m, d), BF16),
        compiler_params=_params("parallel"),
        name="rmsnorm",
    )(x2d, g.reshape(1, d))


def _mm_kernel(a_ref, w_ref, o_ref):
    o_ref[...] = jnp.dot(a_ref[...], w_ref[...], preferred_element_type=F32).astype(o_ref.dtype)


def _mm(a, w, out_dtype, tm=1024, tn=512):
    m, k = a.shape
    n = w.shape[1]
    tm, tn = min(tm, m), min(tn, n)
    return pl.pallas_call(
        _mm_kernel,
        grid=(m // tm, n // tn),
        in_specs=[pl.BlockSpec((tm, k), lambda i, j: (i, 0)), pl.BlockSpec((k, tn), lambda i, j: (0, j))],
        out_specs=pl.BlockSpec((tm, tn), lambda i, j: (i, j)),
        out_shape=jax.ShapeDtypeStruct((m, n), out_dtype),
        compiler_params=_params("parallel", "arbitrary"),
        name="matmul",
    )(a, w)


def _in_proj_kernel(x_ref, g_ref, w_ref, ws_ref, z_ref, zs_ref, h_ref):
    @pl.when(pl.program_id(1) == 0)
    def _():
        h = _rms_rows(x_ref[...], g_ref[...]).astype(BF16)
        h_ref[...] = h
        zs_ref[...] = jnp.dot(h, ws_ref[...], preferred_element_type=F32)

    z_ref[...] = jnp.dot(h_ref[...], w_ref[...], preferred_element_type=F32).astype(z_ref.dtype)


def _in_proj(x2d, g, w, w_small, tm=1024, tn=1024):
    m, d = x2d.shape
    n, ns = w.shape[1], w_small.shape[1]
    tm, tn = min(tm, m), min(tn, n)
    return pl.pallas_call(
        _in_proj_kernel,
        grid=(m // tm, n // tn),
        in_specs=[pl.BlockSpec((tm, d), lambda i, j: (i, 0)), pl.BlockSpec((1, d), lambda i, j: (0, 0)),
                  pl.BlockSpec((d, tn), lambda i, j: (0, j)), pl.BlockSpec((d, ns), lambda i, j: (0, 0))],
        out_specs=[pl.BlockSpec((tm, tn), lambda i, j: (i, j)), pl.BlockSpec((tm, ns), lambda i, j: (i, 0)),
                   pl.BlockSpec((tm, d), lambda i, j: (i, 0))],
        out_shape=[jax.ShapeDtypeStruct((m, n), BF16), jax.ShapeDtypeStruct((m, ns), F32),
                   jax.ShapeDtypeStruct((m, d), BF16)],
        compiler_params=_params("parallel", "arbitrary"),
        name="in_proj",
    )(x2d, g.reshape(1, d), w, w_small)


def _mm_nt_kernel(w_ref, h_ref, o_ref):
    o_ref[...] = lax.dot_general(w_ref[...], h_ref[...], (((1,), (1,)), ((), ())),
                                 preferred_element_type=F32).astype(o_ref.dtype)


def _mm_feature_major(w_t, h, tm=1024, tn=1536):
    n, k = w_t.shape
    m = h.shape[0]
    tm, tn = min(tm, m), min(tn, n)
    return pl.pallas_call(
        _mm_nt_kernel,
        grid=(m // tm, n // tn),
        in_specs=[pl.BlockSpec((tn, k), lambda i, j: (j, 0)), pl.BlockSpec((tm, k), lambda i, j: (i, 0))],
        out_specs=pl.BlockSpec((tn, tm), lambda i, j: (j, i)),
        out_shape=jax.ShapeDtypeStruct((n, m), BF16),
        compiler_params=_params("parallel", "arbitrary"),
        name="in_proj_feature_major",
    )(w_t, h)


def _mm_res_kernel(a_ref, w_ref, x_ref, o_ref):
    o_ref[...] = x_ref[...] + jnp.dot(a_ref[...], w_ref[...], preferred_element_type=F32)


def _layer_block(layer, block, index_map):
    return pl.BlockSpec((None,) + block, lambda *idx: (layer,) + index_map(*idx))


def _mm_residual(a, w, layer, x, tm=512, tn=512):
    m, k = a.shape
    n = w.shape[2]
    tm, tn = min(tm, m), min(tn, n)
    return pl.pallas_call(
        _mm_res_kernel,
        grid=(m // tm, n // tn),
        in_specs=[pl.BlockSpec((tm, k), lambda i, j: (i, 0)), _layer_block(layer, (k, tn), lambda i, j: (0, j)),
                  pl.BlockSpec((tm, tn), lambda i, j: (i, j))],
        out_specs=pl.BlockSpec((tm, tn), lambda i, j: (i, j)),
        out_shape=jax.ShapeDtypeStruct((m, n), F32),
        compiler_params=_params("parallel", "arbitrary"),
        name="matmul_residual",
    )(a, w, x)


def _swiglu_kernel(x_ref, g_ref, wg_ref, wu_ref, o_ref, h_scr):
    @pl.when(pl.program_id(1) == 0)
    def _():
        h_scr[...] = _rms_rows(x_ref[...], g_ref[...]).astype(BF16)

    h = h_scr[...]
    gate = jnp.dot(h, wg_ref[...], preferred_element_type=F32)
    up = jnp.dot(h, wu_ref[...], preferred_element_type=F32)
    o_ref[...] = (_silu(gate) * up).astype(o_ref.dtype)


def _swiglu(x2d, g, wg, wu, layer, tm=1024, tn=512):
    m, d = x2d.shape
    n = wg.shape[2]
    tm, tn = min(tm, m), min(tn, n)
    return pl.pallas_call(
        _swiglu_kernel,
        grid=(m // tm, n // tn),
        in_specs=[pl.BlockSpec((tm, d), lambda i, j: (i, 0)), pl.BlockSpec((1, d), lambda i, j: (0, 0)),
                  _layer_block(layer, (d, tn), lambda i, j: (0, j)), _layer_block(layer, (d, tn), lambda i, j: (0, j))],
        out_specs=pl.BlockSpec((tm, tn), lambda i, j: (i, j)),
        out_shape=jax.ShapeDtypeStruct((m, n), BF16),
        scratch_shapes=[pltpu.VMEM((tm, d), BF16)],
        compiler_params=_params("parallel", "arbitrary"),
        name="swiglu",
    )(x2d, g.reshape(1, d), wg, wu)


def _mix_kernel(ya_ref, ybt_ref, yc_ref, wa_ref, wb_ref, wc_ref, ga_ref, gb_ref, gc_ref, o_ref):
    def gated(g_ref, prod):
        return _sigmoid(g_ref[...].astype(F32)) * prod

    a = jnp.dot(ya_ref[...], wa_ref[...], preferred_element_type=F32)
    b = lax.dot_general(ybt_ref[...], wb_ref[...], (((0,), (0,)), ((), ())), preferred_element_type=F32)
    c = jnp.dot(yc_ref[...], wc_ref[...], preferred_element_type=F32)
    o_ref[...] = (gated(ga_ref, a) + gated(gb_ref, b) + gated(gc_ref, c)).astype(o_ref.dtype)


def _mix(ya, ybt, yc, wa, wb, wc, layer, z, gate_col0, tm=1024, tn=512):
    m, k = ya.shape
    n = wa.shape[2]
    tm, tn = min(tm, m), min(tn, n)
    gofs = [(gate_col0 + b * n) // tn for b in range(3)]
    y_spec = pl.BlockSpec((tm, k), lambda i, j: (i, 0))
    w_spec = _layer_block(layer, (k, tn), lambda i, j: (0, j))
    g_specs = [pl.BlockSpec((tm, tn), functools.partial(lambda i, j, o: (i, o + j), o=o)) for o in gofs]
    return pl.pallas_call(
        _mix_kernel,
        grid=(m // tm, n // tn),
        in_specs=[y_spec, pl.BlockSpec((k, tm), lambda i, j: (0, i)), y_spec, w_spec, w_spec, w_spec] + g_specs,
        out_specs=pl.BlockSpec((tm, tn), lambda i, j: (i, j)),
        out_shape=jax.ShapeDtypeStruct((m, n), BF16),
        compiler_params=_params("parallel", "arbitrary"),
        name="gated_mix",
    )(ya, ybt, yc, wa, wb, wc, z, z, z)


def _mlstm_kernel(uq_ref, uk_ref, v_ref, o_ref, gcol_ref, grow_ref, bias_ref, cwq_ref, cwk_ref, ng_ref, shift_ref,
                  y_ref, c_scr, n_scr, m_scr, qtail, ktail):
    c = pl.program_id(1)
    nbp = uq_ref.shape[0]
    L = uq_ref.shape[1]
    hd = M_HEAD_DIM

    @pl.when(c == 0)
    def _():
        c_scr[...] = jnp.zeros_like(c_scr)
        n_scr[...] = jnp.zeros_like(n_scr)
        m_scr[...] = jnp.zeros_like(m_scr)
        qtail[...] = jnp.zeros_like(qtail)
        ktail[...] = jnp.zeros_like(ktail)

    def conv_silu(u_ref, bb, tail_all, cw_ref):
        tail = tail_all.at[bb * 8:(bb + 1) * 8, :]
        u = u_ref[bb]
        cw = cw_ref[...]
        cwb = cw.astype(BF16)
        x = jnp.concatenate([u * cwb[j:j + 1, :] for j in range(M_CONV)], axis=0)
        y = jnp.dot(shift_ref[...], x, preferred_element_type=F32)
        prev = tail[...]
        row = lax.broadcasted_iota(jnp.int32, prev.shape, 0)
        edge = jnp.zeros(prev.shape, F32)
        for t in range(M_CONV - 1):
            acc = cw[0:1, :] * prev[8 - (M_CONV - 1) + t:8 - (M_CONV - 1) + t + 1, :]
            for j in range(1, M_CONV - 1 - t):
                acc = acc + cw[j:j + 1, :] * prev[8 - (M_CONV - 1) + t + j:8 - (M_CONV - 1) + t + j + 1, :]
            edge = jnp.where(row == t, acc, edge)
        tail[...] = u[L - 8:L, :].astype(F32)
        return _silu(jnp.concatenate([y[0:8] + edge, y[8:]], axis=0))

    q_alls = [conv_silu(uq_ref, bb, qtail, cwq_ref) for bb in range(nbp)]
    k_alls = [conv_silu(uk_ref, bb, ktail, cwk_ref) * (hd ** -0.5) for bb in range(nbp)]
    row = lax.broadcasted_iota(jnp.int32, (L, L), 0)
    col = lax.broadcasted_iota(jnp.int32, (L, L), 1)
    tri = col <= row
    triu = jnp.where(row <= col, 1.0, 0.0).astype(BF16)

    for sh in range(nbp * M_HEADS):
        bb, hh = divmod(sh, M_HEADS)
        sl = slice(hh * hd, (hh + 1) * hd)
        q, k = q_alls[bb][:, sl], k_alls[bb][:, sl]
        qb = q.astype(BF16)
        kb = k.astype(BF16)
        vb = v_ref[bb, :, sl]

        gcol = gcol_ref[bb, hh]
        grow = grow_ref[bb, hh]
        bias = bias_ref[hh]
        ic = gcol[:, 0:1] + bias[:, 0:1]
        ir = grow[0:1, :] + bias[:, 0:1]
        fr = _log_sigmoid(grow[1:2, :] + bias[:, 1:2])

        b_col = jnp.sum(jnp.where(tri, fr, 0.0), axis=1, keepdims=True)
        f1 = fr.astype(BF16)
        r1 = fr - f1.astype(F32)
        f2 = r1.astype(BF16)
        f3 = (r1 - f2.astype(F32)).astype(BF16)
        terms = jnp.concatenate([f1, f2, f3, jnp.zeros((13, L), BF16)], axis=0)
        pre = jnp.dot(terms, triu, preferred_element_type=F32)
        b_row = pre[0:1] + pre[1:2] + pre[2:3]
        m_prev = m_scr[sh, 0:1, 0:1]

        log_d = jnp.where(tri, b_col - b_row + ir, NEG_BIG)
        log_inter = b_col + m_prev
        m_t = jnp.maximum(log_inter, jnp.max(log_d, axis=1, keepdims=True))
        w_intra = jnp.exp(log_d - m_t)
        w_inter = jnp.exp(log_inter - m_t)

        s = lax.dot_general(qb, kb, (((1,), (1,)), ((), ())), preferred_element_type=F32) * w_intra
        c_state = c_scr[sh]
        n_state = n_scr[sh, 0:1, :]
        num = (jnp.dot(s.astype(BF16), vb, preferred_element_type=F32)
               + w_inter * jnp.dot(qb, c_state.astype(BF16), preferred_element_type=F32))
        den = jnp.sum(s, axis=1, keepdims=True) + w_inter * jnp.sum(q * n_state, axis=1, keepdims=True)
        h = num / jnp.maximum(jnp.abs(den), jnp.exp(-m_t))

        b_last = b_row[:, L - 1:L]
        log_g = b_last - b_col + ic
        m_new = jnp.maximum(b_last + m_prev, jnp.max(log_g, axis=0, keepdims=True))
        decay = jnp.exp(b_last + m_prev - m_new)
        wk = k * jnp.exp(log_g - m_new)
        c_scr[sh] = decay * c_state + lax.dot_general(wk.astype(BF16), vb, (((0,), (0,)), ((), ())),
                                                      preferred_element_type=F32)
        n_scr[sh, 0:1, :] = decay * n_state + jnp.sum(wk, axis=0, keepdims=True)
        m_scr[sh] = jnp.broadcast_to(m_new, m_scr.shape[1:])

        hn = _rms_rows(h, ng_ref[hh])
        y_ref[bb, :, sl] = (_sigmoid(o_ref[bb, :, sl].astype(F32)) * hn).astype(y_ref.dtype)


def _conv_shift_matrix(L):
    t = np.arange(L)[:, None]
    r = np.arange(L)[None, :]
    return jnp.asarray(np.concatenate([(r == t - (M_CONV - 1) + j) for j in range(M_CONV)], axis=1), BF16)


def _mlstm(z3, gcol, grow, gbias, conv_w, norm_g, batch, seq):
    L = min(MLSTM_CHUNK, seq)
    hd = M_HEAD_DIM
    nh = M_HEADS
    w = nh * hd

    nbp = MLSTM_BATCH_ROWS if batch % MLSTM_BATCH_ROWS == 0 else 1

    def zspec(col0):
        return pl.BlockSpec((nbp, L, w), functools.partial(lambda b, c, o: (b, c, o), o=col0 // w))

    return pl.pallas_call(
        _mlstm_kernel,
        grid=(batch // nbp, seq // L),
        in_specs=[zspec(Z_MQK), zspec(Z_MQK + w), zspec(Z_MV), zspec(Z_MO),
                  pl.BlockSpec((nbp, nh, L, 2), lambda b, c: (b, 0, c, 0)),
                  pl.BlockSpec((nbp, nh, 2, L), lambda b, c: (b, 0, 0, c)),
                  pl.BlockSpec((nh, 1, 2), lambda b, c: (0, 0, 0)),
                  pl.BlockSpec((M_CONV, w), lambda b, c: (0, 0)),
                  pl.BlockSpec((M_CONV, w), lambda b, c: (0, 1)),
                  pl.BlockSpec((nh, 1, hd), lambda b, c: (0, 0, 0)),
                  pl.BlockSpec((L, M_CONV * L), lambda b, c: (0, 0))],
        out_specs=pl.BlockSpec((nbp, L, w), lambda b, c: (b, c, 0)),
        out_shape=jax.ShapeDtypeStruct((batch, seq, w), BF16),
        scratch_shapes=[pltpu.VMEM((nbp * nh, hd, hd), F32), pltpu.VMEM((nbp * nh, 8, hd), F32),
                        pltpu.VMEM((nbp * nh, 8, 128), F32),
                        pltpu.VMEM((nbp * 8, w), F32), pltpu.VMEM((nbp * 8, w), F32)],
        compiler_params=_params("parallel", "arbitrary"),
        name="mlstm",
    )(z3, z3, z3, z3, gcol, grow, gbias, conv_w, conv_w, norm_g.reshape(nh, 1, hd), _conv_shift_matrix(L))


def _xattn_kernel(q_ref, k_ref, v_ref, gq_ref, gk_ref, o_ref):
    hd = C_HEAD_DIM
    for h in range(C_HEADS):
        sl = slice(h * hd, (h + 1) * hd)
        q = _rms_rows(q_ref[0, :, sl].astype(F32), gq_ref[...]) * (hd ** -0.5)
        k = _rms_rows(k_ref[0, :, sl].astype(F32), gk_ref[...])
        s = lax.dot_general(q.astype(BF16), k.astype(BF16), (((1,), (1,)), ((), ())), preferred_element_type=F32)
        e = jnp.exp(s - jnp.max(s, axis=-1, keepdims=True))
        p = e / jnp.sum(e, axis=-1, keepdims=True)
        o = jnp.dot(p.astype(BF16), v_ref[0, :, sl], preferred_element_type=F32)
        o_ref[0, :, sl] = o.astype(o_ref.dtype)


def _xattn(z3, kv, gq, gk, batch, seq, tq=512):
    tq = min(tq, seq)
    mlen, w = kv.shape[1], kv.shape[2] // 2
    return pl.pallas_call(
        _xattn_kernel,
        grid=(batch, seq // tq),
        in_specs=[pl.BlockSpec((1, tq, w), lambda b, i: (b, i, Z_CQ // w)),
                  pl.BlockSpec((1, mlen, w), lambda b, i: (b, 0, 0)),
                  pl.BlockSpec((1, mlen, w), lambda b, i: (b, 0, 1)),
                  pl.BlockSpec((1, C_HEAD_DIM), lambda b, i: (0, 0)),
                  pl.BlockSpec((1, C_HEAD_DIM), lambda b, i: (0, 0))],
        out_specs=pl.BlockSpec((1, tq, w), lambda b, i: (b, i, 0)),
        out_shape=jax.ShapeDtypeStruct((batch, seq, w), BF16),
        compiler_params=_params("parallel", "arbitrary"),
        name="memory_cross_attention",
    )(z3, kv, kv, gq.reshape(1, -1), gk.reshape(1, -1))


def _rope_rows(xn, cos, sin_signed):
    half = N_HEAD_DIM // 2
    rot = jnp.concatenate([xn[:, half:], xn[:, :half]], axis=-1)
    return xn * cos + rot * sin_signed


def _knorm_rope_kernel(x_ref, g_ref, cos_ref, sin_ref, seg_ref, swap_ref, o_ref, *, n_blocks):
    hd = N_HEAD_DIM
    ts = x_ref.shape[1]
    x = x_ref[0].astype(F32)
    xx = x * x
    hi = xx.astype(BF16)
    lo = (xx - hi.astype(F32)).astype(BF16)
    ss = (jnp.dot(hi, seg_ref[...], preferred_element_type=F32)
          + jnp.dot(lo, seg_ref[...], preferred_element_type=F32))
    xn = x * lax.rsqrt(ss * (1.0 / hd) + EPS) * g_ref[...]
    rot = jnp.dot(xn.astype(BF16), swap_ref[...], preferred_element_type=F32)
    reps = N_GROUPS * hd // cos_ref.shape[1]
    cos = jnp.concatenate([cos_ref[...]] * reps, axis=1)
    sin = jnp.concatenate([sin_ref[...]] * reps, axis=1)
    k = xn * cos + rot * sin
    if n_blocks:
        k = jnp.clip(k, -F8_MAX, F8_MAX)
    k = k.astype(o_ref.dtype)
    if n_blocks:
        pos = pl.program_id(1) * ts + lax.broadcasted_iota(jnp.int32, (ts, n_blocks), 0)
        blk = lax.broadcasted_iota(jnp.int32, (ts, n_blocks), 1)
        onehot = jnp.where(pos // SEL_BLOCK == blk, F8_MAX, 0.0).astype(o_ref.dtype)
    for g in range(N_GROUPS):
        if n_blocks:
            o_ref[0, g, :, 0:n_blocks] = onehot
        o_ref[0, g, :, n_blocks:n_blocks + hd] = k[:, g * hd:(g + 1) * hd]


def _head_matrices():
    w = N_GROUPS * N_HEAD_DIM
    i = np.arange(w)[:, None]
    j = np.arange(w)[None, :]
    same = (i // N_HEAD_DIM) == (j // N_HEAD_DIM)
    swap = same & ((i % N_HEAD_DIM) == (j % N_HEAD_DIM + N_HEAD_DIM // 2) % N_HEAD_DIM)
    return jnp.asarray(same, BF16), jnp.asarray(swap, BF16)


def _knorm_rope(z3, col0, g, cos2, sin2, n_blocks, ts=1024):
    b, s, _ = z3.shape
    hd, w = N_HEAD_DIM, N_GROUPS * N_HEAD_DIM
    ts = min(ts, s)
    same, swap = _head_matrices()
    const = lambda shape: pl.BlockSpec(shape, lambda i, j: (0, 0))
    return pl.pallas_call(
        functools.partial(_knorm_rope_kernel, n_blocks=n_blocks),
        grid=(b, s // ts),
        in_specs=[pl.BlockSpec((1, ts, w), lambda i, j: (i, j, col0 // w)), const((1, w)),
                  pl.BlockSpec((ts, 2 * hd), lambda i, j: (j, 0)), pl.BlockSpec((ts, 2 * hd), lambda i, j: (j, 0)),
                  const((w, w)), const((w, w))],
        out_specs=pl.BlockSpec((1, N_GROUPS, ts, n_blocks + hd), lambda i, j: (i, 0, j, 0)),
        out_shape=jax.ShapeDtypeStruct((b, N_GROUPS, s, n_blocks + hd), F8 if n_blocks else BF16),
        compiler_params=_params("parallel", "arbitrary"),
        name="key_norm_rope",
    )(z3, jnp.tile(g.reshape(1, hd), (1, N_GROUPS)), cos2, sin2, same, swap)


def _compress_kernel(sub_ref, pe_ref, w1_ref, w2_ref, g_ref, cos_ref, sin_ref, o_ref, *, is_key):
    ns = sub_ref.shape[1]
    sub = sub_ref[0].astype(F32)
    lo = jnp.dot((sub + pe_ref[0:1, :]).astype(BF16), w1_ref[0], preferred_element_type=F32)
    hi = jnp.dot((sub + pe_ref[1:2, :]).astype(BF16), w1_ref[1], preferred_element_type=F32)
    hid = _silu(lo + pltpu.roll(hi, shift=ns - 1, axis=0))
    out = jnp.dot(hid.astype(BF16), w2_ref[...], preferred_element_type=F32)
    if is_key:
        out = _rope_rows(_rms_rows(out, g_ref[...]), cos_ref[...], sin_ref[...])
    o_ref[0] = out.astype(o_ref.dtype)


def _compress(u, pe, w1, w2, g, cos, sin_signed, is_key):
    n, s, hd = u.shape
    ns = s // CMP_STRIDE
    width = CMP_STRIDE * hd
    sub = u.reshape(n, ns, width)
    pe2 = pe.reshape(CMP_LEN // CMP_STRIDE, width)
    w1s = w1.reshape(CMP_LEN // CMP_STRIDE, width, CMP_HIDDEN).astype(BF16)
    return pl.pallas_call(
        functools.partial(_compress_kernel, is_key=is_key),
        grid=(n,),
        in_specs=[pl.BlockSpec((1, ns, width), lambda i: (i, 0, 0)),
                  pl.BlockSpec(pe2.shape, lambda i: (0, 0)),
                  pl.BlockSpec(w1s.shape, lambda i: (0, 0, 0)),
                  pl.BlockSpec(w2.shape, lambda i: (0, 0)),
                  pl.BlockSpec((1, hd), lambda i: (0, 0)),
                  pl.BlockSpec((ns, hd), lambda i: (0, 0)),
                  pl.BlockSpec((ns, hd), lambda i: (0, 0))],
        out_specs=pl.BlockSpec((1, ns, hd), lambda i: (i, 0, 0)),
        out_shape=jax.ShapeDtypeStruct((n, ns, hd), BF16),
        compiler_params=_params("parallel"),
        name="compress_key" if is_key else "compress_value",
    )(sub, pe2, w1s, w2.astype(BF16), g.reshape(1, hd), cos, sin_signed)


def _nsa_kernel(q_ref, gate_ref, kc_ref, vct_ref, ks_ref, vst_ref, kw_ref, vwt_ref, gq_ref, cos_ref, sin_ref,
                wimp_ref, y_ref, s_c, e_c, s_w, e_w, s_a, s_b, e_a, e_b, *, seq):
    tq = q_ref.shape[1]
    gp = kc_ref.shape[1]
    hd = N_HEAD_DIM
    nb = seq // SEL_BLOCK
    top = min(SEL_TOP, nb)
    ncol = N_REP * tq
    t0 = pl.program_id(2) * tq

    def lanes(parts):
        return jnp.concatenate(parts, axis=1)

    def head_cols(a, r):
        return a[:, r * tq:(r + 1) * tq]

    def with_ones(v):
        n = v.shape[1]
        tail = jnp.where(lax.broadcasted_iota(jnp.int32, (V_AUG_ROWS, n), 0) == 0, 1.0, 0.0).astype(v.dtype)
        return jnp.concatenate([v, tail], axis=0)

    cb = min(ncol, NSA_CB)

    def chunk_scores(src, c, j, valid_fn):
        sc = src[c * NSA_RC:(c + 1) * NSA_RC, j * cb:(j + 1) * cb]
        if valid_fn is None:
            return sc
        valid = valid_fn(c * NSA_RC + lax.broadcasted_iota(jnp.int32, (NSA_RC, 1), 0))
        return lanes([jnp.where(valid, head_cols(sc, r), NEG_BIG) for r in range(cb // tq)])

    def put_scores(dst, s):
        dst[...] = s
        return jnp.max(s.reshape(s.shape[0] // 8, 8, ncol), axis=0)

    def col_max(src, rows, valid_fn):
        out = []
        for j in range(ncol // cb):
            mx = jnp.full((8, cb), NEG_BIG, F32)
            for c in range(rows // NSA_RC):
                sc = chunk_scores(src, c, j, valid_fn)
                mx = jnp.maximum(mx, jnp.max(sc.reshape(NSA_RC // 8, 8, cb), axis=0))
            out.append(jnp.max(mx, axis=0, keepdims=True))
        return lanes(out)

    def put_weights(src, dst, rows, valid_fn, m):
        for j in range(ncol // cb):
            mj = m[:, j * cb:(j + 1) * cb]
            for c in range(rows // NSA_RC):
                sc = chunk_scores(src, c, j, valid_fn)
                dst[c * NSA_RC:(c + 1) * NSA_RC, j * cb:(j + 1) * cb] = (
                    jnp.exp2((sc - mj).astype(BF16)).astype(dst.dtype))

    cos = lanes([cos_ref[...]] * N_REP)
    sin = lanes([sin_ref[...]] * N_REP)
    t_q = t0 + lax.broadcasted_iota(jnp.int32, (1, tq), 1)
    tcol = lanes([t_q] * N_REP)
    nc = kc_ref.shape[2]
    valid_c = lambda n: n * CMP_STRIDE + (CMP_LEN - 1) <= t_q
    span = WINDOW + tq
    ws0 = pl.multiple_of(jnp.maximum(t0 - WINDOW, 0), tq)
    valid_w = lambda k: (ws0 + k <= t_q) & (ws0 + k > t_q - WINDOW)
    blk = lax.broadcasted_iota(jnp.int32, (nb, tq), 0)
    cur = (t0 + lax.broadcasted_iota(jnp.int32, (nb, tq), 1)) // SEL_BLOCK
    causal_b = blk <= cur
    forced = (blk == 0) | (blk == cur) | (blk == cur - 1)
    taken = -3.0e38

    def prefix(u):
        q0 = u * N_REP * hd
        q = lanes([q_ref[q0 + r * hd:q0 + (r + 1) * hd, :].astype(F32) for r in range(N_REP)])
        qn = q * lax.rsqrt(jnp.mean(q * q, axis=0, keepdims=True) + EPS) * gq_ref[...]
        rot = jnp.concatenate([qn[hd // 2:], qn[:hd // 2]], axis=0)
        qr = ((qn * cos + rot * sin) * (hd ** -0.5 * LOG2_E)).astype(BF16)

        s_c.at[u][...] = jnp.dot(kc_ref[0, u], qr, preferred_element_type=F32)
        put_weights(s_c.at[u], e_c.at[u], nc, valid_c, col_max(s_c.at[u], nc, valid_c))
        both = jnp.dot(jnp.concatenate([vct_ref[0, u], wimp_ref[...]], axis=0), e_c.at[u][...],
                       preferred_element_type=F32)
        inv_c = jnp.where(tcol >= CMP_LEN - 1, 1.0 / both[hd:hd + 1], 0.0)
        o_c = both[0:hd] * inv_c

        imp_h = both[hd + V_AUG_ROWS:] * inv_c
        imp = head_cols(imp_h, 0)
        for r in range(1, N_REP):
            imp = imp + head_cols(imp_h, r)

        s_w.at[u][...] = jnp.dot(kw_ref[0, u, pl.ds(ws0, span), :], qr, preferred_element_type=F32)
        m_w = col_max(s_w.at[u], span, valid_w)
        score = jnp.where(forced, FORCE_SCORE, jnp.where(causal_b, imp, -1.0))
        return qr, o_c, m_w, score

    def pick_by_value(sc):
        for _ in range(top - 2):
            sc = jnp.where(sc == jnp.max(sc, axis=0, keepdims=True), taken, sc)
        return sc

    def pick_by_value_then_index(sc):
        for _ in range(top):
            mx = jnp.max(sc, axis=0, keepdims=True)
            idx = jnp.min(jnp.where(sc == mx, blk, nb), axis=0, keepdims=True)
            sc = jnp.where(blk == idx, taken, sc)
        return sc

    pre = [prefix(u) for u in range(gp)]
    fast = [pick_by_value(p[3]) for p in pre]
    n_want = jnp.minimum(cur[0:1, :] + 1, top).astype(F32)
    miss = [jnp.abs(jnp.sum(jnp.where((f == taken) & causal_b, 1.0, 0.0), axis=0, keepdims=True) - n_want)
            for f in fast]
    tied = jnp.max(functools.reduce(jnp.maximum, miss)) > 0.0
    picked = lax.cond(tied, lambda: tuple(pick_by_value_then_index(p[3]) for p in pre), lambda: tuple(fast))
    q_aug = []
    for u in range(gp):
        bias = jnp.where((picked[u] == taken) & causal_b, 0.0, BLOCK_BIAS)
        q_aug.append(jnp.concatenate([lanes([bias.astype(BF16)] * N_REP), jnp.clip(pre[u][0], -F8_MAX, F8_MAX)],
                                     axis=0).astype(F8))

    tk = min(NSA_TK, seq)
    last = (t0 + tq - 1) // tk
    s_bufs, e_bufs = (s_a, s_b), (e_a, e_b)

    def scores(u, t, par):
        ks0 = pl.multiple_of(t * tk, tk)
        return put_scores(s_bufs[par].at[u],
                          jnp.dot(ks_ref[0, u, pl.ds(ks0, tk), :], q_aug[u], preferred_element_type=F32))

    def weights(u, t, par, m, mx, masked):
        valid_s = (lambda k: t * tk + k <= t_q) if masked else None
        m_new = jnp.maximum(m, jnp.max(mx, axis=0, keepdims=True))
        put_weights(s_bufs[par].at[u], e_bufs[par].at[u], tk, valid_s, m_new - SEL_WEIGHT_SHIFT)
        return m_new, jnp.exp2(m - m_new)

    def values(u, t, par, alpha, acc):
        ks0 = pl.multiple_of(jnp.maximum(t, 0) * tk, tk)
        v = jnp.clip(vst_ref[u * hd:(u + 1) * hd, pl.ds(ks0, tk)], -F8_MAX, F8_MAX)
        return alpha * acc + jnp.dot(with_ones(v).astype(F8), e_bufs[par].at[u][...],
                                     preferred_element_type=F32)

    def step(t, par, carries, masked, more):
        mx_next = [scores(u, t + 1, 1 - par) if more else c[3] for u, c in enumerate(carries)]
        acc = [values(u, t - 1, 1 - par, c[1], c[2]) for u, c in enumerate(carries)]
        ma = [weights(u, t, par, c[0], c[3], masked) for u, c in enumerate(carries)]
        return tuple((ma[u][0], ma[u][1], acc[u], mx_next[u]) for u in range(gp))

    def pair_body(j, carries):
        return step(2 * j + 1, 1, step(2 * j, 0, carries, False, True), False, True)

    def tail_even(carries):
        carries = step(last, 0, carries, True, False)
        return tuple(values(u, last, 0, c[1], c[2]) for u, c in enumerate(carries))

    def tail_odd(carries):
        carries = step(last, 1, step(last - 1, 0, carries, False, True), True, False)
        return tuple(values(u, last, 1, c[1], c[2]) for u, c in enumerate(carries))

    mx0 = [scores(u, 0, 0) for u in range(gp)]
    e_b[...] = jnp.zeros(e_b.shape, e_b.dtype)
    o_w = []
    for u in range(gp):
        put_weights(s_w.at[u], e_w.at[u], span, valid_w, pre[u][2])
        ow = jnp.dot(with_ones(vwt_ref[u * hd:(u + 1) * hd, pl.ds(ws0, span)]), e_w.at[u][...],
                     preferred_element_type=F32)
        o_w.append(ow[0:hd] / ow[hd:hd + 1])
    carries = tuple((jnp.full((1, ncol), NEG_BIG, F32), jnp.ones((1, ncol), F32),
                     jnp.zeros((hd + V_AUG_ROWS, ncol), F32), mx0[u]) for u in range(gp))
    carries = lax.fori_loop(0, last // 2, pair_body, carries)
    accs = lax.cond(last % 2 == 0, tail_even, tail_odd, carries)

    for u in range(gp):
        o_s = accs[u][0:hd] / accs[u][hd:hd + 1]
        g = _sigmoid(gate_ref[0, u])
        gc, gs, gw = (lanes([g[j, r:r + 1, :] for r in range(N_REP)]) for j in range(3))
        out = gc * pre[u][1] + gs * o_s + gw * o_w[u]
        y0 = u * N_REP * hd
        for r in range(N_REP):
            y_ref[y0 + r * hd:y0 + (r + 1) * hd, :] = head_cols(out, r).astype(y_ref.dtype)


def _nsa(zt, gates, kc, vct, ks_aug, kw, gq, cos_t, sin_t, wimp_t, batch, seq):
    g, hd, gp = N_GROUPS, N_HEAD_DIM, NSA_GROUPS
    tq = min(NSA_TQ, seq)
    tk = min(NSA_TK, seq)
    nq = seq // tq
    ncol = N_REP * tq
    nc = kc.shape[2]
    nb = seq // SEL_BLOCK
    full = lambda *shape: pl.BlockSpec((1, gp) + shape, lambda bi, gi, qi: (bi, gi) + (0,) * len(shape))
    vrow = lambda row0: pl.BlockSpec((gp * hd, seq),
                                     functools.partial(lambda bi, gi, qi, o: (o + gi, bi), o=row0 // (gp * hd)))
    qrows = gp * N_REP * hd
    qspec = pl.BlockSpec((qrows, tq), lambda bi, gi, qi: (ZT_Q // qrows + gi, bi * nq + qi))
    return pl.pallas_call(
        functools.partial(_nsa_kernel, seq=seq),
        grid=(batch, g // gp, nq),
        in_specs=[qspec,
                  pl.BlockSpec((1, gp, 3, N_REP, tq), lambda bi, gi, qi: (bi, gi, 0, 0, qi)),
                  full(nc, hd), full(hd + V_AUG_ROWS, nc), full(seq, nb + hd), vrow(ZT_VS), full(seq, hd),
                  vrow(ZT_VW),
                  pl.BlockSpec((hd, 1), lambda bi, gi, qi: (0, 0)),
                  pl.BlockSpec((hd, tq), lambda bi, gi, qi: (0, qi)),
                  pl.BlockSpec((hd, tq), lambda bi, gi, qi: (0, qi)),
                  pl.BlockSpec((nb, nc), lambda bi, gi, qi: (0, 0))],
        out_specs=pl.BlockSpec((qrows, tq), lambda bi, gi, qi: (gi, bi * nq + qi)),
        out_shape=jax.ShapeDtypeStruct((N_HEADS * hd, batch * seq), BF16),
        scratch_shapes=[pltpu.VMEM((gp, nc, ncol), F32), pltpu.VMEM((gp, nc, ncol), BF16),
                        pltpu.VMEM((gp, WINDOW + tq, ncol), F32), pltpu.VMEM((gp, WINDOW + tq, ncol), BF16),
                        pltpu.VMEM((gp, tk, ncol), F32), pltpu.VMEM((gp, tk, ncol), F32),
                        pltpu.VMEM((gp, tk, ncol), F8), pltpu.VMEM((gp, tk, ncol), F8)],
        compiler_params=_params("parallel", "parallel", "arbitrary"),
        name="nsa_attention",
    )(zt, gates, kc, vct, ks_aug, zt, kw, zt, gq.reshape(hd, 1), cos_t, sin_t, wimp_t)


def _rope_tables(pos):
    half = N_HEAD_DIM // 2
    inv = jnp.power(ROPE_THETA, -jnp.arange(half, dtype=F32) * 2.0 / N_HEAD_DIM)
    ang = pos.astype(F32)[:, None] * inv[None, :]
    cos, sin = jnp.cos(ang), jnp.sin(ang)
    return jnp.concatenate([cos, cos], axis=-1), jnp.concatenate([-sin, sin], axis=-1)


def _importance_matrix(nb, nc_pad):
    per = SEL_BLOCK // CMP_STRIDE
    j = np.arange(nb)[:, None]
    n = np.arange(nc_pad)[None, :]
    w = np.zeros((nb, nc_pad), np.float32)
    for d in range(CMP_LEN // CMP_STRIDE):
        w += ((n + d >= per * j) & (n + d <= per * j + per - 1)).astype(np.float32)
    return jnp.asarray(w, BF16)


def _split_w_in(w_in):
    d = w_in.shape[0]
    sizes = (2048, 1024, 1024, 4, 4, 1024, 256, 256, 256, 256, 256, 256, 48, 1024, 2048, 2048, 2048)
    offs = np.concatenate([[0], np.cumsum(sizes)])
    part = lambda i: w_in[:, offs[i]:offs[i + 1]]
    tok = jnp.concatenate([part(i) for i in (0, 1, 2, 13, 14, 15, 16, 6, 7, 8, 10)], axis=1).astype(BF16)
    feat = jnp.concatenate([part(i) for i in (5, 9, 11)], axis=1).T.astype(BF16)
    small = jnp.concatenate([part(3), part(4), part(12), jnp.zeros((d, 128 - 56), w_in.dtype)], axis=1).astype(BF16)
    return tok, feat, small


def _layer(x, mem, layer, big, norm_mix_g, norm_mem_g, norm_ffn_g, w_in, m_conv_w, m_i_bias, m_f_bias, m_norm_g,
           n_q_norm_g, n_kc_norm_g, n_ks_norm_g, n_kw_norm_g, n_cmp_pe_k, n_cmp_w1_k, n_cmp_w2_k,
           n_cmp_pe_v, n_cmp_w1_v, n_cmp_w2_v, c_q_norm_g, c_k_norm_g, w_mem_k, w_mem_v):
    B, S, D = x.shape
    M = B * S
    G, R, hd = N_GROUPS, N_REP, N_HEAD_DIM
    x2 = x.reshape(M, D)

    w_tok, w_feat, w_small = _split_w_in(w_in)
    z, zs, h = _in_proj(x2, norm_mix_g, w_tok, w_small)
    zt = _mm_feature_major(w_feat, h)
    z3 = z.reshape(B, S, Z_WIDTH)
    zs3 = zs.reshape(B, S, 128)

    gif = jnp.stack([zs3[..., 0:4], zs3[..., 4:8]], axis=-1)
    gcol = gif.transpose(0, 2, 1, 3)
    grow = gif.transpose(0, 2, 3, 1)
    gbias = jnp.stack([m_i_bias, m_f_bias], axis=-1).reshape(M_HEADS, 1, 2).astype(F32)
    y_a = _mlstm(z3, gcol, grow, gbias, m_conv_w, m_norm_g, B, S)

    pos = jnp.arange(S, dtype=jnp.int32)
    cos, sin_s = _rope_tables(pos)
    nc_pad = S // CMP_STRIDE
    cmp_end = jnp.arange(nc_pad, dtype=jnp.int32) * CMP_STRIDE + CMP_LEN - 1
    cos_c, sin_c = _rope_tables(cmp_end)
    nb = S // SEL_BLOCK

    def head_major(col0):
        u = z3[..., col0:col0 + G * hd].reshape(B, S, G, hd)
        return u.transpose(0, 2, 1, 3).reshape(B * G, S, hd)

    kc = _compress(head_major(Z_KC), n_cmp_pe_k, n_cmp_w1_k, n_cmp_w2_k, n_kc_norm_g, cos_c, sin_c, True)
    vc = _compress(head_major(Z_VC), n_cmp_pe_v, n_cmp_w1_v, n_cmp_w2_v, n_kc_norm_g, cos_c, sin_c, False)
    vct = vc.reshape(B, G, nc_pad, hd).transpose(0, 1, 3, 2)
    ones_rows = jnp.concatenate([jnp.ones((B, G, 1, nc_pad), BF16), jnp.zeros((B, G, V_AUG_ROWS - 1, nc_pad), BF16)],
                                axis=2)
    vct = jnp.concatenate([vct, ones_rows], axis=2)
    cos2, sin2 = jnp.concatenate([cos, cos], axis=1), jnp.concatenate([sin_s, sin_s], axis=1)
    ks_aug = _knorm_rope(z3, Z_KS, n_ks_norm_g, cos2, sin2, nb)
    kw = _knorm_rope(z3, Z_KW, n_kw_norm_g, cos2, sin2, 0)
    gates = zs3[..., 8:8 + 3 * N_HEADS].reshape(B, S, G, R, 3).transpose(0, 2, 4, 3, 1)
    y_bt = _nsa(zt, gates, kc.reshape(B, G, nc_pad, hd), vct, ks_aug, kw, n_q_norm_g, cos.T, sin_s.T,
                _importance_matrix(nb, nc_pad), B, S)

    mlen = mem.shape[1]
    mem_n = _rmsnorm(mem.reshape(B * mlen, D), norm_mem_g)
    w_mem_kv = jnp.concatenate([w_mem_k, w_mem_v], axis=1).astype(BF16)
    kv_mem = _mm(mem_n, w_mem_kv, BF16).reshape(B, mlen, -1)
    y_c = _xattn(z3, kv_mem, c_q_norm_g, c_k_norm_g, B, S)

    mix = _mix(y_a.reshape(M, -1), y_bt, y_c.reshape(M, -1), big["w_up_a"], big["w_up_b"], big["w_up_c"], layer,
               z, Z_GATES)
    x2 = _mm_residual(mix, big["w_out"], layer, x2, tm=512, tn=D)

    act = _swiglu(x2, norm_ffn_g, big["w_ffn_gate"], big["w_ffn_up"], layer)
    x2 = _mm_residual(act, big["w_ffn_down"], layer, x2, tm=1024, tn=512)
    return x2.reshape(B, S, D)


def kernel(x, mem, norm_mix_g, norm_mem_g, norm_ffn_g, w_in, m_conv_w, m_i_bias, m_f_bias, m_norm_g, n_q_norm_g, n_kc_norm_g, n_ks_norm_g, n_kw_norm_g, n_cmp_pe_k, n_cmp_w1_k, n_cmp_w2_k, n_cmp_pe_v, n_cmp_w1_v, n_cmp_w2_v, c_q_norm_g, c_k_norm_g, w_mem_k, w_mem_v, w_up_a, w_up_b, w_up_c, w_out, w_ffn_gate, w_ffn_up, w_ffn_down):
    per_layer = (norm_mix_g, norm_mem_g, norm_ffn_g, w_in, m_conv_w, m_i_bias, m_f_bias, m_norm_g, n_q_norm_g,
                 n_kc_norm_g, n_ks_norm_g, n_kw_norm_g, n_cmp_pe_k, n_cmp_w1_k, n_cmp_w2_k, n_cmp_pe_v, n_cmp_w1_v,
                 n_cmp_w2_v, c_q_norm_g, c_k_norm_g, w_mem_k, w_mem_v)
    big = {name: w.astype(BF16) for name, w in (
        ("w_up_a", w_up_a), ("w_up_b", w_up_b), ("w_up_c", w_up_c), ("w_out", w_out),
        ("w_ffn_gate", w_ffn_gate), ("w_ffn_up", w_ffn_up), ("w_ffn_down", w_ffn_down))}
    for layer in range(w_in.shape[0]):
        x = _layer(x, mem, layer, big, *(p[layer] for p in per_layer))
    return x
```
